```python
import math
import jax, jax.numpy as jnp
from jax import lax
import numpy as np

D_MODEL = 1024
BATCH = 8
SEQ = 4096
DEPTH = 1

CHUNK = 64
Q_BLOCK = 128
ATTN_WIDTH = D_MODEL // 2
ATTN_HEADS = 4
ATTN_SUB_DIM = ATTN_WIDTH // (2 * ATTN_HEADS)
ATTN_V_DIM = 2 * ATTN_SUB_DIM
SSM_WIDTH = D_MODEL // 2
SSM_GROUP = 16
SSM_GROUPS = SSM_WIDTH // SSM_GROUP
SSM_STATE = 64
REL_BUCKETS = 32
REL_MAX_DIST = 128
EPS = 1e-6
DT_MIN = 1e-3
DT_MAX = 1e-1
IN_SIZES = [ATTN_WIDTH, ATTN_WIDTH, ATTN_HEADS * ATTN_V_DIM, ATTN_WIDTH,
            SSM_WIDTH, SSM_WIDTH, 2 * D_MODEL]
IN_COLS = sum(IN_SIZES)
IN_SPLITS = [sum(IN_SIZES[:i + 1]) for i in range(len(IN_SIZES) - 1)]

kernel_name = "hybrid_diffattn_s5_gated_merge"


def rms_norm(x, gain):
    xf = x.astype(jnp.float32)
    y = xf * lax.rsqrt(jnp.mean(xf * xf, axis=-1, keepdims=True) + EPS)
    return (y * gain.astype(jnp.float32)).astype(x.dtype)


def t5_relative_bucket(rel):
    nb = REL_BUCKETS // 2
    max_exact = nb // 2
    side = jnp.where(rel > 0, nb, 0)
    n = jnp.abs(rel)
    nf = jnp.maximum(n, 1).astype(jnp.float32)
    large = max_exact + (jnp.log(nf / max_exact) / math.log(REL_MAX_DIST / max_exact)
                         * (nb - max_exact)).astype(jnp.int32)
    large = jnp.minimum(large, nb - 1)
    return side + jnp.where(n < max_exact, n, large)


def diff_attention(q, k, v, lam, rel_table, subln_gain, lam_init):
    B, L = q.shape[0], q.shape[1]
    nb = L // Q_BLOCK
    scale = ATTN_SUB_DIM ** -0.5
    q_blocks = q.reshape(B, nb, Q_BLOCK, ATTN_HEADS, 2, ATTN_SUB_DIM).transpose(1, 0, 2, 3, 4, 5)
    k_pos = jnp.arange(L)

    def one_block(args):
        q_blk, blk = args
        q_pos = blk * Q_BLOCK + jnp.arange(Q_BLOCK)
        s = jnp.einsum('bqhsd,bkhsd->bhsqk', q_blk, k).astype(jnp.float32) * scale
        bias = rel_table[t5_relative_bucket(k_pos[None, :] - q_pos[:, None])]
        bias = jnp.transpose(bias, (2, 0, 1)).astype(jnp.float32)[None, :, None]
        allowed = (k_pos[None, :] // CHUNK) <= (q_pos[:, None] // CHUNK)
        s = jnp.where(allowed, s + bias, -jnp.inf)
        p = jax.nn.softmax(s, axis=-1)
        attn = p[:, :, 0] - lam * p[:, :, 1]
        return jnp.einsum('bhqk,bkhd->bqhd', attn.astype(v.dtype), v)

    out = lax.map(one_block, (q_blocks, jnp.arange(nb)))
    out = out.transpose(1, 0, 2, 3, 4).reshape(B, L, ATTN_HEADS, ATTN_V_DIM)
    out = rms_norm(out, subln_gain) * (1.0 - lam_init)
    return out.reshape(B, L, ATTN_HEADS * ATTN_V_DIM)


def ssm_combine(left, right):
    ar_l, ai_l, br_l, bi_l = left
    ar_r, ai_r, br_r, bi_r = right
    ar = ar_r * ar_l - ai_r * ai_l
    ai = ar_r * ai_l + ai_r * ar_l
    br = ar_r * br_l - ai_r * bi_l + br_r
    bi = ar_r * bi_l + ai_r * br_l + bi_r
    return (ar, ai, br, bi)


def s5_branch(u, A_re, A_im, log_dt, B_re, B_im, C_re, C_im, D, glu_w, glu_b):
    f32 = jnp.float32
    Bsz, L = u.shape[0], u.shape[1]
    uf = u.astype(f32).reshape(Bsz, L, SSM_GROUPS, SSM_GROUP)
    a_re = A_re.astype(f32)
    a_im = A_im.astype(f32)
    dt = jnp.exp(log_dt.astype(f32))[:, None]
    decay = jnp.exp(a_re * dt)
    lb_re = decay * jnp.cos(a_im * dt)
    lb_im = decay * jnp.sin(a_im * dt)
    nr = lb_re - 1.0
    ni = lb_im
    den = a_re * a_re + a_im * a_im
    q_re = (nr * a_re + ni * a_im) / den
    q_im = (ni * a_re - nr * a_im) / den
    b_re = B_re.astype(f32)
    b_im = B_im.astype(f32)
    bb_re = q_re[..., None] * b_re - q_im[..., None] * b_im
    bb_im = q_re[..., None] * b_im + q_im[..., None] * b_re
    bu_re = jnp.einsum('blgh,gph->blgp', uf, bb_re)
    bu_im = jnp.einsum('blgh,gph->blgp', uf, bb_im)
    shape_a = (1, L, SSM_GROUPS, SSM_STATE)
    a_el_re = jnp.broadcast_to(lb_re[None, None], shape_a)
    a_el_im = jnp.broadcast_to(lb_im[None, None], shape_a)
    _, _, x_re, x_im = lax.associative_scan(ssm_combine, (a_el_re, a_el_im, bu_re, bu_im), axis=1)
    y = (jnp.einsum('blgp,ghp->blgh', x_re, C_re.astype(f32))
         - jnp.einsum('blgp,ghp->blgh', x_im, C_im.astype(f32))
         + D.astype(f32) * uf)
    y = y.reshape(Bsz, L, SSM_WIDTH)
    g = jax.nn.gelu(y)
    y = g * jax.nn.sigmoid(g @ glu_w.astype(f32) + glu_b.astype(f32))
    return y.astype(u.dtype)


def setup_inputs(seed: int = 0) -> dict:
    key = jax.random.key(seed)
    ks = jax.random.split(key, 24)
    f32 = jnp.float32
    nrm = lambda k, shape, s: (jax.random.normal(k, shape, f32) * s)
    x = jax.random.normal(ks[0], (BATCH, SEQ, D_MODEL), f32)
    norm_gain = 1.0 + nrm(ks[1], (DEPTH, D_MODEL), 0.01)
    w_in = nrm(ks[2], (DEPTH, D_MODEL, IN_COLS), D_MODEL ** -0.5)
    merge_gate_b = nrm(ks[3], (DEPTH, 2 * D_MODEL), 0.01)
    q_norm_gain = 1.0 + nrm(ks[4], (DEPTH, ATTN_SUB_DIM), 0.01)
    k_norm_gain = 1.0 + nrm(ks[5], (DEPTH, ATTN_SUB_DIM), 0.01)
    lambda_q1 = nrm(ks[6], (DEPTH, ATTN_SUB_DIM), 0.1)
    lambda_k1 = nrm(ks[7], (DEPTH, ATTN_SUB_DIM), 0.1)
    lambda_q2 = nrm(ks[8], (DEPTH, ATTN_SUB_DIM), 0.1)
    lambda_k2 = nrm(ks[9], (DEPTH, ATTN_SUB_DIM), 0.1)
    diff_subln_gain = 1.0 + nrm(ks[10], (DEPTH, ATTN_V_DIM), 0.01)
    rel_bias_table = nrm(ks[11], (REL_BUCKETS, ATTN_HEADS), 0.5)
    n_idx = jnp.arange(SSM_STATE, dtype=f32)
    ssm_A_re = -0.5 + nrm(ks[12], (DEPTH, SSM_GROUPS, SSM_STATE), 0.01)
    ssm_A_im = math.pi * n_idx[None, None, :] + nrm(ks[13], (DEPTH, SSM_GROUPS, SSM_STATE), 0.01)
    ssm_log_dt = jax.random.uniform(ks[14], (DEPTH, SSM_GROUPS), f32,
                                    math.log(DT_MIN), math.log(DT_MAX))
    ssm_B_re = nrm(ks[15], (DEPTH, SSM_GROUPS, SSM_STATE, SSM_GROUP), (2 * SSM_GROUP) ** -0.5)
    ssm_B_im = nrm(ks[16], (DEPTH, SSM_GROUPS, SSM_STATE, SSM_GROUP), (2 * SSM_GROUP) ** -0.5)
    ssm_C_re = nrm(ks[17], (DEPTH, SSM_GROUPS, SSM_GROUP, SSM_STATE), (2 * SSM_STATE) ** -0.5 * 4.0)
    ssm_C_im = nrm(ks[18], (DEPTH, SSM_GROUPS, SSM_GROUP, SSM_STATE), (2 * SSM_STATE) ** -0.5 * 4.0)
    ssm_D = nrm(ks[19], (DEPTH, SSM_GROUPS, SSM_GROUP), 1.0)
    ssm_glu_w = nrm(ks[20], (DEPTH, SSM_WIDTH, SSM_WIDTH), SSM_WIDTH ** -0.5)
    ssm_glu_b = nrm(ks[21], (DEPTH, SSM_WIDTH), 0.01)
    kk = jax.random.split(ks[22], 3)
    proj_attn = nrm(kk[0], (DEPTH, ATTN_HEADS * ATTN_V_DIM, D_MODEL), (ATTN_HEADS * ATTN_V_DIM) ** -0.5)
    proj_ssm = nrm(kk[1], (DEPTH, SSM_WIDTH, D_MODEL), SSM_WIDTH ** -0.5)
    w_out = nrm(kk[2], (DEPTH, D_MODEL, D_MODEL), D_MODEL ** -0.5)
    return {"x": x, "norm_gain": norm_gain, "w_in": w_in, "merge_gate_b": merge_gate_b,
            "q_norm_gain": q_norm_gain, "k_norm_gain": k_norm_gain,
            "lambda_q1": lambda_q1, "lambda_k1": lambda_k1, "lambda_q2": lambda_q2, "lambda_k2": lambda_k2,
            "diff_subln_gain": diff_subln_gain, "rel_bias_table": rel_bias_table,
            "ssm_A_re": ssm_A_re, "ssm_A_im": ssm_A_im, "ssm_log_dt": ssm_log_dt,
            "ssm_B_re": ssm_B_re, "ssm_B_im": ssm_B_im, "ssm_C_re": ssm_C_re, "ssm_C_im": ssm_C_im,
            "ssm_D": ssm_D, "ssm_glu_w": ssm_glu_w, "ssm_glu_b": ssm_glu_b,
            "proj_attn": proj_attn, "proj_ssm": proj_ssm, "w_out": w_out}


def reference(x, norm_gain, w_in, merge_gate_b, q_norm_gain, k_norm_gain,
              lambda_q1, lambda_k1, lambda_q2, lambda_k2, diff_subln_gain, rel_bias_table,
              ssm_A_re, ssm_A_im, ssm_log_dt, ssm_B_re, ssm_B_im, ssm_C_re, ssm_C_im,
              ssm_D, ssm_glu_w, ssm_glu_b, proj_attn, proj_ssm, w_out):
    f32 = jnp.float32
    B, L = x.shape[0], x.shape[1]
    for layer in range(DEPTH):
        lam_init = 0.8 - 0.6 * math.exp(-0.3 * layer)
        h = rms_norm(x, norm_gain[layer])
        z = h @ w_in[layer]
        q, k, v, gate_a, u_s, gate_s, merge_logits = jnp.split(z, IN_SPLITS, axis=-1)
        q = rms_norm(q.reshape(B, L, ATTN_HEADS, 2, ATTN_SUB_DIM), q_norm_gain[layer])
        k = rms_norm(k.reshape(B, L, ATTN_HEADS, 2, ATTN_SUB_DIM), k_norm_gain[layer])
        v = v.reshape(B, L, ATTN_HEADS, ATTN_V_DIM)
        lam = (jnp.exp(jnp.sum(lambda_q1[layer].astype(f32) * lambda_k1[layer].astype(f32)))
               - jnp.exp(jnp.sum(lambda_q2[layer].astype(f32) * lambda_k2[layer].astype(f32)))
               + lam_init)
        o_a = diff_attention(q, k, v, lam, rel_bias_table, diff_subln_gain[layer], lam_init)
        o_a = o_a * jax.nn.silu(gate_a)
        o_s = s5_branch(u_s, ssm_A_re[layer], ssm_A_im[layer], ssm_log_dt[layer],
                        ssm_B_re[layer], ssm_B_im[layer], ssm_C_re[layer], ssm_C_im[layer],
                        ssm_D[layer], ssm_glu_w[layer], ssm_glu_b[layer])
        o_s = o_s * jax.nn.silu(gate_s)
        g = jax.nn.sigmoid((merge_logits + merge_gate_b[layer]).astype(f32)).astype(x.dtype)
        g_a, g_s = jnp.split(g, 2, axis=-1)
        merged = g_a * (o_a @ proj_attn[layer]) + g_s * (o_s @ proj_ssm[layer])
        x = x + merged @ w_out[layer]
    return x
```

```python
import functools
import math

import jax
import jax.numpy as jnp
import numpy as np
from jax import lax
from jax.experimental import pallas as pl
from jax.experimental.pallas import tpu as pltpu

_F32 = jnp.float32
_BF16 = jnp.bfloat16

_CHUNK = 64
_HEADS = 4
_SUB_DIM = 64
_V_DIM = 128
_SSM_GROUP = 16
_SSM_STATE = 64
_REL_BUCKETS = 32
_REL_MAX_DIST = 128
_EPS = 1e-6
_LOG2E = math.log2(math.e)

_VMEM_LIMIT = 48 * 1024 * 1024
_TOKEN_TILE = 512
_ATTN_TILE = 256
_SCAN_STEPS = 64
_SCAN_LANES = 512


def _rms_scale(x, eps=_EPS):
    return lax.rsqrt(jnp.mean(x * x, axis=-1, keepdims=True) + eps)


def _inproj_body(x_ref, ng_ref, w_ref, gq_ref, gk_ref, gsum_ref,
                 q_ref, k_ref, v_ref, u_ref, *, width):
    x = x_ref[...]
    h = (x * _rms_scale(x) * ng_ref[...]).astype(_BF16)
    z = jnp.dot(h, w_ref[...], preferred_element_type=_F32)

    def sub_head_norm(t, g_ref):
        ss = jnp.dot((t * t).astype(_BF16), gsum_ref[...], preferred_element_type=_F32)
        return t * lax.rsqrt(ss * (1.0 / _SUB_DIM) + _EPS) * g_ref[...]

    q_ref[...] = sub_head_norm(z[:, :width], gq_ref).astype(_BF16)
    k_ref[...] = sub_head_norm(z[:, width:2 * width], gk_ref).astype(_BF16)
    v_ref[...] = z[:, 2 * width:3 * width].astype(_BF16)
    u_ref[...] = z[:, 3 * width:].astype(_BF16)


def _inproj(x2, norm_gain, w_qkvu, gq, gk, gsum, width):
    n, d = x2.shape
    tm = _TOKEN_TILE
    full = lambda a: pl.BlockSpec(a.shape, lambda i: (0,) * a.ndim)
    out_sd = jax.ShapeDtypeStruct((n, width), _BF16)
    out_spec = pl.BlockSpec((tm, width), lambda i: (i, 0))
    return pl.pallas_call(
        functools.partial(_inproj_body, width=width),
        grid=(n // tm,),
        in_specs=[pl.BlockSpec((tm, d), lambda i: (i, 0)), full(norm_gain), full(w_qkvu),
                  full(gq), full(gk), full(gsum)],
        out_specs=[out_spec] * 4,
        out_shape=[out_sd] * 4,
        compiler_params=pltpu.CompilerParams(
            dimension_semantics=("parallel",), vmem_limit_bytes=_VMEM_LIMIT),
        name="inproj",
    )(x2, norm_gain, w_qkvu, gq, gk, gsum)


def _t5_bucket_np(rel):
    nb = _REL_BUCKETS // 2
    max_exact = nb // 2
    side = np.where(rel > 0, nb, 0)
    n = np.abs(rel)
    nf = np.maximum(n, 1).astype(np.float32)
    large = max_exact + (np.log(nf / np.float32(max_exact)) / np.float32(math.log(_REL_MAX_DIST / max_exact))
                         * np.float32(nb - max_exact)).astype(np.int32)
    large = np.minimum(large, nb - 1)
    return side + np.where(n < max_exact, n, large)


def _bucket_tiles(t):
    i = np.arange(t)[:, None]
    j = np.arange(t)[None, :]
    diag = np.where((j // _CHUNK) <= (i // _CHUNK), _t5_bucket_np(j - i), -1)
    prev = _t5_bucket_np(j - i - t)
    return np.stack([diag, prev]).astype(np.int32)


def _bias_prep_body(tab_ref, bucket_ref, lam4_ref, bias_ref, lam_ref, *, lam_init):
    h = pl.program_id(0)
    bkt = bucket_ref[...]
    far = tab_ref[_REL_BUCKETS // 2 - 1, h]
    val = jnp.full(bkt.shape, -jnp.inf, _F32)
    for b in range(_REL_BUCKETS):
        val = jnp.where(bkt == b, (tab_ref[b, h] - far) * _LOG2E, val)
    bias_ref[0] = val
    l4 = lam4_ref[...]
    s1 = jnp.sum(l4[0:1] * l4[1:2], axis=-1, keepdims=True)
    s2 = jnp.sum(l4[2:3] * l4[3:4], axis=-1, keepdims=True)
    lam_ref[...] = jnp.broadcast_to(jnp.exp(s1) - jnp.exp(s2) + lam_init, lam_ref.shape)


def _bias_prep(rel_table, lam4, lam_init, t):
    buckets = jnp.asarray(_bucket_tiles(t))
    return pl.pallas_call(
        functools.partial(_bias_prep_body, lam_init=lam_init),
        grid=(_HEADS,),
        in_specs=[pl.BlockSpec(memory_space=pltpu.SMEM),
                  pl.BlockSpec(buckets.shape, lambda h: (0, 0, 0)),
                  pl.BlockSpec(lam4.shape, lambda h: (0, 0))],
        out_specs=[pl.BlockSpec((1, 2, t, t), lambda h: (h, 0, 0, 0)),
                   pl.BlockSpec((1, _V_DIM), lambda h: (0, 0))],
        out_shape=[jax.ShapeDtypeStruct((_HEADS, 2, t, t), _F32),
                   jax.ShapeDtypeStruct((1, _V_DIM), _F32)],
        compiler_params=pltpu.CompilerParams(dimension_semantics=("arbitrary",)),
        name="attn_bias_prep",
    )(rel_table, buckets, lam4)


def _attn_body(q_ref, k_ref, v_ref, bias_ref, lam_ref, gain_ref, o_ref,
               qs_ref, m_ref, l_ref, acc_ref, *, t):
    qi = pl.program_id(2)
    q = q_ref[0]
    lane = lax.broadcasted_iota(jnp.int32, q.shape, 1)
    zero = jnp.zeros_like(q)
    qs_ref[:t] = jnp.where(lane < _SUB_DIM, q, zero)
    qs_ref[t:] = jnp.where(lane >= _SUB_DIM, q, zero)
    m_ref[...] = jnp.full(m_ref.shape, -jnp.inf, _F32)
    l_ref[...] = jnp.zeros(l_ref.shape, _F32)
    acc_ref[...] = jnp.zeros(acc_ref.shape, _F32)

    def step(kj, bias):
        ks = pl.multiple_of(kj * t, t)
        kt = k_ref[0, pl.ds(ks, t), :]
        vt = v_ref[0, pl.ds(ks, t), :]
        s = lax.dot_general(qs_ref[...], kt, (((1,), (1,)), ((), ())),
                            preferred_element_type=_F32)
        if bias is not None:
            s = (s.reshape(2, t, t) + bias[None]).reshape(2 * t, t)
        m_old = m_ref[...]
        m_new = jnp.maximum(m_old, jnp.max(s, axis=-1, keepdims=True))
        alpha = jnp.exp2(m_old - m_new)
        p = jnp.exp2(s - m_new)
        l_ref[...] = alpha * l_ref[...] + jnp.sum(p, axis=-1, keepdims=True)
        acc_ref[...] = alpha * acc_ref[...] + jnp.dot(p.astype(_BF16), vt,
                                                      preferred_element_type=_F32)
        m_ref[...] = m_new

    def far_step(kj, carry):
        step(kj, None)
        return carry

    lax.fori_loop(0, qi - 1, far_step, 0)

    @pl.when(qi >= 1)
    def _():
        step(qi - 1, bias_ref[0, 1])

    step(qi, bias_ref[0, 0])

    o = acc_ref[...] / l_ref[...]
    od = o[:t] - lam_ref[...] * o[t:]
    o_ref[0] = (od * _rms_scale(od) * gain_ref[...]).astype(_BF16)


def _attention(q, k, v, bias, lam, gain):
    b, l, w = q.shape
    t = _ATTN_TILE
    qo_spec = pl.BlockSpec((1, t, _V_DIM), lambda bi, h, qi: (bi, qi, h))
    kv_spec = pl.BlockSpec((1, l, _V_DIM), lambda bi, h, qi: (bi, 0, h))
    return pl.pallas_call(
        functools.partial(_attn_body, t=t),
        grid=(b, _HEADS, l // t),
        in_specs=[qo_spec, kv_spec, kv_spec,
                  pl.BlockSpec((1, 2, t, t), lambda bi, h, qi: (h, 0, 0, 0)),
                  pl.BlockSpec(lam.shape, lambda bi, h, qi: (0, 0)),
                  pl.BlockSpec(gain.shape, lambda bi, h, qi: (0, 0))],
        out_specs=qo_spec,
        out_shape=jax.ShapeDtypeStruct((b, l, w), _BF16),
        scratch_shapes=[pltpu.VMEM((2 * t, _V_DIM), _BF16),
                        pltpu.VMEM((2 * t, 1), _F32),
                        pltpu.VMEM((2 * t, 1), _F32),
                        pltpu.VMEM((2 * t, _V_DIM), _F32)],
        compiler_params=pltpu.CompilerParams(
            dimension_semantics=("parallel", "parallel", "arbitrary"),
            vmem_limit_bytes=_VMEM_LIMIT),
        name="diff_attention",
    )(q, k, v, bias, lam, gain)


def _s5_prep_body(are_ref, aim_ref, ldt_ref, bre_ref, bim_ref,
                  lbre_ref, lbim_ref, bbre_ref, bbim_ref):
    a_re = are_ref[...]
    a_im = aim_ref[...]
    dt = jnp.exp(ldt_ref[...])
    decay = jnp.exp(a_re * dt)
    lb_re = decay * jnp.cos(a_im * dt)
    lb_im = decay * jnp.sin(a_im * dt)
    nr = lb_re - 1.0
    ni = lb_im
    den = a_re * a_re + a_im * a_im
    q_re = (nr * a_re + ni * a_im) / den
    q_im = (ni * a_re - nr * a_im) / den
    b_re = bre_ref[...]
    b_im = bim_ref[...]
    bbre_ref[...] = q_re * b_re - q_im * b_im
    bbim_ref[...] = q_re * b_im + q_im * b_re
    lbre_ref[...] = lb_re
    lbim_ref[...] = lb_im


def _s5_prep(a_re, a_im, log_dt, b_re, b_im):
    sd = jax.ShapeDtypeStruct(a_re.shape, _F32)
    return pl.pallas_call(_s5_prep_body, out_shape=[sd] * 4, name="s5_discretise")(
        a_re, a_im, log_dt, b_re, b_im)


def _s5_body(u_ref, perm_ref, permt_ref, wbre_ref, wbim_ref, lbre_ref, lbim_ref,
             wcre_ref, wcim_ref, d_ref, gw_ref, gb_ref, o_ref,
             sre_ref, sim_ref, cre_ref, cim_ref, *, steps, nb):
    rows = nb * steps
    width = u_ref.shape[-1]
    n_state = sre_ref.shape[-1]
    halves = wbre_ref.shape[0]
    hw_in = width // halves
    hw_st = n_state // halves

    @pl.when(pl.program_id(0) == 0)
    def _():
        cre_ref[...] = jnp.zeros(cre_ref.shape, _F32)
        cim_ref[...] = jnp.zeros(cim_ref.shape, _F32)

    u_bm = u_ref[...].reshape(rows, width)
    u_tm = jnp.dot(perm_ref[...], u_bm, preferred_element_type=_F32).astype(_BF16)
    for hf in range(halves):
        uh = u_tm[:, hf * hw_in:(hf + 1) * hw_in]
        sre_ref[:, hf * hw_st:(hf + 1) * hw_st] = jnp.dot(uh, wbre_ref[hf], preferred_element_type=_F32)
        sim_ref[:, hf * hw_st:(hf + 1) * hw_st] = jnp.dot(uh, wbim_ref[hf], preferred_element_type=_F32)

    for c in range(n_state // _SCAN_LANES):
        sl = slice(c * _SCAN_LANES, (c + 1) * _SCAN_LANES)
        a_r = jnp.broadcast_to(lbre_ref[:, sl], (nb, _SCAN_LANES))
        a_i = jnp.broadcast_to(lbim_ref[:, sl], (nb, _SCAN_LANES))

        def scan_step(ti, carry, sl=sl, a_r=a_r, a_i=a_i):
            x_r, x_i = carry
            r0 = pl.multiple_of(ti * nb, nb)
            n_r = a_r * x_r - a_i * x_i + sre_ref[pl.ds(r0, nb), sl]
            n_i = a_r * x_i + a_i * x_r + sim_ref[pl.ds(r0, nb), sl]
            sre_ref[pl.ds(r0, nb), sl] = n_r
            sim_ref[pl.ds(r0, nb), sl] = n_i
            return n_r, n_i

        x_r, x_i = lax.fori_loop(0, steps, scan_step, (cre_ref[:, sl], cim_ref[:, sl]), unroll=4)
        cre_ref[:, sl] = x_r
        cim_ref[:, sl] = x_i

    ys = []
    for hf in range(halves):
        x_r = sre_ref[:, hf * hw_st:(hf + 1) * hw_st].astype(_BF16)
        x_i = sim_ref[:, hf * hw_st:(hf + 1) * hw_st].astype(_BF16)
        ys.append(jnp.dot(x_r, wcre_ref[hf], preferred_element_type=_F32)
                  + jnp.dot(x_i, wcim_ref[hf], preferred_element_type=_F32))
    y = jnp.concatenate(ys, axis=-1) + d_ref[...] * u_tm.astype(_F32)
    g = jax.nn.gelu(y)
    o = g * jax.nn.sigmoid(jnp.dot(g.astype(_BF16), gw_ref[...], preferred_element_type=_F32)
                           + gb_ref[...])
    o_bm = jnp.dot(permt_ref[...], o.astype(_BF16), preferred_element_type=_F32)
    o_ref[...] = o_bm.astype(_BF16).reshape(o_ref.shape)


def _s5(u, perm, permt, wb_re, wb_im, lb_re, lb_im, wc_re, wc_im, d_vec, glu_w, glu_b):
    nb, l, width = u.shape
    steps = _SCAN_STEPS
    n_state = lb_re.shape[-1]
    full = lambda a: pl.BlockSpec(a.shape, lambda i: (0,) * a.ndim)
    io_spec = pl.BlockSpec((nb, steps, width), lambda i: (0, i, 0))
    consts = (perm, permt, wb_re, wb_im, lb_re, lb_im, wc_re, wc_im, d_vec, glu_w, glu_b)
    return pl.pallas_call(
        functools.partial(_s5_body, steps=steps, nb=nb),
        grid=(l // steps,),
        in_specs=[io_spec] + [full(a) for a in consts],
        out_specs=io_spec,
        out_shape=jax.ShapeDtypeStruct(u.shape, _BF16),
        scratch_shapes=[pltpu.VMEM((nb * steps, n_state), _F32),
                        pltpu.VMEM((nb * steps, n_state), _F32),
                        pltpu.VMEM((nb, n_state), _F32),
                        pltpu.VMEM((nb, n_state), _F32)],
        compiler_params=pltpu.CompilerParams(
            dimension_semantics=("arbitrary",), vmem_limit_bytes=_VMEM_LIMIT),
        name="s5_branch",
    )(u, *consts)


def _outproj_body(x_ref, ng_ref, wg_ref, mb_ref, oa_ref, os_ref, pa_ref, ps_ref, wo_ref,
                  out_ref, *, width):
    x = x_ref[...]
    d = x.shape[-1]
    h = (x * _rms_scale(x) * ng_ref[...]).astype(_BF16)
    zg = jnp.dot(h, wg_ref[...], preferred_element_type=_F32)
    o_a = (oa_ref[...].astype(_F32) * jax.nn.silu(zg[:, :width])).astype(_BF16)
    o_s = (os_ref[...].astype(_F32) * jax.nn.silu(zg[:, width:2 * width])).astype(_BF16)
    p_a = jnp.dot(o_a, pa_ref[...], preferred_element_type=_F32)
    p_s = jnp.dot(o_s, ps_ref[...], preferred_element_type=_F32)
    g = jax.nn.sigmoid(zg[:, 2 * width:] + mb_ref[...])
    merged = g[:, :d] * p_a + g[:, d:] * p_s
    out_ref[...] = x + jnp.dot(merged.astype(_BF16), wo_ref[...], preferred_element_type=_F32)


def _outproj(x2, norm_gain, w_gates, merge_b, o_a, o_s, proj_a, proj_s, w_out, width):
    n, d = x2.shape
    tm = _TOKEN_TILE
    full = lambda a: pl.BlockSpec(a.shape, lambda i: (0,) * a.ndim)
    row = lambda a: pl.BlockSpec((tm, a.shape[-1]), lambda i: (i, 0))
    return pl.pallas_call(
        functools.partial(_outproj_body, width=width),
        grid=(n // tm,),
        in_specs=[row(x2), full(norm_gain), full(w_gates), full(merge_b), row(o_a), row(o_s),
                  full(proj_a), full(proj_s), full(w_out)],
        out_specs=row(x2),
        out_shape=jax.ShapeDtypeStruct(x2.shape, x2.dtype),
        compiler_params=pltpu.CompilerParams(
            dimension_semantics=("parallel",), vmem_limit_bytes=_VMEM_LIMIT),
        name="outproj",
    )(x2, norm_gain, w_gates, merge_b, o_a, o_s, proj_a, proj_s, w_out)


def _block_diag_halves(blocks, halves):
    g, r, c = blocks.shape
    gh = g // halves
    eye = jnp.eye(gh, dtype=blocks.dtype)
    b = blocks.reshape(halves, gh, r, 1, c) * eye.reshape(1, gh, 1, gh, 1)
    return b.reshape(halves, gh * r, gh * c)


def _time_major_perm(nb, steps):
    r = np.arange(nb * steps)
    src = (r % nb) * steps + r // nb
    p = np.zeros((nb * steps, nb * steps), np.float32)
    p[r, src] = 1.0
    return p


def _layer(x, lam_init, norm_gain, w_in, merge_gate_b, q_norm_gain, k_norm_gain,
           lambda_q1, lambda_k1, lambda_q2, lambda_k2, diff_subln_gain, rel_bias_table,
           ssm_a_re, ssm_a_im, ssm_log_dt, ssm_b_re, ssm_b_im, ssm_c_re, ssm_c_im,
           ssm_d, ssm_glu_w, ssm_glu_b, proj_attn, proj_ssm, w_out):
    nb, l, d = x.shape
    n = nb * l
    aw = _HEADS * 2 * _SUB_DIM
    groups = ssm_a_re.shape[0]
    x2 = x.reshape(n, d)
    ng = norm_gain.reshape(1, d).astype(_F32)

    w_qkvu = jnp.concatenate([w_in[:, :3 * aw], w_in[:, 4 * aw:5 * aw]], axis=1).astype(_BF16)
    w_gates = jnp.concatenate([w_in[:, 3 * aw:4 * aw], w_in[:, 5 * aw:]], axis=1).astype(_BF16)
    gq = (jnp.tile(q_norm_gain.astype(_F32), 2 * _HEADS) * (_SUB_DIM ** -0.5 * _LOG2E)).reshape(1, aw)
    gk = jnp.tile(k_norm_gain.astype(_F32), 2 * _HEADS).reshape(1, aw)
    seg = np.arange(aw) // _SUB_DIM
    gsum = jnp.asarray((seg[:, None] == seg[None, :]).astype(np.float32), _BF16)
    q, k, v, u = _inproj(x2, ng, w_qkvu, gq, gk, gsum, aw)

    lam4 = jnp.stack([lambda_q1, lambda_k1, lambda_q2, lambda_k2]).astype(_F32)
    bias, lam = _bias_prep(rel_bias_table.astype(_F32), lam4, lam_init, _ATTN_TILE)
    subln = (diff_subln_gain.astype(_F32) * (1.0 - lam_init)).reshape(1, _V_DIM)
    shp = (nb, l, aw)
    o_a = _attention(q.reshape(shp), k.reshape(shp), v.reshape(shp), bias, lam, subln)

    rep = lambda a: jnp.repeat(a.astype(_F32), _SSM_GROUP, axis=0)
    ldt = jnp.broadcast_to(ssm_log_dt.astype(_F32)[:, None], ssm_a_re.shape)
    bt = lambda a: a.astype(_F32).transpose(0, 2, 1).reshape(groups * _SSM_GROUP, _SSM_STATE)
    lb_re, lb_im, bb_re, bb_im = _s5_prep(rep(ssm_a_re), rep(ssm_a_im), rep(ldt),
                                          bt(ssm_b_re), bt(ssm_b_im))
    n_state = groups * _SSM_STATE
    flat = lambda a: a[::_SSM_GROUP].reshape(1, n_state)
    gshape = (groups, _SSM_GROUP, _SSM_STATE)
    wb_re = _block_diag_halves(bb_re.reshape(gshape), 2).astype(_BF16)
    wb_im = _block_diag_halves(bb_im.reshape(gshape), 2).astype(_BF16)
    wc_re = _block_diag_halves(ssm_c_re.astype(_F32).transpose(0, 2, 1), 2).astype(_BF16)
    wc_im = _block_diag_halves(-ssm_c_im.astype(_F32).transpose(0, 2, 1), 2).astype(_BF16)
    perm = _time_major_perm(nb, _SCAN_STEPS)
    o_s = _s5(u.reshape(shp), jnp.asarray(perm, _BF16), jnp.asarray(perm.T, _BF16),
              wb_re, wb_im, flat(lb_re), flat(lb_im), wc_re, wc_im,
              ssm_d.astype(_F32).reshape(1, aw), ssm_glu_w.astype(_BF16),
              ssm_glu_b.astype(_F32).reshape(1, aw))

    out = _outproj(x2, ng, w_gates, merge_gate_b.astype(_F32).reshape(1, 2 * d),
                   o_a.reshape(n, aw), o_s.reshape(n, aw),
                   proj_attn.astype(_BF16), proj_ssm.astype(_BF16), w_out.astype(_BF16), aw)
    return out.reshape(nb, l, d)


def kernel(x, norm_gain, w_in, merge_gate_b, q_norm_gain, k_norm_gain, lambda_q1, lambda_k1,
           lambda_q2, lambda_k2, diff_subln_gain, rel_bias_table, ssm_A_re, ssm_A_im, ssm_log_dt,
           ssm_B_re, ssm_B_im, ssm_C_re, ssm_C_im, ssm_D, ssm_glu_w, ssm_glu_b,
           proj_attn, proj_ssm, w_out):
    per_layer = (norm_gain, w_in, merge_gate_b, q_norm_gain, k_norm_gain, lambda_q1, lambda_k1,
                 lambda_q2, lambda_k2, diff_subln_gain)
    per_layer_tail = (ssm_A_re, ssm_A_im, ssm_log_dt, ssm_B_re, ssm_B_im, ssm_C_re, ssm_C_im,
                      ssm_D, ssm_glu_w, ssm_glu_b, proj_attn, proj_ssm, w_out)
    for layer in range(norm_gain.shape[0]):
        lam_init = 0.8 - 0.6 * math.exp(-0.3 * layer)
        x = _layer(x, lam_init, *(p[layer] for p in per_layer), rel_bias_table,
                   *(p[layer] for p in per_layer_tail))
    return x
```

```python
import functools
import math

import jax
import jax.numpy as jnp
import numpy as np
from jax import lax
from jax.experimental import pallas as pl
from jax.experimental.pallas import tpu as pltpu

_F32 = jnp.float32
_BF16 = jnp.bfloat16

_CHUNK = 64
_HEADS = 4
_SUB_DIM = 64
_V_DIM = 128
_SSM_GROUP = 16
_SSM_STATE = 64
_REL_BUCKETS = 32
_REL_MAX_DIST = 128
_EPS = 1e-6
_LOG2E = math.log2(math.e)

_VMEM_LIMIT = 48 * 1024 * 1024
_TOKEN_TILE = 512
_ATTN_TILE = 512
_KEY_GRANULE = 256
_ONES_ROWS = 16
_SCAN_STEPS = 64
_SCAN_LANES = 512


def _rms_scale(x, eps=_EPS):
    return lax.rsqrt(jnp.mean(x * x, axis=-1, keepdims=True) + eps)


def _inproj_body(x_ref, ng_ref, wku_ref, wqvt_ref, gqt_ref, gk_ref, gsum_ref,
                 qt_ref, k_ref, vt_ref, u_ref, *, width):
    x = x_ref[...]
    tm = x.shape[0]
    h = (x * _rms_scale(x) * ng_ref[...]).astype(_BF16)
    z = jnp.dot(h, wku_ref[...], preferred_element_type=_F32)
    zt = lax.dot_general(wqvt_ref[...], h, (((1,), (1,)), ((), ())),
                         preferred_element_type=_F32)

    kk = z[:, :width]
    ss = jnp.dot((kk * kk).astype(_BF16), gsum_ref[...], preferred_element_type=_F32)
    k_ref[...] = (kk * lax.rsqrt(ss * (1.0 / _SUB_DIM) + _EPS) * gk_ref[...]).astype(_BF16)
    u_ref[...] = z[:, width:].astype(_BF16)

    qt = zt[:width].reshape(width // _SUB_DIM, _SUB_DIM, tm)
    ms = jnp.mean(qt * qt, axis=1, keepdims=True)
    qt_ref[...] = ((qt * lax.rsqrt(ms + _EPS)).reshape(width, tm) * gqt_ref[...]).astype(_BF16)

    vt = zt[width:].astype(_BF16)
    ones = jnp.ones((_ONES_ROWS, _KEY_GRANULE), _BF16)
    for j in range(tm // _KEY_GRANULE):
        for hd in range(_HEADS):
            vt_ref[j, hd, :_V_DIM, :] = vt[hd * _V_DIM:(hd + 1) * _V_DIM,
                                           j * _KEY_GRANULE:(j + 1) * _KEY_GRANULE]
            vt_ref[j, hd, _V_DIM:, :] = ones


def _inproj(x2, norm_gain, w_ku, w_qvt, gqt, gk, gsum, width):
    n, d = x2.shape
    tm = _TOKEN_TILE
    full = lambda a: pl.BlockSpec(a.shape, lambda i: (0,) * a.ndim)
    row_sd = jax.ShapeDtypeStruct((n, width), _BF16)
    row_spec = pl.BlockSpec((tm, width), lambda i: (i, 0))
    gran = tm // _KEY_GRANULE
    vrows = _V_DIM + _ONES_ROWS
    return pl.pallas_call(
        functools.partial(_inproj_body, width=width),
        grid=(n // tm,),
        in_specs=[pl.BlockSpec((tm, d), lambda i: (i, 0)), full(norm_gain), full(w_ku), full(w_qvt),
                  full(gqt), full(gk), full(gsum)],
        out_specs=[pl.BlockSpec((width, tm), lambda i: (0, i)), row_spec,
                   pl.BlockSpec((gran, _HEADS, vrows, _KEY_GRANULE), lambda i: (i, 0, 0, 0)), row_spec],
        out_shape=[jax.ShapeDtypeStruct((width, n), _BF16), row_sd,
                   jax.ShapeDtypeStruct((n // _KEY_GRANULE, _HEADS, vrows, _KEY_GRANULE), _BF16), row_sd],
        compiler_params=pltpu.CompilerParams(
            dimension_semantics=("parallel",), vmem_limit_bytes=_VMEM_LIMIT),
        name="inproj",
    )(x2, norm_gain, w_ku, w_qvt, gqt, gk, gsum)


def _t5_bucket_np(rel):
    nb = _REL_BUCKETS // 2
    max_exact = nb // 2
    side = np.where(rel > 0, nb, 0)
    n = np.abs(rel)
    nf = np.maximum(n, 1).astype(np.float32)
    large = max_exact + (np.log(nf / np.float32(max_exact)) / np.float32(math.log(_REL_MAX_DIST / max_exact))
                         * np.float32(nb - max_exact)).astype(np.int32)
    large = np.minimum(large, nb - 1)
    return side + np.where(n < max_exact, n, large)


def _bucket_tiles(t):
    i = np.arange(t)[None, :]
    j = np.arange(t)[:, None]
    diag = np.where((j // _CHUNK) <= (i // _CHUNK), _t5_bucket_np(j - i), -1)
    prev = _t5_bucket_np(j - i - t)
    return np.stack([diag, prev]).astype(np.int32)


def _bias_prep_body(tab_ref, bucket_ref, lam4_ref, bias_ref, lam_ref, *, lam_init):
    h = pl.program_id(0)
    bkt = bucket_ref[...]
    far = tab_ref[_REL_BUCKETS // 2 - 1, h]
    val = jnp.full(bkt.shape, -jnp.inf, _F32)
    for b in range(_REL_BUCKETS):
        val = jnp.where(bkt == b, (tab_ref[b, h] - far) * _LOG2E, val)
    bias_ref[0] = val
    l4 = lam4_ref[...]
    s1 = jnp.sum(l4[0:1] * l4[1:2], axis=-1, keepdims=True)
    s2 = jnp.sum(l4[2:3] * l4[3:4], axis=-1, keepdims=True)
    lam_ref[...] = jnp.broadcast_to(jnp.exp(s1) - jnp.exp(s2) + lam_init, lam_ref.shape)


def _bias_prep(rel_table, lam4, lam_init, t):
    buckets = jnp.asarray(_bucket_tiles(t))
    return pl.pallas_call(
        functools.partial(_bias_prep_body, lam_init=lam_init),
        grid=(_HEADS,),
        in_specs=[pl.BlockSpec(memory_space=pltpu.SMEM),
                  pl.BlockSpec(buckets.shape, lambda h: (0, 0, 0)),
                  pl.BlockSpec(lam4.shape, lambda h: (0, 0))],
        out_specs=[pl.BlockSpec((1, 2, t, t), lambda h: (h, 0, 0, 0)),
                   pl.BlockSpec((1, _V_DIM), lambda h: (0, 0))],
        out_shape=[jax.ShapeDtypeStruct((_HEADS, 2, t, t), _F32),
                   jax.ShapeDtypeStruct((1, _V_DIM), _F32)],
        compiler_params=pltpu.CompilerParams(dimension_semantics=("arbitrary",)),
        name="attn_bias_prep",
    )(rel_table, buckets, lam4)


def _attn_body(qt_ref, k_ref, vt_ref, bias_ref, lam_ref, gain_ref, o_ref,
               qw_ref, m_ref, acc_ref, *, t):
    qi = pl.program_id(2)
    gran = t // _KEY_GRANULE
    qt = qt_ref[...]
    row = lax.broadcasted_iota(jnp.int32, qt.shape, 0)
    zero = jnp.zeros_like(qt)
    qw_ref[0] = jnp.where(row < _SUB_DIM, qt, zero)
    qw_ref[1] = jnp.where(row >= _SUB_DIM, qt, zero)
    m_ref[...] = jnp.full(m_ref.shape, -jnp.inf, _F32)
    acc_ref[...] = jnp.zeros(acc_ref.shape, _F32)

    def step(kj, bias):
        ks = pl.multiple_of(kj * t, t)
        kt = k_ref[0, pl.ds(ks, t), :]
        for s in range(2):
            st = jnp.dot(kt, qw_ref[s], preferred_element_type=_F32)
            if bias is not None:
                st = st + bias
            m_old = m_ref[s]
            m_new = jnp.maximum(m_old, jnp.max(st, axis=0, keepdims=True))
            alpha = jnp.exp2(m_old - m_new)
            pt = jnp.exp2(st - m_new).astype(_BF16)
            pv = jnp.dot(vt_ref[kj * gran, 0], pt[:_KEY_GRANULE], preferred_element_type=_F32)
            for g in range(1, gran):
                pv = pv + jnp.dot(vt_ref[kj * gran + g, 0],
                                  pt[g * _KEY_GRANULE:(g + 1) * _KEY_GRANULE],
                                  preferred_element_type=_F32)
            acc_ref[s] = alpha * acc_ref[s] + pv
            m_ref[s] = m_new

    def far_step(kj, carry):
        step(kj, None)
        return carry

    lax.fori_loop(0, qi - 1, far_step, 0)

    @pl.when(qi >= 1)
    def _():
        step(qi - 1, bias_ref[0, 1])

    step(qi, bias_ref[0, 0])

    a0 = acc_ref[0]
    a1 = acc_ref[1]
    o0 = a0[:_V_DIM] / a0[_V_DIM:_V_DIM + 1]
    o1 = a1[:_V_DIM] / a1[_V_DIM:_V_DIM + 1]
    od = o0 - lam_ref[0:1, 0:1] * o1
    scale = lax.rsqrt(jnp.mean(od * od, axis=0, keepdims=True) + _EPS)
    o_ref[0] = (od * scale * gain_ref[...]).T.astype(_BF16)


def _attention(qt, k, vt, bias, lam, gain_t):
    b, l, w = k.shape
    t = _ATTN_TILE
    nq = l // t
    vrows = vt.shape[2]
    return pl.pallas_call(
        functools.partial(_attn_body, t=t),
        grid=(b, _HEADS, nq),
        in_specs=[pl.BlockSpec((_V_DIM, t), lambda bi, h, qi: (h, bi * nq + qi)),
                  pl.BlockSpec((1, l, _V_DIM), lambda bi, h, qi: (bi, 0, h)),
                  pl.BlockSpec((l // _KEY_GRANULE, 1, vrows, _KEY_GRANULE),
                               lambda bi, h, qi: (bi, h, 0, 0)),
                  pl.BlockSpec((1, 2, t, t), lambda bi, h, qi: (h, 0, 0, 0)),
                  pl.BlockSpec(lam.shape, lambda bi, h, qi: (0, 0)),
                  pl.BlockSpec(gain_t.shape, lambda bi, h, qi: (0, 0))],
        out_specs=pl.BlockSpec((1, t, _V_DIM), lambda bi, h, qi: (bi, qi, h)),
        out_shape=jax.ShapeDtypeStruct((b, l, w), _BF16),
        scratch_shapes=[pltpu.VMEM((2, _V_DIM, t), _BF16),
                        pltpu.VMEM((2, 1, t), _F32),
                        pltpu.VMEM((2, vrows, t), _F32)],
        compiler_params=pltpu.CompilerParams(
            dimension_semantics=("parallel", "parallel", "arbitrary"),
            vmem_limit_bytes=_VMEM_LIMIT),
        name="diff_attention",
    )(qt, k, vt, bias, lam, gain_t)


def _s5_prep_body(are_ref, aim_ref, ldt_ref, bre_ref, bim_ref,
                  lbre_ref, lbim_ref, bbre_ref, bbim_ref):
    a_re = are_ref[...]
    a_im = aim_ref[...]
    dt = jnp.exp(ldt_ref[...])
    decay = jnp.exp(a_re * dt)
    lb_re = decay * jnp.cos(a_im * dt)
    lb_im = decay * jnp.sin(a_im * dt)
    nr = lb_re - 1.0
    ni = lb_im
    den = a_re * a_re + a_im * a_im
    q_re = (nr * a_re + ni * a_im) / den
    q_im = (ni * a_re - nr * a_im) / den
    b_re = bre_ref[...]
    b_im = bim_ref[...]
    bbre_ref[...] = q_re * b_re - q_im * b_im
    bbim_ref[...] = q_re * b_im + q_im * b_re
    lbre_ref[...] = lb_re
    lbim_ref[...] = lb_im


def _s5_prep(a_re, a_im, log_dt, b_re, b_im):
    sd = jax.ShapeDtypeStruct(a_re.shape, _F32)
    return pl.pallas_call(_s5_prep_body, out_shape=[sd] * 4, name="s5_discretise")(
        a_re, a_im, log_dt, b_re, b_im)


def _s5_body(u_ref, perm_ref, permt_ref, wbre_ref, wbim_ref, lbre_ref, lbim_ref,
             wcre_ref, wcim_ref, d_ref, gw_ref, gb_ref, o_ref,
             sre_ref, sim_ref, cre_ref, cim_ref, *, steps, nb):
    rows = nb * steps
    width = u_ref.shape[-1]
    n_state = sre_ref.shape[-1]
    halves = wbre_ref.shape[0]
    hw_in = width // halves
    hw_st = n_state // halves

    @pl.when(pl.program_id(0) == 0)
    def _():
        cre_ref[...] = jnp.zeros(cre_ref.shape, _F32)
        cim_ref[...] = jnp.zeros(cim_ref.shape, _F32)

    u_bm = u_ref[...].reshape(rows, width)
    u_tm = jnp.dot(perm_ref[...], u_bm, preferred_element_type=_F32).astype(_BF16)
    for hf in range(halves):
        uh = u_tm[:, hf * hw_in:(hf + 1) * hw_in]
        sre_ref[:, hf * hw_st:(hf + 1) * hw_st] = jnp.dot(uh, wbre_ref[hf], preferred_element_type=_F32)
        sim_ref[:, hf * hw_st:(hf + 1) * hw_st] = jnp.dot(uh, wbim_ref[hf], preferred_element_type=_F32)

    for c in range(n_state // _SCAN_LANES):
        sl = slice(c * _SCAN_LANES, (c + 1) * _SCAN_LANES)
        a_r = jnp.broadcast_to(lbre_ref[:, sl], (nb, _SCAN_LANES))
        a_i = jnp.broadcast_to(lbim_ref[:, sl], (nb, _SCAN_LANES))

        def scan_step(ti, carry, sl=sl, a_r=a_r, a_i=a_i):
            x_r, x_i = carry
            r0 = pl.multiple_of(ti * nb, nb)
            n_r = a_r * x_r - a_i * x_i + sre_ref[pl.ds(r0, nb), sl]
            n_i = a_r * x_i + a_i * x_r + sim_ref[pl.ds(r0, nb), sl]
            sre_ref[pl.ds(r0, nb), sl] = n_r
            sim_ref[pl.ds(r0, nb), sl] = n_i
            return n_r, n_i

        x_r, x_i = lax.fori_loop(0, steps, scan_step, (cre_ref[:, sl], cim_ref[:, sl]), unroll=4)
        cre_ref[:, sl] = x_r
        cim_ref[:, sl] = x_i

    ys = []
    for hf in range(halves):
        x_r = sre_ref[:, hf * hw_st:(hf + 1) * hw_st].astype(_BF16)
        x_i = sim_ref[:, hf * hw_st:(hf + 1) * hw_st].astype(_BF16)
        ys.append(jnp.dot(x_r, wcre_ref[hf], preferred_element_type=_F32)
                  + jnp.dot(x_i, wcim_ref[hf], preferred_element_type=_F32))
    y = jnp.concatenate(ys, axis=-1) + d_ref[...] * u_tm.astype(_F32)
    g = jax.nn.gelu(y)
    o = g * jax.nn.sigmoid(jnp.dot(g.astype(_BF16), gw_ref[...], preferred_element_type=_F32)
                           + gb_ref[...])
    o_bm = jnp.dot(permt_ref[...], o.astype(_BF16), preferred_element_type=_F32)
    o_ref[...] = o_bm.astype(_BF16).reshape(o_ref.shape)


def _s5(u, perm, permt, wb_re, wb_im, lb_re, lb_im, wc_re, wc_im, d_vec, glu_w, glu_b):
    nb, l, width = u.shape
    steps = _SCAN_STEPS
    n_state = lb_re.shape[-1]
    full = lambda a: pl.BlockSpec(a.shape, lambda i: (0,) * a.ndim)
    io_spec = pl.BlockSpec((nb, steps, width), lambda i: (0, i, 0))
    consts = (perm, permt, wb_re, wb_im, lb_re, lb_im, wc_re, wc_im, d_vec, glu_w, glu_b)
    return pl.pallas_call(
        functools.partial(_s5_body, steps=steps, nb=nb),
        grid=(l // steps,),
        in_specs=[io_spec] + [full(a) for a in consts],
        out_specs=io_spec,
        out_shape=jax.ShapeDtypeStruct(u.shape, _BF16),
        scratch_shapes=[pltpu.VMEM((nb * steps, n_state), _F32),
                        pltpu.VMEM((nb * steps, n_state), _F32),
                        pltpu.VMEM((nb, n_state), _F32),
                        pltpu.VMEM((nb, n_state), _F32)],
        compiler_params=pltpu.CompilerParams(
            dimension_semantics=("arbitrary",), vmem_limit_bytes=_VMEM_LIMIT),
        name="s5_branch",
    )(u, *consts)


def _outproj_body(x_ref, ng_ref, wg_ref, mb_ref, oa_ref, os_ref, pa_ref, ps_ref, wo_ref,
                  out_ref, *, width):
    x = x_ref[...]
    d = x.shape[-1]
    h = (x * _rms_scale(x) * ng_ref[...]).astype(_BF16)
    zg = jnp.dot(h, wg_ref[...], preferred_element_type=_F32)
    o_a = (oa_ref[...].astype(_F32) * jax.nn.silu(zg[:, :width])).astype(_BF16)
    o_s = (os_ref[...].astype(_F32) * jax.nn.silu(zg[:, width:2 * width])).astype(_BF16)
    p_a = jnp.dot(o_a, pa_ref[...], preferred_element_type=_F32)
    p_s = jnp.dot(o_s, ps_ref[...], preferred_element_type=_F32)
    g = jax.nn.sigmoid(zg[:, 2 * width:] + mb_ref[...])
    merged = g[:, :d] * p_a + g[:, d:] * p_s
    out_ref[...] = x + jnp.dot(merged.astype(_BF16), wo_ref[...], preferred_element_type=_F32)


def _outproj(x2, norm_gain, w_gates, merge_b, o_a, o_s, proj_a, proj_s, w_out, width):
    n, d = x2.shape
    tm = _TOKEN_TILE
    full = lambda a: pl.BlockSpec(a.shape, lambda i: (0,) * a.ndim)
    row = lambda a: pl.BlockSpec((tm, a.shape[-1]), lambda i: (i, 0))
    return pl.pallas_call(
        functools.partial(_outproj_body, width=width),
        grid=(n // tm,),
        in_specs=[row(x2), full(norm_gain), full(w_gates), full(merge_b), row(o_a), row(o_s),
                  full(proj_a), full(proj_s), full(w_out)],
        out_specs=row(x2),
        out_shape=jax.ShapeDtypeStruct(x2.shape, x2.dtype),
        compiler_params=pltpu.CompilerParams(
            dimension_semantics=("parallel",), vmem_limit_bytes=_VMEM_LIMIT),
        name="outproj",
    )(x2, norm_gain, w_gates, merge_b, o_a, o_s, proj_a, proj_s, w_out)


def _block_diag_halves(blocks, halves):
    g, r, c = blocks.shape
    gh = g // halves
    eye = jnp.eye(gh, dtype=blocks.dtype)
    b = blocks.reshape(halves, gh, r, 1, c) * eye.reshape(1, gh, 1, gh, 1)
    return b.reshape(halves, gh * r, gh * c)


def _time_major_perm(nb, steps):
    r = np.arange(nb * steps)
    src = (r % nb) * steps + r // nb
    p = np.zeros((nb * steps, nb * steps), np.float32)
    p[r, src] = 1.0
    return p


def _layer(x, lam_init, norm_gain, w_in, merge_gate_b, q_norm_gain, k_norm_gain,
           lambda_q1, lambda_k1, lambda_q2, lambda_k2, diff_subln_gain, rel_bias_table,
           ssm_a_re, ssm_a_im, ssm_log_dt, ssm_b_re, ssm_b_im, ssm_c_re, ssm_c_im,
           ssm_d, ssm_glu_w, ssm_glu_b, proj_attn, proj_ssm, w_out):
    nb, l, d = x.shape
    n = nb * l
    aw = _HEADS * 2 * _SUB_DIM
    groups = ssm_a_re.shape[0]
    x2 = x.reshape(n, d)
    ng = norm_gain.reshape(1, d).astype(_F32)

    w_ku = jnp.concatenate([w_in[:, aw:2 * aw], w_in[:, 4 * aw:5 * aw]], axis=1).astype(_BF16)
    w_qvt = jnp.concatenate([w_in[:, :aw], w_in[:, 2 * aw:3 * aw]], axis=1).T.astype(_BF16)
    w_gates = jnp.concatenate([w_in[:, 3 * aw:4 * aw], w_in[:, 5 * aw:]], axis=1).astype(_BF16)
    gq = jnp.tile(q_norm_gain.astype(_F32), 2 * _HEADS) * (_SUB_DIM ** -0.5 * _LOG2E)
    gqt = jnp.broadcast_to(gq[:, None], (aw, _TOKEN_TILE))
    gk = jnp.tile(k_norm_gain.astype(_F32), 2 * _HEADS).reshape(1, aw)
    seg = np.arange(aw) // _SUB_DIM
    gsum = jnp.asarray((seg[:, None] == seg[None, :]).astype(np.float32), _BF16)
    qt, k, vt, u = _inproj(x2, ng, w_ku, w_qvt, gqt, gk, gsum, aw)

    lam4 = jnp.stack([lambda_q1, lambda_k1, lambda_q2, lambda_k2]).astype(_F32)
    bias, lam = _bias_prep(rel_bias_table.astype(_F32), lam4, lam_init, _ATTN_TILE)
    subln = diff_subln_gain.astype(_F32) * (1.0 - lam_init)
    subln_t = jnp.broadcast_to(subln[:, None], (_V_DIM, _ATTN_TILE))
    shp = (nb, l, aw)
    o_a = _attention(qt, k.reshape(shp), vt, bias, lam, subln_t)

    rep = lambda a: jnp.repeat(a.astype(_F32), _SSM_GROUP, axis=0)
    ldt = jnp.broadcast_to(ssm_log_dt.astype(_F32)[:, None], ssm_a_re.shape)
    bt = lambda a: a.astype(_F32).transpose(0, 2, 1).reshape(groups * _SSM_GROUP, _SSM_STATE)
    lb_re, lb_im, bb_re, bb_im = _s5_prep(rep(ssm_a_re), rep(ssm_a_im), rep(ldt),
                                          bt(ssm_b_re), bt(ssm_b_im))
    n_state = groups * _SSM_STATE
    flat = lambda a: a[::_SSM_GROUP].reshape(1, n_state)
    gshape = (groups, _SSM_GROUP, _SSM_STATE)
    wb_re = _block_diag_halves(bb_re.reshape(gshape), 2).astype(_BF16)
    wb_im = _block_diag_halves(bb_im.reshape(gshape), 2).astype(_BF16)
    wc_re = _block_diag_halves(ssm_c_re.astype(_F32).transpose(0, 2, 1), 2).astype(_BF16)
    wc_im = _block_diag_halves(-ssm_c_im.astype(_F32).transpose(0, 2, 1), 2).astype(_BF16)
    perm = _time_major_perm(nb, _SCAN_STEPS)
    o_s = _s5(u.reshape(shp), jnp.asarray(perm, _BF16), jnp.asarray(perm.T, _BF16),
              wb_re, wb_im, flat(lb_re), flat(lb_im), wc_re, wc_im,
              ssm_d.astype(_F32).reshape(1, aw), ssm_glu_w.astype(_BF16),
              ssm_glu_b.astype(_F32).reshape(1, aw))

    out = _outproj(x2, ng, w_gates, merge_gate_b.astype(_F32).reshape(1, 2 * d),
                   o_a.reshape(n, aw), o_s.reshape(n, aw),
                   proj_attn.astype(_BF16), proj_ssm.astype(_BF16), w_out.astype(_BF16), aw)
    return out.reshape(nb, l, d)


def kernel(x, norm_gain, w_in, merge_gate_b, q_norm_gain, k_norm_gain, lambda_q1, lambda_k1,
           lambda_q2, lambda_k2, diff_subln_gain, rel_bias_table, ssm_A_re, ssm_A_im, ssm_log_dt,
           ssm_B_re, ssm_B_im, ssm_C_re, ssm_C_im, ssm_D, ssm_glu_w, ssm_glu_b,
           proj_attn, proj_ssm, w_out):
    per_layer = (norm_gain, w_in, merge_gate_b, q_norm_gain, k_norm_gain, lambda_q1, lambda_k1,
                 lambda_q2, lambda_k2, diff_subln_gain)
    per_layer_tail = (ssm_A_re, ssm_A_im, ssm_log_dt, ssm_B_re, ssm_B_im, ssm_C_re, ssm_C_im,
                      ssm_D, ssm_glu_w, ssm_glu_b, proj_attn, proj_ssm, w_out)
    for layer in range(norm_gain.shape[0]):
        lam_init = 0.8 - 0.6 * math.exp(-0.3 * layer)
        x = _layer(x, lam_init, *(p[layer] for p in per_layer), rel_bias_table,
                   *(p[layer] for p in per_layer_tail))
    return x
```

```python
import functools
import math

import jax
import jax.numpy as jnp
import numpy as np
from jax import lax
from jax.experimental import pallas as pl
from jax.experimental.pallas import tpu as pltpu

_F32 = jnp.float32
_BF16 = jnp.bfloat16

_CHUNK = 64
_HEADS = 4
_SUB_DIM = 64
_V_DIM = 128
_SSM_GROUP = 16
_SSM_STATE = 64
_REL_BUCKETS = 32
_REL_MAX_DIST = 128
_EPS = 1e-6
_LOG2E = math.log2(math.e)
_SAFE_LOG2_RANGE = 100.0

_VMEM_LIMIT = 48 * 1024 * 1024
_TOKEN_TILE = 512
_ATTN_TILE = 512
_KEY_GRANULE = 256
_ONES_ROWS = 16
_SCAN_STEPS = 64
_SCAN_LANES = 512


def _rms_scale(x, eps=_EPS):
    return lax.rsqrt(jnp.mean(x * x, axis=-1, keepdims=True) + eps)


def _inproj_body(x_ref, ng_ref, wku_ref, wqvt_ref, gqt_ref, gk_ref, gsum_ref,
                 qt_ref, k_ref, vt_ref, u_ref, *, width):
    x = x_ref[...]
    tm = x.shape[0]
    h = (x * _rms_scale(x) * ng_ref[...]).astype(_BF16)
    z = jnp.dot(h, wku_ref[...], preferred_element_type=_F32)
    zt = lax.dot_general(wqvt_ref[...], h, (((1,), (1,)), ((), ())),
                         preferred_element_type=_F32)

    kk = z[:, :width]
    ss = jnp.dot((kk * kk).astype(_BF16), gsum_ref[...], preferred_element_type=_F32)
    k_ref[...] = (kk * lax.rsqrt(ss * (1.0 / _SUB_DIM) + _EPS) * gk_ref[...]).astype(_BF16)
    u_ref[...] = z[:, width:].astype(_BF16)

    qt = zt[:width].reshape(width // _SUB_DIM, _SUB_DIM, tm)
    ms = jnp.mean(qt * qt, axis=1, keepdims=True)
    qt_ref[...] = ((qt * lax.rsqrt(ms + _EPS)).reshape(width, tm) * gqt_ref[...]).astype(_BF16)

    vt = zt[width:].astype(_BF16)
    ones = jnp.ones((_ONES_ROWS, _KEY_GRANULE), _BF16)
    for j in range(tm // _KEY_GRANULE):
        for hd in range(_HEADS):
            vt_ref[j, hd, :_V_DIM, :] = vt[hd * _V_DIM:(hd + 1) * _V_DIM,
                                           j * _KEY_GRANULE:(j + 1) * _KEY_GRANULE]
            vt_ref[j, hd, _V_DIM:, :] = ones


def _inproj(x2, norm_gain, w_ku, w_qvt, gqt, gk, gsum, width):
    n, d = x2.shape
    tm = _TOKEN_TILE
    full = lambda a: pl.BlockSpec(a.shape, lambda i: (0,) * a.ndim)
    row_sd = jax.ShapeDtypeStruct((n, width), _BF16)
    row_spec = pl.BlockSpec((tm, width), lambda i: (i, 0))
    gran = tm // _KEY_GRANULE
    vrows = _V_DIM + _ONES_ROWS
    return pl.pallas_call(
        functools.partial(_inproj_body, width=width),
        grid=(n // tm,),
        in_specs=[pl.BlockSpec((tm, d), lambda i: (i, 0)), full(norm_gain), full(w_ku), full(w_qvt),
                  full(gqt), full(gk), full(gsum)],
        out_specs=[pl.BlockSpec((width, tm), lambda i: (0, i)), row_spec,
                   pl.BlockSpec((gran, _HEADS, vrows, _KEY_GRANULE), lambda i: (i, 0, 0, 0)), row_spec],
        out_shape=[jax.ShapeDtypeStruct((width, n), _BF16), row_sd,
                   jax.ShapeDtypeStruct((n // _KEY_GRANULE, _HEADS, vrows, _KEY_GRANULE), _BF16), row_sd],
        compiler_params=pltpu.CompilerParams(
            dimension_semantics=("parallel",), vmem_limit_bytes=_VMEM_LIMIT),
        name="inproj",
    )(x2, norm_gain, w_ku, w_qvt, gqt, gk, gsum)


def _t5_bucket_np(rel):
    nb = _REL_BUCKETS // 2
    max_exact = nb // 2
    side = np.where(rel > 0, nb, 0)
    n = np.abs(rel)
    nf = np.maximum(n, 1).astype(np.float32)
    large = max_exact + (np.log(nf / np.float32(max_exact)) / np.float32(math.log(_REL_MAX_DIST / max_exact))
                         * np.float32(nb - max_exact)).astype(np.int32)
    large = np.minimum(large, nb - 1)
    return side + np.where(n < max_exact, n, large)


def _bucket_tiles(t):
    i = np.arange(t)[None, :]
    j = np.arange(t)[:, None]
    diag = np.where((j // _CHUNK) <= (i // _CHUNK), _t5_bucket_np(j - i), -1)
    prev = _t5_bucket_np(j - i - t)
    return np.stack([diag, prev]).astype(np.int32)


def _bias_prep_body(tab_ref, tabv_ref, bucket_ref, lam4_ref, gains_ref,
                    bias_ref, lam_ref, flag_ref, *, lam_init):
    h = pl.program_id(0)
    far_bucket = _REL_BUCKETS // 2 - 1
    bkt = bucket_ref[...]
    far = tab_ref[far_bucket, h]
    val = jnp.full(bkt.shape, -jnp.inf, _F32)
    for b in range(_REL_BUCKETS):
        val = jnp.where(bkt == b, (tab_ref[b, h] - far) * _LOG2E, val)
    bias_ref[0] = val
    l4 = lam4_ref[...]
    s1 = jnp.sum(l4[0:1] * l4[1:2], axis=-1, keepdims=True)
    s2 = jnp.sum(l4[2:3] * l4[3:4], axis=-1, keepdims=True)
    lam_ref[...] = jnp.broadcast_to(jnp.exp(s1) - jnp.exp(s2) + lam_init, lam_ref.shape)
    gmax = jnp.max(jnp.abs(gains_ref[...]), axis=-1, keepdims=True)
    qk_bound = (_SUB_DIM * 1.02) * gmax[0:1] * gmax[1:2]
    tv = tabv_ref[...]
    dev = jnp.abs(tv - tv[far_bucket:far_bucket + 1]) * _LOG2E
    bias_bound = jnp.max(jnp.max(dev, axis=-1, keepdims=True), axis=0, keepdims=True)
    flag_ref[...] = (qk_bound + bias_bound <= _SAFE_LOG2_RANGE).astype(jnp.int32)


def _bias_prep(rel_table, lam4, gains, lam_init, t):
    buckets = jnp.asarray(_bucket_tiles(t))
    whole = lambda a: pl.BlockSpec(a.shape, lambda h: (0,) * a.ndim)
    return pl.pallas_call(
        functools.partial(_bias_prep_body, lam_init=lam_init),
        grid=(_HEADS,),
        in_specs=[pl.BlockSpec(memory_space=pltpu.SMEM), whole(rel_table), whole(buckets),
                  whole(lam4), whole(gains)],
        out_specs=[pl.BlockSpec((1, 2, t, t), lambda h: (h, 0, 0, 0)),
                   pl.BlockSpec((1, _V_DIM), lambda h: (0, 0)),
                   pl.BlockSpec((1, 1), lambda h: (0, 0))],
        out_shape=[jax.ShapeDtypeStruct((_HEADS, 2, t, t), _F32),
                   jax.ShapeDtypeStruct((1, _V_DIM), _F32),
                   jax.ShapeDtypeStruct((1, 1), jnp.int32)],
        compiler_params=pltpu.CompilerParams(dimension_semantics=("arbitrary",)),
        name="attn_bias_prep",
    )(rel_table, rel_table, buckets, lam4, gains)


def _attn_body(flag_ref, qt_ref, k_ref, vt_ref, bias_ref, lam_ref, gain_ref, o_ref,
               qw_ref, m_ref, acc_ref, *, t):
    qi = pl.program_id(2)
    gran = t // _KEY_GRANULE
    qt = qt_ref[...]
    row = lax.broadcasted_iota(jnp.int32, qt.shape, 0)
    zero = jnp.zeros_like(qt)
    qw_ref[:, :t] = jnp.where(row < _SUB_DIM, qt, zero)
    qw_ref[:, t:] = jnp.where(row >= _SUB_DIM, qt, zero)
    acc_ref[...] = jnp.zeros(acc_ref.shape, _F32)

    def apply_to_values(kj, pt):
        pv = jnp.dot(vt_ref[kj * gran, 0], pt[:_KEY_GRANULE], preferred_element_type=_F32)
        for g in range(1, gran):
            pv = pv + jnp.dot(vt_ref[kj * gran + g, 0],
                              pt[g * _KEY_GRANULE:(g + 1) * _KEY_GRANULE],
                              preferred_element_type=_F32)
        return pv

    def key_tile(kj):
        return k_ref[0, pl.ds(pl.multiple_of(kj * t, t), t), :]

    def bounded_step(kj, bias):
        st = jnp.dot(key_tile(kj), qw_ref[...], preferred_element_type=_F32)
        halves = [st[:, :t], st[:, t:]]
        if bias is not None:
            halves = [hv + bias for hv in halves]
        pt = jnp.concatenate([jnp.exp2(hv).astype(_BF16) for hv in halves], axis=1)
        acc_ref[...] += apply_to_values(kj, pt)

    def online_step(kj, bias):
        kt = key_tile(kj)
        for s in range(2):
            lanes = slice(s * t, (s + 1) * t)
            st = jnp.dot(kt, qw_ref[:, lanes], preferred_element_type=_F32)
            if bias is not None:
                st = st + bias
            m_old = m_ref[s]
            m_new = jnp.maximum(m_old, jnp.max(st, axis=0, keepdims=True))
            alpha = jnp.exp2(m_old - m_new)
            pt = jnp.exp2(st - m_new).astype(_BF16)
            acc_ref[:, lanes] = alpha * acc_ref[:, lanes] + apply_to_values(kj, pt)
            m_ref[s] = m_new

    def sweep(step):
        def far_step(kj, carry):
            step(kj, None)
            return carry

        lax.fori_loop(0, qi - 1, far_step, 0)

        @pl.when(qi >= 1)
        def _():
            step(qi - 1, bias_ref[0, 1])

        step(qi, bias_ref[0, 0])

    bounded = flag_ref[0, 0] > 0

    @pl.when(bounded)
    def _():
        sweep(bounded_step)

    @pl.when(jnp.logical_not(bounded))
    def _():
        m_ref[...] = jnp.full(m_ref.shape, -jnp.inf, _F32)
        sweep(online_step)

    acc = acc_ref[...]
    o = acc[:_V_DIM] / acc[_V_DIM:_V_DIM + 1]
    od = o[:, :t] - lam_ref[0:1, 0:1] * o[:, t:]
    scale = lax.rsqrt(jnp.mean(od * od, axis=0, keepdims=True) + _EPS)
    o_ref[0] = (od * scale * gain_ref[...]).T.astype(_BF16)


def _attention(flag, qt, k, vt, bias, lam, gain_t):
    b, l, w = k.shape
    t = _ATTN_TILE
    nq = l // t
    vrows = vt.shape[2]
    return pl.pallas_call(
        functools.partial(_attn_body, t=t),
        grid=(b, _HEADS, nq),
        in_specs=[pl.BlockSpec(memory_space=pltpu.SMEM),
                  pl.BlockSpec((_V_DIM, t), lambda bi, h, qi: (h, bi * nq + qi)),
                  pl.BlockSpec((1, l, _V_DIM), lambda bi, h, qi: (bi, 0, h)),
                  pl.BlockSpec((l // _KEY_GRANULE, 1, vrows, _KEY_GRANULE),
                               lambda bi, h, qi: (bi, h, 0, 0)),
                  pl.BlockSpec((1, 2, t, t), lambda bi, h, qi: (h, 0, 0, 0)),
                  pl.BlockSpec(lam.shape, lambda bi, h, qi: (0, 0)),
                  pl.BlockSpec(gain_t.shape, lambda bi, h, qi: (0, 0))],
        out_specs=pl.BlockSpec((1, t, _V_DIM), lambda bi, h, qi: (bi, qi, h)),
        out_shape=jax.ShapeDtypeStruct((b, l, w), _BF16),
        scratch_shapes=[pltpu.VMEM((_V_DIM, 2 * t), _BF16),
                        pltpu.VMEM((2, 1, t), _F32),
                        pltpu.VMEM((vrows, 2 * t), _F32)],
        compiler_params=pltpu.CompilerParams(
            dimension_semantics=("parallel", "parallel", "arbitrary"),
            vmem_limit_bytes=_VMEM_LIMIT),
        name="diff_attention",
    )(flag, qt, k, vt, bias, lam, gain_t)


def _s5_prep_body(are_ref, aim_ref, ldt_ref, bre_ref, bim_ref,
                  lbre_ref, lbim_ref, bbre_ref, bbim_ref):
    a_re = are_ref[...]
    a_im = aim_ref[...]
    dt = jnp.exp(ldt_ref[...])
    decay = jnp.exp(a_re * dt)
    lb_re = decay * jnp.cos(a_im * dt)
    lb_im = decay * jnp.sin(a_im * dt)
    nr = lb_re - 1.0
    ni = lb_im
    den = a_re * a_re + a_im * a_im
    q_re = (nr * a_re + ni * a_im) / den
    q_im = (ni * a_re - nr * a_im) / den
    b_re = bre_ref[...]
    b_im = bim_ref[...]
    bbre_ref[...] = q_re * b_re - q_im * b_im
    bbim_ref[...] = q_re * b_im + q_im * b_re
    lbre_ref[...] = lb_re
    lbim_ref[...] = lb_im


def _s5_prep(a_re, a_im, log_dt, b_re, b_im):
    sd = jax.ShapeDtypeStruct(a_re.shape, _F32)
    return pl.pallas_call(_s5_prep_body, out_shape=[sd] * 4, name="s5_discretise")(
        a_re, a_im, log_dt, b_re, b_im)


def _s5_body(u_ref, perm_ref, permt_ref, wbre_ref, wbim_ref, lbre_ref, lbim_ref,
             wcre_ref, wcim_ref, d_ref, gw_ref, gb_ref, o_ref,
             sre_ref, sim_ref, cre_ref, cim_ref, *, steps, nb):
    rows = nb * steps
    width = u_ref.shape[-1]
    n_state = sre_ref.shape[-1]
    halves = wbre_ref.shape[0]
    hw_in = width // halves
    hw_st = n_state // halves

    @pl.when(pl.program_id(0) == 0)
    def _():
        cre_ref[...] = jnp.zeros(cre_ref.shape, _F32)
        cim_ref[...] = jnp.zeros(cim_ref.shape, _F32)

    u_bm = u_ref[...].reshape(rows, width)
    u_tm = jnp.dot(perm_ref[...], u_bm, preferred_element_type=_F32).astype(_BF16)
    for hf in range(halves):
        uh = u_tm[:, hf * hw_in:(hf + 1) * hw_in]
        sre_ref[:, hf * hw_st:(hf + 1) * hw_st] = jnp.dot(uh, wbre_ref[hf], preferred_element_type=_F32)
        sim_ref[:, hf * hw_st:(hf + 1) * hw_st] = jnp.dot(uh, wbim_ref[hf], preferred_element_type=_F32)

    for c in range(n_state // _SCAN_LANES):
        sl = slice(c * _SCAN_LANES, (c + 1) * _SCAN_LANES)
        a_r = jnp.broadcast_to(lbre_ref[:, sl], (nb, _SCAN_LANES))
        a_i = jnp.broadcast_to(lbim_ref[:, sl], (nb, _SCAN_LANES))

        def scan_step(ti, carry, sl=sl, a_r=a_r, a_i=a_i):
            x_r, x_i = carry
            r0 = pl.multiple_of(ti * nb, nb)
            n_r = a_r * x_r - a_i * x_i + sre_ref[pl.ds(r0, nb), sl]
            n_i = a_r * x_i + a_i * x_r + sim_ref[pl.ds(r0, nb), sl]
            sre_ref[pl.ds(r0, nb), sl] = n_r
            sim_ref[pl.ds(r0, nb), sl] = n_i
            return n_r, n_i

        x_r, x_i = lax.fori_loop(0, steps, scan_step, (cre_ref[:, sl], cim_ref[:, sl]), unroll=4)
        cre_ref[:, sl] = x_r
        cim_ref[:, sl] = x_i

    ys = []
    for hf in range(halves):
        x_r = sre_ref[:, hf * hw_st:(hf + 1) * hw_st].astype(_BF16)
        x_i = sim_ref[:, hf * hw_st:(hf + 1) * hw_st].astype(_BF16)
        ys.append(jnp.dot(x_r, wcre_ref[hf], preferred_element_type=_F32)
                  + jnp.dot(x_i, wcim_ref[hf], preferred_element_type=_F32))
    y = jnp.concatenate(ys, axis=-1) + d_ref[...] * u_tm.astype(_F32)
    g = jax.nn.gelu(y)
    o = g * jax.nn.sigmoid(jnp.dot(g.astype(_BF16), gw_ref[...], preferred_element_type=_F32)
                           + gb_ref[...])
    o_bm = jnp.dot(permt_ref[...], o.astype(_BF16), preferred_element_type=_F32)
    o_ref[...] = o_bm.astype(_BF16).reshape(o_ref.shape)


def _s5(u, perm, permt, wb_re, wb_im, lb_re, lb_im, wc_re, wc_im, d_vec, glu_w, glu_b):
    nb, l, width = u.shape
    steps = _SCAN_STEPS
    n_state = lb_re.shape[-1]
    full = lambda a: pl.BlockSpec(a.shape, lambda i: (0,) * a.ndim)
    io_spec = pl.BlockSpec((nb, steps, width), lambda i: (0, i, 0))
    consts = (perm, permt, wb_re, wb_im, lb_re, lb_im, wc_re, wc_im, d_vec, glu_w, glu_b)
    return pl.pallas_call(
        functools.partial(_s5_body, steps=steps, nb=nb),
        grid=(l // steps,),
        in_specs=[io_spec] + [full(a) for a in consts],
        out_specs=io_spec,
        out_shape=jax.ShapeDtypeStruct(u.shape, _BF16),
        scratch_shapes=[pltpu.VMEM((nb * steps, n_state), _F32),
                        pltpu.VMEM((nb * steps, n_state), _F32),
                        pltpu.VMEM((nb, n_state), _F32),
                        pltpu.VMEM((nb, n_state), _F32)],
        compiler_params=pltpu.CompilerParams(
            dimension_semantics=("arbitrary",), vmem_limit_bytes=_VMEM_LIMIT),
        name="s5_branch",
    )(u, *consts)


def _outproj_body(x_ref, ng_ref, wg_ref, mb_ref, oa_ref, os_ref, pa_ref, ps_ref, wo_ref,
                  out_ref, *, width):
    x = x_ref[...]
    d = x.shape[-1]
    h = (x * _rms_scale(x) * ng_ref[...]).astype(_BF16)
    zg = jnp.dot(h, wg_ref[...], preferred_element_type=_F32)
    o_a = (oa_ref[...].astype(_F32) * jax.nn.silu(zg[:, :width])).astype(_BF16)
    o_s = (os_ref[...].astype(_F32) * jax.nn.silu(zg[:, width:2 * width])).astype(_BF16)
    p_a = jnp.dot(o_a, pa_ref[...], preferred_element_type=_F32)
    p_s = jnp.dot(o_s, ps_ref[...], preferred_element_type=_F32)
    g = jax.nn.sigmoid(zg[:, 2 * width:] + mb_ref[...])
    merged = g[:, :d] * p_a + g[:, d:] * p_s
    out_ref[...] = x + jnp.dot(merged.astype(_BF16), wo_ref[...], preferred_element_type=_F32)


def _outproj(x2, norm_gain, w_gates, merge_b, o_a, o_s, proj_a, proj_s, w_out, width):
    n, d = x2.shape
    tm = _TOKEN_TILE
    full = lambda a: pl.BlockSpec(a.shape, lambda i: (0,) * a.ndim)
    row = lambda a: pl.BlockSpec((tm, a.shape[-1]), lambda i: (i, 0))
    return pl.pallas_call(
        functools.partial(_outproj_body, width=width),
        grid=(n // tm,),
        in_specs=[row(x2), full(norm_gain), full(w_gates), full(merge_b), row(o_a), row(o_s),
                  full(proj_a), full(proj_s), full(w_out)],
        out_specs=row(x2),
        out_shape=jax.ShapeDtypeStruct(x2.shape, x2.dtype),
        compiler_params=pltpu.CompilerParams(
            dimension_semantics=("parallel",), vmem_limit_bytes=_VMEM_LIMIT),
        name="outproj",
    )(x2, norm_gain, w_gates, merge_b, o_a, o_s, proj_a, proj_s, w_out)


def _block_diag_halves(blocks, halves):
    g, r, c = blocks.shape
    gh = g // halves
    eye = jnp.eye(gh, dtype=blocks.dtype)
    b = blocks.reshape(halves, gh, r, 1, c) * eye.reshape(1, gh, 1, gh, 1)
    return b.reshape(halves, gh * r, gh * c)


def _time_major_perm(nb, steps):
    r = np.arange(nb * steps)
    src = (r % nb) * steps + r // nb
    p = np.zeros((nb * steps, nb * steps), np.float32)
    p[r, src] = 1.0
    return p


def _layer(x, lam_init, norm_gain, w_in, merge_gate_b, q_norm_gain, k_norm_gain,
           lambda_q1, lambda_k1, lambda_q2, lambda_k2, diff_subln_gain, rel_bias_table,
           ssm_a_re, ssm_a_im, ssm_log_dt, ssm_b_re, ssm_b_im, ssm_c_re, ssm_c_im,
           ssm_d, ssm_glu_w, ssm_glu_b, proj_attn, proj_ssm, w_out):
    nb, l, d = x.shape
    n = nb * l
    aw = _HEADS * 2 * _SUB_DIM
    groups = ssm_a_re.shape[0]
    x2 = x.reshape(n, d)
    ng = norm_gain.reshape(1, d).astype(_F32)

    w_ku = jnp.concatenate([w_in[:, aw:2 * aw], w_in[:, 4 * aw:5 * aw]], axis=1).astype(_BF16)
    w_qvt = jnp.concatenate([w_in[:, :aw], w_in[:, 2 * aw:3 * aw]], axis=1).T.astype(_BF16)
    w_gates = jnp.concatenate([w_in[:, 3 * aw:4 * aw], w_in[:, 5 * aw:]], axis=1).astype(_BF16)
    gq = jnp.tile(q_norm_gain.astype(_F32), 2 * _HEADS) * (_SUB_DIM ** -0.5 * _LOG2E)
    gqt = jnp.broadcast_to(gq[:, None], (aw, _TOKEN_TILE))
    gk = jnp.tile(k_norm_gain.astype(_F32), 2 * _HEADS).reshape(1, aw)
    seg = np.arange(aw) // _SUB_DIM
    gsum = jnp.asarray((seg[:, None] == seg[None, :]).astype(np.float32), _BF16)
    qt, k, vt, u = _inproj(x2, ng, w_ku, w_qvt, gqt, gk, gsum, aw)

    lam4 = jnp.stack([lambda_q1, lambda_k1, lambda_q2, lambda_k2]).astype(_F32)
    qk_gains = jnp.stack([gq[:_SUB_DIM], gk[0, :_SUB_DIM]])
    bias, lam, flag = _bias_prep(rel_bias_table.astype(_F32), lam4, qk_gains, lam_init, _ATTN_TILE)
    subln = diff_subln_gain.astype(_F32) * (1.0 - lam_init)
    subln_t = jnp.broadcast_to(subln[:, None], (_V_DIM, _ATTN_TILE))
    shp = (nb, l, aw)
    o_a = _attention(flag, qt, k.reshape(shp), vt, bias, lam, subln_t)

    rep = lambda a: jnp.repeat(a.astype(_F32), _SSM_GROUP, axis=0)
    ldt = jnp.broadcast_to(ssm_log_dt.astype(_F32)[:, None], ssm_a_re.shape)
    bt = lambda a: a.astype(_F32).transpose(0, 2, 1).reshape(groups * _SSM_GROUP, _SSM_STATE)
    lb_re, lb_im, bb_re, bb_im = _s5_prep(rep(ssm_a_re), rep(ssm_a_im), rep(ldt),
                                          bt(ssm_b_re), bt(ssm_b_im))
    n_state = groups * _SSM_STATE
    flat = lambda a: a[::_SSM_GROUP].reshape(1, n_state)
    gshape = (groups, _SSM_GROUP, _SSM_STATE)
    wb_re = _block_diag_halves(bb_re.reshape(gshape), 2).astype(_BF16)
    wb_im = _block_diag_halves(bb_im.reshape(gshape), 2).astype(_BF16)
    wc_re = _block_diag_halves(ssm_c_re.astype(_F32).transpose(0, 2, 1), 2).astype(_BF16)
    wc_im = _block_diag_halves(-ssm_c_im.astype(_F32).transpose(0, 2, 1), 2).astype(_BF16)
    perm = _time_major_perm(nb, _SCAN_STEPS)
    o_s = _s5(u.reshape(shp), jnp.asarray(perm, _BF16), jnp.asarray(perm.T, _BF16),
              wb_re, wb_im, flat(lb_re), flat(lb_im), wc_re, wc_im,
              ssm_d.astype(_F32).reshape(1, aw), ssm_glu_w.astype(_BF16),
              ssm_glu_b.astype(_F32).reshape(1, aw))

    out = _outproj(x2, ng, w_gates, merge_gate_b.astype(_F32).reshape(1, 2 * d),
                   o_a.reshape(n, aw), o_s.reshape(n, aw),
                   proj_attn.astype(_BF16), proj_ssm.astype(_BF16), w_out.astype(_BF16), aw)
    return out.reshape(nb, l, d)


def kernel(x, norm_gain, w_in, merge_gate_b, q_norm_gain, k_norm_gain, lambda_q1, lambda_k1,
           lambda_q2, lambda_k2, diff_subln_gain, rel_bias_table, ssm_A_re, ssm_A_im, ssm_log_dt,
           ssm_B_re, ssm_B_im, ssm_C_re, ssm_C_im, ssm_D, ssm_glu_w, ssm_glu_b,
           proj_attn, proj_ssm, w_out):
    per_layer = (norm_gain, w_in, merge_gate_b, q_norm_gain, k_norm_gain, lambda_q1, lambda_k1,
                 lambda_q2, lambda_k2, diff_subln_gain)
    per_layer_tail = (ssm_A_re, ssm_A_im, ssm_log_dt, ssm_B_re, ssm_B_im, ssm_C_re, ssm_C_im,
                      ssm_D, ssm_glu_w, ssm_glu_b, proj_attn, proj_ssm, w_out)
    for layer in range(norm_gain.shape[0]):
        lam_init = 0.8 - 0.6 * math.exp(-0.3 * layer)
        x = _layer(x, lam_init, *(p[layer] for p in per_layer), rel_bias_table,
                   *(p[layer] for p in per_layer_tail))
    return x
```

```python
import functools
import math

import jax
import jax.numpy as jnp
import numpy as np
from jax import lax
from jax.experimental import pallas as pl
from jax.experimental.pallas import tpu as pltpu

_F32 = jnp.float32
_BF16 = jnp.bfloat16

_CHUNK = 64
_HEADS = 4
_SUB_DIM = 64
_V_DIM = 128
_SSM_GROUP = 16
_SSM_STATE = 64
_REL_BUCKETS = 32
_REL_MAX_DIST = 128
_EPS = 1e-6
_LOG2E = math.log2(math.e)
_SAFE_LOG2_RANGE = 100.0

_VMEM_LIMIT = 48 * 1024 * 1024
_TOKEN_TILE = 512
_ATTN_TILE = 512
_KEY_GRANULE = 256
_ONES_ROWS = 16
_SCAN_STEPS = 64
_SCAN_LANES = 512


def _rms_scale(x, eps=_EPS):
    return lax.rsqrt(jnp.mean(x * x, axis=-1, keepdims=True) + eps)


def _inproj_body(x_ref, ng_ref, wku_ref, wqvt_ref, gqt_ref, gk_ref, gsum_ref,
                 qt_ref, k_ref, vt_ref, u_ref, *, width):
    x = x_ref[...]
    tm = x.shape[0]
    h = (x * _rms_scale(x) * ng_ref[...]).astype(_BF16)
    z = jnp.dot(h, wku_ref[...], preferred_element_type=_F32)
    zt = lax.dot_general(wqvt_ref[...], h, (((1,), (1,)), ((), ())),
                         preferred_element_type=_F32)

    kk = z[:, :width]
    ss = jnp.dot((kk * kk).astype(_BF16), gsum_ref[...], preferred_element_type=_F32)
    k_ref[...] = (kk * lax.rsqrt(ss * (1.0 / _SUB_DIM) + _EPS) * gk_ref[...]).astype(_BF16)
    u_ref[...] = z[:, width:].astype(_BF16)

    qt = zt[:width].reshape(width // _SUB_DIM, _SUB_DIM, tm)
    ms = jnp.mean(qt * qt, axis=1, keepdims=True)
    qt_ref[...] = ((qt * lax.rsqrt(ms + _EPS)).reshape(width, tm) * gqt_ref[...]).astype(_BF16)

    vt = zt[width:].astype(_BF16)
    ones = jnp.ones((_ONES_ROWS, _KEY_GRANULE), _BF16)
    for j in range(tm // _KEY_GRANULE):
        for hd in range(_HEADS):
            vt_ref[j, hd, :_V_DIM, :] = vt[hd * _V_DIM:(hd + 1) * _V_DIM,
                                           j * _KEY_GRANULE:(j + 1) * _KEY_GRANULE]
            vt_ref[j, hd, _V_DIM:, :] = ones


def _inproj(x2, norm_gain, w_ku, w_qvt, gqt, gk, gsum, width):
    n, d = x2.shape
    tm = _TOKEN_TILE
    full = lambda a: pl.BlockSpec(a.shape, lambda i: (0,) * a.ndim)
    row_sd = jax.ShapeDtypeStruct((n, width), _BF16)
    row_spec = pl.BlockSpec((tm, width), lambda i: (i, 0))
    gran = tm // _KEY_GRANULE
    vrows = _V_DIM + _ONES_ROWS
    return pl.pallas_call(
        functools.partial(_inproj_body, width=width),
        grid=(n // tm,),
        in_specs=[pl.BlockSpec((tm, d), lambda i: (i, 0)), full(norm_gain), full(w_ku), full(w_qvt),
                  full(gqt), full(gk), full(gsum)],
        out_specs=[pl.BlockSpec((width, tm), lambda i: (0, i)), row_spec,
                   pl.BlockSpec((gran, _HEADS, vrows, _KEY_GRANULE), lambda i: (i, 0, 0, 0)), row_spec],
        out_shape=[jax.ShapeDtypeStruct((width, n), _BF16), row_sd,
                   jax.ShapeDtypeStruct((n // _KEY_GRANULE, _HEADS, vrows, _KEY_GRANULE), _BF16), row_sd],
        compiler_params=pltpu.CompilerParams(
            dimension_semantics=("parallel",), vmem_limit_bytes=_VMEM_LIMIT),
        name="inproj",
    )(x2, norm_gain, w_ku, w_qvt, gqt, gk, gsum)


def _t5_bucket_np(rel):
    nb = _REL_BUCKETS // 2
    max_exact = nb // 2
    side = np.where(rel > 0, nb, 0)
    n = np.abs(rel)
    nf = np.maximum(n, 1).astype(np.float32)
    large = max_exact + (np.log(nf / np.float32(max_exact)) / np.float32(math.log(_REL_MAX_DIST / max_exact))
                         * np.float32(nb - max_exact)).astype(np.int32)
    large = np.minimum(large, nb - 1)
    return side + np.where(n < max_exact, n, large)


def _bucket_tiles(t):
    i = np.arange(t)[None, :]
    j = np.arange(t)[:, None]
    far = _t5_bucket_np(j - i - 2 * t)
    prev = _t5_bucket_np(j - i - t)
    diag = np.where((j // _CHUNK) <= (i // _CHUNK), _t5_bucket_np(j - i), -1)
    return np.stack([far, prev, diag]).astype(np.int32)


def _bias_prep_body(tab_ref, tabv_ref, bucket_ref, lam4_ref, gains_ref,
                    bias_ref, lam_ref, flag_ref, *, lam_init):
    h = pl.program_id(0)
    far_bucket = _REL_BUCKETS // 2 - 1
    bkt = bucket_ref[...]
    far = tab_ref[far_bucket, h]
    val = jnp.full(bkt.shape, -jnp.inf, _F32)
    for b in range(_REL_BUCKETS):
        val = jnp.where(bkt == b, (tab_ref[b, h] - far) * _LOG2E, val)
    bias_ref[0] = val
    l4 = lam4_ref[...]
    s1 = jnp.sum(l4[0:1] * l4[1:2], axis=-1, keepdims=True)
    s2 = jnp.sum(l4[2:3] * l4[3:4], axis=-1, keepdims=True)
    lam_ref[...] = jnp.broadcast_to(jnp.exp(s1) - jnp.exp(s2) + lam_init, lam_ref.shape)
    gmax = jnp.max(jnp.abs(gains_ref[...]), axis=-1, keepdims=True)
    qk_bound = (_SUB_DIM * 1.02) * gmax[0:1] * gmax[1:2]
    tv = tabv_ref[...]
    dev = jnp.abs(tv - tv[far_bucket:far_bucket + 1]) * _LOG2E
    bias_bound = jnp.max(jnp.max(dev, axis=-1, keepdims=True), axis=0, keepdims=True)
    flag_ref[...] = (qk_bound + bias_bound <= _SAFE_LOG2_RANGE).astype(jnp.int32)


def _bias_prep(rel_table, lam4, gains, lam_init, t):
    buckets = jnp.asarray(_bucket_tiles(t))
    whole = lambda a: pl.BlockSpec(a.shape, lambda h: (0,) * a.ndim)
    return pl.pallas_call(
        functools.partial(_bias_prep_body, lam_init=lam_init),
        grid=(_HEADS,),
        in_specs=[pl.BlockSpec(memory_space=pltpu.SMEM), whole(rel_table), whole(buckets),
                  whole(lam4), whole(gains)],
        out_specs=[pl.BlockSpec((1,) + buckets.shape, lambda h: (h, 0, 0, 0)),
                   pl.BlockSpec((1, _V_DIM), lambda h: (0, 0)),
                   pl.BlockSpec((1, 1), lambda h: (0, 0))],
        out_shape=[jax.ShapeDtypeStruct((_HEADS,) + buckets.shape, _F32),
                   jax.ShapeDtypeStruct((1, _V_DIM), _F32),
                   jax.ShapeDtypeStruct((1, 1), jnp.int32)],
        compiler_params=pltpu.CompilerParams(dimension_semantics=("arbitrary",)),
        name="attn_bias_prep",
    )(rel_table, rel_table, buckets, lam4, gains)


def _attn_body(flag_ref, qta_ref, qtb_ref, k_ref, vt_ref, bias_ref, lam_ref, gain_ref, o_ref,
               qw_ref, m_ref, acc_ref, *, t, nq):
    p = pl.program_id(2)
    gran = t // _KEY_GRANULE
    for side, qt_ref in enumerate((qta_ref, qtb_ref)):
        qt = qt_ref[...]
        row = lax.broadcasted_iota(jnp.int32, qt.shape, 0)
        zero = jnp.zeros_like(qt)
        qw_ref[side, :, :t] = jnp.where(row < _SUB_DIM, qt, zero)
        qw_ref[side, :, t:] = jnp.where(row >= _SUB_DIM, qt, zero)
    acc_ref[...] = jnp.zeros(acc_ref.shape, _F32)

    def schedule(step_index):
        mirrored = step_index > p
        side = mirrored.astype(jnp.int32)
        kj = jnp.where(mirrored, step_index - p - 1, step_index)
        q_tile = jnp.where(mirrored, nq - 1 - p, p)
        bias_index = jnp.where(kj == q_tile, 2, jnp.where(kj == q_tile - 1, 1, 0))
        return side, kj, bias_index

    def apply_to_values(kj, pt):
        pv = jnp.dot(vt_ref[kj * gran, 0], pt[:_KEY_GRANULE], preferred_element_type=_F32)
        for g in range(1, gran):
            pv = pv + jnp.dot(vt_ref[kj * gran + g, 0],
                              pt[g * _KEY_GRANULE:(g + 1) * _KEY_GRANULE],
                              preferred_element_type=_F32)
        return pv

    def key_tile(kj):
        return k_ref[0, pl.ds(pl.multiple_of(kj * t, t), t), :]

    def bounded_step(step_index):
        side, kj, bias_index = schedule(step_index)
        st = jnp.dot(key_tile(kj), qw_ref[side], preferred_element_type=_F32)
        bias = bias_ref[0, bias_index]
        pt = jnp.concatenate([jnp.exp2(st[:, :t] + bias).astype(_BF16),
                              jnp.exp2(st[:, t:] + bias).astype(_BF16)], axis=1)
        acc_ref[side] += apply_to_values(kj, pt)

    def online_step(step_index):
        side, kj, bias_index = schedule(step_index)
        kt = key_tile(kj)
        bias = bias_ref[0, bias_index]
        for s in range(2):
            lanes = slice(s * t, (s + 1) * t)
            st = jnp.dot(kt, qw_ref[side, :, lanes], preferred_element_type=_F32) + bias
            m_old = m_ref[side, s]
            m_new = jnp.maximum(m_old, jnp.max(st, axis=0, keepdims=True))
            alpha = jnp.exp2(m_old - m_new)
            pt = jnp.exp2(st - m_new).astype(_BF16)
            acc_ref[side, :, lanes] = alpha * acc_ref[side, :, lanes] + apply_to_values(kj, pt)
            m_ref[side, s] = m_new

    bounded = flag_ref[0, 0] > 0

    @pl.when(bounded)
    def _():
        for step_index in range(nq + 1):
            bounded_step(step_index)

    @pl.when(jnp.logical_not(bounded))
    def _():
        m_ref[...] = jnp.full(m_ref.shape, -jnp.inf, _F32)
        for step_index in range(nq + 1):
            online_step(step_index)

    for side in range(2):
        acc = acc_ref[side]
        o = acc[:_V_DIM] / acc[_V_DIM:_V_DIM + 1]
        od = o[:, :t] - lam_ref[0:1, 0:1] * o[:, t:]
        scale = lax.rsqrt(jnp.mean(od * od, axis=0, keepdims=True) + _EPS)
        o_ref[side, 0, 0] = (od * scale * gain_ref[...]).T.astype(_BF16)


def _attention(flag, qt, k, vt, bias, lam, gain_t):
    b, l, w = k.shape
    t = _ATTN_TILE
    nq = l // t
    vrows = vt.shape[2]
    return pl.pallas_call(
        functools.partial(_attn_body, t=t, nq=nq),
        grid=(b, _HEADS, nq // 2),
        in_specs=[pl.BlockSpec(memory_space=pltpu.SMEM),
                  pl.BlockSpec((_V_DIM, t), lambda bi, h, p: (h, bi * nq + p)),
                  pl.BlockSpec((_V_DIM, t), lambda bi, h, p: (h, bi * nq + nq - 1 - p)),
                  pl.BlockSpec((1, l, _V_DIM), lambda bi, h, p: (bi, 0, h)),
                  pl.BlockSpec((l // _KEY_GRANULE, 1, vrows, _KEY_GRANULE),
                               lambda bi, h, p: (bi, h, 0, 0)),
                  pl.BlockSpec((1, 3, t, t), lambda bi, h, p: (h, 0, 0, 0)),
                  pl.BlockSpec(lam.shape, lambda bi, h, p: (0, 0)),
                  pl.BlockSpec(gain_t.shape, lambda bi, h, p: (0, 0))],
        out_specs=pl.BlockSpec((2, 1, 1, t, _V_DIM), lambda bi, h, p: (0, bi, p, 0, h)),
        out_shape=jax.ShapeDtypeStruct((2, b, nq // 2, t, w), _BF16),
        scratch_shapes=[pltpu.VMEM((2, _V_DIM, 2 * t), _BF16),
                        pltpu.VMEM((2, 2, 1, t), _F32),
                        pltpu.VMEM((2, vrows, 2 * t), _F32)],
        compiler_params=pltpu.CompilerParams(
            dimension_semantics=("parallel", "parallel", "arbitrary"),
            vmem_limit_bytes=_VMEM_LIMIT),
        name="diff_attention",
    )(flag, qt, qt, k, vt, bias, lam, gain_t)


def _s5_prep_body(are_ref, aim_ref, ldt_ref, bre_ref, bim_ref,
                  lbre_ref, lbim_ref, bbre_ref, bbim_ref):
    a_re = are_ref[...]
    a_im = aim_ref[...]
    dt = jnp.exp(ldt_ref[...])
    decay = jnp.exp(a_re * dt)
    lb_re = decay * jnp.cos(a_im * dt)
    lb_im = decay * jnp.sin(a_im * dt)
    nr = lb_re - 1.0
    ni = lb_im
    den = a_re * a_re + a_im * a_im
    q_re = (nr * a_re + ni * a_im) / den
    q_im = (ni * a_re - nr * a_im) / den
    b_re = bre_ref[...]
    b_im = bim_ref[...]
    bbre_ref[...] = q_re * b_re - q_im * b_im
    bbim_ref[...] = q_re * b_im + q_im * b_re
    lbre_ref[...] = lb_re
    lbim_ref[...] = lb_im


def _s5_prep(a_re, a_im, log_dt, b_re, b_im):
    sd = jax.ShapeDtypeStruct(a_re.shape, _F32)
    return pl.pallas_call(_s5_prep_body, out_shape=[sd] * 4, name="s5_discretise")(
        a_re, a_im, log_dt, b_re, b_im)


def _s5_body(u_ref, perm_ref, permt_ref, wbre_ref, wbim_ref, lbre_ref, lbim_ref,
             wcre_ref, wcim_ref, d_ref, gw_ref, gb_ref, o_ref,
             sre_ref, sim_ref, cre_ref, cim_ref, *, steps, nb):
    rows = nb * steps
    width = u_ref.shape[-1]
    n_state = sre_ref.shape[-1]
    halves = wbre_ref.shape[0]
    hw_in = width // halves
    hw_st = n_state // halves

    @pl.when(pl.program_id(0) == 0)
    def _():
        cre_ref[...] = jnp.zeros(cre_ref.shape, _F32)
        cim_ref[...] = jnp.zeros(cim_ref.shape, _F32)

    u_bm = u_ref[...].reshape(rows, width)
    u_tm = jnp.dot(perm_ref[...], u_bm, preferred_element_type=_F32).astype(_BF16)
    for hf in range(halves):
        uh = u_tm[:, hf * hw_in:(hf + 1) * hw_in]
        sre_ref[:, hf * hw_st:(hf + 1) * hw_st] = jnp.dot(uh, wbre_ref[hf], preferred_element_type=_F32)
        sim_ref[:, hf * hw_st:(hf + 1) * hw_st] = jnp.dot(uh, wbim_ref[hf], preferred_element_type=_F32)

    for c in range(n_state // _SCAN_LANES):
        sl = slice(c * _SCAN_LANES, (c + 1) * _SCAN_LANES)
        a_r = jnp.broadcast_to(lbre_ref[:, sl], (nb, _SCAN_LANES))
        a_i = jnp.broadcast_to(lbim_ref[:, sl], (nb, _SCAN_LANES))

        def scan_step(ti, carry, sl=sl, a_r=a_r, a_i=a_i):
            x_r, x_i = carry
            r0 = pl.multiple_of(ti * nb, nb)
            n_r = a_r * x_r - a_i * x_i + sre_ref[pl.ds(r0, nb), sl]
            n_i = a_r * x_i + a_i * x_r + sim_ref[pl.ds(r0, nb), sl]
            sre_ref[pl.ds(r0, nb), sl] = n_r
            sim_ref[pl.ds(r0, nb), sl] = n_i
            return n_r, n_i

        x_r, x_i = lax.fori_loop(0, steps, scan_step, (cre_ref[:, sl], cim_ref[:, sl]), unroll=4)
        cre_ref[:, sl] = x_r
        cim_ref[:, sl] = x_i

    ys = []
    for hf in range(halves):
        x_r = sre_ref[:, hf * hw_st:(hf + 1) * hw_st].astype(_BF16)
        x_i = sim_ref[:, hf * hw_st:(hf + 1) * hw_st].astype(_BF16)
        ys.append(jnp.dot(x_r, wcre_ref[hf], preferred_element_type=_F32)
                  + jnp.dot(x_i, wcim_ref[hf], preferred_element_type=_F32))
    y = jnp.concatenate(ys, axis=-1) + d_ref[...] * u_tm.astype(_F32)
    g = jax.nn.gelu(y)
    o = g * jax.nn.sigmoid(jnp.dot(g.astype(_BF16), gw_ref[...], preferred_element_type=_F32)
                           + gb_ref[...])
    o_bm = jnp.dot(permt_ref[...], o.astype(_BF16), preferred_element_type=_F32)
    o_ref[...] = o_bm.astype(_BF16).reshape(o_ref.shape)


def _s5(u, perm, permt, wb_re, wb_im, lb_re, lb_im, wc_re, wc_im, d_vec, glu_w, glu_b):
    nb, l, width = u.shape
    steps = _SCAN_STEPS
    n_state = lb_re.shape[-1]
    full = lambda a: pl.BlockSpec(a.shape, lambda i: (0,) * a.ndim)
    io_spec = pl.BlockSpec((nb, steps, width), lambda i: (0, i, 0))
    consts = (perm, permt, wb_re, wb_im, lb_re, lb_im, wc_re, wc_im, d_vec, glu_w, glu_b)
    return pl.pallas_call(
        functools.partial(_s5_body, steps=steps, nb=nb),
        grid=(l // steps,),
        in_specs=[io_spec] + [full(a) for a in consts],
        out_specs=io_spec,
        out_shape=jax.ShapeDtypeStruct(u.shape, _BF16),
        scratch_shapes=[pltpu.VMEM((nb * steps, n_state), _F32),
                        pltpu.VMEM((nb * steps, n_state), _F32),
                        pltpu.VMEM((nb, n_state), _F32),
                        pltpu.VMEM((nb, n_state), _F32)],
        compiler_params=pltpu.CompilerParams(
            dimension_semantics=("arbitrary",), vmem_limit_bytes=_VMEM_LIMIT),
        name="s5_branch",
    )(u, *consts)


def _outproj_body(x_ref, ng_ref, wg_ref, mb_ref, oa_ref, os_ref, pa_ref, ps_ref, wo_ref,
                  out_ref, *, width):
    x = x_ref[...]
    d = x.shape[-1]
    h = (x * _rms_scale(x) * ng_ref[...]).astype(_BF16)
    zg = jnp.dot(h, wg_ref[...], preferred_element_type=_F32)
    o_a = (oa_ref[0, 0, 0].astype(_F32) * jax.nn.silu(zg[:, :width])).astype(_BF16)
    o_s = (os_ref[...].astype(_F32) * jax.nn.silu(zg[:, width:2 * width])).astype(_BF16)
    p_a = jnp.dot(o_a, pa_ref[...], preferred_element_type=_F32)
    p_s = jnp.dot(o_s, ps_ref[...], preferred_element_type=_F32)
    g = jax.nn.sigmoid(zg[:, 2 * width:] + mb_ref[...])
    merged = g[:, :d] * p_a + g[:, d:] * p_s
    out_ref[...] = x + jnp.dot(merged.astype(_BF16), wo_ref[...], preferred_element_type=_F32)


def _outproj(x2, norm_gain, w_gates, merge_b, o_a, o_s, proj_a, proj_s, w_out, width):
    n, d = x2.shape
    tm = _TOKEN_TILE
    _, _, half, t, _ = o_a.shape
    assert t == tm
    nq = 2 * half
    full = lambda a: pl.BlockSpec(a.shape, lambda i: (0,) * a.ndim)
    row = lambda a: pl.BlockSpec((tm, a.shape[-1]), lambda i: (i, 0))

    def mirrored_tile(i):
        j = i % nq
        return (j // half, i // nq, jnp.where(j < half, j, nq - 1 - j), 0, 0)

    return pl.pallas_call(
        functools.partial(_outproj_body, width=width),
        grid=(n // tm,),
        in_specs=[row(x2), full(norm_gain), full(w_gates), full(merge_b),
                  pl.BlockSpec((1, 1, 1, tm, width), mirrored_tile), row(o_s),
                  full(proj_a), full(proj_s), full(w_out)],
        out_specs=row(x2),
        out_shape=jax.ShapeDtypeStruct(x2.shape, x2.dtype),
        compiler_params=pltpu.CompilerParams(
            dimension_semantics=("parallel",), vmem_limit_bytes=_VMEM_LIMIT),
        name="outproj",
    )(x2, norm_gain, w_gates, merge_b, o_a, o_s, proj_a, proj_s, w_out)


def _block_diag_halves(blocks, halves):
    g, r, c = blocks.shape
    gh = g // halves
    eye = jnp.eye(gh, dtype=blocks.dtype)
    b = blocks.reshape(halves, gh, r, 1, c) * eye.reshape(1, gh, 1, gh, 1)
    return b.reshape(halves, gh * r, gh * c)


def _time_major_perm(nb, steps):
    r = np.arange(nb * steps)
    src = (r % nb) * steps + r // nb
    p = np.zeros((nb * steps, nb * steps), np.float32)
    p[r, src] = 1.0
    return p


def _layer(x, lam_init, norm_gain, w_in, merge_gate_b, q_norm_gain, k_norm_gain,
           lambda_q1, lambda_k1, lambda_q2, lambda_k2, diff_subln_gain, rel_bias_table,
           ssm_a_re, ssm_a_im, ssm_log_dt, ssm_b_re, ssm_b_im, ssm_c_re, ssm_c_im,
           ssm_d, ssm_glu_w, ssm_glu_b, proj_attn, proj_ssm, w_out):
    nb, l, d = x.shape
    n = nb * l
    aw = _HEADS * 2 * _SUB_DIM
    groups = ssm_a_re.shape[0]
    x2 = x.reshape(n, d)
    ng = norm_gain.reshape(1, d).astype(_F32)

    w_ku = jnp.concatenate([w_in[:, aw:2 * aw], w_in[:, 4 * aw:5 * aw]], axis=1).astype(_BF16)
    w_qvt = jnp.concatenate([w_in[:, :aw], w_in[:, 2 * aw:3 * aw]], axis=1).T.astype(_BF16)
    w_gates = jnp.concatenate([w_in[:, 3 * aw:4 * aw], w_in[:, 5 * aw:]], axis=1).astype(_BF16)
    gq = jnp.tile(q_norm_gain.astype(_F32), 2 * _HEADS) * (_SUB_DIM ** -0.5 * _LOG2E)
    gqt = jnp.broadcast_to(gq[:, None], (aw, _TOKEN_TILE))
    gk = jnp.tile(k_norm_gain.astype(_F32), 2 * _HEADS).reshape(1, aw)
    seg = np.arange(aw) // _SUB_DIM
    gsum = jnp.asarray((seg[:, None] == seg[None, :]).astype(np.float32), _BF16)
    qt, k, vt, u = _inproj(x2, ng, w_ku, w_qvt, gqt, gk, gsum, aw)

    lam4 = jnp.stack([lambda_q1, lambda_k1, lambda_q2, lambda_k2]).astype(_F32)
    qk_gains = jnp.stack([gq[:_SUB_DIM], gk[0, :_SUB_DIM]])
    bias, lam, flag = _bias_prep(rel_bias_table.astype(_F32), lam4, qk_gains, lam_init, _ATTN_TILE)
    subln = diff_subln_gain.astype(_F32) * (1.0 - lam_init)
    subln_t = jnp.broadcast_to(subln[:, None], (_V_DIM, _ATTN_TILE))
    shp = (nb, l, aw)
    o_a = _attention(flag, qt, k.reshape(shp), vt, bias, lam, subln_t)

    rep = lambda a: jnp.repeat(a.astype(_F32), _SSM_GROUP, axis=0)
    ldt = jnp.broadcast_to(ssm_log_dt.astype(_F32)[:, None], ssm_a_re.shape)
    bt = lambda a: a.astype(_F32).transpose(0, 2, 1).reshape(groups * _SSM_GROUP, _SSM_STATE)
    lb_re, lb_im, bb_re, bb_im = _s5_prep(rep(ssm_a_re), rep(ssm_a_im), rep(ldt),
                                          bt(ssm_b_re), bt(ssm_b_im))
    n_state = groups * _SSM_STATE
    flat = lambda a: a[::_SSM_GROUP].reshape(1, n_state)
    gshape = (groups, _SSM_GROUP, _SSM_STATE)
    wb_re = _block_diag_halves(bb_re.reshape(gshape), 2).astype(_BF16)
    wb_im = _block_diag_halves(bb_im.reshape(gshape), 2).astype(_BF16)
    wc_re = _block_diag_halves(ssm_c_re.astype(_F32).transpose(0, 2, 1), 2).astype(_BF16)
    wc_im = _block_diag_halves(-ssm_c_im.astype(_F32).transpose(0, 2, 1), 2).astype(_BF16)
    perm = _time_major_perm(nb, _SCAN_STEPS)
    o_s = _s5(u.reshape(shp), jnp.asarray(perm, _BF16), jnp.asarray(perm.T, _BF16),
              wb_re, wb_im, flat(lb_re), flat(lb_im), wc_re, wc_im,
              ssm_d.astype(_F32).reshape(1, aw), ssm_glu_w.astype(_BF16),
              ssm_glu_b.astype(_F32).reshape(1, aw))

    out = _outproj(x2, ng, w_gates, merge_gate_b.astype(_F32).reshape(1, 2 * d),
                   o_a, o_s.reshape(n, aw),
                   proj_attn.astype(_BF16), proj_ssm.astype(_BF16), w_out.astype(_BF16), aw)
    return out.reshape(nb, l, d)


def kernel(x, norm_gain, w_in, merge_gate_b, q_norm_gain, k_norm_gain, lambda_q1, lambda_k1,
           lambda_q2, lambda_k2, diff_subln_gain, rel_bias_table, ssm_A_re, ssm_A_im, ssm_log_dt,
           ssm_B_re, ssm_B_im, ssm_C_re, ssm_C_im, ssm_D, ssm_glu_w, ssm_glu_b,
           proj_attn, proj_ssm, w_out):
    per_layer = (norm_gain, w_in, merge_gate_b, q_norm_gain, k_norm_gain, lambda_q1, lambda_k1,
                 lambda_q2, lambda_k2, diff_subln_gain)
    per_layer_tail = (ssm_A_re, ssm_A_im, ssm_log_dt, ssm_B_re, ssm_B_im, ssm_C_re, ssm_C_im,
                      ssm_D, ssm_glu_w, ssm_glu_b, proj_attn, proj_ssm, w_out)
    for layer in range(norm_gain.shape[0]):
        lam_init = 0.8 - 0.6 * math.exp(-0.3 * layer)
        x = _layer(x, lam_init, *(p[layer] for p in per_layer), rel_bias_table,
                   *(p[layer] for p in per_layer_tail))
    return x
```

```python
import functools
import math

import jax
import jax.numpy as jnp
import numpy as np
from jax import lax
from jax.experimental import pallas as pl
from jax.experimental.pallas import tpu as pltpu

_F32 = jnp.float32
_BF16 = jnp.bfloat16

_CHUNK = 64
_HEADS = 4
_SUB_DIM = 64
_V_DIM = 128
_SSM_GROUP = 16
_SSM_STATE = 64
_REL_BUCKETS = 32
_REL_MAX_DIST = 128
_EPS = 1e-6
_LOG2E = math.log2(math.e)
_SAFE_LOG2_RANGE = 100.0

_VMEM_LIMIT = 48 * 1024 * 1024
_TOKEN_TILE = 512
_ATTN_TILE = 512
_KEY_GRANULE = 256
_ONES_ROWS = 16
_SCAN_STEPS = 64
_STATE_TILE = 256
_SCAN_GROUP = 4


def _rms_scale(x, eps=_EPS):
    return lax.rsqrt(jnp.mean(x * x, axis=-1, keepdims=True) + eps)


def _inproj_body(x_ref, ng_ref, wku_ref, wqvt_ref, gqt_ref, gk_ref, gsum_ref,
                 qt_ref, k_ref, vt_ref, u_ref, *, width):
    x = x_ref[...]
    tm = x.shape[0]
    h = (x * _rms_scale(x) * ng_ref[...]).astype(_BF16)
    z = jnp.dot(h, wku_ref[...], preferred_element_type=_F32)
    zt = lax.dot_general(wqvt_ref[...], h, (((1,), (1,)), ((), ())),
                         preferred_element_type=_F32)

    kk = z[:, :width]
    ss = jnp.dot((kk * kk).astype(_BF16), gsum_ref[...], preferred_element_type=_F32)
    k_ref[...] = (kk * lax.rsqrt(ss * (1.0 / _SUB_DIM) + _EPS) * gk_ref[...]).astype(_BF16)
    u_ref[...] = z[:, width:].astype(_BF16)

    qt = zt[:width].reshape(width // _SUB_DIM, _SUB_DIM, tm)
    ms = jnp.mean(qt * qt, axis=1, keepdims=True)
    qt_ref[...] = ((qt * lax.rsqrt(ms + _EPS)).reshape(width, tm) * gqt_ref[...]).astype(_BF16)

    vt = zt[width:].astype(_BF16)
    ones = jnp.ones((_ONES_ROWS, _KEY_GRANULE), _BF16)
    for j in range(tm // _KEY_GRANULE):
        for hd in range(_HEADS):
            vt_ref[j, hd, :_V_DIM, :] = vt[hd * _V_DIM:(hd + 1) * _V_DIM,
                                           j * _KEY_GRANULE:(j + 1) * _KEY_GRANULE]
            vt_ref[j, hd, _V_DIM:, :] = ones


def _inproj(x2, norm_gain, w_ku, w_qvt, gqt, gk, gsum, width):
    n, d = x2.shape
    tm = _TOKEN_TILE
    full = lambda a: pl.BlockSpec(a.shape, lambda i: (0,) * a.ndim)
    row_sd = jax.ShapeDtypeStruct((n, width), _BF16)
    row_spec = pl.BlockSpec((tm, width), lambda i: (i, 0))
    gran = tm // _KEY_GRANULE
    vrows = _V_DIM + _ONES_ROWS
    return pl.pallas_call(
        functools.partial(_inproj_body, width=width),
        grid=(n // tm,),
        in_specs=[pl.BlockSpec((tm, d), lambda i: (i, 0)), full(norm_gain), full(w_ku), full(w_qvt),
                  full(gqt), full(gk), full(gsum)],
        out_specs=[pl.BlockSpec((width, tm), lambda i: (0, i)), row_spec,
                   pl.BlockSpec((gran, _HEADS, vrows, _KEY_GRANULE), lambda i: (i, 0, 0, 0)), row_spec],
        out_shape=[jax.ShapeDtypeStruct((width, n), _BF16), row_sd,
                   jax.ShapeDtypeStruct((n // _KEY_GRANULE, _HEADS, vrows, _KEY_GRANULE), _BF16), row_sd],
        compiler_params=pltpu.CompilerParams(
            dimension_semantics=("parallel",), vmem_limit_bytes=_VMEM_LIMIT),
        name="inproj",
    )(x2, norm_gain, w_ku, w_qvt, gqt, gk, gsum)


def _t5_bucket_np(rel):
    nb = _REL_BUCKETS // 2
    max_exact = nb // 2
    side = np.where(rel > 0, nb, 0)
    n = np.abs(rel)
    nf = np.maximum(n, 1).astype(np.float32)
    large = max_exact + (np.log(nf / np.float32(max_exact)) / np.float32(math.log(_REL_MAX_DIST / max_exact))
                         * np.float32(nb - max_exact)).astype(np.int32)
    large = np.minimum(large, nb - 1)
    return side + np.where(n < max_exact, n, large)


def _bucket_tiles(t):
    i = np.arange(t)[None, :]
    j = np.arange(t)[:, None]
    far = _t5_bucket_np(j - i - 2 * t)
    prev = _t5_bucket_np(j - i - t)
    diag = np.where((j // _CHUNK) <= (i // _CHUNK), _t5_bucket_np(j - i), -1)
    return np.stack([far, prev, diag]).astype(np.int32)


def _bias_prep_body(tab_ref, tabv_ref, bucket_ref, lam4_ref, gains_ref,
                    bias_ref, lam_ref, flag_ref, *, lam_init):
    h = pl.program_id(0)
    far_bucket = _REL_BUCKETS // 2 - 1
    bkt = bucket_ref[...]
    far = tab_ref[far_bucket, h]
    val = jnp.full(bkt.shape, -jnp.inf, _F32)
    for b in range(_REL_BUCKETS):
        val = jnp.where(bkt == b, (tab_ref[b, h] - far) * _LOG2E, val)
    bias_ref[0] = val
    l4 = lam4_ref[...]
    s1 = jnp.sum(l4[0:1] * l4[1:2], axis=-1, keepdims=True)
    s2 = jnp.sum(l4[2:3] * l4[3:4], axis=-1, keepdims=True)
    lam_ref[...] = jnp.broadcast_to(jnp.exp(s1) - jnp.exp(s2) + lam_init, lam_ref.shape)
    gmax = jnp.max(jnp.abs(gains_ref[...]), axis=-1, keepdims=True)
    qk_bound = (_SUB_DIM * 1.02) * gmax[0:1] * gmax[1:2]
    tv = tabv_ref[...]
    dev = jnp.abs(tv - tv[far_bucket:far_bucket + 1]) * _LOG2E
    bias_bound = jnp.max(jnp.max(dev, axis=-1, keepdims=True), axis=0, keepdims=True)
    flag_ref[...] = (qk_bound + bias_bound <= _SAFE_LOG2_RANGE).astype(jnp.int32)


def _bias_prep(rel_table, lam4, gains, lam_init, t):
    buckets = jnp.asarray(_bucket_tiles(t))
    whole = lambda a: pl.BlockSpec(a.shape, lambda h: (0,) * a.ndim)
    return pl.pallas_call(
        functools.partial(_bias_prep_body, lam_init=lam_init),
        grid=(_HEADS,),
        in_specs=[pl.BlockSpec(memory_space=pltpu.SMEM), whole(rel_table), whole(buckets),
                  whole(lam4), whole(gains)],
        out_specs=[pl.BlockSpec((1,) + buckets.shape, lambda h: (h, 0, 0, 0)),
                   pl.BlockSpec((1, _V_DIM), lambda h: (0, 0)),
                   pl.BlockSpec((1, 1), lambda h: (0, 0))],
        out_shape=[jax.ShapeDtypeStruct((_HEADS,) + buckets.shape, _F32),
                   jax.ShapeDtypeStruct((1, _V_DIM), _F32),
                   jax.ShapeDtypeStruct((1, 1), jnp.int32)],
        compiler_params=pltpu.CompilerParams(dimension_semantics=("arbitrary",)),
        name="attn_bias_prep",
    )(rel_table, rel_table, buckets, lam4, gains)


def _attn_body(flag_ref, qta_ref, qtb_ref, k_ref, vt_ref, bias_ref, lam_ref, gain_ref, o_ref,
               qw_ref, m_ref, acc_ref, *, t, nq):
    p = pl.program_id(2)
    gran = t // _KEY_GRANULE
    for side, qt_ref in enumerate((qta_ref, qtb_ref)):
        qt = qt_ref[...]
        row = lax.broadcasted_iota(jnp.int32, qt.shape, 0)
        zero = jnp.zeros_like(qt)
        qw_ref[side, :, :t] = jnp.where(row < _SUB_DIM, qt, zero)
        qw_ref[side, :, t:] = jnp.where(row >= _SUB_DIM, qt, zero)
    acc_ref[...] = jnp.zeros(acc_ref.shape, _F32)

    def schedule(step_index):
        mirrored = step_index > p
        side = mirrored.astype(jnp.int32)
        kj = jnp.where(mirrored, step_index - p - 1, step_index)
        q_tile = jnp.where(mirrored, nq - 1 - p, p)
        bias_index = jnp.where(kj == q_tile, 2, jnp.where(kj == q_tile - 1, 1, 0))
        return side, kj, bias_index

    def apply_to_values(kj, pt):
        pv = jnp.dot(vt_ref[kj * gran, 0], pt[:_KEY_GRANULE], preferred_element_type=_F32)
        for g in range(1, gran):
            pv = pv + jnp.dot(vt_ref[kj * gran + g, 0],
                              pt[g * _KEY_GRANULE:(g + 1) * _KEY_GRANULE],
                              preferred_element_type=_F32)
        return pv

    def key_tile(kj):
        return k_ref[0, pl.ds(pl.multiple_of(kj * t, t), t), :]

    def bounded_step(step_index):
        side, kj, bias_index = schedule(step_index)
        st = jnp.dot(key_tile(kj), qw_ref[side], preferred_element_type=_F32)
        bias = bias_ref[0, bias_index]
        pt = jnp.concatenate([jnp.exp2(st[:, :t] + bias).astype(_BF16),
                              jnp.exp2(st[:, t:] + bias).astype(_BF16)], axis=1)
        acc_ref[side] += apply_to_values(kj, pt)

    def online_step(step_index):
        side, kj, bias_index = schedule(step_index)
        kt = key_tile(kj)
        bias = bias_ref[0, bias_index]
        for s in range(2):
            lanes = slice(s * t, (s + 1) * t)
            st = jnp.dot(kt, qw_ref[side, :, lanes], preferred_element_type=_F32) + bias
            m_old = m_ref[side, s]
            m_new = jnp.maximum(m_old, jnp.max(st, axis=0, keepdims=True))
            alpha = jnp.exp2(m_old - m_new)
            pt = jnp.exp2(st - m_new).astype(_BF16)
            acc_ref[side, :, lanes] = alpha * acc_ref[side, :, lanes] + apply_to_values(kj, pt)
            m_ref[side, s] = m_new

    bounded = flag_ref[0, 0] > 0

    @pl.when(bounded)
    def _():
        for step_index in range(nq + 1):
            bounded_step(step_index)

    @pl.when(jnp.logical_not(bounded))
    def _():
        m_ref[...] = jnp.full(m_ref.shape, -jnp.inf, _F32)
        for step_index in range(nq + 1):
            online_step(step_index)

    for side in range(2):
        acc = acc_ref[side]
        o = acc[:_V_DIM] / acc[_V_DIM:_V_DIM + 1]
        od = o[:, :t] - lam_ref[0:1, 0:1] * o[:, t:]
        scale = lax.rsqrt(jnp.mean(od * od, axis=0, keepdims=True) + _EPS)
        o_ref[side, 0, 0] = (od * scale * gain_ref[...]).T.astype(_BF16)


def _attention(flag, qt, k, vt, bias, lam, gain_t):
    b, l, w = k.shape
    t = _ATTN_TILE
    nq = l // t
    vrows = vt.shape[2]
    return pl.pallas_call(
        functools.partial(_attn_body, t=t, nq=nq),
        grid=(b, _HEADS, nq // 2),
        in_specs=[pl.BlockSpec(memory_space=pltpu.SMEM),
                  pl.BlockSpec((_V_DIM, t), lambda bi, h, p: (h, bi * nq + p)),
                  pl.BlockSpec((_V_DIM, t), lambda bi, h, p: (h, bi * nq + nq - 1 - p)),
                  pl.BlockSpec((1, l, _V_DIM), lambda bi, h, p: (bi, 0, h)),
                  pl.BlockSpec((l // _KEY_GRANULE, 1, vrows, _KEY_GRANULE),
                               lambda bi, h, p: (bi, h, 0, 0)),
                  pl.BlockSpec((1, 3, t, t), lambda bi, h, p: (h, 0, 0, 0)),
                  pl.BlockSpec(lam.shape, lambda bi, h, p: (0, 0)),
                  pl.BlockSpec(gain_t.shape, lambda bi, h, p: (0, 0))],
        out_specs=pl.BlockSpec((2, 1, 1, t, _V_DIM), lambda bi, h, p: (0, bi, p, 0, h)),
        out_shape=jax.ShapeDtypeStruct((2, b, nq // 2, t, w), _BF16),
        scratch_shapes=[pltpu.VMEM((2, _V_DIM, 2 * t), _BF16),
                        pltpu.VMEM((2, 2, 1, t), _F32),
                        pltpu.VMEM((2, vrows, 2 * t), _F32)],
        compiler_params=pltpu.CompilerParams(
            dimension_semantics=("parallel", "parallel", "arbitrary"),
            vmem_limit_bytes=_VMEM_LIMIT),
        name="diff_attention",
    )(flag, qt, qt, k, vt, bias, lam, gain_t)


def _s5_prep_body(are_ref, aim_ref, ldt_ref, bre_ref, bim_ref,
                  lbre_ref, lbim_ref, bbre_ref, bbim_ref):
    a_re = are_ref[...]
    a_im = aim_ref[...]
    dt = jnp.exp(ldt_ref[...])
    decay = jnp.exp(a_re * dt)
    lb_re = decay * jnp.cos(a_im * dt)
    lb_im = decay * jnp.sin(a_im * dt)
    nr = lb_re - 1.0
    ni = lb_im
    den = a_re * a_re + a_im * a_im
    q_re = (nr * a_re + ni * a_im) / den
    q_im = (ni * a_re - nr * a_im) / den
    b_re = bre_ref[...]
    b_im = bim_ref[...]
    bbre_ref[...] = q_re * b_re - q_im * b_im
    bbim_ref[...] = q_re * b_im + q_im * b_re
    lbre_ref[...] = lb_re
    lbim_ref[...] = lb_im


def _s5_prep(a_re, a_im, log_dt, b_re, b_im):
    sd = jax.ShapeDtypeStruct(a_re.shape, _F32)
    return pl.pallas_call(_s5_prep_body, out_shape=[sd] * 4, name="s5_discretise")(
        a_re, a_im, log_dt, b_re, b_im)


def _s5_body(u_ref, perm_ref, permt_ref, wbre_ref, wbim_ref, lbre_ref, lbim_ref,
             wcre_ref, wcim_ref, d_ref, gw_ref, gb_ref, o_ref,
             utm_a, bre_a, bim_a, utm_b, bre_b, bim_b, xre_ref, xim_ref, cre_ref, cim_ref,
             *, steps, nb):
    i = pl.program_id(0)
    rows = nb * steps
    width = u_ref.shape[-1]
    n_tiles, _, tile_w = lbre_ref.shape
    halves = wcre_ref.shape[0]
    tiles_per_half = n_tiles // halves
    hw_in = width // halves

    @pl.when(i == 0)
    def _():
        cre_ref[...] = jnp.zeros(cre_ref.shape, _F32)
        cim_ref[...] = jnp.zeros(cim_ref.shape, _F32)
        bre_b[...] = jnp.zeros(bre_b.shape, _F32)
        bim_b[...] = jnp.zeros(bim_b.shape, _F32)
        utm_b[...] = jnp.zeros(utm_b.shape, _BF16)

    def step(utm_new, bre_new, bim_new, utm_old, bre_old, bim_old):
        u_bm = u_ref[...].reshape(rows, width)
        u_tm = jnp.dot(perm_ref[...], u_bm, preferred_element_type=_F32).astype(_BF16)
        utm_new[...] = u_tm

        for hf in range(halves):
            uh = u_tm[:, hf * hw_in:(hf + 1) * hw_in]
            bu_r = jnp.dot(uh, wbre_ref[hf], preferred_element_type=_F32)
            bu_i = jnp.dot(uh, wbim_ref[hf], preferred_element_type=_F32)
            for q in range(tiles_per_half):
                lanes = slice(q * tile_w, (q + 1) * tile_w)
                bre_new[hf * tiles_per_half + q] = bu_r[:, lanes]
                bim_new[hf * tiles_per_half + q] = bu_i[:, lanes]

        for j0 in range(0, n_tiles, _SCAN_GROUP):
            group = range(j0, j0 + _SCAN_GROUP)
            a_r = [jnp.broadcast_to(lbre_ref[j], (nb, tile_w)) for j in group]
            a_i = [jnp.broadcast_to(lbim_ref[j], (nb, tile_w)) for j in group]
            x_r = [cre_ref[j] for j in group]
            x_i = [cim_ref[j] for j in group]
            for ti in range(steps):
                r = slice(ti * nb, (ti + 1) * nb)
                for q, j in enumerate(group):
                    n_r = a_r[q] * x_r[q] - a_i[q] * x_i[q] + bre_old[j, r, :]
                    n_i = a_r[q] * x_i[q] + a_i[q] * x_r[q] + bim_old[j, r, :]
                    xre_ref[j, r, :] = n_r
                    xim_ref[j, r, :] = n_i
                    x_r[q], x_i[q] = n_r, n_i
            for q, j in enumerate(group):
                cre_ref[j] = x_r[q]
                cim_ref[j] = x_i[q]

        ys = []
        for hf in range(halves):
            tiles = range(hf * tiles_per_half, (hf + 1) * tiles_per_half)
            x_r = jnp.concatenate([xre_ref[j] for j in tiles], axis=1).astype(_BF16)
            x_i = jnp.concatenate([xim_ref[j] for j in tiles], axis=1).astype(_BF16)
            ys.append(jnp.dot(x_r, wcre_ref[hf], preferred_element_type=_F32)
                      + jnp.dot(x_i, wcim_ref[hf], preferred_element_type=_F32))
        y = jnp.concatenate(ys, axis=-1) + d_ref[...] * utm_old[...].astype(_F32)
        g = jax.nn.gelu(y)
        o = g * jax.nn.sigmoid(jnp.dot(g.astype(_BF16), gw_ref[...], preferred_element_type=_F32)
                               + gb_ref[...])
        o_bm = jnp.dot(permt_ref[...], o.astype(_BF16), preferred_element_type=_F32)
        o_ref[...] = o_bm.astype(_BF16).reshape(o_ref.shape)

    @pl.when(i % 2 == 0)
    def _():
        step(utm_a, bre_a, bim_a, utm_b, bre_b, bim_b)

    @pl.when(i % 2 == 1)
    def _():
        step(utm_b, bre_b, bim_b, utm_a, bre_a, bim_a)


def _s5(u, perm, permt, wb_re, wb_im, lb_re, lb_im, wc_re, wc_im, d_vec, glu_w, glu_b):
    nb, l, width = u.shape
    steps = _SCAN_STEPS
    n_blocks = l // steps
    n_tiles, _, tile_w = lb_re.shape
    full = lambda a: pl.BlockSpec(a.shape, lambda i: (0,) * a.ndim)
    consts = (perm, permt, wb_re, wb_im, lb_re, lb_im, wc_re, wc_im, d_vec, glu_w, glu_b)
    utm_buf = pltpu.VMEM((nb * steps, width), _BF16)
    state_buf = pltpu.VMEM((n_tiles, nb * steps, tile_w), _F32)
    return pl.pallas_call(
        functools.partial(_s5_body, steps=steps, nb=nb),
        grid=(n_blocks + 1,),
        in_specs=[pl.BlockSpec((nb, steps, width), lambda i: (0, jnp.minimum(i, n_blocks - 1), 0))]
                 + [full(a) for a in consts],
        out_specs=pl.BlockSpec((nb, steps, width), lambda i: (0, jnp.maximum(i - 1, 0), 0)),
        out_shape=jax.ShapeDtypeStruct(u.shape, _BF16),
        scratch_shapes=[utm_buf, state_buf, state_buf, utm_buf, state_buf, state_buf,
                        state_buf, state_buf,
                        pltpu.VMEM((n_tiles, nb, tile_w), _F32),
                        pltpu.VMEM((n_tiles, nb, tile_w), _F32)],
        compiler_params=pltpu.CompilerParams(
            dimension_semantics=("arbitrary",), vmem_limit_bytes=_VMEM_LIMIT),
        name="s5_branch",
    )(u, *consts)


def _outproj_body(x_ref, ng_ref, wg_ref, mb_ref, oa_ref, os_ref, pa_ref, ps_ref, wo_ref,
                  out_ref, *, width):
    x = x_ref[...]
    d = x.shape[-1]
    h = (x * _rms_scale(x) * ng_ref[...]).astype(_BF16)
    zg = jnp.dot(h, wg_ref[...], preferred_element_type=_F32)
    o_a = (oa_ref[0, 0, 0].astype(_F32) * jax.nn.silu(zg[:, :width])).astype(_BF16)
    o_s = (os_ref[...].astype(_F32) * jax.nn.silu(zg[:, width:2 * width])).astype(_BF16)
    p_a = jnp.dot(o_a, pa_ref[...], preferred_element_type=_F32)
    p_s = jnp.dot(o_s, ps_ref[...], preferred_element_type=_F32)
    g = jax.nn.sigmoid(zg[:, 2 * width:] + mb_ref[...])
    merged = g[:, :d] * p_a + g[:, d:] * p_s
    out_ref[...] = x + jnp.dot(merged.astype(_BF16), wo_ref[...], preferred_element_type=_F32)


def _outproj(x2, norm_gain, w_gates, merge_b, o_a, o_s, proj_a, proj_s, w_out, width):
    n, d = x2.shape
    tm = _TOKEN_TILE
    _, _, half, t, _ = o_a.shape
    assert t == tm
    nq = 2 * half
    full = lambda a: pl.BlockSpec(a.shape, lambda i: (0,) * a.ndim)
    row = lambda a: pl.BlockSpec((tm, a.shape[-1]), lambda i: (i, 0))

    def mirrored_tile(i):
        j = i % nq
        return (j // half, i // nq, jnp.where(j < half, j, nq - 1 - j), 0, 0)

    return pl.pallas_call(
        functools.partial(_outproj_body, width=width),
        grid=(n // tm,),
        in_specs=[row(x2), full(norm_gain), full(w_gates), full(merge_b),
                  pl.BlockSpec((1, 1, 1, tm, width), mirrored_tile), row(o_s),
                  full(proj_a), full(proj_s), full(w_out)],
        out_specs=row(x2),
        out_shape=jax.ShapeDtypeStruct(x2.shape, x2.dtype),
        compiler_params=pltpu.CompilerParams(
            dimension_semantics=("parallel",), vmem_limit_bytes=_VMEM_LIMIT),
        name="outproj",
    )(x2, norm_gain, w_gates, merge_b, o_a, o_s, proj_a, proj_s, w_out)


def _block_diag_halves(blocks, halves):
    g, r, c = blocks.shape
    gh = g // halves
    eye = jnp.eye(gh, dtype=blocks.dtype)
    b = blocks.reshape(halves, gh, r, 1, c) * eye.reshape(1, gh, 1, gh, 1)
    return b.reshape(halves, gh * r, gh * c)


def _time_major_perm(nb, steps):
    r = np.arange(nb * steps)
    src = (r % nb) * steps + r // nb
    p = np.zeros((nb * steps, nb * steps), np.float32)
    p[r, src] = 1.0
    return p


def _layer(x, lam_init, norm_gain, w_in, merge_gate_b, q_norm_gain, k_norm_gain,
           lambda_q1, lambda_k1, lambda_q2, lambda_k2, diff_subln_gain, rel_bias_table,
           ssm_a_re, ssm_a_im, ssm_log_dt, ssm_b_re, ssm_b_im, ssm_c_re, ssm_c_im,
           ssm_d, ssm_glu_w, ssm_glu_b, proj_attn, proj_ssm, w_out):
    nb, l, d = x.shape
    n = nb * l
    aw = _HEADS * 2 * _SUB_DIM
    groups = ssm_a_re.shape[0]
    x2 = x.reshape(n, d)
    ng = norm_gain.reshape(1, d).astype(_F32)

    w_ku = jnp.concatenate([w_in[:, aw:2 * aw], w_in[:, 4 * aw:5 * aw]], axis=1).astype(_BF16)
    w_qvt = jnp.concatenate([w_in[:, :aw], w_in[:, 2 * aw:3 * aw]], axis=1).T.astype(_BF16)
    w_gates = jnp.concatenate([w_in[:, 3 * aw:4 * aw], w_in[:, 5 * aw:]], axis=1).astype(_BF16)
    gq = jnp.tile(q_norm_gain.astype(_F32), 2 * _HEADS) * (_SUB_DIM ** -0.5 * _LOG2E)
    gqt = jnp.broadcast_to(gq[:, None], (aw, _TOKEN_TILE))
    gk = jnp.tile(k_norm_gain.astype(_F32), 2 * _HEADS).reshape(1, aw)
    seg = np.arange(aw) // _SUB_DIM
    gsum = jnp.asarray((seg[:, None] == seg[None, :]).astype(np.float32), _BF16)
    qt, k, vt, u = _inproj(x2, ng, w_ku, w_qvt, gqt, gk, gsum, aw)

    lam4 = jnp.stack([lambda_q1, lambda_k1, lambda_q2, lambda_k2]).astype(_F32)
    qk_gains = jnp.stack([gq[:_SUB_DIM], gk[0, :_SUB_DIM]])
    bias, lam, flag = _bias_prep(rel_bias_table.astype(_F32), lam4, qk_gains, lam_init, _ATTN_TILE)
    subln = diff_subln_gain.astype(_F32) * (1.0 - lam_init)
    subln_t = jnp.broadcast_to(subln[:, None], (_V_DIM, _ATTN_TILE))
    shp = (nb, l, aw)
    o_a = _attention(flag, qt, k.reshape(shp), vt, bias, lam, subln_t)

    rep = lambda a: jnp.repeat(a.astype(_F32), _SSM_GROUP, axis=0)
    ldt = jnp.broadcast_to(ssm_log_dt.astype(_F32)[:, None], ssm_a_re.shape)
    bt = lambda a: a.astype(_F32).transpose(0, 2, 1).reshape(groups * _SSM_GROUP, _SSM_STATE)
    lb_re, lb_im, bb_re, bb_im = _s5_prep(rep(ssm_a_re), rep(ssm_a_im), rep(ldt),
                                          bt(ssm_b_re), bt(ssm_b_im))
    n_tiles = groups * _SSM_STATE // _STATE_TILE
    flat = lambda a: a[::_SSM_GROUP].reshape(n_tiles, 1, _STATE_TILE)
    gshape = (groups, _SSM_GROUP, _SSM_STATE)
    wb_re = _block_diag_halves(bb_re.reshape(gshape), 2).astype(_BF16)
    wb_im = _block_diag_halves(bb_im.reshape(gshape), 2).astype(_BF16)
    wc_re = _block_diag_halves(ssm_c_re.astype(_F32).transpose(0, 2, 1), 2).astype(_BF16)
    wc_im = _block_diag_halves(-ssm_c_im.astype(_F32).transpose(0, 2, 1), 2).astype(_BF16)
    perm = _time_major_perm(nb, _SCAN_STEPS)
    o_s = _s5(u.reshape(shp), jnp.asarray(perm, _BF16), jnp.asarray(perm.T, _BF16),
              wb_re, wb_im, flat(lb_re), flat(lb_im), wc_re, wc_im,
              ssm_d.astype(_F32).reshape(1, aw), ssm_glu_w.astype(_BF16),
              ssm_glu_b.astype(_F32).reshape(1, aw))

    out = _outproj(x2, ng, w_gates, merge_gate_b.astype(_F32).reshape(1, 2 * d),
                   o_a, o_s.reshape(n, aw),
                   proj_attn.astype(_BF16), proj_ssm.astype(_BF16), w_out.astype(_BF16), aw)
    return out.reshape(nb, l, d)


def kernel(x, norm_gain, w_in, merge_gate_b, q_norm_gain, k_norm_gain, lambda_q1, lambda_k1,
           lambda_q2, lambda_k2, diff_subln_gain, rel_bias_table, ssm_A_re, ssm_A_im, ssm_log_dt,
           ssm_B_re, ssm_B_im, ssm_C_re, ssm_C_im, ssm_D, ssm_glu_w, ssm_glu_b,
           proj_attn, proj_ssm, w_out):
    per_layer = (norm_gain, w_in, merge_gate_b, q_norm_gain, k_norm_gain, lambda_q1, lambda_k1,
                 lambda_q2, lambda_k2, diff_subln_gain)
    per_layer_tail = (ssm_A_re, ssm_A_im, ssm_log_dt, ssm_B_re, ssm_B_im, ssm_C_re, ssm_C_im,
                      ssm_D, ssm_glu_w, ssm_glu_b, proj_attn, proj_ssm, w_out)
    for layer in range(norm_gain.shape[0]):
        lam_init = 0.8 - 0.6 * math.exp(-0.3 * layer)
        x = _layer(x, lam_init, *(p[layer] for p in per_layer), rel_bias_table,
                   *(p[layer] for p in per_layer_tail))
    return x
```

```python
import functools
import math

import jax
import jax.numpy as jnp
import numpy as np
from jax import lax
from jax.experimental import pallas as pl
from jax.experimental.pallas import tpu as pltpu

_F32 = jnp.float32
_BF16 = jnp.bfloat16

_CHUNK = 64
_HEADS = 4
_SUB_DIM = 64
_V_DIM = 128
_SSM_GROUP = 16
_SSM_STATE = 64
_REL_BUCKETS = 32
_REL_MAX_DIST = 128
_EPS = 1e-6
_LOG2E = math.log2(math.e)
_SAFE_LOG2_RANGE = 100.0

_VMEM_LIMIT = 48 * 1024 * 1024
_TOKEN_TILE = 512
_ATTN_TILE = 512
_KEY_GRANULE = 256
_ONES_ROWS = 16
_BIAS_BLOCK = 128
_SCAN_STEPS = 64
_STATE_TILE = 256
_SCAN_GROUP = 4


def _rms_scale(x, eps=_EPS):
    return lax.rsqrt(jnp.mean(x * x, axis=-1, keepdims=True) + eps)


def _inproj_body(x_ref, ng_ref, wku_ref, wqvt_ref, gqt_ref, gk_ref, gsum_ref,
                 qt_ref, k_ref, vt_ref, u_ref, *, width):
    x = x_ref[...]
    tm = x.shape[0]
    h = (x * _rms_scale(x) * ng_ref[...]).astype(_BF16)
    z = jnp.dot(h, wku_ref[...], preferred_element_type=_F32)
    zt = lax.dot_general(wqvt_ref[...], h, (((1,), (1,)), ((), ())),
                         preferred_element_type=_F32)

    kk = z[:, :width]
    ss = jnp.dot((kk * kk).astype(_BF16), gsum_ref[...], preferred_element_type=_F32)
    k_ref[...] = (kk * lax.rsqrt(ss * (1.0 / _SUB_DIM) + _EPS) * gk_ref[...]).astype(_BF16)
    u_ref[...] = z[:, width:].astype(_BF16)

    qt = zt[:width].reshape(width // _SUB_DIM, _SUB_DIM, tm)
    ms = jnp.mean(qt * qt, axis=1, keepdims=True)
    qt_ref[...] = ((qt * lax.rsqrt(ms + _EPS)).reshape(width, tm) * gqt_ref[...]).astype(_BF16)

    vt = zt[width:].astype(_BF16)
    ones = jnp.ones((_ONES_ROWS, _KEY_GRANULE), _BF16)
    for j in range(tm // _KEY_GRANULE):
        for hd in range(_HEADS):
            vt_ref[j, hd, :_V_DIM, :] = vt[hd * _V_DIM:(hd + 1) * _V_DIM,
                                           j * _KEY_GRANULE:(j + 1) * _KEY_GRANULE]
            vt_ref[j, hd, _V_DIM:, :] = ones


def _inproj(x2, norm_gain, w_ku, w_qvt, gqt, gk, gsum, width):
    n, d = x2.shape
    tm = _TOKEN_TILE
    full = lambda a: pl.BlockSpec(a.shape, lambda i: (0,) * a.ndim)
    row_sd = jax.ShapeDtypeStruct((n, width), _BF16)
    row_spec = pl.BlockSpec((tm, width), lambda i: (i, 0))
    gran = tm // _KEY_GRANULE
    vrows = _V_DIM + _ONES_ROWS
    return pl.pallas_call(
        functools.partial(_inproj_body, width=width),
        grid=(n // tm,),
        in_specs=[pl.BlockSpec((tm, d), lambda i: (i, 0)), full(norm_gain), full(w_ku), full(w_qvt),
                  full(gqt), full(gk), full(gsum)],
        out_specs=[pl.BlockSpec((width, tm), lambda i: (0, i)), row_spec,
                   pl.BlockSpec((gran, _HEADS, vrows, _KEY_GRANULE), lambda i: (i, 0, 0, 0)), row_spec],
        out_shape=[jax.ShapeDtypeStruct((width, n), _BF16), row_sd,
                   jax.ShapeDtypeStruct((n // _KEY_GRANULE, _HEADS, vrows, _KEY_GRANULE), _BF16), row_sd],
        compiler_params=pltpu.CompilerParams(
            dimension_semantics=("parallel",), vmem_limit_bytes=_VMEM_LIMIT),
        name="inproj",
    )(x2, norm_gain, w_ku, w_qvt, gqt, gk, gsum)


def _t5_bucket_np(rel):
    nb = _REL_BUCKETS // 2
    max_exact = nb // 2
    side = np.where(rel > 0, nb, 0)
    n = np.abs(rel)
    nf = np.maximum(n, 1).astype(np.float32)
    large = max_exact + (np.log(nf / np.float32(max_exact)) / np.float32(math.log(_REL_MAX_DIST / max_exact))
                         * np.float32(nb - max_exact)).astype(np.int32)
    large = np.minimum(large, nb - 1)
    return side + np.where(n < max_exact, n, large)


def _bucket_blocks():
    assert _BIAS_BLOCK % _CHUNK == 0
    i = np.arange(_BIAS_BLOCK)[None, :]
    j = np.arange(_BIAS_BLOCK)[:, None]
    far_bucket = _REL_BUCKETS // 2 - 1
    assert (_t5_bucket_np(j - i - 2 * _BIAS_BLOCK) == far_bucket).all()
    diag = np.where((j // _CHUNK) <= (i // _CHUNK), _t5_bucket_np(j - i), -1)
    prev = _t5_bucket_np(j - i - _BIAS_BLOCK)
    return np.stack([diag, prev]).astype(np.int32)


def _bias_prep_body(tab_ref, tabv_ref, bucket_ref, lam4_ref, gains_ref,
                    bias_ref, lam_ref, flag_ref, *, lam_init, t):
    h = pl.program_id(0)
    far_bucket = _REL_BUCKETS // 2 - 1
    bkt = bucket_ref[...]
    far = tab_ref[far_bucket, h]
    val = jnp.full(bkt.shape, -jnp.inf, _F32)
    for b in range(_REL_BUCKETS):
        val = jnp.where(bkt == b, (tab_ref[b, h] - far) * _LOG2E, val)
    diag_block, prev_block = val[0], val[1]
    n_blk = t // _BIAS_BLOCK
    bias_ref[0, 0] = jnp.zeros((t, t), _F32)
    bias_ref[0, 1] = jnp.zeros((t, t), _F32)
    bias_ref[0, 1, t - _BIAS_BLOCK:, :_BIAS_BLOCK] = prev_block
    for bj in range(n_blk):
        rows = slice(bj * _BIAS_BLOCK, (bj + 1) * _BIAS_BLOCK)
        for bi in range(n_blk):
            cols = slice(bi * _BIAS_BLOCK, (bi + 1) * _BIAS_BLOCK)
            if bi == bj:
                block = diag_block
            elif bi == bj + 1:
                block = prev_block
            else:
                block = jnp.full((_BIAS_BLOCK, _BIAS_BLOCK), 0.0 if bi > bj else -jnp.inf, _F32)
            bias_ref[0, 2, rows, cols] = block
    l4 = lam4_ref[...]
    s1 = jnp.sum(l4[0:1] * l4[1:2], axis=-1, keepdims=True)
    s2 = jnp.sum(l4[2:3] * l4[3:4], axis=-1, keepdims=True)
    lam_ref[...] = jnp.broadcast_to(jnp.exp(s1) - jnp.exp(s2) + lam_init, lam_ref.shape)
    gmax = jnp.max(jnp.abs(gains_ref[...]), axis=-1, keepdims=True)
    qk_bound = (_SUB_DIM * 1.02) * gmax[0:1] * gmax[1:2]
    tv = tabv_ref[...]
    dev = jnp.abs(tv - tv[far_bucket:far_bucket + 1]) * _LOG2E
    bias_bound = jnp.max(jnp.max(dev, axis=-1, keepdims=True), axis=0, keepdims=True)
    flag_ref[...] = (qk_bound + bias_bound <= _SAFE_LOG2_RANGE).astype(jnp.int32)


def _bias_prep(rel_table, lam4, gains, lam_init, t):
    assert t % _BIAS_BLOCK == 0
    buckets = jnp.asarray(_bucket_blocks())
    whole = lambda a: pl.BlockSpec(a.shape, lambda h: (0,) * a.ndim)
    return pl.pallas_call(
        functools.partial(_bias_prep_body, lam_init=lam_init, t=t),
        grid=(_HEADS,),
        in_specs=[pl.BlockSpec(memory_space=pltpu.SMEM), whole(rel_table), whole(buckets),
                  whole(lam4), whole(gains)],
        out_specs=[pl.BlockSpec((1, 3, t, t), lambda h: (h, 0, 0, 0)),
                   pl.BlockSpec((1, _V_DIM), lambda h: (0, 0)),
                   pl.BlockSpec((1, 1), lambda h: (0, 0))],
        out_shape=[jax.ShapeDtypeStruct((_HEADS, 3, t, t), _F32),
                   jax.ShapeDtypeStruct((1, _V_DIM), _F32),
                   jax.ShapeDtypeStruct((1, 1), jnp.int32)],
        compiler_params=pltpu.CompilerParams(dimension_semantics=("arbitrary",)),
        name="attn_bias_prep",
    )(rel_table, rel_table, buckets, lam4, gains)


def _attn_body(flag_ref, qta_ref, qtb_ref, k_ref, vt_ref, bias_ref, lam_ref, gain_ref, o_ref,
               qw_ref, m_ref, acc_ref, *, t, nq):
    p = pl.program_id(2)
    gran = t // _KEY_GRANULE
    for side, qt_ref in enumerate((qta_ref, qtb_ref)):
        qt = qt_ref[...]
        row = lax.broadcasted_iota(jnp.int32, qt.shape, 0)
        zero = jnp.zeros_like(qt)
        qw_ref[side, :, :t] = jnp.where(row < _SUB_DIM, qt, zero)
        qw_ref[side, :, t:] = jnp.where(row >= _SUB_DIM, qt, zero)
    acc_ref[...] = jnp.zeros(acc_ref.shape, _F32)

    def schedule(step_index):
        mirrored = step_index > p
        side = mirrored.astype(jnp.int32)
        kj = jnp.where(mirrored, step_index - p - 1, step_index)
        q_tile = jnp.where(mirrored, nq - 1 - p, p)
        bias_index = jnp.where(kj == q_tile, 2, jnp.where(kj == q_tile - 1, 1, 0))
        return side, kj, bias_index

    def apply_to_values(kj, pt):
        pv = jnp.dot(vt_ref[kj * gran, 0], pt[:_KEY_GRANULE], preferred_element_type=_F32)
        for g in range(1, gran):
            pv = pv + jnp.dot(vt_ref[kj * gran + g, 0],
                              pt[g * _KEY_GRANULE:(g + 1) * _KEY_GRANULE],
                              preferred_element_type=_F32)
        return pv

    def key_tile(kj):
        return k_ref[0, pl.ds(pl.multiple_of(kj * t, t), t), :]

    def bounded_step(step_index):
        side, kj, bias_index = schedule(step_index)
        st = jnp.dot(key_tile(kj), qw_ref[side], preferred_element_type=_F32)
        bias = bias_ref[0, bias_index]
        pt = jnp.concatenate([jnp.exp2(st[:, :t] + bias).astype(_BF16),
                              jnp.exp2(st[:, t:] + bias).astype(_BF16)], axis=1)
        acc_ref[side] += apply_to_values(kj, pt)

    def online_step(step_index):
        side, kj, bias_index = schedule(step_index)
        kt = key_tile(kj)
        bias = bias_ref[0, bias_index]
        for s in range(2):
            lanes = slice(s * t, (s + 1) * t)
            st = jnp.dot(kt, qw_ref[side, :, lanes], preferred_element_type=_F32) + bias
            m_old = m_ref[side, s]
            m_new = jnp.maximum(m_old, jnp.max(st, axis=0, keepdims=True))
            alpha = jnp.exp2(m_old - m_new)
            pt = jnp.exp2(st - m_new).astype(_BF16)
            acc_ref[side, :, lanes] = alpha * acc_ref[side, :, lanes] + apply_to_values(kj, pt)
            m_ref[side, s] = m_new

    bounded = flag_ref[0, 0] > 0

    @pl.when(bounded)
    def _():
        for step_index in range(nq + 1):
            bounded_step(step_index)

    @pl.when(jnp.logical_not(bounded))
    def _():
        m_ref[...] = jnp.full(m_ref.shape, -jnp.inf, _F32)
        for step_index in range(nq + 1):
            online_step(step_index)

    for side in range(2):
        acc = acc_ref[side]
        o = acc[:_V_DIM] / acc[_V_DIM:_V_DIM + 1]
        od = o[:, :t] - lam_ref[0:1, 0:1] * o[:, t:]
        scale = lax.rsqrt(jnp.mean(od * od, axis=0, keepdims=True) + _EPS)
        o_ref[side, 0, 0] = (od * scale * gain_ref[...]).T.astype(_BF16)


def _attention(flag, qt, k, vt, bias, lam, gain_t):
    b, l, w = k.shape
    t = _ATTN_TILE
    nq = l // t
    vrows = vt.shape[2]
    return pl.pallas_call(
        functools.partial(_attn_body, t=t, nq=nq),
        grid=(b, _HEADS, nq // 2),
        in_specs=[pl.BlockSpec(memory_space=pltpu.SMEM),
                  pl.BlockSpec((_V_DIM, t), lambda bi, h, p: (h, bi * nq + p)),
                  pl.BlockSpec((_V_DIM, t), lambda bi, h, p: (h, bi * nq + nq - 1 - p)),
                  pl.BlockSpec((1, l, _V_DIM), lambda bi, h, p: (bi, 0, h)),
                  pl.BlockSpec((l // _KEY_GRANULE, 1, vrows, _KEY_GRANULE),
                               lambda bi, h, p: (bi, h, 0, 0)),
                  pl.BlockSpec((1, 3, t, t), lambda bi, h, p: (h, 0, 0, 0)),
                  pl.BlockSpec(lam.shape, lambda bi, h, p: (0, 0)),
                  pl.BlockSpec(gain_t.shape, lambda bi, h, p: (0, 0))],
        out_specs=pl.BlockSpec((2, 1, 1, t, _V_DIM), lambda bi, h, p: (0, bi, p, 0, h)),
        out_shape=jax.ShapeDtypeStruct((2, b, nq // 2, t, w), _BF16),
        scratch_shapes=[pltpu.VMEM((2, _V_DIM, 2 * t), _BF16),
                        pltpu.VMEM((2, 2, 1, t), _F32),
                        pltpu.VMEM((2, vrows, 2 * t), _F32)],
        compiler_params=pltpu.CompilerParams(
            dimension_semantics=("parallel", "parallel", "arbitrary"),
            vmem_limit_bytes=_VMEM_LIMIT),
        name="diff_attention",
    )(flag, qt, qt, k, vt, bias, lam, gain_t)


def _s5_prep_body(are_ref, aim_ref, ldt_ref, bre_ref, bim_ref,
                  lbre_ref, lbim_ref, bbre_ref, bbim_ref):
    a_re = are_ref[...]
    a_im = aim_ref[...]
    dt = jnp.exp(ldt_ref[...])
    decay = jnp.exp(a_re * dt)
    lb_re = decay * jnp.cos(a_im * dt)
    lb_im = decay * jnp.sin(a_im * dt)
    nr = lb_re - 1.0
    ni = lb_im
    den = a_re * a_re + a_im * a_im
    q_re = (nr * a_re + ni * a_im) / den
    q_im = (ni * a_re - nr * a_im) / den
    b_re = bre_ref[...]
    b_im = bim_ref[...]
    bbre_ref[...] = q_re * b_re - q_im * b_im
    bbim_ref[...] = q_re * b_im + q_im * b_re
    lbre_ref[...] = lb_re
    lbim_ref[...] = lb_im


def _s5_prep(a_re, a_im, log_dt, b_re, b_im):
    sd = jax.ShapeDtypeStruct(a_re.shape, _F32)
    return pl.pallas_call(_s5_prep_body, out_shape=[sd] * 4, name="s5_discretise")(
        a_re, a_im, log_dt, b_re, b_im)


def _s5_body(u_ref, perm_ref, permt_ref, wbre_ref, wbim_ref, lbre_ref, lbim_ref,
             wcre_ref, wcim_ref, d_ref, gw_ref, gb_ref, o_ref,
             utm_a, bre_a, bim_a, utm_b, bre_b, bim_b, xre_ref, xim_ref, cre_ref, cim_ref,
             *, steps, nb):
    i = pl.program_id(0)
    rows = nb * steps
    width = u_ref.shape[-1]
    n_tiles, _, tile_w = lbre_ref.shape
    halves = wcre_ref.shape[0]
    tiles_per_half = n_tiles // halves
    hw_in = width // halves

    @pl.when(i == 0)
    def _():
        cre_ref[...] = jnp.zeros(cre_ref.shape, _F32)
        cim_ref[...] = jnp.zeros(cim_ref.shape, _F32)
        bre_b[...] = jnp.zeros(bre_b.shape, _F32)
        bim_b[...] = jnp.zeros(bim_b.shape, _F32)
        utm_b[...] = jnp.zeros(utm_b.shape, _BF16)

    def step(utm_new, bre_new, bim_new, utm_old, bre_old, bim_old):
        u_bm = u_ref[...].reshape(rows, width)
        u_tm = jnp.dot(perm_ref[...], u_bm, preferred_element_type=_F32).astype(_BF16)
        utm_new[...] = u_tm

        for hf in range(halves):
            uh = u_tm[:, hf * hw_in:(hf + 1) * hw_in]
            bu_r = jnp.dot(uh, wbre_ref[hf], preferred_element_type=_F32)
            bu_i = jnp.dot(uh, wbim_ref[hf], preferred_element_type=_F32)
            for q in range(tiles_per_half):
                lanes = slice(q * tile_w, (q + 1) * tile_w)
                bre_new[hf * tiles_per_half + q] = bu_r[:, lanes]
                bim_new[hf * tiles_per_half + q] = bu_i[:, lanes]

        for j0 in range(0, n_tiles, _SCAN_GROUP):
            group = range(j0, j0 + _SCAN_GROUP)
            a_r = [jnp.broadcast_to(lbre_ref[j], (nb, tile_w)) for j in group]
            a_i = [jnp.broadcast_to(lbim_ref[j], (nb, tile_w)) for j in group]
            x_r = [cre_ref[j] for j in group]
            x_i = [cim_ref[j] for j in group]
            for ti in range(steps):
                r = slice(ti * nb, (ti + 1) * nb)
                for q, j in enumerate(group):
                    n_r = a_r[q] * x_r[q] - a_i[q] * x_i[q] + bre_old[j, r, :]
                    n_i = a_r[q] * x_i[q] + a_i[q] * x_r[q] + bim_old[j, r, :]
                    xre_ref[j, r, :] = n_r
                    xim_ref[j, r, :] = n_i
                    x_r[q], x_i[q] = n_r, n_i
            for q, j in enumerate(group):
                cre_ref[j] = x_r[q]
                cim_ref[j] = x_i[q]

        ys = []
        for hf in range(halves):
            tiles = range(hf * tiles_per_half, (hf + 1) * tiles_per_half)
            x_r = jnp.concatenate([xre_ref[j] for j in tiles], axis=1).astype(_BF16)
            x_i = jnp.concatenate([xim_ref[j] for j in tiles], axis=1).astype(_BF16)
            ys.append(jnp.dot(x_r, wcre_ref[hf], preferred_element_type=_F32)
                      + jnp.dot(x_i, wcim_ref[hf], preferred_element_type=_F32))
        y = jnp.concatenate(ys, axis=-1) + d_ref[...] * utm_old[...].astype(_F32)
        g = jax.nn.gelu(y)
        o = g * jax.nn.sigmoid(jnp.dot(g.astype(_BF16), gw_ref[...], preferred_element_type=_F32)
                               + gb_ref[...])
        o_bm = jnp.dot(permt_ref[...], o.astype(_BF16), preferred_element_type=_F32)
        o_ref[...] = o_bm.astype(_BF16).reshape(o_ref.shape)

    @pl.when(i % 2 == 0)
    def _():
        step(utm_a, bre_a, bim_a, utm_b, bre_b, bim_b)

    @pl.when(i % 2 == 1)
    def _():
        step(utm_b, bre_b, bim_b, utm_a, bre_a, bim_a)


def _s5(u, perm, permt, wb_re, wb_im, lb_re, lb_im, wc_re, wc_im, d_vec, glu_w, glu_b):
    nb, l, width = u.shape
    steps = _SCAN_STEPS
    n_blocks = l // steps
    n_tiles, _, tile_w = lb_re.shape
    full = lambda a: pl.BlockSpec(a.shape, lambda i: (0,) * a.ndim)
    consts = (perm, permt, wb_re, wb_im, lb_re, lb_im, wc_re, wc_im, d_vec, glu_w, glu_b)
    utm_buf = pltpu.VMEM((nb * steps, width), _BF16)
    state_buf = pltpu.VMEM((n_tiles, nb * steps, tile_w), _F32)
    return pl.pallas_call(
        functools.partial(_s5_body, steps=steps, nb=nb),
        grid=(n_blocks + 1,),
        in_specs=[pl.BlockSpec((nb, steps, width), lambda i: (0, jnp.minimum(i, n_blocks - 1), 0))]
                 + [full(a) for a in consts],
        out_specs=pl.BlockSpec((nb, steps, width), lambda i: (0, jnp.maximum(i - 1, 0), 0)),
        out_shape=jax.ShapeDtypeStruct(u.shape, _BF16),
        scratch_shapes=[utm_buf, state_buf, state_buf, utm_buf, state_buf, state_buf,
                        state_buf, state_buf,
                        pltpu.VMEM((n_tiles, nb, tile_w), _F32),
                        pltpu.VMEM((n_tiles, nb, tile_w), _F32)],
        compiler_params=pltpu.CompilerParams(
            dimension_semantics=("arbitrary",), vmem_limit_bytes=_VMEM_LIMIT),
        name="s5_branch",
    )(u, *consts)


def _outproj_body(x_ref, ng_ref, wg_ref, mb_ref, *refs, width, n_oa):
    oa_refs = refs[:n_oa]
    os_ref, pa_ref, ps_ref, wo_ref, out_ref = refs[n_oa:]
    x = x_ref[...]
    d = x.shape[-1]
    h = (x * _rms_scale(x) * ng_ref[...]).astype(_BF16)
    zg = jnp.dot(h, wg_ref[...], preferred_element_type=_F32)
    attn = jnp.concatenate([r[0, 0, 0] for r in oa_refs], axis=0)
    o_a = (attn.astype(_F32) * jax.nn.silu(zg[:, :width])).astype(_BF16)
    o_s = (os_ref[...].astype(_F32) * jax.nn.silu(zg[:, width:2 * width])).astype(_BF16)
    p_a = jnp.dot(o_a, pa_ref[...], preferred_element_type=_F32)
    p_s = jnp.dot(o_s, ps_ref[...], preferred_element_type=_F32)
    g = jax.nn.sigmoid(zg[:, 2 * width:] + mb_ref[...])
    merged = g[:, :d] * p_a + g[:, d:] * p_s
    out_ref[...] = x + jnp.dot(merged.astype(_BF16), wo_ref[...], preferred_element_type=_F32)


def _outproj(x2, norm_gain, w_gates, merge_b, o_a, o_s, proj_a, proj_s, w_out, width):
    n, d = x2.shape
    tm = _TOKEN_TILE
    _, _, half, t, _ = o_a.shape
    nq = 2 * half
    per_tile = tm // t
    per_batch = nq // per_tile
    full = lambda a: pl.BlockSpec(a.shape, lambda i: (0,) * a.ndim)
    row = lambda a: pl.BlockSpec((tm, a.shape[-1]), lambda i: (i, 0))

    def mirrored_tile(sub):
        def index_map(i):
            j = (i % per_batch) * per_tile + sub
            return (j // half, i // per_batch, jnp.where(j < half, j, nq - 1 - j), 0, 0)
        return pl.BlockSpec((1, 1, 1, t, width), index_map)

    return pl.pallas_call(
        functools.partial(_outproj_body, width=width, n_oa=per_tile),
        grid=(n // tm,),
        in_specs=[row(x2), full(norm_gain), full(w_gates), full(merge_b)]
                 + [mirrored_tile(sub) for sub in range(per_tile)]
                 + [row(o_s), full(proj_a), full(proj_s), full(w_out)],
        out_specs=row(x2),
        out_shape=jax.ShapeDtypeStruct(x2.shape, x2.dtype),
        compiler_params=pltpu.CompilerParams(
            dimension_semantics=("parallel",), vmem_limit_bytes=_VMEM_LIMIT),
        name="outproj",
    )(x2, norm_gain, w_gates, merge_b, *([o_a] * per_tile), o_s, proj_a, proj_s, w_out)


def _block_diag_halves(blocks, halves):
    g, r, c = blocks.shape
    gh = g // halves
    eye = jnp.eye(gh, dtype=blocks.dtype)
    b = blocks.reshape(halves, gh, r, 1, c) * eye.reshape(1, gh, 1, gh, 1)
    return b.reshape(halves, gh * r, gh * c)


def _time_major_perm(nb, steps):
    r = np.arange(nb * steps)
    src = (r % nb) * steps + r // nb
    p = np.zeros((nb * steps, nb * steps), np.float32)
    p[r, src] = 1.0
    return p


def _layer(x, lam_init, norm_gain, w_in, merge_gate_b, q_norm_gain, k_norm_gain,
           lambda_q1, lambda_k1, lambda_q2, lambda_k2, diff_subln_gain, rel_bias_table,
           ssm_a_re, ssm_a_im, ssm_log_dt, ssm_b_re, ssm_b_im, ssm_c_re, ssm_c_im,
           ssm_d, ssm_glu_w, ssm_glu_b, proj_attn, proj_ssm, w_out):
    nb, l, d = x.shape
    n = nb * l
    aw = _HEADS * 2 * _SUB_DIM
    groups = ssm_a_re.shape[0]
    x2 = x.reshape(n, d)
    ng = norm_gain.reshape(1, d).astype(_F32)

    w_ku = jnp.concatenate([w_in[:, aw:2 * aw], w_in[:, 4 * aw:5 * aw]], axis=1).astype(_BF16)
    w_qvt = jnp.concatenate([w_in[:, :aw], w_in[:, 2 * aw:3 * aw]], axis=1).T.astype(_BF16)
    w_gates = jnp.concatenate([w_in[:, 3 * aw:4 * aw], w_in[:, 5 * aw:]], axis=1).astype(_BF16)
    gq = jnp.tile(q_norm_gain.astype(_F32), 2 * _HEADS) * (_SUB_DIM ** -0.5 * _LOG2E)
    gqt = jnp.broadcast_to(gq[:, None], (aw, _TOKEN_TILE))
    gk = jnp.tile(k_norm_gain.astype(_F32), 2 * _HEADS).reshape(1, aw)
    seg = np.arange(aw) // _SUB_DIM
    gsum = jnp.asarray((seg[:, None] == seg[None, :]).astype(np.float32), _BF16)
    qt, k, vt, u = _inproj(x2, ng, w_ku, w_qvt, gqt, gk, gsum, aw)

    lam4 = jnp.stack([lambda_q1, lambda_k1, lambda_q2, lambda_k2]).astype(_F32)
    qk_gains = jnp.stack([gq[:_SUB_DIM], gk[0, :_SUB_DIM]])
    bias, lam, flag = _bias_prep(rel_bias_table.astype(_F32), lam4, qk_gains, lam_init, _ATTN_TILE)
    subln = diff_subln_gain.astype(_F32) * (1.0 - lam_init)
    subln_t = jnp.broadcast_to(subln[:, None], (_V_DIM, _ATTN_TILE))
    shp = (nb, l, aw)
    o_a = _attention(flag, qt, k.reshape(shp), vt, bias, lam, subln_t)

    rep = lambda a: jnp.repeat(a.astype(_F32), _SSM_GROUP, axis=0)
    ldt = jnp.broadcast_to(ssm_log_dt.astype(_F32)[:, None], ssm_a_re.shape)
    bt = lambda a: a.astype(_F32).transpose(0, 2, 1).reshape(groups * _SSM_GROUP, _SSM_STATE)
    lb_re, lb_im, bb_re, bb_im = _s5_prep(rep(ssm_a_re), rep(ssm_a_im), rep(ldt),
                                          bt(ssm_b_re), bt(ssm_b_im))
    n_tiles = groups * _SSM_STATE // _STATE_TILE
    flat = lambda a: a[::_SSM_GROUP].reshape(n_tiles, 1, _STATE_TILE)
    gshape = (groups, _SSM_GROUP, _SSM_STATE)
    wb_re = _block_diag_halves(bb_re.reshape(gshape), 2).astype(_BF16)
    wb_im = _block_diag_halves(bb_im.reshape(gshape), 2).astype(_BF16)
    wc_re = _block_diag_halves(ssm_c_re.astype(_F32).transpose(0, 2, 1), 2).astype(_BF16)
    wc_im = _block_diag_halves(-ssm_c_im.astype(_F32).transpose(0, 2, 1), 2).astype(_BF16)
    perm = _time_major_perm(nb, _SCAN_STEPS)
    o_s = _s5(u.reshape(shp), jnp.asarray(perm, _BF16), jnp.asarray(perm.T, _BF16),
              wb_re, wb_im, flat(lb_re), flat(lb_im), wc_re, wc_im,
              ssm_d.astype(_F32).reshape(1, aw), ssm_glu_w.astype(_BF16),
              ssm_glu_b.astype(_F32).reshape(1, aw))

    out = _outproj(x2, ng, w_gates, merge_gate_b.astype(_F32).reshape(1, 2 * d),
                   o_a, o_s.reshape(n, aw),
                   proj_attn.astype(_BF16), proj_ssm.astype(_BF16), w_out.astype(_BF16), aw)
    return out.reshape(nb, l, d)


def kernel(x, norm_gain, w_in, merge_gate_b, q_norm_gain, k_norm_gain, lambda_q1, lambda_k1,
           lambda_q2, lambda_k2, diff_subln_gain, rel_bias_table, ssm_A_re, ssm_A_im, ssm_log_dt,
           ssm_B_re, ssm_B_im, ssm_C_re, ssm_C_im, ssm_D, ssm_glu_w, ssm_glu_b,
           proj_attn, proj_ssm, w_out):
    per_layer = (norm_gain, w_in, merge_gate_b, q_norm_gain, k_norm_gain, lambda_q1, lambda_k1,
                 lambda_q2, lambda_k2, diff_subln_gain)
    per_layer_tail = (ssm_A_re, ssm_A_im, ssm_log_dt, ssm_B_re, ssm_B_im, ssm_C_re, ssm_C_im,
                      ssm_D, ssm_glu_w, ssm_glu_b, proj_attn, proj_ssm, w_out)
    for layer in range(norm_gain.shape[0]):
        lam_init = 0.8 - 0.6 * math.exp(-0.3 * layer)
        x = _layer(x, lam_init, *(p[layer] for p in per_layer), rel_bias_table,
                   *(p[layer] for p in per_layer_tail))
    return x
```

```python
import functools
import math

import jax
import jax.numpy as jnp
import numpy as np
from jax import lax
from jax.experimental import pallas as pl
from jax.experimental.pallas import tpu as pltpu

_F32 = jnp.float32
_BF16 = jnp.bfloat16

_CHUNK = 64
_HEADS = 4
_SUB_DIM = 64
_V_DIM = 128
_SSM_GROUP = 16
_SSM_STATE = 64
_REL_BUCKETS = 32
_REL_MAX_DIST = 128
_EPS = 1e-6
_LOG2E = math.log2(math.e)
_SAFE_LOG2_RANGE = 100.0

_VMEM_LIMIT = 48 * 1024 * 1024
_TOKEN_TILE = 512
_ATTN_TILE = 512
_KEY_GRANULE = 256
_ONES_ROWS = 16
_BIAS_BLOCK = 128
_SCAN_STEPS = 64
_STATE_TILE = 256
_SCAN_GROUP = 4


def _rms_scale(x, eps=_EPS):
    return lax.rsqrt(jnp.mean(x * x, axis=-1, keepdims=True) + eps)


def _inproj_body(x_ref, ng_ref, wku_ref, wqvt_ref, gqt_ref, gk_ref, gsum_ref,
                 qt_ref, k_ref, vt_ref, u_ref, *, width):
    x = x_ref[...]
    tm = x.shape[0]
    h = (x * _rms_scale(x) * ng_ref[...]).astype(_BF16)
    z = jnp.dot(h, wku_ref[...], preferred_element_type=_F32)
    zt = lax.dot_general(wqvt_ref[...], h, (((1,), (1,)), ((), ())),
                         preferred_element_type=_F32)

    kk = z[:, :width]
    ss = jnp.dot((kk * kk).astype(_BF16), gsum_ref[...], preferred_element_type=_F32)
    k_ref[...] = (kk * lax.rsqrt(ss * (1.0 / _SUB_DIM) + _EPS) * gk_ref[...]).astype(_BF16)
    u_ref[...] = z[:, width:].astype(_BF16)

    qt = zt[:width].reshape(width // _SUB_DIM, _SUB_DIM, tm)
    ms = jnp.mean(qt * qt, axis=1, keepdims=True)
    qt_ref[...] = ((qt * lax.rsqrt(ms + _EPS)).reshape(width, tm) * gqt_ref[...]).astype(_BF16)

    vt = zt[width:].astype(_BF16)
    ones = jnp.ones((_ONES_ROWS, _KEY_GRANULE), _BF16)
    for j in range(tm // _KEY_GRANULE):
        for hd in range(_HEADS):
            vt_ref[j, hd, :_V_DIM, :] = vt[hd * _V_DIM:(hd + 1) * _V_DIM,
                                           j * _KEY_GRANULE:(j + 1) * _KEY_GRANULE]
            vt_ref[j, hd, _V_DIM:, :] = ones


def _inproj(x2, norm_gain, w_ku, w_qvt, gqt, gk, gsum, width):
    n, d = x2.shape
    tm = _TOKEN_TILE
    full = lambda a: pl.BlockSpec(a.shape, lambda i: (0,) * a.ndim)
    row_sd = jax.ShapeDtypeStruct((n, width), _BF16)
    row_spec = pl.BlockSpec((tm, width), lambda i: (i, 0))
    gran = tm // _KEY_GRANULE
    vrows = _V_DIM + _ONES_ROWS
    return pl.pallas_call(
        functools.partial(_inproj_body, width=width),
        grid=(n // tm,),
        in_specs=[pl.BlockSpec((tm, d), lambda i: (i, 0)), full(norm_gain), full(w_ku), full(w_qvt),
                  full(gqt), full(gk), full(gsum)],
        out_specs=[pl.BlockSpec((width, tm), lambda i: (0, i)), row_spec,
                   pl.BlockSpec((gran, _HEADS, vrows, _KEY_GRANULE), lambda i: (i, 0, 0, 0)), row_spec],
        out_shape=[jax.ShapeDtypeStruct((width, n), _BF16), row_sd,
                   jax.ShapeDtypeStruct((n // _KEY_GRANULE, _HEADS, vrows, _KEY_GRANULE), _BF16), row_sd],
        compiler_params=pltpu.CompilerParams(
            dimension_semantics=("parallel",), vmem_limit_bytes=_VMEM_LIMIT),
        name="inproj",
    )(x2, norm_gain, w_ku, w_qvt, gqt, gk, gsum)


def _t5_bucket_np(rel):
    nb = _REL_BUCKETS // 2
    max_exact = nb // 2
    side = np.where(rel > 0, nb, 0)
    n = np.abs(rel)
    nf = np.maximum(n, 1).astype(np.float32)
    large = max_exact + (np.log(nf / np.float32(max_exact)) / np.float32(math.log(_REL_MAX_DIST / max_exact))
                         * np.float32(nb - max_exact)).astype(np.int32)
    large = np.minimum(large, nb - 1)
    return side + np.where(n < max_exact, n, large)


def _bucket_blocks():
    assert _BIAS_BLOCK % _CHUNK == 0
    i = np.arange(_BIAS_BLOCK)[None, :]
    j = np.arange(_BIAS_BLOCK)[:, None]
    far_bucket = _REL_BUCKETS // 2 - 1
    assert (_t5_bucket_np(j - i - 2 * _BIAS_BLOCK) == far_bucket).all()
    diag = np.where((j // _CHUNK) <= (i // _CHUNK), _t5_bucket_np(j - i), -1)
    prev = _t5_bucket_np(j - i - _BIAS_BLOCK)
    return np.stack([diag, prev]).astype(np.int32)


def _bias_prep_body(tab_ref, tabv_ref, bucket_ref, lam4_ref, gains_ref,
                    bias_ref, lam_ref, flag_ref, *, lam_init, t):
    h = pl.program_id(0)
    far_bucket = _REL_BUCKETS // 2 - 1
    bkt = bucket_ref[...]
    far = tab_ref[far_bucket, h]
    val = jnp.full(bkt.shape, -jnp.inf, _F32)
    for b in range(_REL_BUCKETS):
        val = jnp.where(bkt == b, (tab_ref[b, h] - far) * _LOG2E, val)
    diag_block, prev_block = val[0], val[1]
    n_blk = t // _BIAS_BLOCK
    bias_ref[0, 0] = jnp.zeros((t, t), _F32)
    bias_ref[0, 1] = jnp.zeros((t, t), _F32)
    bias_ref[0, 1, t - _BIAS_BLOCK:, :_BIAS_BLOCK] = prev_block
    for bj in range(n_blk):
        rows = slice(bj * _BIAS_BLOCK, (bj + 1) * _BIAS_BLOCK)
        for bi in range(n_blk):
            cols = slice(bi * _BIAS_BLOCK, (bi + 1) * _BIAS_BLOCK)
            if bi == bj:
                block = diag_block
            elif bi == bj + 1:
                block = prev_block
            else:
                block = jnp.full((_BIAS_BLOCK, _BIAS_BLOCK), 0.0 if bi > bj else -jnp.inf, _F32)
            bias_ref[0, 2, rows, cols] = block
    l4 = lam4_ref[...]
    s1 = jnp.sum(l4[0:1] * l4[1:2], axis=-1, keepdims=True)
    s2 = jnp.sum(l4[2:3] * l4[3:4], axis=-1, keepdims=True)
    lam_ref[...] = jnp.broadcast_to(jnp.exp(s1) - jnp.exp(s2) + lam_init, lam_ref.shape)
    gmax = jnp.max(jnp.abs(gains_ref[...]), axis=-1, keepdims=True)
    qk_bound = (_SUB_DIM * 1.02) * gmax[0:1] * gmax[1:2]
    tv = tabv_ref[...]
    dev = jnp.abs(tv - tv[far_bucket:far_bucket + 1]) * _LOG2E
    bias_bound = jnp.max(jnp.max(dev, axis=-1, keepdims=True), axis=0, keepdims=True)
    flag_ref[...] = (qk_bound + bias_bound <= _SAFE_LOG2_RANGE).astype(jnp.int32)


def _bias_prep(rel_table, lam4, gains, lam_init, t):
    assert t % _BIAS_BLOCK == 0
    buckets = jnp.asarray(_bucket_blocks())
    whole = lambda a: pl.BlockSpec(a.shape, lambda h: (0,) * a.ndim)
    return pl.pallas_call(
        functools.partial(_bias_prep_body, lam_init=lam_init, t=t),
        grid=(_HEADS,),
        in_specs=[pl.BlockSpec(memory_space=pltpu.SMEM), whole(rel_table), whole(buckets),
                  whole(lam4), whole(gains)],
        out_specs=[pl.BlockSpec((1, 3, t, t), lambda h: (h, 0, 0, 0)),
                   pl.BlockSpec((1, _V_DIM), lambda h: (0, 0)),
                   pl.BlockSpec((1, 1), lambda h: (0, 0))],
        out_shape=[jax.ShapeDtypeStruct((_HEADS, 3, t, t), _F32),
                   jax.ShapeDtypeStruct((1, _V_DIM), _F32),
                   jax.ShapeDtypeStruct((1, 1), jnp.int32)],
        compiler_params=pltpu.CompilerParams(dimension_semantics=("arbitrary",)),
        name="attn_bias_prep",
    )(rel_table, rel_table, buckets, lam4, gains)


def _attn_body(flag_ref, qta_ref, qtb_ref, k_ref, vt_ref, bias_ref, lam_ref, gain_ref, o_ref,
               qw_ref, m_ref, acc_ref, *, t, nq):
    p = pl.program_id(2)
    gran = t // _KEY_GRANULE
    for side, qt_ref in enumerate((qta_ref, qtb_ref)):
        qt = qt_ref[...]
        row = lax.broadcasted_iota(jnp.int32, qt.shape, 0)
        zero = jnp.zeros_like(qt)
        qw_ref[side, :, :t] = jnp.where(row < _SUB_DIM, qt, zero)
        qw_ref[side, :, t:] = jnp.where(row >= _SUB_DIM, qt, zero)
    acc_ref[...] = jnp.zeros(acc_ref.shape, _F32)

    def schedule(step_index):
        mirrored = step_index > p
        side = mirrored.astype(jnp.int32)
        kj = jnp.where(mirrored, step_index - p - 1, p - step_index)
        q_tile = jnp.where(mirrored, nq - 1 - p, p)
        bias_index = jnp.where(kj == q_tile, 2, jnp.where(kj == q_tile - 1, 1, 0))
        return side, kj, bias_index

    def apply_to_values(kj, pt):
        pv = jnp.dot(vt_ref[kj * gran, 0], pt[:_KEY_GRANULE], preferred_element_type=_F32)
        for g in range(1, gran):
            pv = pv + jnp.dot(vt_ref[kj * gran + g, 0],
                              pt[g * _KEY_GRANULE:(g + 1) * _KEY_GRANULE],
                              preferred_element_type=_F32)
        return pv

    def key_tile(kj):
        return k_ref[0, pl.ds(pl.multiple_of(kj * t, t), t), :]

    def bounded_step(step_index):
        side, kj, bias_index = schedule(step_index)
        st = jnp.dot(key_tile(kj), qw_ref[side], preferred_element_type=_F32)
        bias = bias_ref[0, bias_index]
        pt = jnp.concatenate([jnp.exp2(st[:, :t] + bias).astype(_BF16),
                              jnp.exp2(st[:, t:] + bias).astype(_BF16)], axis=1)
        acc_ref[side] += apply_to_values(kj, pt)

    def bounded_diagonal_step(side, q_tile):
        assert gran == 2
        half = _KEY_GRANULE
        late = [slice(half, t), slice(t + half, 2 * t)]
        k0 = pl.multiple_of(q_tile * t, t)
        bias = bias_ref[0, 2]
        st0 = jnp.dot(k_ref[0, pl.ds(k0, half), :], qw_ref[side], preferred_element_type=_F32)
        pt0 = jnp.concatenate([jnp.exp2(st0[:, :t] + bias[:half]).astype(_BF16),
                               jnp.exp2(st0[:, t:] + bias[:half]).astype(_BF16)], axis=1)
        qw_late = jnp.concatenate([qw_ref[side, :, lanes] for lanes in late], axis=1)
        st1 = jnp.dot(k_ref[0, pl.ds(k0 + half, half), :], qw_late, preferred_element_type=_F32)
        pt1 = jnp.concatenate([jnp.exp2(st1[:, :half] + bias[half:, half:]).astype(_BF16),
                               jnp.exp2(st1[:, half:] + bias[half:, half:]).astype(_BF16)], axis=1)
        acc_ref[side] += jnp.dot(vt_ref[q_tile * gran, 0], pt0, preferred_element_type=_F32)
        pv1 = jnp.dot(vt_ref[q_tile * gran + 1, 0], pt1, preferred_element_type=_F32)
        for n, lanes in enumerate(late):
            acc_ref[side, :, lanes] += pv1[:, n * half:(n + 1) * half]

    def online_step(step_index):
        side, kj, bias_index = schedule(step_index)
        kt = key_tile(kj)
        bias = bias_ref[0, bias_index]
        for s in range(2):
            lanes = slice(s * t, (s + 1) * t)
            st = jnp.dot(kt, qw_ref[side, :, lanes], preferred_element_type=_F32) + bias
            m_old = m_ref[side, s]
            m_new = jnp.maximum(m_old, jnp.max(st, axis=0, keepdims=True))
            alpha = jnp.exp2(m_old - m_new)
            pt = jnp.exp2(st - m_new).astype(_BF16)
            acc_ref[side, :, lanes] = alpha * acc_ref[side, :, lanes] + apply_to_values(kj, pt)
            m_ref[side, s] = m_new

    bounded = flag_ref[0, 0] > 0

    @pl.when(bounded)
    def _():
        bounded_diagonal_step(0, p)
        for step_index in range(1, nq):
            bounded_step(step_index)
        bounded_diagonal_step(1, nq - 1 - p)

    @pl.when(jnp.logical_not(bounded))
    def _():
        m_ref[...] = jnp.full(m_ref.shape, -jnp.inf, _F32)
        for step_index in range(nq + 1):
            online_step(step_index)

    for side in range(2):
        acc = acc_ref[side]
        o = acc[:_V_DIM] / acc[_V_DIM:_V_DIM + 1]
        od = o[:, :t] - lam_ref[0:1, 0:1] * o[:, t:]
        scale = lax.rsqrt(jnp.mean(od * od, axis=0, keepdims=True) + _EPS)
        o_ref[side, 0, 0] = (od * scale * gain_ref[...]).T.astype(_BF16)


def _attention(flag, qt, k, vt, bias, lam, gain_t):
    b, l, w = k.shape
    t = _ATTN_TILE
    nq = l // t
    vrows = vt.shape[2]
    return pl.pallas_call(
        functools.partial(_attn_body, t=t, nq=nq),
        grid=(b, _HEADS, nq // 2),
        in_specs=[pl.BlockSpec(memory_space=pltpu.SMEM),
                  pl.BlockSpec((_V_DIM, t), lambda bi, h, p: (h, bi * nq + p)),
                  pl.BlockSpec((_V_DIM, t), lambda bi, h, p: (h, bi * nq + nq - 1 - p)),
                  pl.BlockSpec((1, l, _V_DIM), lambda bi, h, p: (bi, 0, h)),
                  pl.BlockSpec((l // _KEY_GRANULE, 1, vrows, _KEY_GRANULE),
                               lambda bi, h, p: (bi, h, 0, 0)),
                  pl.BlockSpec((1, 3, t, t), lambda bi, h, p: (h, 0, 0, 0)),
                  pl.BlockSpec(lam.shape, lambda bi, h, p: (0, 0)),
                  pl.BlockSpec(gain_t.shape, lambda bi, h, p: (0, 0))],
        out_specs=pl.BlockSpec((2, 1, 1, t, _V_DIM), lambda bi, h, p: (0, bi, p, 0, h)),
        out_shape=jax.ShapeDtypeStruct((2, b, nq // 2, t, w), _BF16),
        scratch_shapes=[pltpu.VMEM((2, _V_DIM, 2 * t), _BF16),
                        pltpu.VMEM((2, 2, 1, t), _F32),
                        pltpu.VMEM((2, vrows, 2 * t), _F32)],
        compiler_params=pltpu.CompilerParams(
            dimension_semantics=("parallel", "parallel", "arbitrary"),
            vmem_limit_bytes=_VMEM_LIMIT),
        name="diff_attention",
    )(flag, qt, qt, k, vt, bias, lam, gain_t)


def _s5_prep_body(are_ref, aim_ref, ldt_ref, bre_ref, bim_ref,
                  lbre_ref, lbim_ref, bbre_ref, bbim_ref):
    a_re = are_ref[...]
    a_im = aim_ref[...]
    dt = jnp.exp(ldt_ref[...])
    decay = jnp.exp(a_re * dt)
    lb_re = decay * jnp.cos(a_im * dt)
    lb_im = decay * jnp.sin(a_im * dt)
    nr = lb_re - 1.0
    ni = lb_im
    den = a_re * a_re + a_im * a_im
    q_re = (nr * a_re + ni * a_im) / den
    q_im = (ni * a_re - nr * a_im) / den
    b_re = bre_ref[...]
    b_im = bim_ref[...]
    bbre_ref[...] = q_re * b_re - q_im * b_im
    bbim_ref[...] = q_re * b_im + q_im * b_re
    lbre_ref[...] = lb_re
    lbim_ref[...] = lb_im


def _s5_prep(a_re, a_im, log_dt, b_re, b_im):
    sd = jax.ShapeDtypeStruct(a_re.shape, _F32)
    return pl.pallas_call(_s5_prep_body, out_shape=[sd] * 4, name="s5_discretise")(
        a_re, a_im, log_dt, b_re, b_im)


def _s5_body(u_ref, perm_ref, permt_ref, wbre_ref, wbim_ref, lbre_ref, lbim_ref,
             wcre_ref, wcim_ref, d_ref, gw_ref, gb_ref, o_ref,
             utm_a, bre_a, bim_a, utm_b, bre_b, bim_b, xre_ref, xim_ref, cre_ref, cim_ref,
             *, steps, nb):
    i = pl.program_id(0)
    rows = nb * steps
    width = u_ref.shape[-1]
    n_tiles, _, tile_w = lbre_ref.shape
    halves = wcre_ref.shape[0]
    tiles_per_half = n_tiles // halves
    hw_in = width // halves

    @pl.when(i == 0)
    def _():
        cre_ref[...] = jnp.zeros(cre_ref.shape, _F32)
        cim_ref[...] = jnp.zeros(cim_ref.shape, _F32)
        bre_b[...] = jnp.zeros(bre_b.shape, _F32)
        bim_b[...] = jnp.zeros(bim_b.shape, _F32)
        utm_b[...] = jnp.zeros(utm_b.shape, _BF16)

    def step(utm_new, bre_new, bim_new, utm_old, bre_old, bim_old):
        u_bm = u_ref[...].reshape(rows, width)
        u_tm = jnp.dot(perm_ref[...], u_bm, preferred_element_type=_F32).astype(_BF16)
        utm_new[...] = u_tm

        for hf in range(halves):
            uh = u_tm[:, hf * hw_in:(hf + 1) * hw_in]
            bu_r = jnp.dot(uh, wbre_ref[hf], preferred_element_type=_F32)
            bu_i = jnp.dot(uh, wbim_ref[hf], preferred_element_type=_F32)
            for q in range(tiles_per_half):
                lanes = slice(q * tile_w, (q + 1) * tile_w)
                bre_new[hf * tiles_per_half + q] = bu_r[:, lanes]
                bim_new[hf * tiles_per_half + q] = bu_i[:, lanes]

        for j0 in range(0, n_tiles, _SCAN_GROUP):
            group = range(j0, j0 + _SCAN_GROUP)
            a_r = [jnp.broadcast_to(lbre_ref[j], (nb, tile_w)) for j in group]
            a_i = [jnp.broadcast_to(lbim_ref[j], (nb, tile_w)) for j in group]
            x_r = [cre_ref[j] for j in group]
            x_i = [cim_ref[j] for j in group]
            for ti in range(steps):
                r = slice(ti * nb, (ti + 1) * nb)
                for q, j in enumerate(group):
                    n_r = a_r[q] * x_r[q] - a_i[q] * x_i[q] + bre_old[j, r, :]
                    n_i = a_r[q] * x_i[q] + a_i[q] * x_r[q] + bim_old[j, r, :]
                    xre_ref[j, r, :] = n_r
                    xim_ref[j, r, :] = n_i
                    x_r[q], x_i[q] = n_r, n_i
            for q, j in enumerate(group):
                cre_ref[j] = x_r[q]
                cim_ref[j] = x_i[q]

        ys = []
        for hf in range(halves):
            tiles = range(hf * tiles_per_half, (hf + 1) * tiles_per_half)
            x_r = jnp.concatenate([xre_ref[j] for j in tiles], axis=1).astype(_BF16)
            x_i = jnp.concatenate([xim_ref[j] for j in tiles], axis=1).astype(_BF16)
            ys.append(jnp.dot(x_r, wcre_ref[hf], preferred_element_type=_F32)
                      + jnp.dot(x_i, wcim_ref[hf], preferred_element_type=_F32))
        y = jnp.concatenate(ys, axis=-1) + d_ref[...] * utm_old[...].astype(_F32)
        g = jax.nn.gelu(y)
        o = g * jax.nn.sigmoid(jnp.dot(g.astype(_BF16), gw_ref[...], preferred_element_type=_F32)
                               + gb_ref[...])
        o_bm = jnp.dot(permt_ref[...], o.astype(_BF16), preferred_element_type=_F32)
        o_ref[...] = o_bm.astype(_BF16).reshape(o_ref.shape)

    @pl.when(i % 2 == 0)
    def _():
        step(utm_a, bre_a, bim_a, utm_b, bre_b, bim_b)

    @pl.when(i % 2 == 1)
    def _():
        step(utm_b, bre_b, bim_b, utm_a, bre_a, bim_a)


def _s5(u, perm, permt, wb_re, wb_im, lb_re, lb_im, wc_re, wc_im, d_vec, glu_w, glu_b):
    nb, l, width = u.shape
    steps = _SCAN_STEPS
    n_blocks = l // steps
    n_tiles, _, tile_w = lb_re.shape
    full = lambda a: pl.BlockSpec(a.shape, lambda i: (0,) * a.ndim)
    consts = (perm, permt, wb_re, wb_im, lb_re, lb_im, wc_re, wc_im, d_vec, glu_w, glu_b)
    utm_buf = pltpu.VMEM((nb * steps, width), _BF16)
    state_buf = pltpu.VMEM((n_tiles, nb * steps, tile_w), _F32)
    return pl.pallas_call(
        functools.partial(_s5_body, steps=steps, nb=nb),
        grid=(n_blocks + 1,),
        in_specs=[pl.BlockSpec((nb, steps, width), lambda i: (0, jnp.minimum(i, n_blocks - 1), 0))]
                 + [full(a) for a in consts],
        out_specs=pl.BlockSpec((nb, steps, width), lambda i: (0, jnp.maximum(i - 1, 0), 0)),
        out_shape=jax.ShapeDtypeStruct(u.shape, _BF16),
        scratch_shapes=[utm_buf, state_buf, state_buf, utm_buf, state_buf, state_buf,
                        state_buf, state_buf,
                        pltpu.VMEM((n_tiles, nb, tile_w), _F32),
                        pltpu.VMEM((n_tiles, nb, tile_w), _F32)],
        compiler_params=pltpu.CompilerParams(
            dimension_semantics=("arbitrary",), vmem_limit_bytes=_VMEM_LIMIT),
        name="s5_branch",
    )(u, *consts)


def _outproj_body(x_ref, ng_ref, wg_ref, mb_ref, *refs, width, n_oa):
    oa_refs = refs[:n_oa]
    os_ref, pa_ref, ps_ref, wo_ref, out_ref = refs[n_oa:]
    x = x_ref[...]
    d = x.shape[-1]
    h = (x * _rms_scale(x) * ng_ref[...]).astype(_BF16)
    zg = jnp.dot(h, wg_ref[...], preferred_element_type=_F32)
    attn = jnp.concatenate([r[0, 0, 0] for r in oa_refs], axis=0)
    o_a = (attn.astype(_F32) * jax.nn.silu(zg[:, :width])).astype(_BF16)
    o_s = (os_ref[...].astype(_F32) * jax.nn.silu(zg[:, width:2 * width])).astype(_BF16)
    p_a = jnp.dot(o_a, pa_ref[...], preferred_element_type=_F32)
    p_s = jnp.dot(o_s, ps_ref[...], preferred_element_type=_F32)
    g = jax.nn.sigmoid(zg[:, 2 * width:] + mb_ref[...])
    merged = g[:, :d] * p_a + g[:, d:] * p_s
    out_ref[...] = x + jnp.dot(merged.astype(_BF16), wo_ref[...], preferred_element_type=_F32)


def _outproj(x2, norm_gain, w_gates, merge_b, o_a, o_s, proj_a, proj_s, w_out, width):
    n, d = x2.shape
    tm = _TOKEN_TILE
    _, _, half, t, _ = o_a.shape
    nq = 2 * half
    per_tile = tm // t
    per_batch = nq // per_tile
    full = lambda a: pl.BlockSpec(a.shape, lambda i: (0,) * a.ndim)
    row = lambda a: pl.BlockSpec((tm, a.shape[-1]), lambda i: (i, 0))

    def mirrored_tile(sub):
        def index_map(i):
            j = (i % per_batch) * per_tile + sub
            return (j // half, i // per_batch, jnp.where(j < half, j, nq - 1 - j), 0, 0)
        return pl.BlockSpec((1, 1, 1, t, width), index_map)

    return pl.pallas_call(
        functools.partial(_outproj_body, width=width, n_oa=per_tile),
        grid=(n // tm,),
        in_specs=[row(x2), full(norm_gain), full(w_gates), full(merge_b)]
                 + [mirrored_tile(sub) for sub in range(per_tile)]
                 + [row(o_s), full(proj_a), full(proj_s), full(w_out)],
        out_specs=row(x2),
        out_shape=jax.ShapeDtypeStruct(x2.shape, x2.dtype),
        compiler_params=pltpu.CompilerParams(
            dimension_semantics=("parallel",), vmem_limit_bytes=_VMEM_LIMIT),
        name="outproj",
    )(x2, norm_gain, w_gates, merge_b, *([o_a] * per_tile), o_s, proj_a, proj_s, w_out)


def _block_diag_halves(blocks, halves):
    g, r, c = blocks.shape
    gh = g // halves
    eye = jnp.eye(gh, dtype=blocks.dtype)
    b = blocks.reshape(halves, gh, r, 1, c) * eye.reshape(1, gh, 1, gh, 1)
    return b.reshape(halves, gh * r, gh * c)


def _time_major_perm(nb, steps):
    r = np.arange(nb * steps)
    src = (r % nb) * steps + r // nb
    p = np.zeros((nb * steps, nb * steps), np.float32)
    p[r, src] = 1.0
    return p


def _layer(x, lam_init, norm_gain, w_in, merge_gate_b, q_norm_gain, k_norm_gain,
           lambda_q1, lambda_k1, lambda_q2, lambda_k2, diff_subln_gain, rel_bias_table,
           ssm_a_re, ssm_a_im, ssm_log_dt, ssm_b_re, ssm_b_im, ssm_c_re, ssm_c_im,
           ssm_d, ssm_glu_w, ssm_glu_b, proj_attn, proj_ssm, w_out):
    nb, l, d = x.shape
    n = nb * l
    aw = _HEADS * 2 * _SUB_DIM
    groups = ssm_a_re.shape[0]
    x2 = x.reshape(n, d)
    ng = norm_gain.reshape(1, d).astype(_F32)

    w_ku = jnp.concatenate([w_in[:, aw:2 * aw], w_in[:, 4 * aw:5 * aw]], axis=1).astype(_BF16)
    w_qvt = jnp.concatenate([w_in[:, :aw], w_in[:, 2 * aw:3 * aw]], axis=1).T.astype(_BF16)
    w_gates = jnp.concatenate([w_in[:, 3 * aw:4 * aw], w_in[:, 5 * aw:]], axis=1).astype(_BF16)
    gq = jnp.tile(q_norm_gain.astype(_F32), 2 * _HEADS) * (_SUB_DIM ** -0.5 * _LOG2E)
    gqt = jnp.broadcast_to(gq[:, None], (aw, _TOKEN_TILE))
    gk = jnp.tile(k_norm_gain.astype(_F32), 2 * _HEADS).reshape(1, aw)
    seg = np.arange(aw) // _SUB_DIM
    gsum = jnp.asarray((seg[:, None] == seg[None, :]).astype(np.float32), _BF16)
    qt, k, vt, u = _inproj(x2, ng, w_ku, w_qvt, gqt, gk, gsum, aw)

    lam4 = jnp.stack([lambda_q1, lambda_k1, lambda_q2, lambda_k2]).astype(_F32)
    qk_gains = jnp.stack([gq[:_SUB_DIM], gk[0, :_SUB_DIM]])
    bias, lam, flag = _bias_prep(rel_bias_table.astype(_F32), lam4, qk_gains, lam_init, _ATTN_TILE)
    subln = diff_subln_gain.astype(_F32) * (1.0 - lam_init)
    subln_t = jnp.broadcast_to(subln[:, None], (_V_DIM, _ATTN_TILE))
    shp = (nb, l, aw)
    o_a = _attention(flag, qt, k.reshape(shp), vt, bias, lam, subln_t)

    rep = lambda a: jnp.repeat(a.astype(_F32), _SSM_GROUP, axis=0)
    ldt = jnp.broadcast_to(ssm_log_dt.astype(_F32)[:, None], ssm_a_re.shape)
    bt = lambda a: a.astype(_F32).transpose(0, 2, 1).reshape(groups * _SSM_GROUP, _SSM_STATE)
    lb_re, lb_im, bb_re, bb_im = _s5_prep(rep(ssm_a_re), rep(ssm_a_im), rep(ldt),
                                          bt(ssm_b_re), bt(ssm_b_im))
    n_tiles = groups * _SSM_STATE // _STATE_TILE
    flat = lambda a: a[::_SSM_GROUP].reshape(n_tiles, 1, _STATE_TILE)
    gshape = (groups, _SSM_GROUP, _SSM_STATE)
    wb_re = _block_diag_halves(bb_re.reshape(gshape), 2).astype(_BF16)
    wb_im = _block_diag_halves(bb_im.reshape(gshape), 2).astype(_BF16)
    wc_re = _block_diag_halves(ssm_c_re.astype(_F32).transpose(0, 2, 1), 2).astype(_BF16)
    wc_im = _block_diag_halves(-ssm_c_im.astype(_F32).transpose(0, 2, 1), 2).astype(_BF16)
    perm = _time_major_perm(nb, _SCAN_STEPS)
    o_s = _s5(u.reshape(shp), jnp.asarray(perm, _BF16), jnp.asarray(perm.T, _BF16),
              wb_re, wb_im, flat(lb_re), flat(lb_im), wc_re, wc_im,
              ssm_d.astype(_F32).reshape(1, aw), ssm_glu_w.astype(_BF16),
              ssm_glu_b.astype(_F32).reshape(1, aw))

    out = _outproj(x2, ng, w_gates, merge_gate_b.astype(_F32).reshape(1, 2 * d),
                   o_a, o_s.reshape(n, aw),
                   proj_attn.astype(_BF16), proj_ssm.astype(_BF16), w_out.astype(_BF16), aw)
    return out.reshape(nb, l, d)


def kernel(x, norm_gain, w_in, merge_gate_b, q_norm_gain, k_norm_gain, lambda_q1, lambda_k1,
           lambda_q2, lambda_k2, diff_subln_gain, rel_bias_table, ssm_A_re, ssm_A_im, ssm_log_dt,
           ssm_B_re, ssm_B_im, ssm_C_re, ssm_C_im, ssm_D, ssm_glu_w, ssm_glu_b,
           proj_attn, proj_ssm, w_out):
    per_layer = (norm_gain, w_in, merge_gate_b, q_norm_gain, k_norm_gain, lambda_q1, lambda_k1,
                 lambda_q2, lambda_k2, diff_subln_gain)
    per_layer_tail = (ssm_A_re, ssm_A_im, ssm_log_dt, ssm_B_re, ssm_B_im, ssm_C_re, ssm_C_im,
                      ssm_D, ssm_glu_w, ssm_glu_b, proj_attn, proj_ssm, w_out)
    for layer in range(norm_gain.shape[0]):
        lam_init = 0.8 - 0.6 * math.exp(-0.3 * layer)
        x = _layer(x, lam_init, *(p[layer] for p in per_layer), rel_bias_table,
                   *(p[layer] for p in per_layer_tail))
    return x
```

```python
import functools
import math

import jax
import jax.numpy as jnp
import numpy as np
from jax import lax
from jax.experimental import pallas as pl
from jax.experimental.pallas import tpu as pltpu

_F32 = jnp.float32
_BF16 = jnp.bfloat16

_CHUNK = 64
_HEADS = 4
_SUB_DIM = 64
_V_DIM = 128
_SSM_GROUP = 16
_SSM_STATE = 64
_REL_BUCKETS = 32
_REL_MAX_DIST = 128
_EPS = 1e-6
_LOG2E = math.log2(math.e)
_SAFE_LOG2_RANGE = 100.0

_VMEM_LIMIT = 48 * 1024 * 1024
_TOKEN_TILE = 512
_ATTN_TILE = 512
_KEY_GRANULE = 256
_ONES_ROWS = 16
_BIAS_BLOCK = 128
_SCAN_STEPS = 64
_STATE_TILE = 256
_SCAN_GROUP = 4


def _rms_scale(x, eps=_EPS):
    return lax.rsqrt(jnp.mean(x * x, axis=-1, keepdims=True) + eps)


def _inproj_body(x_ref, ng_ref, wku_ref, wqvt_ref, gqt_ref, gk_ref, gsum_ref,
                 qt_ref, k_ref, vt_ref, u_ref, *, width):
    x = x_ref[...]
    tm = x.shape[0]
    h = (x * _rms_scale(x) * ng_ref[...]).astype(_BF16)
    z = jnp.dot(h, wku_ref[...], preferred_element_type=_F32)
    zt = lax.dot_general(wqvt_ref[...], h, (((1,), (1,)), ((), ())),
                         preferred_element_type=_F32)

    kk = z[:, :width]
    ss = jnp.dot((kk * kk).astype(_BF16), gsum_ref[...], preferred_element_type=_F32)
    k_ref[...] = (kk * lax.rsqrt(ss * (1.0 / _SUB_DIM) + _EPS) * gk_ref[...]).astype(_BF16)
    u_ref[...] = z[:, width:].astype(_BF16)

    qt = zt[:width].reshape(width // _SUB_DIM, _SUB_DIM, tm)
    ms = jnp.mean(qt * qt, axis=1, keepdims=True)
    qt_ref[...] = ((qt * lax.rsqrt(ms + _EPS)).reshape(width, tm) * gqt_ref[...]).astype(_BF16)

    vt = zt[width:].astype(_BF16)
    ones = jnp.ones((_ONES_ROWS, _KEY_GRANULE), _BF16)
    for j in range(tm // _KEY_GRANULE):
        for hd in range(_HEADS):
            vt_ref[j, hd, :_V_DIM, :] = vt[hd * _V_DIM:(hd + 1) * _V_DIM,
                                           j * _KEY_GRANULE:(j + 1) * _KEY_GRANULE]
            vt_ref[j, hd, _V_DIM:, :] = ones


def _inproj(x2, norm_gain, w_ku, w_qvt, gqt, gk, gsum, width):
    n, d = x2.shape
    tm = _TOKEN_TILE
    full = lambda a: pl.BlockSpec(a.shape, lambda i: (0,) * a.ndim)
    row_sd = jax.ShapeDtypeStruct((n, width), _BF16)
    row_spec = pl.BlockSpec((tm, width), lambda i: (i, 0))
    gran = tm // _KEY_GRANULE
    vrows = _V_DIM + _ONES_ROWS
    return pl.pallas_call(
        functools.partial(_inproj_body, width=width),
        grid=(n // tm,),
        in_specs=[pl.BlockSpec((tm, d), lambda i: (i, 0)), full(norm_gain), full(w_ku), full(w_qvt),
                  full(gqt), full(gk), full(gsum)],
        out_specs=[pl.BlockSpec((width, tm), lambda i: (0, i)), row_spec,
                   pl.BlockSpec((gran, _HEADS, vrows, _KEY_GRANULE), lambda i: (i, 0, 0, 0)), row_spec],
        out_shape=[jax.ShapeDtypeStruct((width, n), _BF16), row_sd,
                   jax.ShapeDtypeStruct((n // _KEY_GRANULE, _HEADS, vrows, _KEY_GRANULE), _BF16), row_sd],
        compiler_params=pltpu.CompilerParams(
            dimension_semantics=("parallel",), vmem_limit_bytes=_VMEM_LIMIT),
        name="inproj",
    )(x2, norm_gain, w_ku, w_qvt, gqt, gk, gsum)


def _t5_bucket_np(rel):
    nb = _REL_BUCKETS // 2
    max_exact = nb // 2
    side = np.where(rel > 0, nb, 0)
    n = np.abs(rel)
    nf = np.maximum(n, 1).astype(np.float32)
    large = max_exact + (np.log(nf / np.float32(max_exact)) / np.float32(math.log(_REL_MAX_DIST / max_exact))
                         * np.float32(nb - max_exact)).astype(np.int32)
    large = np.minimum(large, nb - 1)
    return side + np.where(n < max_exact, n, large)


def _bucket_blocks():
    assert _BIAS_BLOCK % _CHUNK == 0
    i = np.arange(_BIAS_BLOCK)[None, :]
    j = np.arange(_BIAS_BLOCK)[:, None]
    far_bucket = _REL_BUCKETS // 2 - 1
    assert (_t5_bucket_np(j - i - 2 * _BIAS_BLOCK) == far_bucket).all()
    diag = np.where((j // _CHUNK) <= (i // _CHUNK), _t5_bucket_np(j - i), -1)
    prev = _t5_bucket_np(j - i - _BIAS_BLOCK)
    return np.stack([diag, prev]).astype(np.int32)


def _bias_prep_body(tab_ref, tabv_ref, bucket_ref, lam4_ref, gains_ref,
                    bias_ref, lam_ref, flag_ref, *, lam_init, t):
    h = pl.program_id(0)
    far_bucket = _REL_BUCKETS // 2 - 1
    bkt = bucket_ref[...]
    far = tab_ref[far_bucket, h]
    val = jnp.full(bkt.shape, -jnp.inf, _F32)
    for b in range(_REL_BUCKETS):
        val = jnp.where(bkt == b, (tab_ref[b, h] - far) * _LOG2E, val)
    diag_block, prev_block = val[0], val[1]
    n_blk = t // _BIAS_BLOCK
    bias_ref[0, 0] = jnp.zeros((t, t), _F32)
    bias_ref[0, 1] = jnp.zeros((t, t), _F32)
    bias_ref[0, 1, t - _BIAS_BLOCK:, :_BIAS_BLOCK] = prev_block
    for bj in range(n_blk):
        rows = slice(bj * _BIAS_BLOCK, (bj + 1) * _BIAS_BLOCK)
        for bi in range(n_blk):
            cols = slice(bi * _BIAS_BLOCK, (bi + 1) * _BIAS_BLOCK)
            if bi == bj:
                block = diag_block
            elif bi == bj + 1:
                block = prev_block
            else:
                block = jnp.full((_BIAS_BLOCK, _BIAS_BLOCK), 0.0 if bi > bj else -jnp.inf, _F32)
            bias_ref[0, 2, rows, cols] = block
    l4 = lam4_ref[...]
    s1 = jnp.sum(l4[0:1] * l4[1:2], axis=-1, keepdims=True)
    s2 = jnp.sum(l4[2:3] * l4[3:4], axis=-1, keepdims=True)
    lam_ref[...] = jnp.broadcast_to(jnp.exp(s1) - jnp.exp(s2) + lam_init, lam_ref.shape)
    gmax = jnp.max(jnp.abs(gains_ref[...]), axis=-1, keepdims=True)
    qk_bound = (_SUB_DIM * 1.02) * gmax[0:1] * gmax[1:2]
    tv = tabv_ref[...]
    dev = jnp.abs(tv - tv[far_bucket:far_bucket + 1]) * _LOG2E
    bias_bound = jnp.max(jnp.max(dev, axis=-1, keepdims=True), axis=0, keepdims=True)
    flag_ref[...] = (qk_bound + bias_bound <= _SAFE_LOG2_RANGE).astype(jnp.int32)


def _bias_prep(rel_table, lam4, gains, lam_init, t):
    assert t % _BIAS_BLOCK == 0
    buckets = jnp.asarray(_bucket_blocks())
    whole = lambda a: pl.BlockSpec(a.shape, lambda h: (0,) * a.ndim)
    return pl.pallas_call(
        functools.partial(_bias_prep_body, lam_init=lam_init, t=t),
        grid=(_HEADS,),
        in_specs=[pl.BlockSpec(memory_space=pltpu.SMEM), whole(rel_table), whole(buckets),
                  whole(lam4), whole(gains)],
        out_specs=[pl.BlockSpec((1, 3, t, t), lambda h: (h, 0, 0, 0)),
                   pl.BlockSpec((1, _V_DIM), lambda h: (0, 0)),
                   pl.BlockSpec((1, 1), lambda h: (0, 0))],
        out_shape=[jax.ShapeDtypeStruct((_HEADS, 3, t, t), _F32),
                   jax.ShapeDtypeStruct((1, _V_DIM), _F32),
                   jax.ShapeDtypeStruct((1, 1), jnp.int32)],
        compiler_params=pltpu.CompilerParams(dimension_semantics=("arbitrary",)),
        name="attn_bias_prep",
    )(rel_table, rel_table, buckets, lam4, gains)


def _attn_body(flag_ref, qta_ref, qtb_ref, k_ref, vt_ref, bias_ref, lam_ref, gain_ref, o_ref,
               qw_ref, m_ref, acc_ref, *, t, nq):
    p = pl.program_id(2)
    gran = t // _KEY_GRANULE
    for side, qt_ref in enumerate((qta_ref, qtb_ref)):
        qt = qt_ref[...]
        row = lax.broadcasted_iota(jnp.int32, qt.shape, 0)
        zero = jnp.zeros_like(qt)
        qw_ref[side, :, :t] = jnp.where(row < _SUB_DIM, qt, zero)
        qw_ref[side, :, t:] = jnp.where(row >= _SUB_DIM, qt, zero)
    acc_ref[...] = jnp.zeros(acc_ref.shape, _F32)

    def schedule(step_index):
        mirrored = step_index > p
        side = mirrored.astype(jnp.int32)
        kj = jnp.where(mirrored, step_index - p - 1, p - step_index)
        q_tile = jnp.where(mirrored, nq - 1 - p, p)
        bias_index = jnp.where(kj == q_tile, 2, jnp.where(kj == q_tile - 1, 1, 0))
        return side, kj, bias_index

    def apply_to_values(kj, pt):
        pv = jnp.dot(vt_ref[kj * gran, 0], pt[:_KEY_GRANULE], preferred_element_type=_F32)
        for g in range(1, gran):
            pv = pv + jnp.dot(vt_ref[kj * gran + g, 0],
                              pt[g * _KEY_GRANULE:(g + 1) * _KEY_GRANULE],
                              preferred_element_type=_F32)
        return pv

    def key_tile(kj):
        return k_ref[0, pl.ds(pl.multiple_of(kj * t, t), t), :]

    def bounded_step(step_index):
        side, kj, bias_index = schedule(step_index)
        st = jnp.dot(key_tile(kj), qw_ref[side], preferred_element_type=_F32)
        bias = bias_ref[0, bias_index]
        pt = jnp.concatenate([jnp.exp2(st[:, :t] + bias).astype(_BF16),
                              jnp.exp2(st[:, t:] + bias).astype(_BF16)], axis=1)
        acc_ref[side] += apply_to_values(kj, pt)

    def bounded_diagonal_step(side, q_tile):
        assert gran == 2
        half = _KEY_GRANULE
        late = [slice(half, t), slice(t + half, 2 * t)]
        k0 = pl.multiple_of(q_tile * t, t)
        bias = bias_ref[0, 2]
        st0 = jnp.dot(k_ref[0, pl.ds(k0, half), :], qw_ref[side], preferred_element_type=_F32)
        pt0 = jnp.concatenate([jnp.exp2(st0[:, :t] + bias[:half]).astype(_BF16),
                               jnp.exp2(st0[:, t:] + bias[:half]).astype(_BF16)], axis=1)
        qw_late = jnp.concatenate([qw_ref[side, :, lanes] for lanes in late], axis=1)
        st1 = jnp.dot(k_ref[0, pl.ds(k0 + half, half), :], qw_late, preferred_element_type=_F32)
        pt1 = jnp.concatenate([jnp.exp2(st1[:, :half] + bias[half:, half:]).astype(_BF16),
                               jnp.exp2(st1[:, half:] + bias[half:, half:]).astype(_BF16)], axis=1)
        acc_ref[side] += jnp.dot(vt_ref[q_tile * gran, 0], pt0, preferred_element_type=_F32)
        pv1 = jnp.dot(vt_ref[q_tile * gran + 1, 0], pt1, preferred_element_type=_F32)
        for n, lanes in enumerate(late):
            acc_ref[side, :, lanes] += pv1[:, n * half:(n + 1) * half]

    def online_step(step_index):
        side, kj, bias_index = schedule(step_index)
        kt = key_tile(kj)
        bias = bias_ref[0, bias_index]
        for s in range(2):
            lanes = slice(s * t, (s + 1) * t)
            st = jnp.dot(kt, qw_ref[side, :, lanes], preferred_element_type=_F32) + bias
            m_old = m_ref[side, s]
            m_new = jnp.maximum(m_old, jnp.max(st, axis=0, keepdims=True))
            alpha = jnp.exp2(m_old - m_new)
            pt = jnp.exp2(st - m_new).astype(_BF16)
            acc_ref[side, :, lanes] = alpha * acc_ref[side, :, lanes] + apply_to_values(kj, pt)
            m_ref[side, s] = m_new

    bounded = flag_ref[0, 0] > 0

    @pl.when(bounded)
    def _():
        bounded_diagonal_step(0, p)
        for step_index in range(1, nq):
            bounded_step(step_index)
        bounded_diagonal_step(1, nq - 1 - p)

    @pl.when(jnp.logical_not(bounded))
    def _():
        m_ref[...] = jnp.full(m_ref.shape, -jnp.inf, _F32)
        for step_index in range(nq + 1):
            online_step(step_index)

    for side in range(2):
        acc = acc_ref[side]
        o = acc[:_V_DIM] / acc[_V_DIM:_V_DIM + 1]
        od = o[:, :t] - lam_ref[0:1, 0:1] * o[:, t:]
        scale = lax.rsqrt(jnp.mean(od * od, axis=0, keepdims=True) + _EPS)
        o_ref[side, 0, 0] = (od * scale * gain_ref[...]).T.astype(_BF16)


def _attention(flag, qt, k, vt, bias, lam, gain_t):
    b, l, w = k.shape
    t = _ATTN_TILE
    nq = l // t
    vrows = vt.shape[2]
    return pl.pallas_call(
        functools.partial(_attn_body, t=t, nq=nq),
        grid=(b, _HEADS, nq // 2),
        in_specs=[pl.BlockSpec(memory_space=pltpu.SMEM),
                  pl.BlockSpec((_V_DIM, t), lambda bi, h, p: (h, bi * nq + p)),
                  pl.BlockSpec((_V_DIM, t), lambda bi, h, p: (h, bi * nq + nq - 1 - p)),
                  pl.BlockSpec((1, l, _V_DIM), lambda bi, h, p: (bi, 0, h)),
                  pl.BlockSpec((l // _KEY_GRANULE, 1, vrows, _KEY_GRANULE),
                               lambda bi, h, p: (bi, h, 0, 0)),
                  pl.BlockSpec((1, 3, t, t), lambda bi, h, p: (h, 0, 0, 0)),
                  pl.BlockSpec(lam.shape, lambda bi, h, p: (0, 0)),
                  pl.BlockSpec(gain_t.shape, lambda bi, h, p: (0, 0))],
        out_specs=pl.BlockSpec((2, 1, 1, t, _V_DIM), lambda bi, h, p: (0, bi, p, 0, h)),
        out_shape=jax.ShapeDtypeStruct((2, b, nq // 2, t, w), _BF16),
        scratch_shapes=[pltpu.VMEM((2, _V_DIM, 2 * t), _BF16),
                        pltpu.VMEM((2, 2, 1, t), _F32),
                        pltpu.VMEM((2, vrows, 2 * t), _F32)],
        compiler_params=pltpu.CompilerParams(
            dimension_semantics=("parallel", "parallel", "arbitrary"),
            vmem_limit_bytes=_VMEM_LIMIT),
        name="diff_attention",
    )(flag, qt, qt, k, vt, bias, lam, gain_t)


def _s5_prep_body(are_ref, aim_ref, ldt_ref, bre_ref, bim_ref,
                  lbre_ref, lbim_ref, bbre_ref, bbim_ref):
    a_re = are_ref[...]
    a_im = aim_ref[...]
    dt = jnp.exp(ldt_ref[...])
    decay = jnp.exp(a_re * dt)
    lb_re = decay * jnp.cos(a_im * dt)
    lb_im = decay * jnp.sin(a_im * dt)
    nr = lb_re - 1.0
    ni = lb_im
    den = a_re * a_re + a_im * a_im
    q_re = (nr * a_re + ni * a_im) / den
    q_im = (ni * a_re - nr * a_im) / den
    b_re = bre_ref[...]
    b_im = bim_ref[...]
    bbre_ref[...] = q_re * b_re - q_im * b_im
    bbim_ref[...] = q_re * b_im + q_im * b_re
    lbre_ref[...] = lb_re
    lbim_ref[...] = lb_im


def _s5_prep(a_re, a_im, log_dt, b_re, b_im):
    sd = jax.ShapeDtypeStruct(a_re.shape, _F32)
    return pl.pallas_call(_s5_prep_body, out_shape=[sd] * 4, name="s5_discretise")(
        a_re, a_im, log_dt, b_re, b_im)


def _s5_body(u_ref, perm_ref, permt_ref, wbre_ref, wbim_ref, lbre_ref, lbim_ref,
             wcre_ref, wcim_ref, d_ref, gw_ref, gb_ref, o_ref,
             xre_ref, xim_ref, cre_ref, cim_ref, *, steps, nb):
    rows = nb * steps
    width = u_ref.shape[-1]
    n_tiles, _, tile_w = lbre_ref.shape
    halves = wcre_ref.shape[0]
    tiles_per_half = n_tiles // halves
    hw_in = width // halves

    @pl.when(pl.program_id(0) == 0)
    def _():
        cre_ref[...] = jnp.zeros(cre_ref.shape, _F32)
        cim_ref[...] = jnp.zeros(cim_ref.shape, _F32)

    u_bm = u_ref[...].reshape(rows, width)
    u_tm = jnp.dot(perm_ref[...], u_bm, preferred_element_type=_F32).astype(_BF16)

    for hf in range(halves):
        uh = u_tm[:, hf * hw_in:(hf + 1) * hw_in]
        bu_r = jnp.dot(uh, wbre_ref[hf], preferred_element_type=_F32)
        bu_i = jnp.dot(uh, wbim_ref[hf], preferred_element_type=_F32)
        group = range(hf * tiles_per_half, (hf + 1) * tiles_per_half)
        a_r = [jnp.broadcast_to(lbre_ref[j], (nb, tile_w)) for j in group]
        a_i = [jnp.broadcast_to(lbim_ref[j], (nb, tile_w)) for j in group]
        x_r = [cre_ref[j] for j in group]
        x_i = [cim_ref[j] for j in group]
        for ti in range(steps):
            r = slice(ti * nb, (ti + 1) * nb)
            for q, j in enumerate(group):
                lanes = slice(q * tile_w, (q + 1) * tile_w)
                n_r = a_r[q] * x_r[q] - a_i[q] * x_i[q] + bu_r[r, lanes]
                n_i = a_r[q] * x_i[q] + a_i[q] * x_r[q] + bu_i[r, lanes]
                xre_ref[j, r, :] = n_r
                xim_ref[j, r, :] = n_i
                x_r[q], x_i[q] = n_r, n_i
        for q, j in enumerate(group):
            cre_ref[j] = x_r[q]
            cim_ref[j] = x_i[q]

    ys = []
    for hf in range(halves):
        tiles = range(hf * tiles_per_half, (hf + 1) * tiles_per_half)
        x_r = jnp.concatenate([xre_ref[j] for j in tiles], axis=1).astype(_BF16)
        x_i = jnp.concatenate([xim_ref[j] for j in tiles], axis=1).astype(_BF16)
        ys.append(jnp.dot(x_r, wcre_ref[hf], preferred_element_type=_F32)
                  + jnp.dot(x_i, wcim_ref[hf], preferred_element_type=_F32))
    y = jnp.concatenate(ys, axis=-1) + d_ref[...] * u_tm.astype(_F32)
    g = jax.nn.gelu(y)
    o = g * jax.nn.sigmoid(jnp.dot(g.astype(_BF16), gw_ref[...], preferred_element_type=_F32)
                           + gb_ref[...])
    o_bm = jnp.dot(permt_ref[...], o.astype(_BF16), preferred_element_type=_F32)
    o_ref[...] = o_bm.astype(_BF16).reshape(o_ref.shape)


def _s5(u, perm, permt, wb_re, wb_im, lb_re, lb_im, wc_re, wc_im, d_vec, glu_w, glu_b):
    nb, l, width = u.shape
    steps = _SCAN_STEPS
    n_tiles, _, tile_w = lb_re.shape
    full = lambda a: pl.BlockSpec(a.shape, lambda i: (0,) * a.ndim)
    io_spec = pl.BlockSpec((nb, steps, width), lambda i: (0, i, 0))
    consts = (perm, permt, wb_re, wb_im, lb_re, lb_im, wc_re, wc_im, d_vec, glu_w, glu_b)
    state_buf = pltpu.VMEM((n_tiles, nb * steps, tile_w), _F32)
    return pl.pallas_call(
        functools.partial(_s5_body, steps=steps, nb=nb),
        grid=(l // steps,),
        in_specs=[io_spec] + [full(a) for a in consts],
        out_specs=io_spec,
        out_shape=jax.ShapeDtypeStruct(u.shape, _BF16),
        scratch_shapes=[state_buf, state_buf,
                        pltpu.VMEM((n_tiles, nb, tile_w), _F32),
                        pltpu.VMEM((n_tiles, nb, tile_w), _F32)],
        compiler_params=pltpu.CompilerParams(
            dimension_semantics=("arbitrary",), vmem_limit_bytes=_VMEM_LIMIT),
        name="s5_branch",
    )(u, *consts)


def _outproj_body(x_ref, ng_ref, wg_ref, mb_ref, *refs, width, n_oa):
    oa_refs = refs[:n_oa]
    os_ref, pa_ref, ps_ref, wo_ref, out_ref = refs[n_oa:]
    x = x_ref[...]
    d = x.shape[-1]
    h = (x * _rms_scale(x) * ng_ref[...]).astype(_BF16)
    zg = jnp.dot(h, wg_ref[...], preferred_element_type=_F32)
    attn = jnp.concatenate([r[0, 0, 0] for r in oa_refs], axis=0)
    o_a = (attn.astype(_F32) * jax.nn.silu(zg[:, :width])).astype(_BF16)
    o_s = (os_ref[...].astype(_F32) * jax.nn.silu(zg[:, width:2 * width])).astype(_BF16)
    p_a = jnp.dot(o_a, pa_ref[...], preferred_element_type=_F32)
    p_s = jnp.dot(o_s, ps_ref[...], preferred_element_type=_F32)
    g = jax.nn.sigmoid(zg[:, 2 * width:] + mb_ref[...])
    merged = g[:, :d] * p_a + g[:, d:] * p_s
    out_ref[...] = x + jnp.dot(merged.astype(_BF16), wo_ref[...], preferred_element_type=_F32)


def _outproj(x2, norm_gain, w_gates, merge_b, o_a, o_s, proj_a, proj_s, w_out, width):
    n, d = x2.shape
    tm = _TOKEN_TILE
    _, _, half, t, _ = o_a.shape
    nq = 2 * half
    per_tile = tm // t
    per_batch = nq // per_tile
    full = lambda a: pl.BlockSpec(a.shape, lambda i: (0,) * a.ndim)
    row = lambda a: pl.BlockSpec((tm, a.shape[-1]), lambda i: (i, 0))

    def mirrored_tile(sub):
        def index_map(i):
            j = (i % per_batch) * per_tile + sub
            return (j // half, i // per_batch, jnp.where(j < half, j, nq - 1 - j), 0, 0)
        return pl.BlockSpec((1, 1, 1, t, width), index_map)

    return pl.pallas_call(
        functools.partial(_outproj_body, width=width, n_oa=per_tile),
        grid=(n // tm,),
        in_specs=[row(x2), full(norm_gain), full(w_gates), full(merge_b)]
                 + [mirrored_tile(sub) for sub in range(per_tile)]
                 + [row(o_s), full(proj_a), full(proj_s), full(w_out)],
        out_specs=row(x2),
        out_shape=jax.ShapeDtypeStruct(x2.shape, x2.dtype),
        compiler_params=pltpu.CompilerParams(
            dimension_semantics=("parallel",), vmem_limit_bytes=_VMEM_LIMIT),
        name="outproj",
    )(x2, norm_gain, w_gates, merge_b, *([o_a] * per_tile), o_s, proj_a, proj_s, w_out)


def _block_diag_halves(blocks, halves):
    g, r, c = blocks.shape
    gh = g // halves
    eye = jnp.eye(gh, dtype=blocks.dtype)
    b = blocks.reshape(halves, gh, r, 1, c) * eye.reshape(1, gh, 1, gh, 1)
    return b.reshape(halves, gh * r, gh * c)


def _time_major_perm(nb, steps):
    r = np.arange(nb * steps)
    src = (r % nb) * steps + r // nb
    p = np.zeros((nb * steps, nb * steps), np.float32)
    p[r, src] = 1.0
    return p


def _layer(x, lam_init, norm_gain, w_in, merge_gate_b, q_norm_gain, k_norm_gain,
           lambda_q1, lambda_k1, lambda_q2, lambda_k2, diff_subln_gain, rel_bias_table,
           ssm_a_re, ssm_a_im, ssm_log_dt, ssm_b_re, ssm_b_im, ssm_c_re, ssm_c_im,
           ssm_d, ssm_glu_w, ssm_glu_b, proj_attn, proj_ssm, w_out):
    nb, l, d = x.shape
    n = nb * l
    aw = _HEADS * 2 * _SUB_DIM
    groups = ssm_a_re.shape[0]
    x2 = x.reshape(n, d)
    ng = norm_gain.reshape(1, d).astype(_F32)

    w_ku = jnp.concatenate([w_in[:, aw:2 * aw], w_in[:, 4 * aw:5 * aw]], axis=1).astype(_BF16)
    w_qvt = jnp.concatenate([w_in[:, :aw], w_in[:, 2 * aw:3 * aw]], axis=1).T.astype(_BF16)
    w_gates = jnp.concatenate([w_in[:, 3 * aw:4 * aw], w_in[:, 5 * aw:]], axis=1).astype(_BF16)
    gq = jnp.tile(q_norm_gain.astype(_F32), 2 * _HEADS) * (_SUB_DIM ** -0.5 * _LOG2E)
    gqt = jnp.broadcast_to(gq[:, None], (aw, _TOKEN_TILE))
    gk = jnp.tile(k_norm_gain.astype(_F32), 2 * _HEADS).reshape(1, aw)
    seg = np.arange(aw) // _SUB_DIM
    gsum = jnp.asarray((seg[:, None] == seg[None, :]).astype(np.float32), _BF16)
    qt, k, vt, u = _inproj(x2, ng, w_ku, w_qvt, gqt, gk, gsum, aw)

    lam4 = jnp.stack([lambda_q1, lambda_k1, lambda_q2, lambda_k2]).astype(_F32)
    qk_gains = jnp.stack([gq[:_SUB_DIM], gk[0, :_SUB_DIM]])
    bias, lam, flag = _bias_prep(rel_bias_table.astype(_F32), lam4, qk_gains, lam_init, _ATTN_TILE)
    subln = diff_subln_gain.astype(_F32) * (1.0 - lam_init)
    subln_t = jnp.broadcast_to(subln[:, None], (_V_DIM, _ATTN_TILE))
    shp = (nb, l, aw)
    o_a = _attention(flag, qt, k.reshape(shp), vt, bias, lam, subln_t)

    rep = lambda a: jnp.repeat(a.astype(_F32), _SSM_GROUP, axis=0)
    ldt = jnp.broadcast_to(ssm_log_dt.astype(_F32)[:, None], ssm_a_re.shape)
    bt = lambda a: a.astype(_F32).transpose(0, 2, 1).reshape(groups * _SSM_GROUP, _SSM_STATE)
    lb_re, lb_im, bb_re, bb_im = _s5_prep(rep(ssm_a_re), rep(ssm_a_im), rep(ldt),
                                          bt(ssm_b_re), bt(ssm_b_im))
    n_tiles = groups * _SSM_STATE // _STATE_TILE
    flat = lambda a: a[::_SSM_GROUP].reshape(n_tiles, 1, _STATE_TILE)
    gshape = (groups, _SSM_GROUP, _SSM_STATE)
    wb_re = _block_diag_halves(bb_re.reshape(gshape), 2).astype(_BF16)
    wb_im = _block_diag_halves(bb_im.reshape(gshape), 2).astype(_BF16)
    wc_re = _block_diag_halves(ssm_c_re.astype(_F32).transpose(0, 2, 1), 2).astype(_BF16)
    wc_im = _block_diag_halves(-ssm_c_im.astype(_F32).transpose(0, 2, 1), 2).astype(_BF16)
    perm = _time_major_perm(nb, _SCAN_STEPS)
    o_s = _s5(u.reshape(shp), jnp.asarray(perm, _BF16), jnp.asarray(perm.T, _BF16),
              wb_re, wb_im, flat(lb_re), flat(lb_im), wc_re, wc_im,
              ssm_d.astype(_F32).reshape(1, aw), ssm_glu_w.astype(_BF16),
              ssm_glu_b.astype(_F32).reshape(1, aw))

    out = _outproj(x2, ng, w_gates, merge_gate_b.astype(_F32).reshape(1, 2 * d),
                   o_a, o_s.reshape(n, aw),
                   proj_attn.astype(_BF16), proj_ssm.astype(_BF16), w_out.astype(_BF16), aw)
    return out.reshape(nb, l, d)


def kernel(x, norm_gain, w_in, merge_gate_b, q_norm_gain, k_norm_gain, lambda_q1, lambda_k1,
           lambda_q2, lambda_k2, diff_subln_gain, rel_bias_table, ssm_A_re, ssm_A_im, ssm_log_dt,
           ssm_B_re, ssm_B_im, ssm_C_re, ssm_C_im, ssm_D, ssm_glu_w, ssm_glu_b,
           proj_attn, proj_ssm, w_out):
    per_layer = (norm_gain, w_in, merge_gate_b, q_norm_gain, k_norm_gain, lambda_q1, lambda_k1,
                 lambda_q2, lambda_k2, diff_subln_gain)
    per_layer_tail = (ssm_A_re, ssm_A_im, ssm_log_dt, ssm_B_re, ssm_B_im, ssm_C_re, ssm_C_im,
                      ssm_D, ssm_glu_w, ssm_glu_b, proj_attn, proj_ssm, w_out)
    for layer in range(norm_gain.shape[0]):
        lam_init = 0.8 - 0.6 * math.exp(-0.3 * layer)
        x = _layer(x, lam_init, *(p[layer] for p in per_layer), rel_bias_table,
                   *(p[layer] for p in per_layer_tail))
    return x
```

```python
import functools
import math

import jax
import jax.numpy as jnp
import numpy as np
from jax import lax
from jax.experimental import pallas as pl
from jax.experimental.pallas import tpu as pltpu

_F32 = jnp.float32
_BF16 = jnp.bfloat16

_CHUNK = 64
_HEADS = 4
_SUB_DIM = 64
_V_DIM = 128
_SSM_GROUP = 16
_SSM_STATE = 64
_REL_BUCKETS = 32
_REL_MAX_DIST = 128
_EPS = 1e-6
_LOG2E = math.log2(math.e)
_SAFE_LOG2_RANGE = 100.0

_VMEM_LIMIT = 48 * 1024 * 1024
_TOKEN_TILE = 1024
_ATTN_TILE = 512
_KEY_GRANULE = 256
_ONES_ROWS = 16
_BIAS_BLOCK = 128
_SCAN_STEPS = 64
_STATE_TILE = 256
_SCAN_GROUP = 4


def _rms_scale(x, eps=_EPS):
    return lax.rsqrt(jnp.mean(x * x, axis=-1, keepdims=True) + eps)


def _inproj_body(x_ref, ng_ref, wku_ref, wqvt_ref, gqt_ref, gk_ref, gsum_ref,
                 qt_ref, k_ref, vt_ref, u_ref, *, width):
    x = x_ref[...]
    tm = x.shape[0]
    h = (x * _rms_scale(x) * ng_ref[...]).astype(_BF16)
    z = jnp.dot(h, wku_ref[...], preferred_element_type=_F32)
    zt = lax.dot_general(wqvt_ref[...], h, (((1,), (1,)), ((), ())),
                         preferred_element_type=_F32)

    kk = z[:, :width]
    ss = jnp.dot((kk * kk).astype(_BF16), gsum_ref[...], preferred_element_type=_F32)
    k_ref[...] = (kk * lax.rsqrt(ss * (1.0 / _SUB_DIM) + _EPS) * gk_ref[...]).astype(_BF16)
    u_ref[...] = z[:, width:].astype(_BF16)

    qt = zt[:width].reshape(width // _SUB_DIM, _SUB_DIM, tm)
    ms = jnp.mean(qt * qt, axis=1, keepdims=True)
    qt_ref[...] = ((qt * lax.rsqrt(ms + _EPS)).reshape(width, tm) * gqt_ref[...]).astype(_BF16)

    vt = zt[width:].astype(_BF16)
    ones = jnp.ones((_ONES_ROWS, _KEY_GRANULE), _BF16)
    for j in range(tm // _KEY_GRANULE):
        for hd in range(_HEADS):
            vt_ref[j, hd, :_V_DIM, :] = vt[hd * _V_DIM:(hd + 1) * _V_DIM,
                                           j * _KEY_GRANULE:(j + 1) * _KEY_GRANULE]
            vt_ref[j, hd, _V_DIM:, :] = ones


def _inproj(x2, norm_gain, w_ku, w_qvt, gqt, gk, gsum, width):
    n, d = x2.shape
    tm = _TOKEN_TILE
    full = lambda a: pl.BlockSpec(a.shape, lambda i: (0,) * a.ndim)
    row_sd = jax.ShapeDtypeStruct((n, width), _BF16)
    row_spec = pl.BlockSpec((tm, width), lambda i: (i, 0))
    gran = tm // _KEY_GRANULE
    vrows = _V_DIM + _ONES_ROWS
    return pl.pallas_call(
        functools.partial(_inproj_body, width=width),
        grid=(n // tm,),
        in_specs=[pl.BlockSpec((tm, d), lambda i: (i, 0)), full(norm_gain), full(w_ku), full(w_qvt),
                  full(gqt), full(gk), full(gsum)],
        out_specs=[pl.BlockSpec((width, tm), lambda i: (0, i)), row_spec,
                   pl.BlockSpec((gran, _HEADS, vrows, _KEY_GRANULE), lambda i: (i, 0, 0, 0)), row_spec],
        out_shape=[jax.ShapeDtypeStruct((width, n), _BF16), row_sd,
                   jax.ShapeDtypeStruct((n // _KEY_GRANULE, _HEADS, vrows, _KEY_GRANULE), _BF16), row_sd],
        compiler_params=pltpu.CompilerParams(
            dimension_semantics=("parallel",), vmem_limit_bytes=_VMEM_LIMIT),
        name="inproj",
    )(x2, norm_gain, w_ku, w_qvt, gqt, gk, gsum)


def _t5_bucket_np(rel):
    nb = _REL_BUCKETS // 2
    max_exact = nb // 2
    side = np.where(rel > 0, nb, 0)
    n = np.abs(rel)
    nf = np.maximum(n, 1).astype(np.float32)
    large = max_exact + (np.log(nf / np.float32(max_exact)) / np.float32(math.log(_REL_MAX_DIST / max_exact))
                         * np.float32(nb - max_exact)).astype(np.int32)
    large = np.minimum(large, nb - 1)
    return side + np.where(n < max_exact, n, large)


def _bucket_blocks():
    assert _BIAS_BLOCK % _CHUNK == 0
    i = np.arange(_BIAS_BLOCK)[None, :]
    j = np.arange(_BIAS_BLOCK)[:, None]
    far_bucket = _REL_BUCKETS // 2 - 1
    assert (_t5_bucket_np(j - i - 2 * _BIAS_BLOCK) == far_bucket).all()
    diag = np.where((j // _CHUNK) <= (i // _CHUNK), _t5_bucket_np(j - i), -1)
    prev = _t5_bucket_np(j - i - _BIAS_BLOCK)
    return np.stack([diag, prev]).astype(np.int32)


def _bias_prep_body(tab_ref, tabv_ref, bucket_ref, lam4_ref, gains_ref,
                    bias_ref, lam_ref, flag_ref, *, lam_init, t):
    h = pl.program_id(0)
    far_bucket = _REL_BUCKETS // 2 - 1
    bkt = bucket_ref[...]
    far = tab_ref[far_bucket, h]
    val = jnp.full(bkt.shape, -jnp.inf, _F32)
    for b in range(_REL_BUCKETS):
        val = jnp.where(bkt == b, (tab_ref[b, h] - far) * _LOG2E, val)
    diag_block, prev_block = val[0], val[1]
    n_blk = t // _BIAS_BLOCK
    bias_ref[0, 0] = jnp.zeros((t, t), _F32)
    bias_ref[0, 1] = jnp.zeros((t, t), _F32)
    bias_ref[0, 1, t - _BIAS_BLOCK:, :_BIAS_BLOCK] = prev_block
    for bj in range(n_blk):
        rows = slice(bj * _BIAS_BLOCK, (bj + 1) * _BIAS_BLOCK)
        for bi in range(n_blk):
            cols = slice(bi * _BIAS_BLOCK, (bi + 1) * _BIAS_BLOCK)
            if bi == bj:
                block = diag_block
            elif bi == bj + 1:
                block = prev_block
            else:
                block = jnp.full((_BIAS_BLOCK, _BIAS_BLOCK), 0.0 if bi > bj else -jnp.inf, _F32)
            bias_ref[0, 2, rows, cols] = block
    l4 = lam4_ref[...]
    s1 = jnp.sum(l4[0:1] * l4[1:2], axis=-1, keepdims=True)
    s2 = jnp.sum(l4[2:3] * l4[3:4], axis=-1, keepdims=True)
    lam_ref[...] = jnp.broadcast_to(jnp.exp(s1) - jnp.exp(s2) + lam_init, lam_ref.shape)
    gmax = jnp.max(jnp.abs(gains_ref[...]), axis=-1, keepdims=True)
    qk_bound = (_SUB_DIM * 1.02) * gmax[0:1] * gmax[1:2]
    tv = tabv_ref[...]
    dev = jnp.abs(tv - tv[far_bucket:far_bucket + 1]) * _LOG2E
    bias_bound = jnp.max(jnp.max(dev, axis=-1, keepdims=True), axis=0, keepdims=True)
    flag_ref[...] = (qk_bound + bias_bound <= _SAFE_LOG2_RANGE).astype(jnp.int32)


def _bias_prep(rel_table, lam4, gains, lam_init, t):
    assert t % _BIAS_BLOCK == 0
    buckets = jnp.asarray(_bucket_blocks())
    whole = lambda a: pl.BlockSpec(a.shape, lambda h: (0,) * a.ndim)
    return pl.pallas_call(
        functools.partial(_bias_prep_body, lam_init=lam_init, t=t),
        grid=(_HEADS,),
        in_specs=[pl.BlockSpec(memory_space=pltpu.SMEM), whole(rel_table), whole(buckets),
                  whole(lam4), whole(gains)],
        out_specs=[pl.BlockSpec((1, 3, t, t), lambda h: (h, 0, 0, 0)),
                   pl.BlockSpec((1, _V_DIM), lambda h: (0, 0)),
                   pl.BlockSpec((1, 1), lambda h: (0, 0))],
        out_shape=[jax.ShapeDtypeStruct((_HEADS, 3, t, t), _F32),
                   jax.ShapeDtypeStruct((1, _V_DIM), _F32),
                   jax.ShapeDtypeStruct((1, 1), jnp.int32)],
        compiler_params=pltpu.CompilerParams(dimension_semantics=("arbitrary",)),
        name="attn_bias_prep",
    )(rel_table, rel_table, buckets, lam4, gains)


def _attn_body(flag_ref, qta_ref, qtb_ref, k_ref, vt_ref, bias_ref, lam_ref, gain_ref, o_ref,
               qw_ref, m_ref, acc_ref, *, t, nq):
    p = pl.program_id(2)
    gran = t // _KEY_GRANULE
    for side, qt_ref in enumerate((qta_ref, qtb_ref)):
        qt = qt_ref[...]
        row = lax.broadcasted_iota(jnp.int32, qt.shape, 0)
        zero = jnp.zeros_like(qt)
        qw_ref[side, :, :t] = jnp.where(row < _SUB_DIM, qt, zero)
        qw_ref[side, :, t:] = jnp.where(row >= _SUB_DIM, qt, zero)
    acc_ref[...] = jnp.zeros(acc_ref.shape, _F32)

    def schedule(step_index):
        mirrored = step_index > p
        side = mirrored.astype(jnp.int32)
        kj = jnp.where(mirrored, step_index - p - 1, p - step_index)
        q_tile = jnp.where(mirrored, nq - 1 - p, p)
        bias_index = jnp.where(kj == q_tile, 2, jnp.where(kj == q_tile - 1, 1, 0))
        return side, kj, bias_index

    def apply_to_values(kj, pt):
        pv = jnp.dot(vt_ref[kj * gran, 0], pt[:_KEY_GRANULE], preferred_element_type=_F32)
        for g in range(1, gran):
            pv = pv + jnp.dot(vt_ref[kj * gran + g, 0],
                              pt[g * _KEY_GRANULE:(g + 1) * _KEY_GRANULE],
                              preferred_element_type=_F32)
        return pv

    def key_tile(kj):
        return k_ref[0, pl.ds(pl.multiple_of(kj * t, t), t), :]

    def bounded_step(step_index):
        side, kj, bias_index = schedule(step_index)
        st = jnp.dot(key_tile(kj), qw_ref[side], preferred_element_type=_F32)
        bias = bias_ref[0, bias_index]
        pt = jnp.concatenate([jnp.exp2(st[:, :t] + bias).astype(_BF16),
                              jnp.exp2(st[:, t:] + bias).astype(_BF16)], axis=1)
        acc_ref[side] += apply_to_values(kj, pt)

    def bounded_diagonal_step(side, q_tile):
        assert gran == 2
        half = _KEY_GRANULE
        late = [slice(half, t), slice(t + half, 2 * t)]
        k0 = pl.multiple_of(q_tile * t, t)
        bias = bias_ref[0, 2]
        st0 = jnp.dot(k_ref[0, pl.ds(k0, half), :], qw_ref[side], preferred_element_type=_F32)
        pt0 = jnp.concatenate([jnp.exp2(st0[:, :t] + bias[:half]).astype(_BF16),
                               jnp.exp2(st0[:, t:] + bias[:half]).astype(_BF16)], axis=1)
        qw_late = jnp.concatenate([qw_ref[side, :, lanes] for lanes in late], axis=1)
        st1 = jnp.dot(k_ref[0, pl.ds(k0 + half, half), :], qw_late, preferred_element_type=_F32)
        pt1 = jnp.concatenate([jnp.exp2(st1[:, :half] + bias[half:, half:]).astype(_BF16),
                               jnp.exp2(st1[:, half:] + bias[half:, half:]).astype(_BF16)], axis=1)
        acc_ref[side] += jnp.dot(vt_ref[q_tile * gran, 0], pt0, preferred_element_type=_F32)
        pv1 = jnp.dot(vt_ref[q_tile * gran + 1, 0], pt1, preferred_element_type=_F32)
        for n, lanes in enumerate(late):
            acc_ref[side, :, lanes] += pv1[:, n * half:(n + 1) * half]

    def online_step(step_index):
        side, kj, bias_index = schedule(step_index)
        kt = key_tile(kj)
        bias = bias_ref[0, bias_index]
        for s in range(2):
            lanes = slice(s * t, (s + 1) * t)
            st = jnp.dot(kt, qw_ref[side, :, lanes], preferred_element_type=_F32) + bias
            m_old = m_ref[side, s]
            m_new = jnp.maximum(m_old, jnp.max(st, axis=0, keepdims=True))
            alpha = jnp.exp2(m_old - m_new)
            pt = jnp.exp2(st - m_new).astype(_BF16)
            acc_ref[side, :, lanes] = alpha * acc_ref[side, :, lanes] + apply_to_values(kj, pt)
            m_ref[side, s] = m_new

    bounded = flag_ref[0, 0] > 0

    @pl.when(bounded)
    def _():
        bounded_diagonal_step(0, p)
        for step_index in range(1, nq):
            bounded_step(step_index)
        bounded_diagonal_step(1, nq - 1 - p)

    @pl.when(jnp.logical_not(bounded))
    def _():
        m_ref[...] = jnp.full(m_ref.shape, -jnp.inf, _F32)
        for step_index in range(nq + 1):
            online_step(step_index)

    for side in range(2):
        acc = acc_ref[side]
        o = acc[:_V_DIM] / acc[_V_DIM:_V_DIM + 1]
        od = o[:, :t] - lam_ref[0:1, 0:1] * o[:, t:]
        scale = lax.rsqrt(jnp.mean(od * od, axis=0, keepdims=True) + _EPS)
        o_ref[side, 0, 0] = (od * scale * gain_ref[...]).T.astype(_BF16)


def _attention(flag, qt, k, vt, bias, lam, gain_t):
    b, l, w = k.shape
    t = _ATTN_TILE
    nq = l // t
    vrows = vt.shape[2]
    return pl.pallas_call(
        functools.partial(_attn_body, t=t, nq=nq),
        grid=(b, _HEADS, nq // 2),
        in_specs=[pl.BlockSpec(memory_space=pltpu.SMEM),
                  pl.BlockSpec((_V_DIM, t), lambda bi, h, p: (h, bi * nq + p)),
                  pl.BlockSpec((_V_DIM, t), lambda bi, h, p: (h, bi * nq + nq - 1 - p)),
                  pl.BlockSpec((1, l, _V_DIM), lambda bi, h, p: (bi, 0, h)),
                  pl.BlockSpec((l // _KEY_GRANULE, 1, vrows, _KEY_GRANULE),
                               lambda bi, h, p: (bi, h, 0, 0)),
                  pl.BlockSpec((1, 3, t, t), lambda bi, h, p: (h, 0, 0, 0)),
                  pl.BlockSpec(lam.shape, lambda bi, h, p: (0, 0)),
                  pl.BlockSpec(gain_t.shape, lambda bi, h, p: (0, 0))],
        out_specs=pl.BlockSpec((2, 1, 1, t, _V_DIM), lambda bi, h, p: (0, bi, p, 0, h)),
        out_shape=jax.ShapeDtypeStruct((2, b, nq // 2, t, w), _BF16),
        scratch_shapes=[pltpu.VMEM((2, _V_DIM, 2 * t), _BF16),
                        pltpu.VMEM((2, 2, 1, t), _F32),
                        pltpu.VMEM((2, vrows, 2 * t), _F32)],
        compiler_params=pltpu.CompilerParams(
            dimension_semantics=("parallel", "parallel", "arbitrary"),
            vmem_limit_bytes=_VMEM_LIMIT),
        name="diff_attention",
    )(flag, qt, qt, k, vt, bias, lam, gain_t)


def _s5_prep_body(are_ref, aim_ref, ldt_ref, bre_ref, bim_ref,
                  lbre_ref, lbim_ref, bbre_ref, bbim_ref):
    a_re = are_ref[...]
    a_im = aim_ref[...]
    dt = jnp.exp(ldt_ref[...])
    decay = jnp.exp(a_re * dt)
    lb_re = decay * jnp.cos(a_im * dt)
    lb_im = decay * jnp.sin(a_im * dt)
    nr = lb_re - 1.0
    ni = lb_im
    den = a_re * a_re + a_im * a_im
    q_re = (nr * a_re + ni * a_im) / den
    q_im = (ni * a_re - nr * a_im) / den
    b_re = bre_ref[...]
    b_im = bim_ref[...]
    bbre_ref[...] = q_re * b_re - q_im * b_im
    bbim_ref[...] = q_re * b_im + q_im * b_re
    lbre_ref[...] = lb_re
    lbim_ref[...] = lb_im


def _s5_prep(a_re, a_im, log_dt, b_re, b_im):
    sd = jax.ShapeDtypeStruct(a_re.shape, _F32)
    return pl.pallas_call(_s5_prep_body, out_shape=[sd] * 4, name="s5_discretise")(
        a_re, a_im, log_dt, b_re, b_im)


def _s5_body(u_ref, perm_ref, permt_ref, wbre_ref, wbim_ref, lbre_ref, lbim_ref,
             wcre_ref, wcim_ref, d_ref, gw_ref, gb_ref, o_ref,
             utm_a, bre_a, bim_a, utm_b, bre_b, bim_b, xre_ref, xim_ref, cre_ref, cim_ref,
             *, steps, nb):
    i = pl.program_id(0)
    rows = nb * steps
    width = u_ref.shape[-1]
    n_tiles, _, tile_w = lbre_ref.shape
    halves = wcre_ref.shape[0]
    tiles_per_half = n_tiles // halves
    hw_in = width // halves

    @pl.when(i == 0)
    def _():
        cre_ref[...] = jnp.zeros(cre_ref.shape, _F32)
        cim_ref[...] = jnp.zeros(cim_ref.shape, _F32)
        bre_b[...] = jnp.zeros(bre_b.shape, _F32)
        bim_b[...] = jnp.zeros(bim_b.shape, _F32)
        utm_b[...] = jnp.zeros(utm_b.shape, _BF16)

    def step(utm_new, bre_new, bim_new, utm_old, bre_old, bim_old):
        u_bm = u_ref[...].reshape(rows, width)
        u_tm = jnp.dot(perm_ref[...], u_bm, preferred_element_type=_F32).astype(_BF16)
        utm_new[...] = u_tm

        for hf in range(halves):
            uh = u_tm[:, hf * hw_in:(hf + 1) * hw_in]
            bu_r = jnp.dot(uh, wbre_ref[hf], preferred_element_type=_F32)
            bu_i = jnp.dot(uh, wbim_ref[hf], preferred_element_type=_F32)
            for q in range(tiles_per_half):
                lanes = slice(q * tile_w, (q + 1) * tile_w)
                bre_new[hf * tiles_per_half + q] = bu_r[:, lanes]
                bim_new[hf * tiles_per_half + q] = bu_i[:, lanes]

        for j0 in range(0, n_tiles, _SCAN_GROUP):
            group = range(j0, j0 + _SCAN_GROUP)
            a_r = [jnp.broadcast_to(lbre_ref[j], (nb, tile_w)) for j in group]
            a_i = [jnp.broadcast_to(lbim_ref[j], (nb, tile_w)) for j in group]
            x_r = [cre_ref[j] for j in group]
            x_i = [cim_ref[j] for j in group]
            for ti in range(steps):
                r = slice(ti * nb, (ti + 1) * nb)
                for q, j in enumerate(group):
                    n_r = a_r[q] * x_r[q] - a_i[q] * x_i[q] + bre_old[j, r, :]
                    n_i = a_r[q] * x_i[q] + a_i[q] * x_r[q] + bim_old[j, r, :]
                    xre_ref[j, r, :] = n_r
                    xim_ref[j, r, :] = n_i
                    x_r[q], x_i[q] = n_r, n_i
            for q, j in enumerate(group):
                cre_ref[j] = x_r[q]
                cim_ref[j] = x_i[q]

        ys = []
        for hf in range(halves):
            tiles = range(hf * tiles_per_half, (hf + 1) * tiles_per_half)
            x_r = jnp.concatenate([xre_ref[j] for j in tiles], axis=1).astype(_BF16)
            x_i = jnp.concatenate([xim_ref[j] for j in tiles], axis=1).astype(_BF16)
            ys.append(jnp.dot(x_r, wcre_ref[hf], preferred_element_type=_F32)
                      + jnp.dot(x_i, wcim_ref[hf], preferred_element_type=_F32))
        y = jnp.concatenate(ys, axis=-1) + d_ref[...] * utm_old[...].astype(_F32)
        g = jax.nn.gelu(y)
        o = g * jax.nn.sigmoid(jnp.dot(g.astype(_BF16), gw_ref[...], preferred_element_type=_F32)
                               + gb_ref[...])
        o_bm = jnp.dot(permt_ref[...], o.astype(_BF16), preferred_element_type=_F32)
        o_ref[...] = o_bm.astype(_BF16).reshape(o_ref.shape)

    @pl.when(i % 2 == 0)
    def _():
        step(utm_a, bre_a, bim_a, utm_b, bre_b, bim_b)

    @pl.when(i % 2 == 1)
    def _():
        step(utm_b, bre_b, bim_b, utm_a, bre_a, bim_a)


def _s5(u, perm, permt, wb_re, wb_im, lb_re, lb_im, wc_re, wc_im, d_vec, glu_w, glu_b):
    nb, l, width = u.shape
    steps = _SCAN_STEPS
    n_blocks = l // steps
    n_tiles, _, tile_w = lb_re.shape
    full = lambda a: pl.BlockSpec(a.shape, lambda i: (0,) * a.ndim)
    consts = (perm, permt, wb_re, wb_im, lb_re, lb_im, wc_re, wc_im, d_vec, glu_w, glu_b)
    utm_buf = pltpu.VMEM((nb * steps, width), _BF16)
    state_buf = pltpu.VMEM((n_tiles, nb * steps, tile_w), _F32)
    return pl.pallas_call(
        functools.partial(_s5_body, steps=steps, nb=nb),
        grid=(n_blocks + 1,),
        in_specs=[pl.BlockSpec((nb, steps, width), lambda i: (0, jnp.minimum(i, n_blocks - 1), 0))]
                 + [full(a) for a in consts],
        out_specs=pl.BlockSpec((nb, steps, width), lambda i: (0, jnp.maximum(i - 1, 0), 0)),
        out_shape=jax.ShapeDtypeStruct(u.shape, _BF16),
        scratch_shapes=[utm_buf, state_buf, state_buf, utm_buf, state_buf, state_buf,
                        state_buf, state_buf,
                        pltpu.VMEM((n_tiles, nb, tile_w), _F32),
                        pltpu.VMEM((n_tiles, nb, tile_w), _F32)],
        compiler_params=pltpu.CompilerParams(
            dimension_semantics=("arbitrary",), vmem_limit_bytes=_VMEM_LIMIT),
        name="s5_branch",
    )(u, *consts)


def _outproj_body(x_ref, ng_ref, wg_ref, mb_ref, *refs, width, n_oa):
    oa_refs = refs[:n_oa]
    os_ref, pa_ref, ps_ref, wo_ref, out_ref = refs[n_oa:]
    x = x_ref[...]
    d = x.shape[-1]
    h = (x * _rms_scale(x) * ng_ref[...]).astype(_BF16)
    zg = jnp.dot(h, wg_ref[...], preferred_element_type=_F32)
    attn = jnp.concatenate([r[0, 0, 0] for r in oa_refs], axis=0)
    o_a = (attn.astype(_F32) * jax.nn.silu(zg[:, :width])).astype(_BF16)
    o_s = (os_ref[...].astype(_F32) * jax.nn.silu(zg[:, width:2 * width])).astype(_BF16)
    p_a = jnp.dot(o_a, pa_ref[...], preferred_element_type=_F32)
    p_s = jnp.dot(o_s, ps_ref[...], preferred_element_type=_F32)
    g = jax.nn.sigmoid(zg[:, 2 * width:] + mb_ref[...])
    merged = g[:, :d] * p_a + g[:, d:] * p_s
    out_ref[...] = x + jnp.dot(merged.astype(_BF16), wo_ref[...], preferred_element_type=_F32)


def _outproj(x2, norm_gain, w_gates, merge_b, o_a, o_s, proj_a, proj_s, w_out, width):
    n, d = x2.shape
    tm = _TOKEN_TILE
    _, _, half, t, _ = o_a.shape
    nq = 2 * half
    per_tile = tm // t
    per_batch = nq // per_tile
    full = lambda a: pl.BlockSpec(a.shape, lambda i: (0,) * a.ndim)
    row = lambda a: pl.BlockSpec((tm, a.shape[-1]), lambda i: (i, 0))

    def mirrored_tile(sub):
        def index_map(i):
            j = (i % per_batch) * per_tile + sub
            return (j // half, i // per_batch, jnp.where(j < half, j, nq - 1 - j), 0, 0)
        return pl.BlockSpec((1, 1, 1, t, width), index_map)

    return pl.pallas_call(
        functools.partial(_outproj_body, width=width, n_oa=per_tile),
        grid=(n // tm,),
        in_specs=[row(x2), full(norm_gain), full(w_gates), full(merge_b)]
                 + [mirrored_tile(sub) for sub in range(per_tile)]
                 + [row(o_s), full(proj_a), full(proj_s), full(w_out)],
        out_specs=row(x2),
        out_shape=jax.ShapeDtypeStruct(x2.shape, x2.dtype),
        compiler_params=pltpu.CompilerParams(
            dimension_semantics=("parallel",), vmem_limit_bytes=_VMEM_LIMIT),
        name="outproj",
    )(x2, norm_gain, w_gates, merge_b, *([o_a] * per_tile), o_s, proj_a, proj_s, w_out)


def _block_diag_halves(blocks, halves):
    g, r, c = blocks.shape
    gh = g // halves
    eye = jnp.eye(gh, dtype=blocks.dtype)
    b = blocks.reshape(halves, gh, r, 1, c) * eye.reshape(1, gh, 1, gh, 1)
    return b.reshape(halves, gh * r, gh * c)


def _time_major_perm(nb, steps):
    r = np.arange(nb * steps)
    src = (r % nb) * steps + r // nb
    p = np.zeros((nb * steps, nb * steps), np.float32)
    p[r, src] = 1.0
    return p


def _layer(x, lam_init, norm_gain, w_in, merge_gate_b, q_norm_gain, k_norm_gain,
           lambda_q1, lambda_k1, lambda_q2, lambda_k2, diff_subln_gain, rel_bias_table,
           ssm_a_re, ssm_a_im, ssm_log_dt, ssm_b_re, ssm_b_im, ssm_c_re, ssm_c_im,
           ssm_d, ssm_glu_w, ssm_glu_b, proj_attn, proj_ssm, w_out):
    nb, l, d = x.shape
    n = nb * l
    aw = _HEADS * 2 * _SUB_DIM
    groups = ssm_a_re.shape[0]
    x2 = x.reshape(n, d)
    ng = norm_gain.reshape(1, d).astype(_F32)

    w_ku = jnp.concatenate([w_in[:, aw:2 * aw], w_in[:, 4 * aw:5 * aw]], axis=1).astype(_BF16)
    w_qvt = jnp.concatenate([w_in[:, :aw], w_in[:, 2 * aw:3 * aw]], axis=1).T.astype(_BF16)
    w_gates = jnp.concatenate([w_in[:, 3 * aw:4 * aw], w_in[:, 5 * aw:]], axis=1).astype(_BF16)
    gq = jnp.tile(q_norm_gain.astype(_F32), 2 * _HEADS) * (_SUB_DIM ** -0.5 * _LOG2E)
    gqt = jnp.broadcast_to(gq[:, None], (aw, _TOKEN_TILE))
    gk = jnp.tile(k_norm_gain.astype(_F32), 2 * _HEADS).reshape(1, aw)
    seg = np.arange(aw) // _SUB_DIM
    gsum = jnp.asarray((seg[:, None] == seg[None, :]).astype(np.float32), _BF16)
    qt, k, vt, u = _inproj(x2, ng, w_ku, w_qvt, gqt, gk, gsum, aw)

    lam4 = jnp.stack([lambda_q1, lambda_k1, lambda_q2, lambda_k2]).astype(_F32)
    qk_gains = jnp.stack([gq[:_SUB_DIM], gk[0, :_SUB_DIM]])
    bias, lam, flag = _bias_prep(rel_bias_table.astype(_F32), lam4, qk_gains, lam_init, _ATTN_TILE)
    subln = diff_subln_gain.astype(_F32) * (1.0 - lam_init)
    subln_t = jnp.broadcast_to(subln[:, None], (_V_DIM, _ATTN_TILE))
    shp = (nb, l, aw)
    o_a = _attention(flag, qt, k.reshape(shp), vt, bias, lam, subln_t)

    rep = lambda a: jnp.repeat(a.astype(_F32), _SSM_GROUP, axis=0)
    ldt = jnp.broadcast_to(ssm_log_dt.astype(_F32)[:, None], ssm_a_re.shape)
    bt = lambda a: a.astype(_F32).transpose(0, 2, 1).reshape(groups * _SSM_GROUP, _SSM_STATE)
    lb_re, lb_im, bb_re, bb_im = _s5_prep(rep(ssm_a_re), rep(ssm_a_im), rep(ldt),
                                          bt(ssm_b_re), bt(ssm_b_im))
    n_tiles = groups * _SSM_STATE // _STATE_TILE
    flat = lambda a: a[::_SSM_GROUP].reshape(n_tiles, 1, _STATE_TILE)
    gshape = (groups, _SSM_GROUP, _SSM_STATE)
    wb_re = _block_diag_halves(bb_re.reshape(gshape), 2).astype(_BF16)
    wb_im = _block_diag_halves(bb_im.reshape(gshape), 2).astype(_BF16)
    wc_re = _block_diag_halves(ssm_c_re.astype(_F32).transpose(0, 2, 1), 2).astype(_BF16)
    wc_im = _block_diag_halves(-ssm_c_im.astype(_F32).transpose(0, 2, 1), 2).astype(_BF16)
    perm = _time_major_perm(nb, _SCAN_STEPS)
    o_s = _s5(u.reshape(shp), jnp.asarray(perm, _BF16), jnp.asarray(perm.T, _BF16),
              wb_re, wb_im, flat(lb_re), flat(lb_im), wc_re, wc_im,
              ssm_d.astype(_F32).reshape(1, aw), ssm_glu_w.astype(_BF16),
              ssm_glu_b.astype(_F32).reshape(1, aw))

    out = _outproj(x2, ng, w_gates, merge_gate_b.astype(_F32).reshape(1, 2 * d),
                   o_a, o_s.reshape(n, aw),
                   proj_attn.astype(_BF16), proj_ssm.astype(_BF16), w_out.astype(_BF16), aw)
    return out.reshape(nb, l, d)


def kernel(x, norm_gain, w_in, merge_gate_b, q_norm_gain, k_norm_gain, lambda_q1, lambda_k1,
           lambda_q2, lambda_k2, diff_subln_gain, rel_bias_table, ssm_A_re, ssm_A_im, ssm_log_dt,
           ssm_B_re, ssm_B_im, ssm_C_re, ssm_C_im, ssm_D, ssm_glu_w, ssm_glu_b,
           proj_attn, proj_ssm, w_out):
    per_layer = (norm_gain, w_in, merge_gate_b, q_norm_gain, k_norm_gain, lambda_q1, lambda_k1,
                 lambda_q2, lambda_k2, diff_subln_gain)
    per_layer_tail = (ssm_A_re, ssm_A_im, ssm_log_dt, ssm_B_re, ssm_B_im, ssm_C_re, ssm_C_im,
                      ssm_D, ssm_glu_w, ssm_glu_b, proj_attn, proj_ssm, w_out)
    for layer in range(norm_gain.shape[0]):
        lam_init = 0.8 - 0.6 * math.exp(-0.3 * layer)
        x = _layer(x, lam_init, *(p[layer] for p in per_layer), rel_bias_table,
                   *(p[layer] for p in per_layer_tail))
    return x
```

```python
import functools
import math

import jax
import jax.numpy as jnp
import numpy as np
from jax import lax
from jax.experimental import pallas as pl
from jax.experimental.pallas import tpu as pltpu

_F32 = jnp.float32
_BF16 = jnp.bfloat16

_CHUNK = 64
_HEADS = 4
_SUB_DIM = 64
_V_DIM = 128
_SSM_GROUP = 16
_SSM_STATE = 64
_REL_BUCKETS = 32
_REL_MAX_DIST = 128
_EPS = 1e-6
_LOG2E = math.log2(math.e)
_SAFE_LOG2_RANGE = 100.0

_VMEM_LIMIT = 48 * 1024 * 1024
_TOKEN_TILE = 1024
_ATTN_TILE = 512
_KEY_GRANULE = 256
_BIAS_BLOCK = 128
_SCAN_STEPS = 64
_STATE_TILE = 256
_SCAN_GROUP = 4


def _rms_scale(x, eps=_EPS):
    return lax.rsqrt(jnp.mean(x * x, axis=-1, keepdims=True) + eps)


def _inproj_body(x_ref, ng_ref, wku_ref, wqvt_ref, gqt_ref, gk_ref, gsum_ref,
                 qt_ref, k_ref, vt_ref, u_ref, *, width):
    x = x_ref[...]
    tm = x.shape[0]
    h = (x * _rms_scale(x) * ng_ref[...]).astype(_BF16)
    z = jnp.dot(h, wku_ref[...], preferred_element_type=_F32)
    zt = lax.dot_general(wqvt_ref[...], h, (((1,), (1,)), ((), ())),
                         preferred_element_type=_F32)

    kk = z[:, :width]
    ss = jnp.dot((kk * kk).astype(_BF16), gsum_ref[...], preferred_element_type=_F32)
    k_ref[...] = (kk * lax.rsqrt(ss * (1.0 / _SUB_DIM) + _EPS) * gk_ref[...]).astype(_BF16)
    u_ref[...] = z[:, width:].astype(_BF16)

    qt = zt[:width].reshape(width // _SUB_DIM, _SUB_DIM, tm)
    ms = jnp.mean(qt * qt, axis=1, keepdims=True)
    qt_ref[...] = ((qt * lax.rsqrt(ms + _EPS)).reshape(width, tm) * gqt_ref[...]).astype(_BF16)

    vt = zt[width:].astype(_BF16)
    for j in range(tm // _KEY_GRANULE):
        for hd in range(_HEADS):
            vt_ref[j, hd] = vt[hd * _V_DIM:(hd + 1) * _V_DIM,
                               j * _KEY_GRANULE:(j + 1) * _KEY_GRANULE]


def _inproj(x2, norm_gain, w_ku, w_qvt, gqt, gk, gsum, width):
    n, d = x2.shape
    tm = _TOKEN_TILE
    full = lambda a: pl.BlockSpec(a.shape, lambda i: (0,) * a.ndim)
    row_sd = jax.ShapeDtypeStruct((n, width), _BF16)
    row_spec = pl.BlockSpec((tm, width), lambda i: (i, 0))
    gran = tm // _KEY_GRANULE
    vrows = _V_DIM
    return pl.pallas_call(
        functools.partial(_inproj_body, width=width),
        grid=(n // tm,),
        in_specs=[pl.BlockSpec((tm, d), lambda i: (i, 0)), full(norm_gain), full(w_ku), full(w_qvt),
                  full(gqt), full(gk), full(gsum)],
        out_specs=[pl.BlockSpec((width, tm), lambda i: (0, i)), row_spec,
                   pl.BlockSpec((gran, _HEADS, vrows, _KEY_GRANULE), lambda i: (i, 0, 0, 0)), row_spec],
        out_shape=[jax.ShapeDtypeStruct((width, n), _BF16), row_sd,
                   jax.ShapeDtypeStruct((n // _KEY_GRANULE, _HEADS, vrows, _KEY_GRANULE), _BF16), row_sd],
        compiler_params=pltpu.CompilerParams(
            dimension_semantics=("parallel",), vmem_limit_bytes=_VMEM_LIMIT),
        name="inproj",
    )(x2, norm_gain, w_ku, w_qvt, gqt, gk, gsum)


def _t5_bucket_np(rel):
    nb = _REL_BUCKETS // 2
    max_exact = nb // 2
    side = np.where(rel > 0, nb, 0)
    n = np.abs(rel)
    nf = np.maximum(n, 1).astype(np.float32)
    large = max_exact + (np.log(nf / np.float32(max_exact)) / np.float32(math.log(_REL_MAX_DIST / max_exact))
                         * np.float32(nb - max_exact)).astype(np.int32)
    large = np.minimum(large, nb - 1)
    return side + np.where(n < max_exact, n, large)


def _bucket_blocks():
    assert _BIAS_BLOCK % _CHUNK == 0
    i = np.arange(_BIAS_BLOCK)[None, :]
    j = np.arange(_BIAS_BLOCK)[:, None]
    far_bucket = _REL_BUCKETS // 2 - 1
    assert (_t5_bucket_np(j - i - 2 * _BIAS_BLOCK) == far_bucket).all()
    diag = np.where((j // _CHUNK) <= (i // _CHUNK), _t5_bucket_np(j - i), -1)
    prev = _t5_bucket_np(j - i - _BIAS_BLOCK)
    return np.stack([diag, prev]).astype(np.int32)


def _bias_prep_body(tab_ref, tabv_ref, bucket_ref, lam4_ref, gains_ref,
                    bias_ref, lam_ref, flag_ref, *, lam_init, t):
    h = pl.program_id(0)
    far_bucket = _REL_BUCKETS // 2 - 1
    bkt = bucket_ref[...]
    far = tab_ref[far_bucket, h]
    val = jnp.full(bkt.shape, -jnp.inf, _F32)
    for b in range(_REL_BUCKETS):
        val = jnp.where(bkt == b, (tab_ref[b, h] - far) * _LOG2E, val)
    diag_block, prev_block = val[0], val[1]
    n_blk = t // _BIAS_BLOCK
    bias_ref[0, 0] = jnp.zeros((t, t), _F32)
    bias_ref[0, 1] = jnp.zeros((t, t), _F32)
    bias_ref[0, 1, t - _BIAS_BLOCK:, :_BIAS_BLOCK] = prev_block
    for bj in range(n_blk):
        rows = slice(bj * _BIAS_BLOCK, (bj + 1) * _BIAS_BLOCK)
        for bi in range(n_blk):
            cols = slice(bi * _BIAS_BLOCK, (bi + 1) * _BIAS_BLOCK)
            if bi == bj:
                block = diag_block
            elif bi == bj + 1:
                block = prev_block
            else:
                block = jnp.full((_BIAS_BLOCK, _BIAS_BLOCK), 0.0 if bi > bj else -jnp.inf, _F32)
            bias_ref[0, 2, rows, cols] = block
    l4 = lam4_ref[...]
    s1 = jnp.sum(l4[0:1] * l4[1:2], axis=-1, keepdims=True)
    s2 = jnp.sum(l4[2:3] * l4[3:4], axis=-1, keepdims=True)
    lam_ref[...] = jnp.broadcast_to(jnp.exp(s1) - jnp.exp(s2) + lam_init, lam_ref.shape)
    gmax = jnp.max(jnp.abs(gains_ref[...]), axis=-1, keepdims=True)
    qk_bound = (_SUB_DIM * 1.02) * gmax[0:1] * gmax[1:2]
    tv = tabv_ref[...]
    dev = jnp.abs(tv - tv[far_bucket:far_bucket + 1]) * _LOG2E
    bias_bound = jnp.max(jnp.max(dev, axis=-1, keepdims=True), axis=0, keepdims=True)
    flag_ref[...] = (qk_bound + bias_bound <= _SAFE_LOG2_RANGE).astype(jnp.int32)


def _bias_prep(rel_table, lam4, gains, lam_init, t):
    assert t % _BIAS_BLOCK == 0
    buckets = jnp.asarray(_bucket_blocks())
    whole = lambda a: pl.BlockSpec(a.shape, lambda h: (0,) * a.ndim)
    return pl.pallas_call(
        functools.partial(_bias_prep_body, lam_init=lam_init, t=t),
        grid=(_HEADS,),
        in_specs=[pl.BlockSpec(memory_space=pltpu.SMEM), whole(rel_table), whole(buckets),
                  whole(lam4), whole(gains)],
        out_specs=[pl.BlockSpec((1, 3, t, t), lambda h: (h, 0, 0, 0)),
                   pl.BlockSpec((1, _V_DIM), lambda h: (0, 0)),
                   pl.BlockSpec((1, 1), lambda h: (0, 0))],
        out_shape=[jax.ShapeDtypeStruct((_HEADS, 3, t, t), _F32),
                   jax.ShapeDtypeStruct((1, _V_DIM), _F32),
                   jax.ShapeDtypeStruct((1, 1), jnp.int32)],
        compiler_params=pltpu.CompilerParams(dimension_semantics=("arbitrary",)),
        name="attn_bias_prep",
    )(rel_table, rel_table, buckets, lam4, gains)


def _attn_body(flag_ref, qta_ref, qtb_ref, k_ref, vt_ref, bias_ref, lam_ref, gain_ref, o_ref,
               qw_ref, m_ref, acc_ref, den_ref, *, t, nq):
    p = pl.program_id(2)
    gran = t // _KEY_GRANULE
    for side, qt_ref in enumerate((qta_ref, qtb_ref)):
        qt = qt_ref[...]
        row = lax.broadcasted_iota(jnp.int32, qt.shape, 0)
        zero = jnp.zeros_like(qt)
        qw_ref[side, :, :t] = jnp.where(row < _SUB_DIM, qt, zero)
        qw_ref[side, :, t:] = jnp.where(row >= _SUB_DIM, qt, zero)
    acc_ref[...] = jnp.zeros(acc_ref.shape, _F32)
    den_ref[...] = jnp.zeros(den_ref.shape, _F32)

    def key_sum(e):
        return jnp.sum(e.reshape(e.shape[0] // 8, 8, e.shape[1]), axis=0)

    def schedule(step_index):
        mirrored = step_index > p
        side = mirrored.astype(jnp.int32)
        kj = jnp.where(mirrored, step_index - p - 1, p - step_index)
        q_tile = jnp.where(mirrored, nq - 1 - p, p)
        bias_index = jnp.where(kj == q_tile, 2, jnp.where(kj == q_tile - 1, 1, 0))
        return side, kj, bias_index

    def apply_to_values(kj, pt):
        pv = jnp.dot(vt_ref[kj * gran, 0], pt[:_KEY_GRANULE], preferred_element_type=_F32)
        for g in range(1, gran):
            pv = pv + jnp.dot(vt_ref[kj * gran + g, 0],
                              pt[g * _KEY_GRANULE:(g + 1) * _KEY_GRANULE],
                              preferred_element_type=_F32)
        return pv

    def key_tile(kj):
        return k_ref[0, pl.ds(pl.multiple_of(kj * t, t), t), :]

    def bounded_step(step_index):
        side, kj, bias_index = schedule(step_index)
        st = jnp.dot(key_tile(kj), qw_ref[side], preferred_element_type=_F32)
        bias = bias_ref[0, bias_index]
        e = [jnp.exp2(st[:, :t] + bias), jnp.exp2(st[:, t:] + bias)]
        den_ref[side] += jnp.concatenate([key_sum(e[0]), key_sum(e[1])], axis=1)
        pt = jnp.concatenate([e[0].astype(_BF16), e[1].astype(_BF16)], axis=1)
        acc_ref[side] += apply_to_values(kj, pt)

    def bounded_diagonal_step(side, q_tile):
        assert gran == 2
        half = _KEY_GRANULE
        late = [slice(half, t), slice(t + half, 2 * t)]
        k0 = pl.multiple_of(q_tile * t, t)
        bias = bias_ref[0, 2]
        st0 = jnp.dot(k_ref[0, pl.ds(k0, half), :], qw_ref[side], preferred_element_type=_F32)
        e0 = [jnp.exp2(st0[:, :t] + bias[:half]), jnp.exp2(st0[:, t:] + bias[:half])]
        den_ref[side] += jnp.concatenate([key_sum(e0[0]), key_sum(e0[1])], axis=1)
        pt0 = jnp.concatenate([e0[0].astype(_BF16), e0[1].astype(_BF16)], axis=1)
        qw_late = jnp.concatenate([qw_ref[side, :, lanes] for lanes in late], axis=1)
        st1 = jnp.dot(k_ref[0, pl.ds(k0 + half, half), :], qw_late, preferred_element_type=_F32)
        e1 = [jnp.exp2(st1[:, :half] + bias[half:, half:]), jnp.exp2(st1[:, half:] + bias[half:, half:])]
        pt1 = jnp.concatenate([e1[0].astype(_BF16), e1[1].astype(_BF16)], axis=1)
        acc_ref[side] += jnp.dot(vt_ref[q_tile * gran, 0], pt0, preferred_element_type=_F32)
        pv1 = jnp.dot(vt_ref[q_tile * gran + 1, 0], pt1, preferred_element_type=_F32)
        for n, lanes in enumerate(late):
            acc_ref[side, :, lanes] += pv1[:, n * half:(n + 1) * half]
            den_ref[side, :, lanes] += key_sum(e1[n])

    def online_step(step_index):
        side, kj, bias_index = schedule(step_index)
        kt = key_tile(kj)
        bias = bias_ref[0, bias_index]
        for s in range(2):
            lanes = slice(s * t, (s + 1) * t)
            st = jnp.dot(kt, qw_ref[side, :, lanes], preferred_element_type=_F32) + bias
            m_old = m_ref[side, s]
            m_new = jnp.maximum(m_old, jnp.max(st, axis=0, keepdims=True))
            alpha = jnp.exp2(m_old - m_new)
            e = jnp.exp2(st - m_new)
            den_ref[side, :, lanes] = alpha * den_ref[side, :, lanes] + key_sum(e)
            acc_ref[side, :, lanes] = alpha * acc_ref[side, :, lanes] + apply_to_values(
                kj, e.astype(_BF16))
            m_ref[side, s] = m_new

    bounded = flag_ref[0, 0] > 0

    @pl.when(bounded)
    def _():
        bounded_diagonal_step(0, p)
        for step_index in range(1, nq):
            bounded_step(step_index)
        bounded_diagonal_step(1, nq - 1 - p)

    @pl.when(jnp.logical_not(bounded))
    def _():
        m_ref[...] = jnp.full(m_ref.shape, -jnp.inf, _F32)
        for step_index in range(nq + 1):
            online_step(step_index)

    for side in range(2):
        o = acc_ref[side] / jnp.sum(den_ref[side], axis=0, keepdims=True)
        od = o[:, :t] - lam_ref[0:1, 0:1] * o[:, t:]
        scale = lax.rsqrt(jnp.mean(od * od, axis=0, keepdims=True) + _EPS)
        o_ref[side, 0, 0] = (od * scale * gain_ref[...]).T.astype(_BF16)


def _attention(flag, qt, k, vt, bias, lam, gain_t):
    b, l, w = k.shape
    t = _ATTN_TILE
    nq = l // t
    vrows = vt.shape[2]
    return pl.pallas_call(
        functools.partial(_attn_body, t=t, nq=nq),
        grid=(b, _HEADS, nq // 2),
        in_specs=[pl.BlockSpec(memory_space=pltpu.SMEM),
                  pl.BlockSpec((_V_DIM, t), lambda bi, h, p: (h, bi * nq + p)),
                  pl.BlockSpec((_V_DIM, t), lambda bi, h, p: (h, bi * nq + nq - 1 - p)),
                  pl.BlockSpec((1, l, _V_DIM), lambda bi, h, p: (bi, 0, h)),
                  pl.BlockSpec((l // _KEY_GRANULE, 1, vrows, _KEY_GRANULE),
                               lambda bi, h, p: (bi, h, 0, 0)),
                  pl.BlockSpec((1, 3, t, t), lambda bi, h, p: (h, 0, 0, 0)),
                  pl.BlockSpec(lam.shape, lambda bi, h, p: (0, 0)),
                  pl.BlockSpec(gain_t.shape, lambda bi, h, p: (0, 0))],
        out_specs=pl.BlockSpec((2, 1, 1, t, _V_DIM), lambda bi, h, p: (0, bi, p, 0, h)),
        out_shape=jax.ShapeDtypeStruct((2, b, nq // 2, t, w), _BF16),
        scratch_shapes=[pltpu.VMEM((2, _V_DIM, 2 * t), _BF16),
                        pltpu.VMEM((2, 2, 1, t), _F32),
                        pltpu.VMEM((2, vrows, 2 * t), _F32),
                        pltpu.VMEM((2, 8, 2 * t), _F32)],
        compiler_params=pltpu.CompilerParams(
            dimension_semantics=("parallel", "parallel", "arbitrary"),
            vmem_limit_bytes=_VMEM_LIMIT),
        name="diff_attention",
    )(flag, qt, qt, k, vt, bias, lam, gain_t)


def _s5_prep_body(are_ref, aim_ref, ldt_ref, bre_ref, bim_ref,
                  lbre_ref, lbim_ref, bbre_ref, bbim_ref):
    a_re = are_ref[...]
    a_im = aim_ref[...]
    dt = jnp.exp(ldt_ref[...])
    decay = jnp.exp(a_re * dt)
    lb_re = decay * jnp.cos(a_im * dt)
    lb_im = decay * jnp.sin(a_im * dt)
    nr = lb_re - 1.0
    ni = lb_im
    den = a_re * a_re + a_im * a_im
    q_re = (nr * a_re + ni * a_im) / den
    q_im = (ni * a_re - nr * a_im) / den
    b_re = bre_ref[...]
    b_im = bim_ref[...]
    bbre_ref[...] = q_re * b_re - q_im * b_im
    bbim_ref[...] = q_re * b_im + q_im * b_re
    lbre_ref[...] = lb_re
    lbim_ref[...] = lb_im


def _s5_prep(a_re, a_im, log_dt, b_re, b_im):
    sd = jax.ShapeDtypeStruct(a_re.shape, _F32)
    return pl.pallas_call(_s5_prep_body, out_shape=[sd] * 4, name="s5_discretise")(
        a_re, a_im, log_dt, b_re, b_im)


def _s5_body(u_ref, perm_ref, permt_ref, wbre_ref, wbim_ref, lbre_ref, lbim_ref,
             wcre_ref, wcim_ref, d_ref, gw_ref, gb_ref, o_ref,
             utm_a, bre_a, bim_a, utm_b, bre_b, bim_b, xre_ref, xim_ref, cre_ref, cim_ref,
             *, steps, nb):
    i = pl.program_id(0)
    rows = nb * steps
    width = u_ref.shape[-1]
    n_tiles, _, tile_w = lbre_ref.shape
    halves = wcre_ref.shape[0]
    tiles_per_half = n_tiles // halves
    hw_in = width // halves

    @pl.when(i == 0)
    def _():
        cre_ref[...] = jnp.zeros(cre_ref.shape, _F32)
        cim_ref[...] = jnp.zeros(cim_ref.shape, _F32)
        bre_b[...] = jnp.zeros(bre_b.shape, _F32)
        bim_b[...] = jnp.zeros(bim_b.shape, _F32)
        utm_b[...] = jnp.zeros(utm_b.shape, _BF16)

    def step(utm_new, bre_new, bim_new, utm_old, bre_old, bim_old):
        u_bm = u_ref[...].reshape(rows, width)
        u_tm = jnp.dot(perm_ref[...], u_bm, preferred_element_type=_F32).astype(_BF16)
        utm_new[...] = u_tm

        for hf in range(halves):
            uh = u_tm[:, hf * hw_in:(hf + 1) * hw_in]
            bu_r = jnp.dot(uh, wbre_ref[hf], preferred_element_type=_F32)
            bu_i = jnp.dot(uh, wbim_ref[hf], preferred_element_type=_F32)
            for q in range(tiles_per_half):
                lanes = slice(q * tile_w, (q + 1) * tile_w)
                bre_new[hf * tiles_per_half + q] = bu_r[:, lanes]
                bim_new[hf * tiles_per_half + q] = bu_i[:, lanes]

        for j0 in range(0, n_tiles, _SCAN_GROUP):
            group = range(j0, j0 + _SCAN_GROUP)
            a_r = [jnp.broadcast_to(lbre_ref[j], (nb, tile_w)) for j in group]
            a_i = [jnp.broadcast_to(lbim_ref[j], (nb, tile_w)) for j in group]
            x_r = [cre_ref[j] for j in group]
            x_i = [cim_ref[j] for j in group]
            for ti in range(steps):
                r = slice(ti * nb, (ti + 1) * nb)
                for q, j in enumerate(group):
                    n_r = a_r[q] * x_r[q] - a_i[q] * x_i[q] + bre_old[j, r, :]
                    n_i = a_r[q] * x_i[q] + a_i[q] * x_r[q] + bim_old[j, r, :]
                    xre_ref[j, r, :] = n_r
                    xim_ref[j, r, :] = n_i
                    x_r[q], x_i[q] = n_r, n_i
            for q, j in enumerate(group):
                cre_ref[j] = x_r[q]
                cim_ref[j] = x_i[q]

        ys = []
        for hf in range(halves):
            tiles = range(hf * tiles_per_half, (hf + 1) * tiles_per_half)
            x_r = jnp.concatenate([xre_ref[j] for j in tiles], axis=1).astype(_BF16)
            x_i = jnp.concatenate([xim_ref[j] for j in tiles], axis=1).astype(_BF16)
            ys.append(jnp.dot(x_r, wcre_ref[hf], preferred_element_type=_F32)
                      + jnp.dot(x_i, wcim_ref[hf], preferred_element_type=_F32))
        y = jnp.concatenate(ys, axis=-1) + d_ref[...] * utm_old[...].astype(_F32)
        g = jax.nn.gelu(y)
        o = g * jax.nn.sigmoid(jnp.dot(g.astype(_BF16), gw_ref[...], preferred_element_type=_F32)
                               + gb_ref[...])
        o_bm = jnp.dot(permt_ref[...], o.astype(_BF16), preferred_element_type=_F32)
        o_ref[...] = o_bm.astype(_BF16).reshape(o_ref.shape)

    @pl.when(i % 2 == 0)
    def _():
        step(utm_a, bre_a, bim_a, utm_b, bre_b, bim_b)

    @pl.when(i % 2 == 1)
    def _():
        step(utm_b, bre_b, bim_b, utm_a, bre_a, bim_a)


def _s5(u, perm, permt, wb_re, wb_im, lb_re, lb_im, wc_re, wc_im, d_vec, glu_w, glu_b):
    nb, l, width = u.shape
    steps = _SCAN_STEPS
    n_blocks = l // steps
    n_tiles, _, tile_w = lb_re.shape
    full = lambda a: pl.BlockSpec(a.shape, lambda i: (0,) * a.ndim)
    consts = (perm, permt, wb_re, wb_im, lb_re, lb_im, wc_re, wc_im, d_vec, glu_w, glu_b)
    utm_buf = pltpu.VMEM((nb * steps, width), _BF16)
    state_buf = pltpu.VMEM((n_tiles, nb * steps, tile_w), _F32)
    return pl.pallas_call(
        functools.partial(_s5_body, steps=steps, nb=nb),
        grid=(n_blocks + 1,),
        in_specs=[pl.BlockSpec((nb, steps, width), lambda i: (0, jnp.minimum(i, n_blocks - 1), 0))]
                 + [full(a) for a in consts],
        out_specs=pl.BlockSpec((nb, steps, width), lambda i: (0, jnp.maximum(i - 1, 0), 0)),
        out_shape=jax.ShapeDtypeStruct(u.shape, _BF16),
        scratch_shapes=[utm_buf, state_buf, state_buf, utm_buf, state_buf, state_buf,
                        state_buf, state_buf,
                        pltpu.VMEM((n_tiles, nb, tile_w), _F32),
                        pltpu.VMEM((n_tiles, nb, tile_w), _F32)],
        compiler_params=pltpu.CompilerParams(
            dimension_semantics=("arbitrary",), vmem_limit_bytes=_VMEM_LIMIT),
        name="s5_branch",
    )(u, *consts)


def _outproj_body(x_ref, ng_ref, wg_ref, mb_ref, *refs, width, n_oa):
    oa_refs = refs[:n_oa]
    os_ref, pa_ref, ps_ref, wo_ref, out_ref = refs[n_oa:]
    x = x_ref[...]
    d = x.shape[-1]
    h = (x * _rms_scale(x) * ng_ref[...]).astype(_BF16)
    zg = jnp.dot(h, wg_ref[...], preferred_element_type=_F32)
    attn = jnp.concatenate([r[0, 0, 0] for r in oa_refs], axis=0)
    o_a = (attn.astype(_F32) * jax.nn.silu(zg[:, :width])).astype(_BF16)
    o_s = (os_ref[...].astype(_F32) * jax.nn.silu(zg[:, width:2 * width])).astype(_BF16)
    p_a = jnp.dot(o_a, pa_ref[...], preferred_element_type=_F32)
    p_s = jnp.dot(o_s, ps_ref[...], preferred_element_type=_F32)
    g = jax.nn.sigmoid(zg[:, 2 * width:] + mb_ref[...])
    merged = g[:, :d] * p_a + g[:, d:] * p_s
    out_ref[...] = x + jnp.dot(merged.astype(_BF16), wo_ref[...], preferred_element_type=_F32)


def _outproj(x2, norm_gain, w_gates, merge_b, o_a, o_s, proj_a, proj_s, w_out, width):
    n, d = x2.shape
    tm = _TOKEN_TILE
    _, _, half, t, _ = o_a.shape
    nq = 2 * half
    per_tile = tm // t
    per_batch = nq // per_tile
    full = lambda a: pl.BlockSpec(a.shape, lambda i: (0,) * a.ndim)
    row = lambda a: pl.BlockSpec((tm, a.shape[-1]), lambda i: (i, 0))

    def mirrored_tile(sub):
        def index_map(i):
            j = (i % per_batch) * per_tile + sub
            return (j // half, i // per_batch, jnp.where(j < half, j, nq - 1 - j), 0, 0)
        return pl.BlockSpec((1, 1, 1, t, width), index_map)

    return pl.pallas_call(
        functools.partial(_outproj_body, width=width, n_oa=per_tile),
        grid=(n // tm,),
        in_specs=[row(x2), full(norm_gain), full(w_gates), full(merge_b)]
                 + [mirrored_tile(sub) for sub in range(per_tile)]
                 + [row(o_s), full(proj_a), full(proj_s), full(w_out)],
        out_specs=row(x2),
        out_shape=jax.ShapeDtypeStruct(x2.shape, x2.dtype),
        compiler_params=pltpu.CompilerParams(
            dimension_semantics=("parallel",), vmem_limit_bytes=_VMEM_LIMIT),
        name="outproj",
    )(x2, norm_gain, w_gates, merge_b, *([o_a] * per_tile), o_s, proj_a, proj_s, w_out)


def _block_diag_halves(blocks, halves):
    g, r, c = blocks.shape
    gh = g // halves
    eye = jnp.eye(gh, dtype=blocks.dtype)
    b = blocks.reshape(halves, gh, r, 1, c) * eye.reshape(1, gh, 1, gh, 1)
    return b.reshape(halves, gh * r, gh * c)


def _time_major_perm(nb, steps):
    r = np.arange(nb * steps)
    src = (r % nb) * steps + r // nb
    p = np.zeros((nb * steps, nb * steps), np.float32)
    p[r, src] = 1.0
    return p


def _layer(x, lam_init, norm_gain, w_in, merge_gate_b, q_norm_gain, k_norm_gain,
           lambda_q1, lambda_k1, lambda_q2, lambda_k2, diff_subln_gain, rel_bias_table,
           ssm_a_re, ssm_a_im, ssm_log_dt, ssm_b_re, ssm_b_im, ssm_c_re, ssm_c_im,
           ssm_d, ssm_glu_w, ssm_glu_b, proj_attn, proj_ssm, w_out):
    nb, l, d = x.shape
    n = nb * l
    aw = _HEADS * 2 * _SUB_DIM
    groups = ssm_a_re.shape[0]
    x2 = x.reshape(n, d)
    ng = norm_gain.reshape(1, d).astype(_F32)

    w_ku = jnp.concatenate([w_in[:, aw:2 * aw], w_in[:, 4 * aw:5 * aw]], axis=1).astype(_BF16)
    w_qvt = jnp.concatenate([w_in[:, :aw], w_in[:, 2 * aw:3 * aw]], axis=1).T.astype(_BF16)
    w_gates = jnp.concatenate([w_in[:, 3 * aw:4 * aw], w_in[:, 5 * aw:]], axis=1).astype(_BF16)
    gq = jnp.tile(q_norm_gain.astype(_F32), 2 * _HEADS) * (_SUB_DIM ** -0.5 * _LOG2E)
    gqt = jnp.broadcast_to(gq[:, None], (aw, _TOKEN_TILE))
    gk = jnp.tile(k_norm_gain.astype(_F32), 2 * _HEADS).reshape(1, aw)
    seg = np.arange(aw) // _SUB_DIM
    gsum = jnp.asarray((seg[:, None] == seg[None, :]).astype(np.float32), _BF16)
    qt, k, vt, u = _inproj(x2, ng, w_ku, w_qvt, gqt, gk, gsum, aw)

    lam4 = jnp.stack([lambda_q1, lambda_k1, lambda_q2, lambda_k2]).astype(_F32)
    qk_gains = jnp.stack([gq[:_SUB_DIM], gk[0, :_SUB_DIM]])
    bias, lam, flag = _bias_prep(rel_bias_table.astype(_F32), lam4, qk_gains, lam_init, _ATTN_TILE)
    subln = diff_subln_gain.astype(_F32) * (1.0 - lam_init)
    subln_t = jnp.broadcast_to(subln[:, None], (_V_DIM, _ATTN_TILE))
    shp = (nb, l, aw)
    o_a = _attention(flag, qt, k.reshape(shp), vt, bias, lam, subln_t)

    rep = lambda a: jnp.repeat(a.astype(_F32), _SSM_GROUP, axis=0)
    ldt = jnp.broadcast_to(ssm_log_dt.astype(_F32)[:, None], ssm_a_re.shape)
    bt = lambda a: a.astype(_F32).transpose(0, 2, 1).reshape(groups * _SSM_GROUP, _SSM_STATE)
    lb_re, lb_im, bb_re, bb_im = _s5_prep(rep(ssm_a_re), rep(ssm_a_im), rep(ldt),
                                          bt(ssm_b_re), bt(ssm_b_im))
    n_tiles = groups * _SSM_STATE // _STATE_TILE
    flat = lambda a: a[::_SSM_GROUP].reshape(n_tiles, 1, _STATE_TILE)
    gshape = (groups, _SSM_GROUP, _SSM_STATE)
    wb_re = _block_diag_halves(bb_re.reshape(gshape), 2).astype(_BF16)
    wb_im = _block_diag_halves(bb_im.reshape(gshape), 2).astype(_BF16)
    wc_re = _block_diag_halves(ssm_c_re.astype(_F32).transpose(0, 2, 1), 2).astype(_BF16)
    wc_im = _block_diag_halves(-ssm_c_im.astype(_F32).transpose(0, 2, 1), 2).astype(_BF16)
    perm = _time_major_perm(nb, _SCAN_STEPS)
    o_s = _s5(u.reshape(shp), jnp.asarray(perm, _BF16), jnp.asarray(perm.T, _BF16),
              wb_re, wb_im, flat(lb_re), flat(lb_im), wc_re, wc_im,
              ssm_d.astype(_F32).reshape(1, aw), ssm_glu_w.astype(_BF16),
              ssm_glu_b.astype(_F32).reshape(1, aw))

    out = _outproj(x2, ng, w_gates, merge_gate_b.astype(_F32).reshape(1, 2 * d),
                   o_a, o_s.reshape(n, aw),
                   proj_attn.astype(_BF16), proj_ssm.astype(_BF16), w_out.astype(_BF16), aw)
    return out.reshape(nb, l, d)


def kernel(x, norm_gain, w_in, merge_gate_b, q_norm_gain, k_norm_gain, lambda_q1, lambda_k1,
           lambda_q2, lambda_k2, diff_subln_gain, rel_bias_table, ssm_A_re, ssm_A_im, ssm_log_dt,
           ssm_B_re, ssm_B_im, ssm_C_re, ssm_C_im, ssm_D, ssm_glu_w, ssm_glu_b,
           proj_attn, proj_ssm, w_out):
    per_layer = (norm_gain, w_in, merge_gate_b, q_norm_gain, k_norm_gain, lambda_q1, lambda_k1,
                 lambda_q2, lambda_k2, diff_subln_gain)
    per_layer_tail = (ssm_A_re, ssm_A_im, ssm_log_dt, ssm_B_re, ssm_B_im, ssm_C_re, ssm_C_im,
                      ssm_D, ssm_glu_w, ssm_glu_b, proj_attn, proj_ssm, w_out)
    for layer in range(norm_gain.shape[0]):
        lam_init = 0.8 - 0.6 * math.exp(-0.3 * layer)
        x = _layer(x, lam_init, *(p[layer] for p in per_layer), rel_bias_table,
                   *(p[layer] for p in per_layer_tail))
    return x
```

```python
import functools
import math

import jax
import jax.numpy as jnp
import numpy as np
from jax import lax
from jax.experimental import pallas as pl
from jax.experimental.pallas import tpu as pltpu

_F32 = jnp.float32
_BF16 = jnp.bfloat16

_CHUNK = 64
_HEADS = 4
_SUB_DIM = 64
_V_DIM = 128
_SSM_GROUP = 16
_SSM_STATE = 64
_REL_BUCKETS = 32
_REL_MAX_DIST = 128
_EPS = 1e-6
_LOG2E = math.log2(math.e)
_SAFE_LOG2_RANGE = 100.0

_VMEM_LIMIT = 48 * 1024 * 1024
_TOKEN_TILE = 1024
_ATTN_TILE = 512
_KEY_GRANULE = 256
_BIAS_BLOCK = 128
_QUERY_GROUP = 2
_SCAN_STEPS = 64
_STATE_TILE = 256
_SCAN_GROUP = 4


def _rms_scale(x, eps=_EPS):
    return lax.rsqrt(jnp.mean(x * x, axis=-1, keepdims=True) + eps)


def _inproj_body(x_ref, ng_ref, wku_ref, wqvt_ref, gqt_ref, gk_ref, gsum_ref,
                 qt_ref, k_ref, vt_ref, u_ref, *, width):
    x = x_ref[...]
    tm = x.shape[0]
    h = (x * _rms_scale(x) * ng_ref[...]).astype(_BF16)
    z = jnp.dot(h, wku_ref[...], preferred_element_type=_F32)
    zt = lax.dot_general(wqvt_ref[...], h, (((1,), (1,)), ((), ())),
                         preferred_element_type=_F32)

    kk = z[:, :width]
    ss = jnp.dot((kk * kk).astype(_BF16), gsum_ref[...], preferred_element_type=_F32)
    k_ref[...] = (kk * lax.rsqrt(ss * (1.0 / _SUB_DIM) + _EPS) * gk_ref[...]).astype(_BF16)
    u_ref[...] = z[:, width:].astype(_BF16)

    qt = zt[:width].reshape(width // _SUB_DIM, _SUB_DIM, tm)
    ms = jnp.mean(qt * qt, axis=1, keepdims=True)
    qt_ref[...] = ((qt * lax.rsqrt(ms + _EPS)).reshape(width, tm) * gqt_ref[...]).astype(_BF16)

    vt = zt[width:].astype(_BF16)
    for j in range(tm // _KEY_GRANULE):
        for hd in range(_HEADS):
            vt_ref[j, hd] = vt[hd * _V_DIM:(hd + 1) * _V_DIM,
                               j * _KEY_GRANULE:(j + 1) * _KEY_GRANULE]


def _inproj(x2, norm_gain, w_ku, w_qvt, gqt, gk, gsum, width):
    n, d = x2.shape
    tm = _TOKEN_TILE
    full = lambda a: pl.BlockSpec(a.shape, lambda i: (0,) * a.ndim)
    row_sd = jax.ShapeDtypeStruct((n, width), _BF16)
    row_spec = pl.BlockSpec((tm, width), lambda i: (i, 0))
    gran = tm // _KEY_GRANULE
    vrows = _V_DIM
    return pl.pallas_call(
        functools.partial(_inproj_body, width=width),
        grid=(n // tm,),
        in_specs=[pl.BlockSpec((tm, d), lambda i: (i, 0)), full(norm_gain), full(w_ku), full(w_qvt),
                  full(gqt), full(gk), full(gsum)],
        out_specs=[pl.BlockSpec((width, tm), lambda i: (0, i)), row_spec,
                   pl.BlockSpec((gran, _HEADS, vrows, _KEY_GRANULE), lambda i: (i, 0, 0, 0)), row_spec],
        out_shape=[jax.ShapeDtypeStruct((width, n), _BF16), row_sd,
                   jax.ShapeDtypeStruct((n // _KEY_GRANULE, _HEADS, vrows, _KEY_GRANULE), _BF16), row_sd],
        compiler_params=pltpu.CompilerParams(
            dimension_semantics=("parallel",), vmem_limit_bytes=_VMEM_LIMIT),
        name="inproj",
    )(x2, norm_gain, w_ku, w_qvt, gqt, gk, gsum)


def _t5_bucket_np(rel):
    nb = _REL_BUCKETS // 2
    max_exact = nb // 2
    side = np.where(rel > 0, nb, 0)
    n = np.abs(rel)
    nf = np.maximum(n, 1).astype(np.float32)
    large = max_exact + (np.log(nf / np.float32(max_exact)) / np.float32(math.log(_REL_MAX_DIST / max_exact))
                         * np.float32(nb - max_exact)).astype(np.int32)
    large = np.minimum(large, nb - 1)
    return side + np.where(n < max_exact, n, large)


def _bucket_blocks():
    assert _BIAS_BLOCK % _CHUNK == 0
    i = np.arange(_BIAS_BLOCK)[None, :]
    j = np.arange(_BIAS_BLOCK)[:, None]
    far_bucket = _REL_BUCKETS // 2 - 1
    assert (_t5_bucket_np(j - i - 2 * _BIAS_BLOCK) == far_bucket).all()
    diag = np.where((j // _CHUNK) <= (i // _CHUNK), _t5_bucket_np(j - i), -1)
    prev = _t5_bucket_np(j - i - _BIAS_BLOCK)
    return np.stack([diag, prev]).astype(np.int32)


def _bias_prep_body(tab_ref, tabv_ref, bucket_ref, lam4_ref, gains_ref,
                    bias_ref, lam_ref, flag_ref, *, lam_init, t):
    h = pl.program_id(0)
    far_bucket = _REL_BUCKETS // 2 - 1
    bkt = bucket_ref[...]
    far = tab_ref[far_bucket, h]
    val = jnp.full(bkt.shape, -jnp.inf, _F32)
    for b in range(_REL_BUCKETS):
        val = jnp.where(bkt == b, (tab_ref[b, h] - far) * _LOG2E, val)
    diag_block, prev_block = val[0], val[1]
    n_blk = t // _BIAS_BLOCK
    bias_ref[0, 0] = jnp.zeros((t, t), _F32)
    bias_ref[0, 1] = jnp.zeros((t, t), _F32)
    bias_ref[0, 1, t - _BIAS_BLOCK:, :_BIAS_BLOCK] = prev_block
    for bj in range(n_blk):
        rows = slice(bj * _BIAS_BLOCK, (bj + 1) * _BIAS_BLOCK)
        for bi in range(n_blk):
            cols = slice(bi * _BIAS_BLOCK, (bi + 1) * _BIAS_BLOCK)
            if bi == bj:
                block = diag_block
            elif bi == bj + 1:
                block = prev_block
            else:
                block = jnp.full((_BIAS_BLOCK, _BIAS_BLOCK), 0.0 if bi > bj else -jnp.inf, _F32)
            bias_ref[0, 2, rows, cols] = block
    l4 = lam4_ref[...]
    s1 = jnp.sum(l4[0:1] * l4[1:2], axis=-1, keepdims=True)
    s2 = jnp.sum(l4[2:3] * l4[3:4], axis=-1, keepdims=True)
    lam_ref[...] = jnp.broadcast_to(jnp.exp(s1) - jnp.exp(s2) + lam_init, lam_ref.shape)
    gmax = jnp.max(jnp.abs(gains_ref[...]), axis=-1, keepdims=True)
    qk_bound = (_SUB_DIM * 1.02) * gmax[0:1] * gmax[1:2]
    tv = tabv_ref[...]
    dev = jnp.abs(tv - tv[far_bucket:far_bucket + 1]) * _LOG2E
    bias_bound = jnp.max(jnp.max(dev, axis=-1, keepdims=True), axis=0, keepdims=True)
    flag_ref[...] = (qk_bound + bias_bound <= _SAFE_LOG2_RANGE).astype(jnp.int32)


def _bias_prep(rel_table, lam4, gains, lam_init, t):
    assert t % _BIAS_BLOCK == 0
    buckets = jnp.asarray(_bucket_blocks())
    whole = lambda a: pl.BlockSpec(a.shape, lambda h: (0,) * a.ndim)
    return pl.pallas_call(
        functools.partial(_bias_prep_body, lam_init=lam_init, t=t),
        grid=(_HEADS,),
        in_specs=[pl.BlockSpec(memory_space=pltpu.SMEM), whole(rel_table), whole(buckets),
                  whole(lam4), whole(gains)],
        out_specs=[pl.BlockSpec((1, 3, t, t), lambda h: (h, 0, 0, 0)),
                   pl.BlockSpec((1, _V_DIM), lambda h: (0, 0)),
                   pl.BlockSpec((1, 1), lambda h: (0, 0))],
        out_shape=[jax.ShapeDtypeStruct((_HEADS, 3, t, t), _F32),
                   jax.ShapeDtypeStruct((1, _V_DIM), _F32),
                   jax.ShapeDtypeStruct((1, 1), jnp.int32)],
        compiler_params=pltpu.CompilerParams(dimension_semantics=("arbitrary",)),
        name="attn_bias_prep",
    )(rel_table, rel_table, buckets, lam4, gains)


def _attn_body(flag_ref, qta_ref, qtb_ref, k_ref, vt_ref, bias_ref, lam_ref, gain_ref, o_ref,
               qw_ref, m_ref, acc_ref, den_ref, *, t, nq):
    p = pl.program_id(2)
    gran = t // _KEY_GRANULE
    for side, qt_ref in enumerate((qta_ref, qtb_ref)):
        qt = qt_ref[...]
        row = lax.broadcasted_iota(jnp.int32, qt.shape, 0)
        zero = jnp.zeros_like(qt)
        qw_ref[side, :, :t] = jnp.where(row < _SUB_DIM, qt, zero)
        qw_ref[side, :, t:] = jnp.where(row >= _SUB_DIM, qt, zero)
    acc_ref[...] = jnp.zeros(acc_ref.shape, _F32)
    den_ref[...] = jnp.zeros(den_ref.shape, _F32)

    def key_sum(e):
        return jnp.sum(e.reshape(e.shape[0] // 8, 8, e.shape[1]), axis=0)

    def schedule(step_index):
        mirrored = step_index > p
        side = mirrored.astype(jnp.int32)
        kj = jnp.where(mirrored, step_index - p - 1, p - step_index)
        q_tile = jnp.where(mirrored, nq - 1 - p, p)
        bias_index = jnp.where(kj == q_tile, 2, jnp.where(kj == q_tile - 1, 1, 0))
        return side, kj, bias_index

    def apply_to_values(kj, pt):
        pv = jnp.dot(vt_ref[kj * gran, 0], pt[:_KEY_GRANULE], preferred_element_type=_F32)
        for g in range(1, gran):
            pv = pv + jnp.dot(vt_ref[kj * gran + g, 0],
                              pt[g * _KEY_GRANULE:(g + 1) * _KEY_GRANULE],
                              preferred_element_type=_F32)
        return pv

    def key_tile(kj):
        return k_ref[0, pl.ds(pl.multiple_of(kj * t, t), t), :]

    def bounded_step(step_index):
        side, kj, bias_index = schedule(step_index)
        st = jnp.dot(key_tile(kj), qw_ref[side], preferred_element_type=_F32)
        bias = bias_ref[0, bias_index]
        e = [jnp.exp2(st[:, :t] + bias), jnp.exp2(st[:, t:] + bias)]
        den_ref[side] += jnp.concatenate([key_sum(e[0]), key_sum(e[1])], axis=1)
        pt = jnp.concatenate([e[0].astype(_BF16), e[1].astype(_BF16)], axis=1)
        acc_ref[side] += apply_to_values(kj, pt)

    def bounded_diagonal_step(side, q_tile):
        assert gran == 2
        half = _KEY_GRANULE
        late = [slice(half, t), slice(t + half, 2 * t)]
        k0 = pl.multiple_of(q_tile * t, t)
        bias = bias_ref[0, 2]
        st0 = jnp.dot(k_ref[0, pl.ds(k0, half), :], qw_ref[side], preferred_element_type=_F32)
        e0 = [jnp.exp2(st0[:, :t] + bias[:half]), jnp.exp2(st0[:, t:] + bias[:half])]
        den_ref[side] += jnp.concatenate([key_sum(e0[0]), key_sum(e0[1])], axis=1)
        pt0 = jnp.concatenate([e0[0].astype(_BF16), e0[1].astype(_BF16)], axis=1)
        qw_late = jnp.concatenate([qw_ref[side, :, lanes] for lanes in late], axis=1)
        st1 = jnp.dot(k_ref[0, pl.ds(k0 + half, half), :], qw_late, preferred_element_type=_F32)
        e1 = [jnp.exp2(st1[:, :half] + bias[half:, half:]), jnp.exp2(st1[:, half:] + bias[half:, half:])]
        pt1 = jnp.concatenate([e1[0].astype(_BF16), e1[1].astype(_BF16)], axis=1)
        acc_ref[side] += jnp.dot(vt_ref[q_tile * gran, 0], pt0, preferred_element_type=_F32)
        pv1 = jnp.dot(vt_ref[q_tile * gran + 1, 0], pt1, preferred_element_type=_F32)
        for n, lanes in enumerate(late):
            acc_ref[side, :, lanes] += pv1[:, n * half:(n + 1) * half]
            den_ref[side, :, lanes] += key_sum(e1[n])

    def online_step(step_index):
        side, kj, bias_index = schedule(step_index)
        kt = key_tile(kj)
        bias = bias_ref[0, bias_index]
        for s in range(2):
            lanes = slice(s * t, (s + 1) * t)
            st = jnp.dot(kt, qw_ref[side, :, lanes], preferred_element_type=_F32) + bias
            m_old = m_ref[side, s]
            m_new = jnp.maximum(m_old, jnp.max(st, axis=0, keepdims=True))
            alpha = jnp.exp2(m_old - m_new)
            e = jnp.exp2(st - m_new)
            den_ref[side, :, lanes] = alpha * den_ref[side, :, lanes] + key_sum(e)
            acc_ref[side, :, lanes] = alpha * acc_ref[side, :, lanes] + apply_to_values(
                kj, e.astype(_BF16))
            m_ref[side, s] = m_new

    bounded = flag_ref[0, 0] > 0

    @pl.when(bounded)
    def _():
        bounded_diagonal_step(0, p)
        for step_index in range(1, nq):
            bounded_step(step_index)
        bounded_diagonal_step(1, nq - 1 - p)

    @pl.when(jnp.logical_not(bounded))
    def _():
        m_ref[...] = jnp.full(m_ref.shape, -jnp.inf, _F32)
        for step_index in range(nq + 1):
            online_step(step_index)

    for side in range(2):
        o = acc_ref[side] / jnp.sum(den_ref[side], axis=0, keepdims=True)
        od = o[:, :t] - lam_ref[0:1, 0:1] * o[:, t:]
        scale = lax.rsqrt(jnp.mean(od * od, axis=0, keepdims=True) + _EPS)
        o_ref[side, 0, 0] = (od * scale * gain_ref[...]).T.astype(_BF16)


def _attention(flag, qt, k, vt, bias, lam, gain_t):
    b, l, w = k.shape
    t = _ATTN_TILE
    nq = l // t
    vrows = vt.shape[2]
    return pl.pallas_call(
        functools.partial(_attn_body, t=t, nq=nq),
        grid=(b, _HEADS, nq // 2),
        in_specs=[pl.BlockSpec(memory_space=pltpu.SMEM),
                  pl.BlockSpec((_V_DIM, t), lambda bi, h, p: (h, bi * nq + p)),
                  pl.BlockSpec((_V_DIM, t), lambda bi, h, p: (h, bi * nq + nq - 1 - p)),
                  pl.BlockSpec((1, l, _V_DIM), lambda bi, h, p: (bi, 0, h)),
                  pl.BlockSpec((l // _KEY_GRANULE, 1, vrows, _KEY_GRANULE),
                               lambda bi, h, p: (bi, h, 0, 0)),
                  pl.BlockSpec((1, 3, t, t), lambda bi, h, p: (h, 0, 0, 0)),
                  pl.BlockSpec(lam.shape, lambda bi, h, p: (0, 0)),
                  pl.BlockSpec(gain_t.shape, lambda bi, h, p: (0, 0))],
        out_specs=pl.BlockSpec((2, 1, 1, t, _V_DIM), lambda bi, h, p: (0, bi, p, 0, h)),
        out_shape=jax.ShapeDtypeStruct((2, b, nq // 2, t, w), _BF16),
        scratch_shapes=[pltpu.VMEM((2, _V_DIM, 2 * t), _BF16),
                        pltpu.VMEM((2, 2, 1, t), _F32),
                        pltpu.VMEM((2, vrows, 2 * t), _F32),
                        pltpu.VMEM((2, 8, 2 * t), _F32)],
        compiler_params=pltpu.CompilerParams(
            dimension_semantics=("parallel", "parallel", "arbitrary"),
            vmem_limit_bytes=_VMEM_LIMIT),
        name="diff_attention",
    )(flag, qt, qt, k, vt, bias, lam, gain_t)


def _attn_bounded_body(qt_ref, k_ref, vt_ref, bias_ref, lam_ref, gain_ref, o_ref,
                       qw_ref, acc_ref, den_ref, *, t, nq):
    gran = t // _KEY_GRANULE
    assert gran == 2
    half = _KEY_GRANULE
    row = lax.broadcasted_iota(jnp.int32, (_V_DIM, t), 0)
    for q in range(nq):
        qt = qt_ref[:, q * t:(q + 1) * t]
        zero = jnp.zeros_like(qt)
        qw_ref[q, :, :t] = jnp.where(row < _SUB_DIM, qt, zero)
        qw_ref[q, :, t:] = jnp.where(row >= _SUB_DIM, qt, zero)
    acc_ref[...] = jnp.zeros(acc_ref.shape, _F32)
    den_ref[...] = jnp.zeros(den_ref.shape, _F32)

    def key_sum(e):
        return jnp.sum(e.reshape(e.shape[0] // 8, 8, e.shape[1]), axis=0)

    def diagonal(kj):
        late = [slice(half, t), slice(t + half, 2 * t)]
        k0 = kj * t
        bias = bias_ref[0, 2]
        st0 = jnp.dot(k_ref[0, k0:k0 + half, :], qw_ref[kj], preferred_element_type=_F32)
        e0 = [jnp.exp2(st0[:, :t] + bias[:half]), jnp.exp2(st0[:, t:] + bias[:half])]
        den_ref[kj] += jnp.concatenate([key_sum(e0[0]), key_sum(e0[1])], axis=1)
        pt0 = jnp.concatenate([e0[0].astype(_BF16), e0[1].astype(_BF16)], axis=1)
        qw_late = jnp.concatenate([qw_ref[kj, :, lanes] for lanes in late], axis=1)
        st1 = jnp.dot(k_ref[0, k0 + half:k0 + t, :], qw_late, preferred_element_type=_F32)
        e1 = [jnp.exp2(st1[:, :half] + bias[half:, half:]), jnp.exp2(st1[:, half:] + bias[half:, half:])]
        pt1 = jnp.concatenate([e1[0].astype(_BF16), e1[1].astype(_BF16)], axis=1)
        acc_ref[kj] += jnp.dot(vt_ref[kj * gran, 0], pt0, preferred_element_type=_F32)
        pv1 = jnp.dot(vt_ref[kj * gran + 1, 0], pt1, preferred_element_type=_F32)
        for n, lanes in enumerate(late):
            acc_ref[kj, :, lanes] += pv1[:, n * half:(n + 1) * half]
            den_ref[kj, :, lanes] += key_sum(e1[n])

    def later_queries(kj, q_tiles):
        qw = jnp.concatenate([qw_ref[q] for q in q_tiles], axis=1)
        st = jnp.dot(k_ref[0, kj * t:(kj + 1) * t, :], qw, preferred_element_type=_F32)
        if q_tiles[0] == kj + 1:
            blk = _BIAS_BLOCK
            corner = bias_ref[0, 1, t - blk:, :blk]
            tail = st[t - blk:]
            tail = jnp.concatenate([tail[:, :blk] + corner, tail[:, blk:t],
                                    tail[:, t:t + blk] + corner, tail[:, t + blk:]], axis=1)
            st = jnp.concatenate([st[:t - blk], tail], axis=0)
        e = jnp.exp2(st)
        pv = (jnp.dot(vt_ref[kj * gran, 0], e[:half].astype(_BF16), preferred_element_type=_F32)
              + jnp.dot(vt_ref[kj * gran + 1, 0], e[half:].astype(_BF16), preferred_element_type=_F32))
        for n, q in enumerate(q_tiles):
            lanes = slice(n * 2 * t, (n + 1) * 2 * t)
            den_ref[q] += key_sum(e[:, lanes])
            acc_ref[q] += pv[:, lanes]

    for kj in range(nq):
        diagonal(kj)
        later = list(range(kj + 1, nq))
        for g0 in range(0, len(later), _QUERY_GROUP):
            later_queries(kj, later[g0:g0 + _QUERY_GROUP])

    for q in range(nq):
        o = acc_ref[q] / jnp.sum(den_ref[q], axis=0, keepdims=True)
        od = o[:, :t] - lam_ref[0:1, 0:1] * o[:, t:]
        scale = lax.rsqrt(jnp.mean(od * od, axis=0, keepdims=True) + _EPS)
        o_ref[0, q * t:(q + 1) * t, :] = (od * scale * gain_ref[...]).T.astype(_BF16)


def _attention_bounded(qt, k, vt, bias, lam, gain_t):
    b, l, w = k.shape
    t = _ATTN_TILE
    nq = l // t
    return pl.pallas_call(
        functools.partial(_attn_bounded_body, t=t, nq=nq),
        grid=(b, _HEADS),
        in_specs=[pl.BlockSpec((_V_DIM, l), lambda bi, h: (h, bi)),
                  pl.BlockSpec((1, l, _V_DIM), lambda bi, h: (bi, 0, h)),
                  pl.BlockSpec((l // _KEY_GRANULE, 1, _V_DIM, _KEY_GRANULE), lambda bi, h: (bi, h, 0, 0)),
                  pl.BlockSpec((1, 3, t, t), lambda bi, h: (h, 0, 0, 0)),
                  pl.BlockSpec(lam.shape, lambda bi, h: (0, 0)),
                  pl.BlockSpec(gain_t.shape, lambda bi, h: (0, 0))],
        out_specs=pl.BlockSpec((1, l, _V_DIM), lambda bi, h: (bi, 0, h)),
        out_shape=jax.ShapeDtypeStruct((b, l, w), _BF16),
        scratch_shapes=[pltpu.VMEM((nq, _V_DIM, 2 * t), _BF16),
                        pltpu.VMEM((nq, _V_DIM, 2 * t), _F32),
                        pltpu.VMEM((nq, 8, 2 * t), _F32)],
        compiler_params=pltpu.CompilerParams(
            dimension_semantics=("parallel", "parallel"), vmem_limit_bytes=_VMEM_LIMIT),
        name="diff_attention_bounded",
    )(qt, k, vt, bias, lam, gain_t)


def _s5_prep_body(are_ref, aim_ref, ldt_ref, bre_ref, bim_ref,
                  lbre_ref, lbim_ref, bbre_ref, bbim_ref):
    a_re = are_ref[...]
    a_im = aim_ref[...]
    dt = jnp.exp(ldt_ref[...])
    decay = jnp.exp(a_re * dt)
    lb_re = decay * jnp.cos(a_im * dt)
    lb_im = decay * jnp.sin(a_im * dt)
    nr = lb_re - 1.0
    ni = lb_im
    den = a_re * a_re + a_im * a_im
    q_re = (nr * a_re + ni * a_im) / den
    q_im = (ni * a_re - nr * a_im) / den
    b_re = bre_ref[...]
    b_im = bim_ref[...]
    bbre_ref[...] = q_re * b_re - q_im * b_im
    bbim_ref[...] = q_re * b_im + q_im * b_re
    lbre_ref[...] = lb_re
    lbim_ref[...] = lb_im


def _s5_prep(a_re, a_im, log_dt, b_re, b_im):
    sd = jax.ShapeDtypeStruct(a_re.shape, _F32)
    return pl.pallas_call(_s5_prep_body, out_shape=[sd] * 4, name="s5_discretise")(
        a_re, a_im, log_dt, b_re, b_im)


def _s5_body(u_ref, perm_ref, permt_ref, wbre_ref, wbim_ref, lbre_ref, lbim_ref,
             wcre_ref, wcim_ref, d_ref, gw_ref, gb_ref, o_ref,
             utm_a, bre_a, bim_a, utm_b, bre_b, bim_b, xre_ref, xim_ref, cre_ref, cim_ref,
             *, steps, nb):
    i = pl.program_id(0)
    rows = nb * steps
    width = u_ref.shape[-1]
    n_tiles, _, tile_w = lbre_ref.shape
    halves = wcre_ref.shape[0]
    tiles_per_half = n_tiles // halves
    hw_in = width // halves

    @pl.when(i == 0)
    def _():
        cre_ref[...] = jnp.zeros(cre_ref.shape, _F32)
        cim_ref[...] = jnp.zeros(cim_ref.shape, _F32)
        bre_b[...] = jnp.zeros(bre_b.shape, _F32)
        bim_b[...] = jnp.zeros(bim_b.shape, _F32)
        utm_b[...] = jnp.zeros(utm_b.shape, _BF16)

    def step(utm_new, bre_new, bim_new, utm_old, bre_old, bim_old):
        u_bm = u_ref[...].reshape(rows, width)
        u_tm = jnp.dot(perm_ref[...], u_bm, preferred_element_type=_F32).astype(_BF16)
        utm_new[...] = u_tm

        for hf in range(halves):
            uh = u_tm[:, hf * hw_in:(hf + 1) * hw_in]
            bu_r = jnp.dot(uh, wbre_ref[hf], preferred_element_type=_F32)
            bu_i = jnp.dot(uh, wbim_ref[hf], preferred_element_type=_F32)
            for q in range(tiles_per_half):
                lanes = slice(q * tile_w, (q + 1) * tile_w)
                bre_new[hf * tiles_per_half + q] = bu_r[:, lanes]
                bim_new[hf * tiles_per_half + q] = bu_i[:, lanes]

        for j0 in range(0, n_tiles, _SCAN_GROUP):
            group = range(j0, j0 + _SCAN_GROUP)
            a_r = [jnp.broadcast_to(lbre_ref[j], (nb, tile_w)) for j in group]
            a_i = [jnp.broadcast_to(lbim_ref[j], (nb, tile_w)) for j in group]
            x_r = [cre_ref[j] for j in group]
            x_i = [cim_ref[j] for j in group]
            for ti in range(steps):
                r = slice(ti * nb, (ti + 1) * nb)
                for q, j in enumerate(group):
                    n_r = a_r[q] * x_r[q] - a_i[q] * x_i[q] + bre_old[j, r, :]
                    n_i = a_r[q] * x_i[q] + a_i[q] * x_r[q] + bim_old[j, r, :]
                    xre_ref[j, r, :] = n_r
                    xim_ref[j, r, :] = n_i
                    x_r[q], x_i[q] = n_r, n_i
            for q, j in enumerate(group):
                cre_ref[j] = x_r[q]
                cim_ref[j] = x_i[q]

        ys = []
        for hf in range(halves):
            tiles = range(hf * tiles_per_half, (hf + 1) * tiles_per_half)
            x_r = jnp.concatenate([xre_ref[j] for j in tiles], axis=1).astype(_BF16)
            x_i = jnp.concatenate([xim_ref[j] for j in tiles], axis=1).astype(_BF16)
            ys.append(jnp.dot(x_r, wcre_ref[hf], preferred_element_type=_F32)
                      + jnp.dot(x_i, wcim_ref[hf], preferred_element_type=_F32))
        y = jnp.concatenate(ys, axis=-1) + d_ref[...] * utm_old[...].astype(_F32)
        g = jax.nn.gelu(y)
        o = g * jax.nn.sigmoid(jnp.dot(g.astype(_BF16), gw_ref[...], preferred_element_type=_F32)
                               + gb_ref[...])
        o_bm = jnp.dot(permt_ref[...], o.astype(_BF16), preferred_element_type=_F32)
        o_ref[...] = o_bm.astype(_BF16).reshape(o_ref.shape)

    @pl.when(i % 2 == 0)
    def _():
        step(utm_a, bre_a, bim_a, utm_b, bre_b, bim_b)

    @pl.when(i % 2 == 1)
    def _():
        step(utm_b, bre_b, bim_b, utm_a, bre_a, bim_a)


def _s5(u, perm, permt, wb_re, wb_im, lb_re, lb_im, wc_re, wc_im, d_vec, glu_w, glu_b):
    nb, l, width = u.shape
    steps = _SCAN_STEPS
    n_blocks = l // steps
    n_tiles, _, tile_w = lb_re.shape
    full = lambda a: pl.BlockSpec(a.shape, lambda i: (0,) * a.ndim)
    consts = (perm, permt, wb_re, wb_im, lb_re, lb_im, wc_re, wc_im, d_vec, glu_w, glu_b)
    utm_buf = pltpu.VMEM((nb * steps, width), _BF16)
    state_buf = pltpu.VMEM((n_tiles, nb * steps, tile_w), _F32)
    return pl.pallas_call(
        functools.partial(_s5_body, steps=steps, nb=nb),
        grid=(n_blocks + 1,),
        in_specs=[pl.BlockSpec((nb, steps, width), lambda i: (0, jnp.minimum(i, n_blocks - 1), 0))]
                 + [full(a) for a in consts],
        out_specs=pl.BlockSpec((nb, steps, width), lambda i: (0, jnp.maximum(i - 1, 0), 0)),
        out_shape=jax.ShapeDtypeStruct(u.shape, _BF16),
        scratch_shapes=[utm_buf, state_buf, state_buf, utm_buf, state_buf, state_buf,
                        state_buf, state_buf,
                        pltpu.VMEM((n_tiles, nb, tile_w), _F32),
                        pltpu.VMEM((n_tiles, nb, tile_w), _F32)],
        compiler_params=pltpu.CompilerParams(
            dimension_semantics=("arbitrary",), vmem_limit_bytes=_VMEM_LIMIT),
        name="s5_branch",
    )(u, *consts)


def _outproj_body(x_ref, ng_ref, wg_ref, mb_ref, oa_ref, os_ref, pa_ref, ps_ref, wo_ref,
                  out_ref, *, width):
    x = x_ref[...]
    d = x.shape[-1]
    h = (x * _rms_scale(x) * ng_ref[...]).astype(_BF16)
    zg = jnp.dot(h, wg_ref[...], preferred_element_type=_F32)
    o_a = (oa_ref[...].astype(_F32) * jax.nn.silu(zg[:, :width])).astype(_BF16)
    o_s = (os_ref[...].astype(_F32) * jax.nn.silu(zg[:, width:2 * width])).astype(_BF16)
    p_a = jnp.dot(o_a, pa_ref[...], preferred_element_type=_F32)
    p_s = jnp.dot(o_s, ps_ref[...], preferred_element_type=_F32)
    g = jax.nn.sigmoid(zg[:, 2 * width:] + mb_ref[...])
    merged = g[:, :d] * p_a + g[:, d:] * p_s
    out_ref[...] = x + jnp.dot(merged.astype(_BF16), wo_ref[...], preferred_element_type=_F32)


def _outproj(x2, norm_gain, w_gates, merge_b, o_a, o_s, proj_a, proj_s, w_out, width):
    n, d = x2.shape
    tm = _TOKEN_TILE
    full = lambda a: pl.BlockSpec(a.shape, lambda i: (0,) * a.ndim)
    row = lambda a: pl.BlockSpec((tm, a.shape[-1]), lambda i: (i, 0))
    return pl.pallas_call(
        functools.partial(_outproj_body, width=width),
        grid=(n // tm,),
        in_specs=[row(x2), full(norm_gain), full(w_gates), full(merge_b), row(o_a), row(o_s),
                  full(proj_a), full(proj_s), full(w_out)],
        out_specs=row(x2),
        out_shape=jax.ShapeDtypeStruct(x2.shape, x2.dtype),
        compiler_params=pltpu.CompilerParams(
            dimension_semantics=("parallel",), vmem_limit_bytes=_VMEM_LIMIT),
        name="outproj",
    )(x2, norm_gain, w_gates, merge_b, o_a, o_s, proj_a, proj_s, w_out)


def _block_diag_halves(blocks, halves):
    g, r, c = blocks.shape
    gh = g // halves
    eye = jnp.eye(gh, dtype=blocks.dtype)
    b = blocks.reshape(halves, gh, r, 1, c) * eye.reshape(1, gh, 1, gh, 1)
    return b.reshape(halves, gh * r, gh * c)


def _time_major_perm(nb, steps):
    r = np.arange(nb * steps)
    src = (r % nb) * steps + r // nb
    p = np.zeros((nb * steps, nb * steps), np.float32)
    p[r, src] = 1.0
    return p


def _layer(x, lam_init, norm_gain, w_in, merge_gate_b, q_norm_gain, k_norm_gain,
           lambda_q1, lambda_k1, lambda_q2, lambda_k2, diff_subln_gain, rel_bias_table,
           ssm_a_re, ssm_a_im, ssm_log_dt, ssm_b_re, ssm_b_im, ssm_c_re, ssm_c_im,
           ssm_d, ssm_glu_w, ssm_glu_b, proj_attn, proj_ssm, w_out):
    nb, l, d = x.shape
    n = nb * l
    aw = _HEADS * 2 * _SUB_DIM
    groups = ssm_a_re.shape[0]
    x2 = x.reshape(n, d)
    ng = norm_gain.reshape(1, d).astype(_F32)

    w_ku = jnp.concatenate([w_in[:, aw:2 * aw], w_in[:, 4 * aw:5 * aw]], axis=1).astype(_BF16)
    w_qvt = jnp.concatenate([w_in[:, :aw], w_in[:, 2 * aw:3 * aw]], axis=1).T.astype(_BF16)
    w_gates = jnp.concatenate([w_in[:, 3 * aw:4 * aw], w_in[:, 5 * aw:]], axis=1).astype(_BF16)
    gq = jnp.tile(q_norm_gain.astype(_F32), 2 * _HEADS) * (_SUB_DIM ** -0.5 * _LOG2E)
    gqt = jnp.broadcast_to(gq[:, None], (aw, _TOKEN_TILE))
    gk = jnp.tile(k_norm_gain.astype(_F32), 2 * _HEADS).reshape(1, aw)
    seg = np.arange(aw) // _SUB_DIM
    gsum = jnp.asarray((seg[:, None] == seg[None, :]).astype(np.float32), _BF16)
    qt, k, vt, u = _inproj(x2, ng, w_ku, w_qvt, gqt, gk, gsum, aw)

    lam4 = jnp.stack([lambda_q1, lambda_k1, lambda_q2, lambda_k2]).astype(_F32)
    qk_gains = jnp.stack([gq[:_SUB_DIM], gk[0, :_SUB_DIM]])
    bias, lam, flag = _bias_prep(rel_bias_table.astype(_F32), lam4, qk_gains, lam_init, _ATTN_TILE)
    subln = diff_subln_gain.astype(_F32) * (1.0 - lam_init)
    subln_t = jnp.broadcast_to(subln[:, None], (_V_DIM, _ATTN_TILE))
    shp = (nb, l, aw)
    k3 = k.reshape(shp)

    def online_attention():
        o = _attention(flag, qt, k3, vt, bias, lam, subln_t)
        return jnp.concatenate([o[0], o[1][:, ::-1]], axis=1).reshape(shp)

    o_a = lax.cond(flag[0, 0] > 0,
                   lambda: _attention_bounded(qt, k3, vt, bias, lam, subln_t),
                   online_attention)

    rep = lambda a: jnp.repeat(a.astype(_F32), _SSM_GROUP, axis=0)
    ldt = jnp.broadcast_to(ssm_log_dt.astype(_F32)[:, None], ssm_a_re.shape)
    bt = lambda a: a.astype(_F32).transpose(0, 2, 1).reshape(groups * _SSM_GROUP, _SSM_STATE)
    lb_re, lb_im, bb_re, bb_im = _s5_prep(rep(ssm_a_re), rep(ssm_a_im), rep(ldt),
                                          bt(ssm_b_re), bt(ssm_b_im))
    n_tiles = groups * _SSM_STATE // _STATE_TILE
    flat = lambda a: a[::_SSM_GROUP].reshape(n_tiles, 1, _STATE_TILE)
    gshape = (groups, _SSM_GROUP, _SSM_STATE)
    wb_re = _block_diag_halves(bb_re.reshape(gshape), 2).astype(_BF16)
    wb_im = _block_diag_halves(bb_im.reshape(gshape), 2).astype(_BF16)
    wc_re = _block_diag_halves(ssm_c_re.astype(_F32).transpose(0, 2, 1), 2).astype(_BF16)
    wc_im = _block_diag_halves(-ssm_c_im.astype(_F32).transpose(0, 2, 1), 2).astype(_BF16)
    perm = _time_major_perm(nb, _SCAN_STEPS)
    o_s = _s5(u.reshape(shp), jnp.asarray(perm, _BF16), jnp.asarray(perm.T, _BF16),
              wb_re, wb_im, flat(lb_re), flat(lb_im), wc_re, wc_im,
              ssm_d.astype(_F32).reshape(1, aw), ssm_glu_w.astype(_BF16),
              ssm_glu_b.astype(_F32).reshape(1, aw))

    out = _outproj(x2, ng, w_gates, merge_gate_b.astype(_F32).reshape(1, 2 * d),
                   o_a.reshape(n, aw), o_s.reshape(n, aw),
                   proj_attn.astype(_BF16), proj_ssm.astype(_BF16), w_out.astype(_BF16), aw)
    return out.reshape(nb, l, d)


def kernel(x, norm_gain, w_in, merge_gate_b, q_norm_gain, k_norm_gain, lambda_q1, lambda_k1,
           lambda_q2, lambda_k2, diff_subln_gain, rel_bias_table, ssm_A_re, ssm_A_im, ssm_log_dt,
           ssm_B_re, ssm_B_im, ssm_C_re, ssm_C_im, ssm_D, ssm_glu_w, ssm_glu_b,
           proj_attn, proj_ssm, w_out):
    per_layer = (norm_gain, w_in, merge_gate_b, q_norm_gain, k_norm_gain, lambda_q1, lambda_k1,
                 lambda_q2, lambda_k2, diff_subln_gain)
    per_layer_tail = (ssm_A_re, ssm_A_im, ssm_log_dt, ssm_B_re, ssm_B_im, ssm_C_re, ssm_C_im,
                      ssm_D, ssm_glu_w, ssm_glu_b, proj_attn, proj_ssm, w_out)
    for layer in range(norm_gain.shape[0]):
        lam_init = 0.8 - 0.6 * math.exp(-0.3 * layer)
        x = _layer(x, lam_init, *(p[layer] for p in per_layer), rel_bias_table,
                   *(p[layer] for p in per_layer_tail))
    return x
```

```python
import functools
import math

import jax
import jax.numpy as jnp
import numpy as np
from jax import lax
from jax.experimental import pallas as pl
from jax.experimental.pallas import tpu as pltpu

_F32 = jnp.float32
_BF16 = jnp.bfloat16

_CHUNK = 64
_HEADS = 4
_SUB_DIM = 64
_V_DIM = 128
_SSM_GROUP = 16
_SSM_STATE = 64
_REL_BUCKETS = 32
_REL_MAX_DIST = 128
_EPS = 1e-6
_LOG2E = math.log2(math.e)
_SAFE_LOG2_RANGE = 100.0

_VMEM_LIMIT = 48 * 1024 * 1024
_MXU_TILE = 256
_TOKEN_TILE = 1024
_ATTN_TILE = 512
_KEY_GRANULE = 256
_BIAS_BLOCK = 128
_QUERY_GROUP = 2
_SCAN_STEPS = 64
_STATE_TILE = 256
_SCAN_GROUP = 4


def _rms_scale(x, eps=_EPS):
    return lax.rsqrt(jnp.mean(x * x, axis=-1, keepdims=True) + eps)


def _inproj_body(x_ref, ng_ref, wku_ref, wqvt_ref, gqt_ref, gk_ref, gsum_ref,
                 qt_ref, k_ref, vt_ref, u_ref, *, width):
    x = x_ref[...]
    tm = x.shape[0]
    h = (x * _rms_scale(x) * ng_ref[...]).astype(_BF16)
    z = jnp.dot(h, wku_ref[...], preferred_element_type=_F32)
    zt = lax.dot_general(wqvt_ref[...], h, (((1,), (1,)), ((), ())),
                         preferred_element_type=_F32)

    kk = z[:, :width]
    k2 = (kk * kk).astype(_BF16)
    mxu = gsum_ref.shape[0]
    ss = jnp.concatenate([jnp.dot(k2[:, c:c + mxu], gsum_ref[...], preferred_element_type=_F32)
                          for c in range(0, width, mxu)], axis=1)
    k_ref[...] = (kk * lax.rsqrt(ss * (1.0 / _SUB_DIM) + _EPS) * gk_ref[...]).astype(_BF16)
    u_ref[...] = z[:, width:].astype(_BF16)

    qt = zt[:width].reshape(width // _SUB_DIM, _SUB_DIM, tm)
    ms = jnp.mean(qt * qt, axis=1, keepdims=True)
    qt_ref[...] = ((qt * lax.rsqrt(ms + _EPS)).reshape(width, tm) * gqt_ref[...]).astype(_BF16)

    vt = zt[width:].astype(_BF16)
    for j in range(tm // _KEY_GRANULE):
        for hd in range(_HEADS):
            vt_ref[j, hd] = vt[hd * _V_DIM:(hd + 1) * _V_DIM,
                               j * _KEY_GRANULE:(j + 1) * _KEY_GRANULE]


def _inproj(x2, norm_gain, w_ku, w_qvt, gqt, gk, gsum, width):
    n, d = x2.shape
    tm = _TOKEN_TILE
    full = lambda a: pl.BlockSpec(a.shape, lambda i: (0,) * a.ndim)
    row_sd = jax.ShapeDtypeStruct((n, width), _BF16)
    row_spec = pl.BlockSpec((tm, width), lambda i: (i, 0))
    gran = tm // _KEY_GRANULE
    vrows = _V_DIM
    return pl.pallas_call(
        functools.partial(_inproj_body, width=width),
        grid=(n // tm,),
        in_specs=[pl.BlockSpec((tm, d), lambda i: (i, 0)), full(norm_gain), full(w_ku), full(w_qvt),
                  full(gqt), full(gk), full(gsum)],
        out_specs=[pl.BlockSpec((width, tm), lambda i: (0, i)), row_spec,
                   pl.BlockSpec((gran, _HEADS, vrows, _KEY_GRANULE), lambda i: (i, 0, 0, 0)), row_spec],
        out_shape=[jax.ShapeDtypeStruct((width, n), _BF16), row_sd,
                   jax.ShapeDtypeStruct((n // _KEY_GRANULE, _HEADS, vrows, _KEY_GRANULE), _BF16), row_sd],
        compiler_params=pltpu.CompilerParams(
            dimension_semantics=("parallel",), vmem_limit_bytes=_VMEM_LIMIT),
        name="inproj",
    )(x2, norm_gain, w_ku, w_qvt, gqt, gk, gsum)


def _t5_bucket_np(rel):
    nb = _REL_BUCKETS // 2
    max_exact = nb // 2
    side = np.where(rel > 0, nb, 0)
    n = np.abs(rel)
    nf = np.maximum(n, 1).astype(np.float32)
    large = max_exact + (np.log(nf / np.float32(max_exact)) / np.float32(math.log(_REL_MAX_DIST / max_exact))
                         * np.float32(nb - max_exact)).astype(np.int32)
    large = np.minimum(large, nb - 1)
    return side + np.where(n < max_exact, n, large)


def _bucket_blocks():
    assert _BIAS_BLOCK % _CHUNK == 0
    i = np.arange(_BIAS_BLOCK)[None, :]
    j = np.arange(_BIAS_BLOCK)[:, None]
    far_bucket = _REL_BUCKETS // 2 - 1
    assert (_t5_bucket_np(j - i - 2 * _BIAS_BLOCK) == far_bucket).all()
    diag = np.where((j // _CHUNK) <= (i // _CHUNK), _t5_bucket_np(j - i), -1)
    prev = _t5_bucket_np(j - i - _BIAS_BLOCK)
    return np.stack([diag, prev]).astype(np.int32)


def _bias_prep_body(tab_ref, tabv_ref, bucket_ref, lam4_ref, gains_ref,
                    bias_ref, lam_ref, flag_ref, *, lam_init, t):
    h = pl.program_id(0)
    far_bucket = _REL_BUCKETS // 2 - 1
    bkt = bucket_ref[...]
    far = tab_ref[far_bucket, h]
    val = jnp.full(bkt.shape, -jnp.inf, _F32)
    for b in range(_REL_BUCKETS):
        val = jnp.where(bkt == b, (tab_ref[b, h] - far) * _LOG2E, val)
    diag_block, prev_block = val[0], val[1]
    n_blk = t // _BIAS_BLOCK
    bias_ref[0, 0] = jnp.zeros((t, t), _F32)
    bias_ref[0, 1] = jnp.zeros((t, t), _F32)
    bias_ref[0, 1, t - _BIAS_BLOCK:, :_BIAS_BLOCK] = prev_block
    for bj in range(n_blk):
        rows = slice(bj * _BIAS_BLOCK, (bj + 1) * _BIAS_BLOCK)
        for bi in range(n_blk):
            cols = slice(bi * _BIAS_BLOCK, (bi + 1) * _BIAS_BLOCK)
            if bi == bj:
                block = diag_block
            elif bi == bj + 1:
                block = prev_block
            else:
                block = jnp.full((_BIAS_BLOCK, _BIAS_BLOCK), 0.0 if bi > bj else -jnp.inf, _F32)
            bias_ref[0, 2, rows, cols] = block
    l4 = lam4_ref[...]
    s1 = jnp.sum(l4[0:1] * l4[1:2], axis=-1, keepdims=True)
    s2 = jnp.sum(l4[2:3] * l4[3:4], axis=-1, keepdims=True)
    lam_ref[...] = jnp.broadcast_to(jnp.exp(s1) - jnp.exp(s2) + lam_init, lam_ref.shape)
    gmax = jnp.max(jnp.abs(gains_ref[...]), axis=-1, keepdims=True)
    qk_bound = (_SUB_DIM * 1.02) * gmax[0:1] * gmax[1:2]
    tv = tabv_ref[...]
    dev = jnp.abs(tv - tv[far_bucket:far_bucket + 1]) * _LOG2E
    bias_bound = jnp.max(jnp.max(dev, axis=-1, keepdims=True), axis=0, keepdims=True)
    flag_ref[...] = (qk_bound + bias_bound <= _SAFE_LOG2_RANGE).astype(jnp.int32)


def _bias_prep(rel_table, lam4, gains, lam_init, t):
    assert t % _BIAS_BLOCK == 0
    buckets = jnp.asarray(_bucket_blocks())
    whole = lambda a: pl.BlockSpec(a.shape, lambda h: (0,) * a.ndim)
    return pl.pallas_call(
        functools.partial(_bias_prep_body, lam_init=lam_init, t=t),
        grid=(_HEADS,),
        in_specs=[pl.BlockSpec(memory_space=pltpu.SMEM), whole(rel_table), whole(buckets),
                  whole(lam4), whole(gains)],
        out_specs=[pl.BlockSpec((1, 3, t, t), lambda h: (h, 0, 0, 0)),
                   pl.BlockSpec((1, _V_DIM), lambda h: (0, 0)),
                   pl.BlockSpec((1, 1), lambda h: (0, 0))],
        out_shape=[jax.ShapeDtypeStruct((_HEADS, 3, t, t), _F32),
                   jax.ShapeDtypeStruct((1, _V_DIM), _F32),
                   jax.ShapeDtypeStruct((1, 1), jnp.int32)],
        compiler_params=pltpu.CompilerParams(dimension_semantics=("arbitrary",)),
        name="attn_bias_prep",
    )(rel_table, rel_table, buckets, lam4, gains)


def _attn_body(flag_ref, qta_ref, qtb_ref, k_ref, vt_ref, bias_ref, lam_ref, gain_ref, o_ref,
               qw_ref, m_ref, acc_ref, den_ref, *, t, nq):
    p = pl.program_id(2)
    gran = t // _KEY_GRANULE
    for side, qt_ref in enumerate((qta_ref, qtb_ref)):
        qt = qt_ref[...]
        row = lax.broadcasted_iota(jnp.int32, qt.shape, 0)
        zero = jnp.zeros_like(qt)
        qw_ref[side, :, :t] = jnp.where(row < _SUB_DIM, qt, zero)
        qw_ref[side, :, t:] = jnp.where(row >= _SUB_DIM, qt, zero)
    acc_ref[...] = jnp.zeros(acc_ref.shape, _F32)
    den_ref[...] = jnp.zeros(den_ref.shape, _F32)

    def key_sum(e):
        return jnp.sum(e.reshape(e.shape[0] // 8, 8, e.shape[1]), axis=0)

    def schedule(step_index):
        mirrored = step_index > p
        side = mirrored.astype(jnp.int32)
        kj = jnp.where(mirrored, step_index - p - 1, p - step_index)
        q_tile = jnp.where(mirrored, nq - 1 - p, p)
        bias_index = jnp.where(kj == q_tile, 2, jnp.where(kj == q_tile - 1, 1, 0))
        return side, kj, bias_index

    def apply_to_values(kj, pt):
        pv = jnp.dot(vt_ref[kj * gran, 0], pt[:_KEY_GRANULE], preferred_element_type=_F32)
        for g in range(1, gran):
            pv = pv + jnp.dot(vt_ref[kj * gran + g, 0],
                              pt[g * _KEY_GRANULE:(g + 1) * _KEY_GRANULE],
                              preferred_element_type=_F32)
        return pv

    def key_tile(kj):
        return k_ref[0, pl.ds(pl.multiple_of(kj * t, t), t), :]

    def bounded_step(step_index):
        side, kj, bias_index = schedule(step_index)
        st = jnp.dot(key_tile(kj), qw_ref[side], preferred_element_type=_F32)
        bias = bias_ref[0, bias_index]
        e = [jnp.exp2(st[:, :t] + bias), jnp.exp2(st[:, t:] + bias)]
        den_ref[side] += jnp.concatenate([key_sum(e[0]), key_sum(e[1])], axis=1)
        pt = jnp.concatenate([e[0].astype(_BF16), e[1].astype(_BF16)], axis=1)
        acc_ref[side] += apply_to_values(kj, pt)

    def bounded_diagonal_step(side, q_tile):
        assert gran == 2
        half = _KEY_GRANULE
        late = [slice(half, t), slice(t + half, 2 * t)]
        k0 = pl.multiple_of(q_tile * t, t)
        bias = bias_ref[0, 2]
        st0 = jnp.dot(k_ref[0, pl.ds(k0, half), :], qw_ref[side], preferred_element_type=_F32)
        e0 = [jnp.exp2(st0[:, :t] + bias[:half]), jnp.exp2(st0[:, t:] + bias[:half])]
        den_ref[side] += jnp.concatenate([key_sum(e0[0]), key_sum(e0[1])], axis=1)
        pt0 = jnp.concatenate([e0[0].astype(_BF16), e0[1].astype(_BF16)], axis=1)
        qw_late = jnp.concatenate([qw_ref[side, :, lanes] for lanes in late], axis=1)
        st1 = jnp.dot(k_ref[0, pl.ds(k0 + half, half), :], qw_late, preferred_element_type=_F32)
        e1 = [jnp.exp2(st1[:, :half] + bias[half:, half:]), jnp.exp2(st1[:, half:] + bias[half:, half:])]
        pt1 = jnp.concatenate([e1[0].astype(_BF16), e1[1].astype(_BF16)], axis=1)
        acc_ref[side] += jnp.dot(vt_ref[q_tile * gran, 0], pt0, preferred_element_type=_F32)
        pv1 = jnp.dot(vt_ref[q_tile * gran + 1, 0], pt1, preferred_element_type=_F32)
        for n, lanes in enumerate(late):
            acc_ref[side, :, lanes] += pv1[:, n * half:(n + 1) * half]
            den_ref[side, :, lanes] += key_sum(e1[n])

    def online_step(step_index):
        side, kj, bias_index = schedule(step_index)
        kt = key_tile(kj)
        bias = bias_ref[0, bias_index]
        for s in range(2):
            lanes = slice(s * t, (s + 1) * t)
            st = jnp.dot(kt, qw_ref[side, :, lanes], preferred_element_type=_F32) + bias
            m_old = m_ref[side, s]
            m_new = jnp.maximum(m_old, jnp.max(st, axis=0, keepdims=True))
            alpha = jnp.exp2(m_old - m_new)
            e = jnp.exp2(st - m_new)
            den_ref[side, :, lanes] = alpha * den_ref[side, :, lanes] + key_sum(e)
            acc_ref[side, :, lanes] = alpha * acc_ref[side, :, lanes] + apply_to_values(
                kj, e.astype(_BF16))
            m_ref[side, s] = m_new

    bounded = flag_ref[0, 0] > 0

    @pl.when(bounded)
    def _():
        bounded_diagonal_step(0, p)
        for step_index in range(1, nq):
            bounded_step(step_index)
        bounded_diagonal_step(1, nq - 1 - p)

    @pl.when(jnp.logical_not(bounded))
    def _():
        m_ref[...] = jnp.full(m_ref.shape, -jnp.inf, _F32)
        for step_index in range(nq + 1):
            online_step(step_index)

    for side in range(2):
        o = acc_ref[side] / jnp.sum(den_ref[side], axis=0, keepdims=True)
        od = o[:, :t] - lam_ref[0:1, 0:1] * o[:, t:]
        scale = lax.rsqrt(jnp.mean(od * od, axis=0, keepdims=True) + _EPS)
        o_ref[side, 0, 0] = (od * scale * gain_ref[...]).T.astype(_BF16)


def _attention(flag, qt, k, vt, bias, lam, gain_t):
    b, l, w = k.shape
    t = _ATTN_TILE
    nq = l // t
    vrows = vt.shape[2]
    return pl.pallas_call(
        functools.partial(_attn_body, t=t, nq=nq),
        grid=(b, _HEADS, nq // 2),
        in_specs=[pl.BlockSpec(memory_space=pltpu.SMEM),
                  pl.BlockSpec((_V_DIM, t), lambda bi, h, p: (h, bi * nq + p)),
                  pl.BlockSpec((_V_DIM, t), lambda bi, h, p: (h, bi * nq + nq - 1 - p)),
                  pl.BlockSpec((1, l, _V_DIM), lambda bi, h, p: (bi, 0, h)),
                  pl.BlockSpec((l // _KEY_GRANULE, 1, vrows, _KEY_GRANULE),
                               lambda bi, h, p: (bi, h, 0, 0)),
                  pl.BlockSpec((1, 3, t, t), lambda bi, h, p: (h, 0, 0, 0)),
                  pl.BlockSpec(lam.shape, lambda bi, h, p: (0, 0)),
                  pl.BlockSpec(gain_t.shape, lambda bi, h, p: (0, 0))],
        out_specs=pl.BlockSpec((2, 1, 1, t, _V_DIM), lambda bi, h, p: (0, bi, p, 0, h)),
        out_shape=jax.ShapeDtypeStruct((2, b, nq // 2, t, w), _BF16),
        scratch_shapes=[pltpu.VMEM((2, _V_DIM, 2 * t), _BF16),
                        pltpu.VMEM((2, 2, 1, t), _F32),
                        pltpu.VMEM((2, vrows, 2 * t), _F32),
                        pltpu.VMEM((2, 8, 2 * t), _F32)],
        compiler_params=pltpu.CompilerParams(
            dimension_semantics=("parallel", "parallel", "arbitrary"),
            vmem_limit_bytes=_VMEM_LIMIT),
        name="diff_attention",
    )(flag, qt, qt, k, vt, bias, lam, gain_t)


def _attn_bounded_body(qt_ref, k_ref, vt_ref, bias_ref, lam_ref, gain_ref, o_ref,
                       qw_ref, acc_ref, den_ref, *, t, nq):
    gran = t // _KEY_GRANULE
    assert gran == 2
    half = _KEY_GRANULE
    row = lax.broadcasted_iota(jnp.int32, (_V_DIM, t), 0)
    for q in range(nq):
        qt = qt_ref[:, q * t:(q + 1) * t]
        zero = jnp.zeros_like(qt)
        qw_ref[q, :, :t] = jnp.where(row < _SUB_DIM, qt, zero)
        qw_ref[q, :, t:] = jnp.where(row >= _SUB_DIM, qt, zero)
    acc_ref[...] = jnp.zeros(acc_ref.shape, _F32)
    den_ref[...] = jnp.zeros(den_ref.shape, _F32)

    def key_sum(e):
        return jnp.sum(e.reshape(e.shape[0] // 8, 8, e.shape[1]), axis=0)

    def diagonal(kj):
        late = [slice(half, t), slice(t + half, 2 * t)]
        k0 = kj * t
        bias = bias_ref[0, 2]
        st0 = jnp.dot(k_ref[0, k0:k0 + half, :], qw_ref[kj], preferred_element_type=_F32)
        e0 = [jnp.exp2(st0[:, :t] + bias[:half]), jnp.exp2(st0[:, t:] + bias[:half])]
        den_ref[kj] += jnp.concatenate([key_sum(e0[0]), key_sum(e0[1])], axis=1)
        pt0 = jnp.concatenate([e0[0].astype(_BF16), e0[1].astype(_BF16)], axis=1)
        qw_late = jnp.concatenate([qw_ref[kj, :, lanes] for lanes in late], axis=1)
        st1 = jnp.dot(k_ref[0, k0 + half:k0 + t, :], qw_late, preferred_element_type=_F32)
        e1 = [jnp.exp2(st1[:, :half] + bias[half:, half:]), jnp.exp2(st1[:, half:] + bias[half:, half:])]
        pt1 = jnp.concatenate([e1[0].astype(_BF16), e1[1].astype(_BF16)], axis=1)
        acc_ref[kj] += jnp.dot(vt_ref[kj * gran, 0], pt0, preferred_element_type=_F32)
        pv1 = jnp.dot(vt_ref[kj * gran + 1, 0], pt1, preferred_element_type=_F32)
        for n, lanes in enumerate(late):
            acc_ref[kj, :, lanes] += pv1[:, n * half:(n + 1) * half]
            den_ref[kj, :, lanes] += key_sum(e1[n])

    def later_queries(kj, q_tiles):
        qw = jnp.concatenate([qw_ref[q] for q in q_tiles], axis=1)
        st = jnp.dot(k_ref[0, kj * t:(kj + 1) * t, :], qw, preferred_element_type=_F32)
        if q_tiles[0] == kj + 1:
            blk = _BIAS_BLOCK
            corner = bias_ref[0, 1, t - blk:, :blk]
            tail = st[t - blk:]
            tail = jnp.concatenate([tail[:, :blk] + corner, tail[:, blk:t],
                                    tail[:, t:t + blk] + corner, tail[:, t + blk:]], axis=1)
            st = jnp.concatenate([st[:t - blk], tail], axis=0)
        e = jnp.exp2(st)
        pv = (jnp.dot(vt_ref[kj * gran, 0], e[:half].astype(_BF16), preferred_element_type=_F32)
              + jnp.dot(vt_ref[kj * gran + 1, 0], e[half:].astype(_BF16), preferred_element_type=_F32))
        for n, q in enumerate(q_tiles):
            lanes = slice(n * 2 * t, (n + 1) * 2 * t)
            den_ref[q] += key_sum(e[:, lanes])
            acc_ref[q] += pv[:, lanes]

    for kj in range(nq):
        diagonal(kj)
        later = list(range(kj + 1, nq))
        for g0 in range(0, len(later), _QUERY_GROUP):
            later_queries(kj, later[g0:g0 + _QUERY_GROUP])

    for q in range(nq):
        o = acc_ref[q] / jnp.sum(den_ref[q], axis=0, keepdims=True)
        od = o[:, :t] - lam_ref[0:1, 0:1] * o[:, t:]
        scale = lax.rsqrt(jnp.mean(od * od, axis=0, keepdims=True) + _EPS)
        o_ref[0, q * t:(q + 1) * t, :] = (od * scale * gain_ref[...]).T.astype(_BF16)


def _attention_bounded(qt, k, vt, bias, lam, gain_t):
    b, l, w = k.shape
    t = _ATTN_TILE
    nq = l // t
    return pl.pallas_call(
        functools.partial(_attn_bounded_body, t=t, nq=nq),
        grid=(b, _HEADS),
        in_specs=[pl.BlockSpec((_V_DIM, l), lambda bi, h: (h, bi)),
                  pl.BlockSpec((1, l, _V_DIM), lambda bi, h: (bi, 0, h)),
                  pl.BlockSpec((l // _KEY_GRANULE, 1, _V_DIM, _KEY_GRANULE), lambda bi, h: (bi, h, 0, 0)),
                  pl.BlockSpec((1, 3, t, t), lambda bi, h: (h, 0, 0, 0)),
                  pl.BlockSpec(lam.shape, lambda bi, h: (0, 0)),
                  pl.BlockSpec(gain_t.shape, lambda bi, h: (0, 0))],
        out_specs=pl.BlockSpec((1, l, _V_DIM), lambda bi, h: (bi, 0, h)),
        out_shape=jax.ShapeDtypeStruct((b, l, w), _BF16),
        scratch_shapes=[pltpu.VMEM((nq, _V_DIM, 2 * t), _BF16),
                        pltpu.VMEM((nq, _V_DIM, 2 * t), _F32),
                        pltpu.VMEM((nq, 8, 2 * t), _F32)],
        compiler_params=pltpu.CompilerParams(
            dimension_semantics=("parallel", "parallel"), vmem_limit_bytes=_VMEM_LIMIT),
        name="diff_attention_bounded",
    )(qt, k, vt, bias, lam, gain_t)


def _s5_prep_body(are_ref, aim_ref, ldt_ref, bre_ref, bim_ref,
                  lbre_ref, lbim_ref, bbre_ref, bbim_ref):
    a_re = are_ref[...]
    a_im = aim_ref[...]
    dt = jnp.exp(ldt_ref[...])
    decay = jnp.exp(a_re * dt)
    lb_re = decay * jnp.cos(a_im * dt)
    lb_im = decay * jnp.sin(a_im * dt)
    nr = lb_re - 1.0
    ni = lb_im
    den = a_re * a_re + a_im * a_im
    q_re = (nr * a_re + ni * a_im) / den
    q_im = (ni * a_re - nr * a_im) / den
    b_re = bre_ref[...]
    b_im = bim_ref[...]
    bbre_ref[...] = q_re * b_re - q_im * b_im
    bbim_ref[...] = q_re * b_im + q_im * b_re
    lbre_ref[...] = lb_re
    lbim_ref[...] = lb_im


def _s5_prep(a_re, a_im, log_dt, b_re, b_im):
    sd = jax.ShapeDtypeStruct(a_re.shape, _F32)
    return pl.pallas_call(_s5_prep_body, out_shape=[sd] * 4, name="s5_discretise")(
        a_re, a_im, log_dt, b_re, b_im)


def _s5_body(u_ref, perm_ref, permt_ref, wbre_ref, wbim_ref, lbre_ref, lbim_ref,
             wcre_ref, wcim_ref, d_ref, gw_ref, gb_ref, o_ref,
             utm_a, bre_a, bim_a, utm_b, bre_b, bim_b, xre_ref, xim_ref, cre_ref, cim_ref,
             *, steps, nb):
    i = pl.program_id(0)
    rows = nb * steps
    width = u_ref.shape[-1]
    n_tiles, _, tile_w = lbre_ref.shape
    halves = wcre_ref.shape[0]
    tiles_per_half = n_tiles // halves
    hw_in = width // halves

    @pl.when(i == 0)
    def _():
        cre_ref[...] = jnp.zeros(cre_ref.shape, _F32)
        cim_ref[...] = jnp.zeros(cim_ref.shape, _F32)
        bre_b[...] = jnp.zeros(bre_b.shape, _F32)
        bim_b[...] = jnp.zeros(bim_b.shape, _F32)
        utm_b[...] = jnp.zeros(utm_b.shape, _BF16)

    def step(utm_new, bre_new, bim_new, utm_old, bre_old, bim_old):
        u_bm = u_ref[...].reshape(rows, width)
        u_tm = jnp.dot(perm_ref[...], u_bm, preferred_element_type=_F32).astype(_BF16)
        utm_new[...] = u_tm

        for hf in range(halves):
            uh = u_tm[:, hf * hw_in:(hf + 1) * hw_in]
            bu_r = jnp.dot(uh, wbre_ref[hf], preferred_element_type=_F32)
            bu_i = jnp.dot(uh, wbim_ref[hf], preferred_element_type=_F32)
            for q in range(tiles_per_half):
                lanes = slice(q * tile_w, (q + 1) * tile_w)
                bre_new[hf * tiles_per_half + q] = bu_r[:, lanes]
                bim_new[hf * tiles_per_half + q] = bu_i[:, lanes]

        for j0 in range(0, n_tiles, _SCAN_GROUP):
            group = range(j0, j0 + _SCAN_GROUP)
            a_r = [jnp.broadcast_to(lbre_ref[j], (nb, tile_w)) for j in group]
            a_i = [jnp.broadcast_to(lbim_ref[j], (nb, tile_w)) for j in group]
            x_r = [cre_ref[j] for j in group]
            x_i = [cim_ref[j] for j in group]
            for ti in range(steps):
                r = slice(ti * nb, (ti + 1) * nb)
                for q, j in enumerate(group):
                    n_r = a_r[q] * x_r[q] - a_i[q] * x_i[q] + bre_old[j, r, :]
                    n_i = a_r[q] * x_i[q] + a_i[q] * x_r[q] + bim_old[j, r, :]
                    xre_ref[j, r, :] = n_r
                    xim_ref[j, r, :] = n_i
                    x_r[q], x_i[q] = n_r, n_i
            for q, j in enumerate(group):
                cre_ref[j] = x_r[q]
                cim_ref[j] = x_i[q]

        ys = []
        for hf in range(halves):
            tiles = range(hf * tiles_per_half, (hf + 1) * tiles_per_half)
            x_r = jnp.concatenate([xre_ref[j] for j in tiles], axis=1).astype(_BF16)
            x_i = jnp.concatenate([xim_ref[j] for j in tiles], axis=1).astype(_BF16)
            ys.append(jnp.dot(x_r, wcre_ref[hf], preferred_element_type=_F32)
                      + jnp.dot(x_i, wcim_ref[hf], preferred_element_type=_F32))
        y = jnp.concatenate(ys, axis=-1) + d_ref[...] * utm_old[...].astype(_F32)
        g = jax.nn.gelu(y)
        o = g * jax.nn.sigmoid(jnp.dot(g.astype(_BF16), gw_ref[...], preferred_element_type=_F32)
                               + gb_ref[...])
        o_bm = jnp.dot(permt_ref[...], o.astype(_BF16), preferred_element_type=_F32)
        o_ref[...] = o_bm.astype(_BF16).reshape(o_ref.shape)

    @pl.when(i % 2 == 0)
    def _():
        step(utm_a, bre_a, bim_a, utm_b, bre_b, bim_b)

    @pl.when(i % 2 == 1)
    def _():
        step(utm_b, bre_b, bim_b, utm_a, bre_a, bim_a)


def _s5(u, perm, permt, wb_re, wb_im, lb_re, lb_im, wc_re, wc_im, d_vec, glu_w, glu_b):
    nb, l, width = u.shape
    steps = _SCAN_STEPS
    n_blocks = l // steps
    n_tiles, _, tile_w = lb_re.shape
    full = lambda a: pl.BlockSpec(a.shape, lambda i: (0,) * a.ndim)
    consts = (perm, permt, wb_re, wb_im, lb_re, lb_im, wc_re, wc_im, d_vec, glu_w, glu_b)
    utm_buf = pltpu.VMEM((nb * steps, width), _BF16)
    state_buf = pltpu.VMEM((n_tiles, nb * steps, tile_w), _F32)
    return pl.pallas_call(
        functools.partial(_s5_body, steps=steps, nb=nb),
        grid=(n_blocks + 1,),
        in_specs=[pl.BlockSpec((nb, steps, width), lambda i: (0, jnp.minimum(i, n_blocks - 1), 0))]
                 + [full(a) for a in consts],
        out_specs=pl.BlockSpec((nb, steps, width), lambda i: (0, jnp.maximum(i - 1, 0), 0)),
        out_shape=jax.ShapeDtypeStruct(u.shape, _BF16),
        scratch_shapes=[utm_buf, state_buf, state_buf, utm_buf, state_buf, state_buf,
                        state_buf, state_buf,
                        pltpu.VMEM((n_tiles, nb, tile_w), _F32),
                        pltpu.VMEM((n_tiles, nb, tile_w), _F32)],
        compiler_params=pltpu.CompilerParams(
            dimension_semantics=("arbitrary",), vmem_limit_bytes=_VMEM_LIMIT),
        name="s5_branch",
    )(u, *consts)


def _outproj_body(x_ref, ng_ref, wg_ref, mb_ref, oa_ref, os_ref, pa_ref, ps_ref, wo_ref,
                  out_ref, *, width):
    x = x_ref[...]
    d = x.shape[-1]
    h = (x * _rms_scale(x) * ng_ref[...]).astype(_BF16)
    zg = jnp.dot(h, wg_ref[...], preferred_element_type=_F32)
    o_a = (oa_ref[...].astype(_F32) * jax.nn.silu(zg[:, :width])).astype(_BF16)
    o_s = (os_ref[...].astype(_F32) * jax.nn.silu(zg[:, width:2 * width])).astype(_BF16)
    p_a = jnp.dot(o_a, pa_ref[...], preferred_element_type=_F32)
    p_s = jnp.dot(o_s, ps_ref[...], preferred_element_type=_F32)
    g = jax.nn.sigmoid(zg[:, 2 * width:] + mb_ref[...])
    merged = g[:, :d] * p_a + g[:, d:] * p_s
    out_ref[...] = x + jnp.dot(merged.astype(_BF16), wo_ref[...], preferred_element_type=_F32)


def _outproj(x2, norm_gain, w_gates, merge_b, o_a, o_s, proj_a, proj_s, w_out, width):
    n, d = x2.shape
    tm = _TOKEN_TILE
    full = lambda a: pl.BlockSpec(a.shape, lambda i: (0,) * a.ndim)
    row = lambda a: pl.BlockSpec((tm, a.shape[-1]), lambda i: (i, 0))
    return pl.pallas_call(
        functools.partial(_outproj_body, width=width),
        grid=(n // tm,),
        in_specs=[row(x2), full(norm_gain), full(w_gates), full(merge_b), row(o_a), row(o_s),
                  full(proj_a), full(proj_s), full(w_out)],
        out_specs=row(x2),
        out_shape=jax.ShapeDtypeStruct(x2.shape, x2.dtype),
        compiler_params=pltpu.CompilerParams(
            dimension_semantics=("parallel",), vmem_limit_bytes=_VMEM_LIMIT),
        name="outproj",
    )(x2, norm_gain, w_gates, merge_b, o_a, o_s, proj_a, proj_s, w_out)


def _block_diag_halves(blocks, halves):
    g, r, c = blocks.shape
    gh = g // halves
    eye = jnp.eye(gh, dtype=blocks.dtype)
    b = blocks.reshape(halves, gh, r, 1, c) * eye.reshape(1, gh, 1, gh, 1)
    return b.reshape(halves, gh * r, gh * c)


def _time_major_perm(nb, steps):
    r = np.arange(nb * steps)
    src = (r % nb) * steps + r // nb
    p = np.zeros((nb * steps, nb * steps), np.float32)
    p[r, src] = 1.0
    return p


def _layer(x, lam_init, norm_gain, w_in, merge_gate_b, q_norm_gain, k_norm_gain,
           lambda_q1, lambda_k1, lambda_q2, lambda_k2, diff_subln_gain, rel_bias_table,
           ssm_a_re, ssm_a_im, ssm_log_dt, ssm_b_re, ssm_b_im, ssm_c_re, ssm_c_im,
           ssm_d, ssm_glu_w, ssm_glu_b, proj_attn, proj_ssm, w_out):
    nb, l, d = x.shape
    n = nb * l
    aw = _HEADS * 2 * _SUB_DIM
    groups = ssm_a_re.shape[0]
    x2 = x.reshape(n, d)
    ng = norm_gain.reshape(1, d).astype(_F32)

    w_ku = jnp.concatenate([w_in[:, aw:2 * aw], w_in[:, 4 * aw:5 * aw]], axis=1).astype(_BF16)
    w_qvt = jnp.concatenate([w_in[:, :aw], w_in[:, 2 * aw:3 * aw]], axis=1).T.astype(_BF16)
    w_gates = jnp.concatenate([w_in[:, 3 * aw:4 * aw], w_in[:, 5 * aw:]], axis=1).astype(_BF16)
    gq = jnp.tile(q_norm_gain.astype(_F32), 2 * _HEADS) * (_SUB_DIM ** -0.5 * _LOG2E)
    gqt = jnp.broadcast_to(gq[:, None], (aw, _TOKEN_TILE))
    gk = jnp.tile(k_norm_gain.astype(_F32), 2 * _HEADS).reshape(1, aw)
    seg = np.arange(_MXU_TILE) // _SUB_DIM
    gsum = jnp.asarray((seg[:, None] == seg[None, :]).astype(np.float32), _BF16)
    qt, k, vt, u = _inproj(x2, ng, w_ku, w_qvt, gqt, gk, gsum, aw)

    lam4 = jnp.stack([lambda_q1, lambda_k1, lambda_q2, lambda_k2]).astype(_F32)
    qk_gains = jnp.stack([gq[:_SUB_DIM], gk[0, :_SUB_DIM]])
    bias, lam, flag = _bias_prep(rel_bias_table.astype(_F32), lam4, qk_gains, lam_init, _ATTN_TILE)
    subln = diff_subln_gain.astype(_F32) * (1.0 - lam_init)
    subln_t = jnp.broadcast_to(subln[:, None], (_V_DIM, _ATTN_TILE))
    shp = (nb, l, aw)
    k3 = k.reshape(shp)

    def online_attention():
        o = _attention(flag, qt, k3, vt, bias, lam, subln_t)
        return jnp.concatenate([o[0], o[1][:, ::-1]], axis=1).reshape(shp)

    o_a = lax.cond(flag[0, 0] > 0,
                   lambda: _attention_bounded(qt, k3, vt, bias, lam, subln_t),
                   online_attention)

    rep = lambda a: jnp.repeat(a.astype(_F32), _SSM_GROUP, axis=0)
    ldt = jnp.broadcast_to(ssm_log_dt.astype(_F32)[:, None], ssm_a_re.shape)
    bt = lambda a: a.astype(_F32).transpose(0, 2, 1).reshape(groups * _SSM_GROUP, _SSM_STATE)
    lb_re, lb_im, bb_re, bb_im = _s5_prep(rep(ssm_a_re), rep(ssm_a_im), rep(ldt),
                                          bt(ssm_b_re), bt(ssm_b_im))
    n_tiles = groups * _SSM_STATE // _STATE_TILE
    flat = lambda a: a[::_SSM_GROUP].reshape(n_tiles, 1, _STATE_TILE)
    gshape = (groups, _SSM_GROUP, _SSM_STATE)
    wb_re = _block_diag_halves(bb_re.reshape(gshape), 2).astype(_BF16)
    wb_im = _block_diag_halves(bb_im.reshape(gshape), 2).astype(_BF16)
    wc_re = _block_diag_halves(ssm_c_re.astype(_F32).transpose(0, 2, 1), 2).astype(_BF16)
    wc_im = _block_diag_halves(-ssm_c_im.astype(_F32).transpose(0, 2, 1), 2).astype(_BF16)
    perm = _time_major_perm(nb, _SCAN_STEPS)
    o_s = _s5(u.reshape(shp), jnp.asarray(perm, _BF16), jnp.asarray(perm.T, _BF16),
              wb_re, wb_im, flat(lb_re), flat(lb_im), wc_re, wc_im,
              ssm_d.astype(_F32).reshape(1, aw), ssm_glu_w.astype(_BF16),
              ssm_glu_b.astype(_F32).reshape(1, aw))

    out = _outproj(x2, ng, w_gates, merge_gate_b.astype(_F32).reshape(1, 2 * d),
                   o_a.reshape(n, aw), o_s.reshape(n, aw),
                   proj_attn.astype(_BF16), proj_ssm.astype(_BF16), w_out.astype(_BF16), aw)
    return out.reshape(nb, l, d)


def kernel(x, norm_gain, w_in, merge_gate_b, q_norm_gain, k_norm_gain, lambda_q1, lambda_k1,
           lambda_q2, lambda_k2, diff_subln_gain, rel_bias_table, ssm_A_re, ssm_A_im, ssm_log_dt,
           ssm_B_re, ssm_B_im, ssm_C_re, ssm_C_im, ssm_D, ssm_glu_w, ssm_glu_b,
           proj_attn, proj_ssm, w_out):
    per_layer = (norm_gain, w_in, merge_gate_b, q_norm_gain, k_norm_gain, lambda_q1, lambda_k1,
                 lambda_q2, lambda_k2, diff_subln_gain)
    per_layer_tail = (ssm_A_re, ssm_A_im, ssm_log_dt, ssm_B_re, ssm_B_im, ssm_C_re, ssm_C_im,
                      ssm_D, ssm_glu_w, ssm_glu_b, proj_attn, proj_ssm, w_out)
    for layer in range(norm_gain.shape[0]):
        lam_init = 0.8 - 0.6 * math.exp(-0.3 * layer)
        x = _layer(x, lam_init, *(p[layer] for p in per_layer), rel_bias_table,
                   *(p[layer] for p in per_layer_tail))
    return x
```

```python
import functools
import math

import jax
import jax.numpy as jnp
import numpy as np
from jax import lax
from jax.experimental import pallas as pl
from jax.experimental.pallas import tpu as pltpu

_F32 = jnp.float32
_BF16 = jnp.bfloat16

_CHUNK = 64
_HEADS = 4
_SUB_DIM = 64
_V_DIM = 128
_SSM_GROUP = 16
_SSM_STATE = 64
_REL_BUCKETS = 32
_REL_MAX_DIST = 128
_EPS = 1e-6
_LOG2E = math.log2(math.e)
_SAFE_LOG2_RANGE = 100.0

_VMEM_LIMIT = 48 * 1024 * 1024
_MXU_TILE = 256
_TOKEN_TILE = 1024
_ATTN_TILE = 512
_KEY_GRANULE = 256
_BIAS_BLOCK = 128
_QUERY_GROUP = 2
_SCAN_STEPS = 64
_STATE_TILE = 256
_SCAN_GROUP = 4


def _rms_scale(x, eps=_EPS):
    return lax.rsqrt(jnp.mean(x * x, axis=-1, keepdims=True) + eps)


def _inproj_body(x_ref, ng_ref, wku_ref, wqvt_ref, gqt_ref, gk_ref, gsum_ref,
                 qt_ref, k_ref, vt_ref, u_ref, *, width):
    x = x_ref[...]
    tm = x.shape[0]
    h = (x * _rms_scale(x) * ng_ref[...]).astype(_BF16)
    z = jnp.dot(h, wku_ref[...], preferred_element_type=_F32)
    zt = lax.dot_general(wqvt_ref[...], h, (((1,), (1,)), ((), ())),
                         preferred_element_type=_F32)

    kk = z[:, :width]
    k2 = (kk * kk).astype(_BF16)
    mxu = gsum_ref.shape[0]
    ss = jnp.concatenate([jnp.dot(k2[:, c:c + mxu], gsum_ref[...], preferred_element_type=_F32)
                          for c in range(0, width, mxu)], axis=1)
    k_ref[...] = (kk * lax.rsqrt(ss * (1.0 / _SUB_DIM) + _EPS) * gk_ref[...]).astype(_BF16)
    u_ref[...] = z[:, width:].astype(_BF16)

    qt = zt[:width].reshape(width // _SUB_DIM, _SUB_DIM, tm)
    ms = jnp.mean(qt * qt, axis=1, keepdims=True)
    qt_ref[...] = ((qt * lax.rsqrt(ms + _EPS)).reshape(width, tm) * gqt_ref[...]).astype(_BF16)

    vt = zt[width:].astype(_BF16)
    for j in range(tm // _KEY_GRANULE):
        for hd in range(_HEADS):
            vt_ref[j, hd] = vt[hd * _V_DIM:(hd + 1) * _V_DIM,
                               j * _KEY_GRANULE:(j + 1) * _KEY_GRANULE]


def _inproj(x2, norm_gain, w_ku, w_qvt, gqt, gk, gsum, width):
    n, d = x2.shape
    tm = _TOKEN_TILE
    full = lambda a: pl.BlockSpec(a.shape, lambda i: (0,) * a.ndim)
    row_sd = jax.ShapeDtypeStruct((n, width), _BF16)
    row_spec = pl.BlockSpec((tm, width), lambda i: (i, 0))
    gran = tm // _KEY_GRANULE
    vrows = _V_DIM
    return pl.pallas_call(
        functools.partial(_inproj_body, width=width),
        grid=(n // tm,),
        in_specs=[pl.BlockSpec((tm, d), lambda i: (i, 0)), full(norm_gain), full(w_ku), full(w_qvt),
                  full(gqt), full(gk), full(gsum)],
        out_specs=[pl.BlockSpec((width, tm), lambda i: (0, i)), row_spec,
                   pl.BlockSpec((gran, _HEADS, vrows, _KEY_GRANULE), lambda i: (i, 0, 0, 0)), row_spec],
        out_shape=[jax.ShapeDtypeStruct((width, n), _BF16), row_sd,
                   jax.ShapeDtypeStruct((n // _KEY_GRANULE, _HEADS, vrows, _KEY_GRANULE), _BF16), row_sd],
        compiler_params=pltpu.CompilerParams(
            dimension_semantics=("parallel",), vmem_limit_bytes=_VMEM_LIMIT),
        name="inproj",
    )(x2, norm_gain, w_ku, w_qvt, gqt, gk, gsum)


def _t5_bucket_np(rel):
    nb = _REL_BUCKETS // 2
    max_exact = nb // 2
    side = np.where(rel > 0, nb, 0)
    n = np.abs(rel)
    nf = np.maximum(n, 1).astype(np.float32)
    large = max_exact + (np.log(nf / np.float32(max_exact)) / np.float32(math.log(_REL_MAX_DIST / max_exact))
                         * np.float32(nb - max_exact)).astype(np.int32)
    large = np.minimum(large, nb - 1)
    return side + np.where(n < max_exact, n, large)


def _bucket_blocks():
    assert _BIAS_BLOCK % _CHUNK == 0
    i = np.arange(_BIAS_BLOCK)[None, :]
    j = np.arange(_BIAS_BLOCK)[:, None]
    far_bucket = _REL_BUCKETS // 2 - 1
    assert (_t5_bucket_np(j - i - 2 * _BIAS_BLOCK) == far_bucket).all()
    diag = np.where((j // _CHUNK) <= (i // _CHUNK), _t5_bucket_np(j - i), -1)
    prev = _t5_bucket_np(j - i - _BIAS_BLOCK)
    return np.stack([diag, prev]).astype(np.int32)


def _bias_prep_body(tab_ref, tabv_ref, bucket_ref, lam4_ref, gains_ref,
                    bias_ref, lam_ref, flag_ref, *, lam_init, t):
    h = pl.program_id(0)
    far_bucket = _REL_BUCKETS // 2 - 1
    bkt = bucket_ref[...]
    far = tab_ref[far_bucket, h]
    val = jnp.full(bkt.shape, -jnp.inf, _F32)
    for b in range(_REL_BUCKETS):
        val = jnp.where(bkt == b, (tab_ref[b, h] - far) * _LOG2E, val)
    diag_block, prev_block = val[0], val[1]
    n_blk = t // _BIAS_BLOCK
    bias_ref[0, 0] = jnp.zeros((t, t), _F32)
    bias_ref[0, 1] = jnp.zeros((t, t), _F32)
    bias_ref[0, 1, t - _BIAS_BLOCK:, :_BIAS_BLOCK] = prev_block
    for bj in range(n_blk):
        rows = slice(bj * _BIAS_BLOCK, (bj + 1) * _BIAS_BLOCK)
        for bi in range(n_blk):
            cols = slice(bi * _BIAS_BLOCK, (bi + 1) * _BIAS_BLOCK)
            if bi == bj:
                block = diag_block
            elif bi == bj + 1:
                block = prev_block
            else:
                block = jnp.full((_BIAS_BLOCK, _BIAS_BLOCK), 0.0 if bi > bj else -jnp.inf, _F32)
            bias_ref[0, 2, rows, cols] = block
    l4 = lam4_ref[...]
    s1 = jnp.sum(l4[0:1] * l4[1:2], axis=-1, keepdims=True)
    s2 = jnp.sum(l4[2:3] * l4[3:4], axis=-1, keepdims=True)
    lam_ref[...] = jnp.broadcast_to(jnp.exp(s1) - jnp.exp(s2) + lam_init, lam_ref.shape)
    gmax = jnp.max(jnp.abs(gains_ref[...]), axis=-1, keepdims=True)
    qk_bound = (_SUB_DIM * 1.02) * gmax[0:1] * gmax[1:2]
    tv = tabv_ref[...]
    dev = jnp.abs(tv - tv[far_bucket:far_bucket + 1]) * _LOG2E
    bias_bound = jnp.max(jnp.max(dev, axis=-1, keepdims=True), axis=0, keepdims=True)
    flag_ref[...] = (qk_bound + bias_bound <= _SAFE_LOG2_RANGE).astype(jnp.int32)


def _bias_prep(rel_table, lam4, gains, lam_init, t):
    assert t % _BIAS_BLOCK == 0
    buckets = jnp.asarray(_bucket_blocks())
    whole = lambda a: pl.BlockSpec(a.shape, lambda h: (0,) * a.ndim)
    return pl.pallas_call(
        functools.partial(_bias_prep_body, lam_init=lam_init, t=t),
        grid=(_HEADS,),
        in_specs=[pl.BlockSpec(memory_space=pltpu.SMEM), whole(rel_table), whole(buckets),
                  whole(lam4), whole(gains)],
        out_specs=[pl.BlockSpec((1, 3, t, t), lambda h: (h, 0, 0, 0)),
                   pl.BlockSpec((1, _V_DIM), lambda h: (0, 0)),
                   pl.BlockSpec((1, 1), lambda h: (0, 0))],
        out_shape=[jax.ShapeDtypeStruct((_HEADS, 3, t, t), _F32),
                   jax.ShapeDtypeStruct((1, _V_DIM), _F32),
                   jax.ShapeDtypeStruct((1, 1), jnp.int32)],
        compiler_params=pltpu.CompilerParams(dimension_semantics=("arbitrary",)),
        name="attn_bias_prep",
    )(rel_table, rel_table, buckets, lam4, gains)


def _attn_body(flag_ref, qta_ref, qtb_ref, k_ref, vt_ref, bias_ref, lam_ref, gain_ref, o_ref,
               qw_ref, m_ref, acc_ref, den_ref, *, t, nq):
    p = pl.program_id(2)
    gran = t // _KEY_GRANULE
    for side, qt_ref in enumerate((qta_ref, qtb_ref)):
        qt = qt_ref[...]
        row = lax.broadcasted_iota(jnp.int32, qt.shape, 0)
        zero = jnp.zeros_like(qt)
        qw_ref[side, :, :t] = jnp.where(row < _SUB_DIM, qt, zero)
        qw_ref[side, :, t:] = jnp.where(row >= _SUB_DIM, qt, zero)
    acc_ref[...] = jnp.zeros(acc_ref.shape, _F32)
    den_ref[...] = jnp.zeros(den_ref.shape, _F32)

    def key_sum(e):
        return jnp.sum(e.reshape(e.shape[0] // 8, 8, e.shape[1]), axis=0)

    def schedule(step_index):
        mirrored = step_index > p
        side = mirrored.astype(jnp.int32)
        kj = jnp.where(mirrored, step_index - p - 1, p - step_index)
        q_tile = jnp.where(mirrored, nq - 1 - p, p)
        bias_index = jnp.where(kj == q_tile, 2, jnp.where(kj == q_tile - 1, 1, 0))
        return side, kj, bias_index

    def apply_to_values(kj, pt):
        pv = jnp.dot(vt_ref[kj * gran, 0], pt[:_KEY_GRANULE], preferred_element_type=_F32)
        for g in range(1, gran):
            pv = pv + jnp.dot(vt_ref[kj * gran + g, 0],
                              pt[g * _KEY_GRANULE:(g + 1) * _KEY_GRANULE],
                              preferred_element_type=_F32)
        return pv

    def key_tile(kj):
        return k_ref[0, pl.ds(pl.multiple_of(kj * t, t), t), :]

    def bounded_step(step_index):
        side, kj, bias_index = schedule(step_index)
        st = jnp.dot(key_tile(kj), qw_ref[side], preferred_element_type=_F32)
        bias = bias_ref[0, bias_index]
        e = [jnp.exp2(st[:, :t] + bias), jnp.exp2(st[:, t:] + bias)]
        den_ref[side] += jnp.concatenate([key_sum(e[0]), key_sum(e[1])], axis=1)
        pt = jnp.concatenate([e[0].astype(_BF16), e[1].astype(_BF16)], axis=1)
        acc_ref[side] += apply_to_values(kj, pt)

    def bounded_diagonal_step(side, q_tile):
        assert gran == 2
        half = _KEY_GRANULE
        late = [slice(half, t), slice(t + half, 2 * t)]
        k0 = pl.multiple_of(q_tile * t, t)
        bias = bias_ref[0, 2]
        st0 = jnp.dot(k_ref[0, pl.ds(k0, half), :], qw_ref[side], preferred_element_type=_F32)
        e0 = [jnp.exp2(st0[:, :t] + bias[:half]), jnp.exp2(st0[:, t:] + bias[:half])]
        den_ref[side] += jnp.concatenate([key_sum(e0[0]), key_sum(e0[1])], axis=1)
        pt0 = jnp.concatenate([e0[0].astype(_BF16), e0[1].astype(_BF16)], axis=1)
        qw_late = jnp.concatenate([qw_ref[side, :, lanes] for lanes in late], axis=1)
        st1 = jnp.dot(k_ref[0, pl.ds(k0 + half, half), :], qw_late, preferred_element_type=_F32)
        e1 = [jnp.exp2(st1[:, :half] + bias[half:, half:]), jnp.exp2(st1[:, half:] + bias[half:, half:])]
        pt1 = jnp.concatenate([e1[0].astype(_BF16), e1[1].astype(_BF16)], axis=1)
        acc_ref[side] += jnp.dot(vt_ref[q_tile * gran, 0], pt0, preferred_element_type=_F32)
        pv1 = jnp.dot(vt_ref[q_tile * gran + 1, 0], pt1, preferred_element_type=_F32)
        for n, lanes in enumerate(late):
            acc_ref[side, :, lanes] += pv1[:, n * half:(n + 1) * half]
            den_ref[side, :, lanes] += key_sum(e1[n])

    def online_step(step_index):
        side, kj, bias_index = schedule(step_index)
        kt = key_tile(kj)
        bias = bias_ref[0, bias_index]
        for s in range(2):
            lanes = slice(s * t, (s + 1) * t)
            st = jnp.dot(kt, qw_ref[side, :, lanes], preferred_element_type=_F32) + bias
            m_old = m_ref[side, s]
            m_new = jnp.maximum(m_old, jnp.max(st, axis=0, keepdims=True))
            alpha = jnp.exp2(m_old - m_new)
            e = jnp.exp2(st - m_new)
            den_ref[side, :, lanes] = alpha * den_ref[side, :, lanes] + key_sum(e)
            acc_ref[side, :, lanes] = alpha * acc_ref[side, :, lanes] + apply_to_values(
                kj, e.astype(_BF16))
            m_ref[side, s] = m_new

    bounded = flag_ref[0, 0] > 0

    @pl.when(bounded)
    def _():
        bounded_diagonal_step(0, p)
        for step_index in range(1, nq):
            bounded_step(step_index)
        bounded_diagonal_step(1, nq - 1 - p)

    @pl.when(jnp.logical_not(bounded))
    def _():
        m_ref[...] = jnp.full(m_ref.shape, -jnp.inf, _F32)
        for step_index in range(nq + 1):
            online_step(step_index)

    for side in range(2):
        o = acc_ref[side] / jnp.sum(den_ref[side], axis=0, keepdims=True)
        od = o[:, :t] - lam_ref[0:1, 0:1] * o[:, t:]
        scale = lax.rsqrt(jnp.mean(od * od, axis=0, keepdims=True) + _EPS)
        o_ref[side, 0, 0] = (od * scale * gain_ref[...]).T.astype(_BF16)


def _attention(flag, qt, k, vt, bias, lam, gain_t):
    b, l, w = k.shape
    t = _ATTN_TILE
    nq = l // t
    vrows = vt.shape[2]
    return pl.pallas_call(
        functools.partial(_attn_body, t=t, nq=nq),
        grid=(b, _HEADS, nq // 2),
        in_specs=[pl.BlockSpec(memory_space=pltpu.SMEM),
                  pl.BlockSpec((_V_DIM, t), lambda bi, h, p: (h, bi * nq + p)),
                  pl.BlockSpec((_V_DIM, t), lambda bi, h, p: (h, bi * nq + nq - 1 - p)),
                  pl.BlockSpec((1, l, _V_DIM), lambda bi, h, p: (bi, 0, h)),
                  pl.BlockSpec((l // _KEY_GRANULE, 1, vrows, _KEY_GRANULE),
                               lambda bi, h, p: (bi, h, 0, 0)),
                  pl.BlockSpec((1, 3, t, t), lambda bi, h, p: (h, 0, 0, 0)),
                  pl.BlockSpec(lam.shape, lambda bi, h, p: (0, 0)),
                  pl.BlockSpec(gain_t.shape, lambda bi, h, p: (0, 0))],
        out_specs=pl.BlockSpec((2, 1, 1, t, _V_DIM), lambda bi, h, p: (0, bi, p, 0, h)),
        out_shape=jax.ShapeDtypeStruct((2, b, nq // 2, t, w), _BF16),
        scratch_shapes=[pltpu.VMEM((2, _V_DIM, 2 * t), _BF16),
                        pltpu.VMEM((2, 2, 1, t), _F32),
                        pltpu.VMEM((2, vrows, 2 * t), _F32),
                        pltpu.VMEM((2, 8, 2 * t), _F32)],
        compiler_params=pltpu.CompilerParams(
            dimension_semantics=("parallel", "parallel", "arbitrary"),
            vmem_limit_bytes=_VMEM_LIMIT),
        name="diff_attention",
    )(flag, qt, qt, k, vt, bias, lam, gain_t)


def _attn_bounded_body(qt_ref, k_ref, vt_ref, bias_ref, lam_ref, gain_ref, o_ref,
                       qw_ref, acc_ref, den_ref, *, t, nq):
    gran = t // _KEY_GRANULE
    assert gran == 2
    half = _KEY_GRANULE
    row = lax.broadcasted_iota(jnp.int32, (_V_DIM, t), 0)
    for q in range(nq):
        qt = qt_ref[:, q * t:(q + 1) * t]
        zero = jnp.zeros_like(qt)
        qw_ref[q, :, :t] = jnp.where(row < _SUB_DIM, qt, zero)
        qw_ref[q, :, t:] = jnp.where(row >= _SUB_DIM, qt, zero)
    acc_ref[...] = jnp.zeros(acc_ref.shape, _F32)
    den_ref[...] = jnp.zeros(den_ref.shape, _F32)

    def key_sum(e):
        return jnp.sum(e.reshape(e.shape[0] // 8, 8, e.shape[1]), axis=0)

    def diagonal(kj):
        late = [slice(half, t), slice(t + half, 2 * t)]
        k0 = kj * t
        bias = bias_ref[0, 2]
        st0 = jnp.dot(k_ref[0, k0:k0 + half, :], qw_ref[kj], preferred_element_type=_F32)
        e0 = [jnp.exp2(st0[:, :t] + bias[:half]), jnp.exp2(st0[:, t:] + bias[:half])]
        den_ref[kj] += jnp.concatenate([key_sum(e0[0]), key_sum(e0[1])], axis=1)
        pt0 = jnp.concatenate([e0[0].astype(_BF16), e0[1].astype(_BF16)], axis=1)
        qw_late = jnp.concatenate([qw_ref[kj, :, lanes] for lanes in late], axis=1)
        st1 = jnp.dot(k_ref[0, k0 + half:k0 + t, :], qw_late, preferred_element_type=_F32)
        e1 = [jnp.exp2(st1[:, :half] + bias[half:, half:]), jnp.exp2(st1[:, half:] + bias[half:, half:])]
        pt1 = jnp.concatenate([e1[0].astype(_BF16), e1[1].astype(_BF16)], axis=1)
        acc_ref[kj] += jnp.dot(vt_ref[kj * gran, 0], pt0, preferred_element_type=_F32)
        pv1 = jnp.dot(vt_ref[kj * gran + 1, 0], pt1, preferred_element_type=_F32)
        for n, lanes in enumerate(late):
            acc_ref[kj, :, lanes] += pv1[:, n * half:(n + 1) * half]
            den_ref[kj, :, lanes] += key_sum(e1[n])

    def later_queries(kj, q_tiles):
        qw = jnp.concatenate([qw_ref[q] for q in q_tiles], axis=1)
        st = jnp.dot(k_ref[0, kj * t:(kj + 1) * t, :], qw, preferred_element_type=_F32)
        if q_tiles[0] == kj + 1:
            blk = _BIAS_BLOCK
            corner = bias_ref[0, 1, t - blk:, :blk]
            tail = st[t - blk:]
            tail = jnp.concatenate([tail[:, :blk] + corner, tail[:, blk:t],
                                    tail[:, t:t + blk] + corner, tail[:, t + blk:]], axis=1)
            st = jnp.concatenate([st[:t - blk], tail], axis=0)
        e = jnp.exp2(st)
        pv = (jnp.dot(vt_ref[kj * gran, 0], e[:half].astype(_BF16), preferred_element_type=_F32)
              + jnp.dot(vt_ref[kj * gran + 1, 0], e[half:].astype(_BF16), preferred_element_type=_F32))
        for n, q in enumerate(q_tiles):
            lanes = slice(n * 2 * t, (n + 1) * 2 * t)
            den_ref[q] += key_sum(e[:, lanes])
            acc_ref[q] += pv[:, lanes]

    for kj in range(nq):
        diagonal(kj)
        later = list(range(kj + 1, nq))
        for g0 in range(0, len(later), _QUERY_GROUP):
            later_queries(kj, later[g0:g0 + _QUERY_GROUP])

    for q in range(nq):
        o = acc_ref[q] / jnp.sum(den_ref[q], axis=0, keepdims=True)
        od = o[:, :t] - lam_ref[0:1, 0:1] * o[:, t:]
        scale = lax.rsqrt(jnp.mean(od * od, axis=0, keepdims=True) + _EPS)
        o_ref[0, q * t:(q + 1) * t, :] = (od * scale * gain_ref[...]).T.astype(_BF16)


def _attention_bounded(qt, k, vt, bias, lam, gain_t):
    b, l, w = k.shape
    t = _ATTN_TILE
    nq = l // t
    return pl.pallas_call(
        functools.partial(_attn_bounded_body, t=t, nq=nq),
        grid=(b, _HEADS),
        in_specs=[pl.BlockSpec((_V_DIM, l), lambda bi, h: (h, bi)),
                  pl.BlockSpec((1, l, _V_DIM), lambda bi, h: (bi, 0, h)),
                  pl.BlockSpec((l // _KEY_GRANULE, 1, _V_DIM, _KEY_GRANULE), lambda bi, h: (bi, h, 0, 0)),
                  pl.BlockSpec((1, 3, t, t), lambda bi, h: (h, 0, 0, 0)),
                  pl.BlockSpec(lam.shape, lambda bi, h: (0, 0)),
                  pl.BlockSpec(gain_t.shape, lambda bi, h: (0, 0))],
        out_specs=pl.BlockSpec((1, l, _V_DIM), lambda bi, h: (bi, 0, h)),
        out_shape=jax.ShapeDtypeStruct((b, l, w), _BF16),
        scratch_shapes=[pltpu.VMEM((nq, _V_DIM, 2 * t), _BF16),
                        pltpu.VMEM((nq, _V_DIM, 2 * t), _F32),
                        pltpu.VMEM((nq, 8, 2 * t), _F32)],
        compiler_params=pltpu.CompilerParams(
            dimension_semantics=("parallel", "parallel"), vmem_limit_bytes=_VMEM_LIMIT),
        name="diff_attention_bounded",
    )(qt, k, vt, bias, lam, gain_t)


def _s5_prep_body(are_ref, aim_ref, ldt_ref, bre_ref, bim_ref,
                  lbre_ref, lbim_ref, bbre_ref, bbim_ref):
    a_re = are_ref[...]
    a_im = aim_ref[...]
    dt = jnp.exp(ldt_ref[...])
    decay = jnp.exp(a_re * dt)
    lb_re = decay * jnp.cos(a_im * dt)
    lb_im = decay * jnp.sin(a_im * dt)
    nr = lb_re - 1.0
    ni = lb_im
    den = a_re * a_re + a_im * a_im
    q_re = (nr * a_re + ni * a_im) / den
    q_im = (ni * a_re - nr * a_im) / den
    b_re = bre_ref[...]
    b_im = bim_ref[...]
    bbre_ref[...] = q_re * b_re - q_im * b_im
    bbim_ref[...] = q_re * b_im + q_im * b_re
    lbre_ref[...] = lb_re
    lbim_ref[...] = lb_im


def _s5_prep(a_re, a_im, log_dt, b_re, b_im):
    sd = jax.ShapeDtypeStruct(a_re.shape, _F32)
    return pl.pallas_call(_s5_prep_body, out_shape=[sd] * 4, name="s5_discretise")(
        a_re, a_im, log_dt, b_re, b_im)


def _s5_body(u_ref, perm_ref, permt_ref, wbre_ref, wbim_ref, lbre_ref, lbim_ref,
             wcre_ref, wcim_ref, d_ref, gw_ref, gb_ref, o_ref,
             utm_a, bre_a, bim_a, utm_b, bre_b, bim_b, xre_ref, xim_ref, cre_ref, cim_ref,
             *, steps, nb):
    i = pl.program_id(0)
    rows = nb * steps
    width = u_ref.shape[-1]
    n_tiles, _, tile_w = lbre_ref.shape
    halves = wcre_ref.shape[0]
    tiles_per_half = n_tiles // halves
    hw_in = width // halves

    @pl.when(i == 0)
    def _():
        cre_ref[...] = jnp.zeros(cre_ref.shape, _F32)
        cim_ref[...] = jnp.zeros(cim_ref.shape, _F32)
        bre_b[...] = jnp.zeros(bre_b.shape, _F32)
        bim_b[...] = jnp.zeros(bim_b.shape, _F32)
        utm_b[...] = jnp.zeros(utm_b.shape, _BF16)

    def step(utm_new, bre_new, bim_new, utm_old, bre_old, bim_old):
        u_tm = jnp.swapaxes(u_ref[...].astype(_F32), 0, 1).reshape(rows, width).astype(_BF16)
        utm_new[...] = u_tm

        for hf in range(halves):
            uh = u_tm[:, hf * hw_in:(hf + 1) * hw_in]
            bu_r = jnp.dot(uh, wbre_ref[hf], preferred_element_type=_F32)
            bu_i = jnp.dot(uh, wbim_ref[hf], preferred_element_type=_F32)
            for q in range(tiles_per_half):
                lanes = slice(q * tile_w, (q + 1) * tile_w)
                bre_new[hf * tiles_per_half + q] = bu_r[:, lanes]
                bim_new[hf * tiles_per_half + q] = bu_i[:, lanes]

        for j0 in range(0, n_tiles, _SCAN_GROUP):
            group = range(j0, j0 + _SCAN_GROUP)
            a_r = [jnp.broadcast_to(lbre_ref[j], (nb, tile_w)) for j in group]
            a_i = [jnp.broadcast_to(lbim_ref[j], (nb, tile_w)) for j in group]
            x_r = [cre_ref[j] for j in group]
            x_i = [cim_ref[j] for j in group]
            for ti in range(steps):
                r = slice(ti * nb, (ti + 1) * nb)
                for q, j in enumerate(group):
                    n_r = a_r[q] * x_r[q] - a_i[q] * x_i[q] + bre_old[j, r, :]
                    n_i = a_r[q] * x_i[q] + a_i[q] * x_r[q] + bim_old[j, r, :]
                    xre_ref[j, r, :] = n_r
                    xim_ref[j, r, :] = n_i
                    x_r[q], x_i[q] = n_r, n_i
            for q, j in enumerate(group):
                cre_ref[j] = x_r[q]
                cim_ref[j] = x_i[q]

        ys = []
        for hf in range(halves):
            tiles = range(hf * tiles_per_half, (hf + 1) * tiles_per_half)
            x_r = jnp.concatenate([xre_ref[j] for j in tiles], axis=1).astype(_BF16)
            x_i = jnp.concatenate([xim_ref[j] for j in tiles], axis=1).astype(_BF16)
            ys.append(jnp.dot(x_r, wcre_ref[hf], preferred_element_type=_F32)
                      + jnp.dot(x_i, wcim_ref[hf], preferred_element_type=_F32))
        y = jnp.concatenate(ys, axis=-1) + d_ref[...] * utm_old[...].astype(_F32)
        g = jax.nn.gelu(y)
        o = g * jax.nn.sigmoid(jnp.dot(g.astype(_BF16), gw_ref[...], preferred_element_type=_F32)
                               + gb_ref[...])
        o_ref[...] = jnp.swapaxes(o.reshape(steps, nb, width), 0, 1).astype(_BF16)

    @pl.when(i % 2 == 0)
    def _():
        step(utm_a, bre_a, bim_a, utm_b, bre_b, bim_b)

    @pl.when(i % 2 == 1)
    def _():
        step(utm_b, bre_b, bim_b, utm_a, bre_a, bim_a)


def _s5(u, perm, permt, wb_re, wb_im, lb_re, lb_im, wc_re, wc_im, d_vec, glu_w, glu_b):
    nb, l, width = u.shape
    steps = _SCAN_STEPS
    n_blocks = l // steps
    n_tiles, _, tile_w = lb_re.shape
    full = lambda a: pl.BlockSpec(a.shape, lambda i: (0,) * a.ndim)
    consts = (perm, permt, wb_re, wb_im, lb_re, lb_im, wc_re, wc_im, d_vec, glu_w, glu_b)
    utm_buf = pltpu.VMEM((nb * steps, width), _BF16)
    state_buf = pltpu.VMEM((n_tiles, nb * steps, tile_w), _F32)
    return pl.pallas_call(
        functools.partial(_s5_body, steps=steps, nb=nb),
        grid=(n_blocks + 1,),
        in_specs=[pl.BlockSpec((nb, steps, width), lambda i: (0, jnp.minimum(i, n_blocks - 1), 0))]
                 + [full(a) for a in consts],
        out_specs=pl.BlockSpec((nb, steps, width), lambda i: (0, jnp.maximum(i - 1, 0), 0)),
        out_shape=jax.ShapeDtypeStruct(u.shape, _BF16),
        scratch_shapes=[utm_buf, state_buf, state_buf, utm_buf, state_buf, state_buf,
                        state_buf, state_buf,
                        pltpu.VMEM((n_tiles, nb, tile_w), _F32),
                        pltpu.VMEM((n_tiles, nb, tile_w), _F32)],
        compiler_params=pltpu.CompilerParams(
            dimension_semantics=("arbitrary",), vmem_limit_bytes=_VMEM_LIMIT),
        name="s5_branch",
    )(u, *consts)


def _outproj_body(x_ref, ng_ref, wg_ref, mb_ref, oa_ref, os_ref, pa_ref, ps_ref, wo_ref,
                  out_ref, *, width):
    x = x_ref[...]
    d = x.shape[-1]
    h = (x * _rms_scale(x) * ng_ref[...]).astype(_BF16)
    zg = jnp.dot(h, wg_ref[...], preferred_element_type=_F32)
    o_a = (oa_ref[...].astype(_F32) * jax.nn.silu(zg[:, :width])).astype(_BF16)
    o_s = (os_ref[...].astype(_F32) * jax.nn.silu(zg[:, width:2 * width])).astype(_BF16)
    p_a = jnp.dot(o_a, pa_ref[...], preferred_element_type=_F32)
    p_s = jnp.dot(o_s, ps_ref[...], preferred_element_type=_F32)
    g = jax.nn.sigmoid(zg[:, 2 * width:] + mb_ref[...])
    merged = g[:, :d] * p_a + g[:, d:] * p_s
    out_ref[...] = x + jnp.dot(merged.astype(_BF16), wo_ref[...], preferred_element_type=_F32)


def _outproj(x2, norm_gain, w_gates, merge_b, o_a, o_s, proj_a, proj_s, w_out, width):
    n, d = x2.shape
    tm = _TOKEN_TILE
    full = lambda a: pl.BlockSpec(a.shape, lambda i: (0,) * a.ndim)
    row = lambda a: pl.BlockSpec((tm, a.shape[-1]), lambda i: (i, 0))
    return pl.pallas_call(
        functools.partial(_outproj_body, width=width),
        grid=(n // tm,),
        in_specs=[row(x2), full(norm_gain), full(w_gates), full(merge_b), row(o_a), row(o_s),
                  full(proj_a), full(proj_s), full(w_out)],
        out_specs=row(x2),
        out_shape=jax.ShapeDtypeStruct(x2.shape, x2.dtype),
        compiler_params=pltpu.CompilerParams(
            dimension_semantics=("parallel",), vmem_limit_bytes=_VMEM_LIMIT),
        name="outproj",
    )(x2, norm_gain, w_gates, merge_b, o_a, o_s, proj_a, proj_s, w_out)


def _block_diag_halves(blocks, halves):
    g, r, c = blocks.shape
    gh = g // halves
    eye = jnp.eye(gh, dtype=blocks.dtype)
    b = blocks.reshape(halves, gh, r, 1, c) * eye.reshape(1, gh, 1, gh, 1)
    return b.reshape(halves, gh * r, gh * c)


def _time_major_perm(nb, steps):
    r = np.arange(nb * steps)
    src = (r % nb) * steps + r // nb
    p = np.zeros((nb * steps, nb * steps), np.float32)
    p[r, src] = 1.0
    return p


def _layer(x, lam_init, norm_gain, w_in, merge_gate_b, q_norm_gain, k_norm_gain,
           lambda_q1, lambda_k1, lambda_q2, lambda_k2, diff_subln_gain, rel_bias_table,
           ssm_a_re, ssm_a_im, ssm_log_dt, ssm_b_re, ssm_b_im, ssm_c_re, ssm_c_im,
           ssm_d, ssm_glu_w, ssm_glu_b, proj_attn, proj_ssm, w_out):
    nb, l, d = x.shape
    n = nb * l
    aw = _HEADS * 2 * _SUB_DIM
    groups = ssm_a_re.shape[0]
    x2 = x.reshape(n, d)
    ng = norm_gain.reshape(1, d).astype(_F32)

    w_ku = jnp.concatenate([w_in[:, aw:2 * aw], w_in[:, 4 * aw:5 * aw]], axis=1).astype(_BF16)
    w_qvt = jnp.concatenate([w_in[:, :aw], w_in[:, 2 * aw:3 * aw]], axis=1).T.astype(_BF16)
    w_gates = jnp.concatenate([w_in[:, 3 * aw:4 * aw], w_in[:, 5 * aw:]], axis=1).astype(_BF16)
    gq = jnp.tile(q_norm_gain.astype(_F32), 2 * _HEADS) * (_SUB_DIM ** -0.5 * _LOG2E)
    gqt = jnp.broadcast_to(gq[:, None], (aw, _TOKEN_TILE))
    gk = jnp.tile(k_norm_gain.astype(_F32), 2 * _HEADS).reshape(1, aw)
    seg = np.arange(_MXU_TILE) // _SUB_DIM
    gsum = jnp.asarray((seg[:, None] == seg[None, :]).astype(np.float32), _BF16)
    qt, k, vt, u = _inproj(x2, ng, w_ku, w_qvt, gqt, gk, gsum, aw)

    lam4 = jnp.stack([lambda_q1, lambda_k1, lambda_q2, lambda_k2]).astype(_F32)
    qk_gains = jnp.stack([gq[:_SUB_DIM], gk[0, :_SUB_DIM]])
    bias, lam, flag = _bias_prep(rel_bias_table.astype(_F32), lam4, qk_gains, lam_init, _ATTN_TILE)
    subln = diff_subln_gain.astype(_F32) * (1.0 - lam_init)
    subln_t = jnp.broadcast_to(subln[:, None], (_V_DIM, _ATTN_TILE))
    shp = (nb, l, aw)
    k3 = k.reshape(shp)

    def online_attention():
        o = _attention(flag, qt, k3, vt, bias, lam, subln_t)
        return jnp.concatenate([o[0], o[1][:, ::-1]], axis=1).reshape(shp)

    o_a = lax.cond(flag[0, 0] > 0,
                   lambda: _attention_bounded(qt, k3, vt, bias, lam, subln_t),
                   online_attention)

    rep = lambda a: jnp.repeat(a.astype(_F32), _SSM_GROUP, axis=0)
    ldt = jnp.broadcast_to(ssm_log_dt.astype(_F32)[:, None], ssm_a_re.shape)
    bt = lambda a: a.astype(_F32).transpose(0, 2, 1).reshape(groups * _SSM_GROUP, _SSM_STATE)
    lb_re, lb_im, bb_re, bb_im = _s5_prep(rep(ssm_a_re), rep(ssm_a_im), rep(ldt),
                                          bt(ssm_b_re), bt(ssm_b_im))
    n_tiles = groups * _SSM_STATE // _STATE_TILE
    flat = lambda a: a[::_SSM_GROUP].reshape(n_tiles, 1, _STATE_TILE)
    gshape = (groups, _SSM_GROUP, _SSM_STATE)
    wb_re = _block_diag_halves(bb_re.reshape(gshape), 2).astype(_BF16)
    wb_im = _block_diag_halves(bb_im.reshape(gshape), 2).astype(_BF16)
    wc_re = _block_diag_halves(ssm_c_re.astype(_F32).transpose(0, 2, 1), 2).astype(_BF16)
    wc_im = _block_diag_halves(-ssm_c_im.astype(_F32).transpose(0, 2, 1), 2).astype(_BF16)
    perm = _time_major_perm(nb, _SCAN_STEPS)
    o_s = _s5(u.reshape(shp), jnp.asarray(perm, _BF16), jnp.asarray(perm.T, _BF16),
              wb_re, wb_im, flat(lb_re), flat(lb_im), wc_re, wc_im,
              ssm_d.astype(_F32).reshape(1, aw), ssm_glu_w.astype(_BF16),
              ssm_glu_b.astype(_F32).reshape(1, aw))

    out = _outproj(x2, ng, w_gates, merge_gate_b.astype(_F32).reshape(1, 2 * d),
                   o_a.reshape(n, aw), o_s.reshape(n, aw),
                   proj_attn.astype(_BF16), proj_ssm.astype(_BF16), w_out.astype(_BF16), aw)
    return out.reshape(nb, l, d)


def kernel(x, norm_gain, w_in, merge_gate_b, q_norm_gain, k_norm_gain, lambda_q1, lambda_k1,
           lambda_q2, lambda_k2, diff_subln_gain, rel_bias_table, ssm_A_re, ssm_A_im, ssm_log_dt,
           ssm_B_re, ssm_B_im, ssm_C_re, ssm_C_im, ssm_D, ssm_glu_w, ssm_glu_b,
           proj_attn, proj_ssm, w_out):
    per_layer = (norm_gain, w_in, merge_gate_b, q_norm_gain, k_norm_gain, lambda_q1, lambda_k1,
                 lambda_q2, lambda_k2, diff_subln_gain)
    per_layer_tail = (ssm_A_re, ssm_A_im, ssm_log_dt, ssm_B_re, ssm_B_im, ssm_C_re, ssm_C_im,
                      ssm_D, ssm_glu_w, ssm_glu_b, proj_attn, proj_ssm, w_out)
    for layer in range(norm_gain.shape[0]):
        lam_init = 0.8 - 0.6 * math.exp(-0.3 * layer)
        x = _layer(x, lam_init, *(p[layer] for p in per_layer), rel_bias_table,
                   *(p[layer] for p in per_layer_tail))
    return x
```

```python
import functools
import math

import jax
import jax.numpy as jnp
import numpy as np
from jax import lax
from jax.experimental import pallas as pl
from jax.experimental.pallas import tpu as pltpu

_F32 = jnp.float32
_BF16 = jnp.bfloat16

_CHUNK = 64
_HEADS = 4
_SUB_DIM = 64
_V_DIM = 128
_SSM_GROUP = 16
_SSM_STATE = 64
_REL_BUCKETS = 32
_REL_MAX_DIST = 128
_EPS = 1e-6
_LOG2E = math.log2(math.e)
_SAFE_LOG2_RANGE = 100.0

_VMEM_LIMIT = 48 * 1024 * 1024
_MXU_TILE = 256
_TOKEN_TILE = 1024
_ATTN_TILE = 512
_KEY_GRANULE = 256
_BIAS_BLOCK = 128
_QUERY_GROUP = 2
_SCAN_STEPS = 128
_STATE_TILE = 256


def _rms_scale(x, eps=_EPS):
    return lax.rsqrt(jnp.mean(x * x, axis=-1, keepdims=True) + eps)


def _inproj_body(x_ref, ng_ref, wku_ref, wqvt_ref, gqt_ref, gk_ref, gsum_ref,
                 qt_ref, k_ref, vt_ref, u_ref, *, width):
    x = x_ref[...]
    tm = x.shape[0]
    h = (x * _rms_scale(x) * ng_ref[...]).astype(_BF16)
    z = jnp.dot(h, wku_ref[...], preferred_element_type=_F32)
    zt = lax.dot_general(wqvt_ref[...], h, (((1,), (1,)), ((), ())),
                         preferred_element_type=_F32)

    kk = z[:, :width]
    k2 = (kk * kk).astype(_BF16)
    mxu = gsum_ref.shape[0]
    ss = jnp.concatenate([jnp.dot(k2[:, c:c + mxu], gsum_ref[...], preferred_element_type=_F32)
                          for c in range(0, width, mxu)], axis=1)
    k_ref[...] = (kk * lax.rsqrt(ss * (1.0 / _SUB_DIM) + _EPS) * gk_ref[...]).astype(_BF16)
    u_ref[...] = z[:, width:].astype(_BF16)

    qt = zt[:width].reshape(width // _SUB_DIM, _SUB_DIM, tm)
    ms = jnp.mean(qt * qt, axis=1, keepdims=True)
    qt_ref[...] = ((qt * lax.rsqrt(ms + _EPS)).reshape(width, tm) * gqt_ref[...]).astype(_BF16)

    vt = zt[width:].astype(_BF16)
    for j in range(tm // _KEY_GRANULE):
        for hd in range(_HEADS):
            vt_ref[j, hd] = vt[hd * _V_DIM:(hd + 1) * _V_DIM,
                               j * _KEY_GRANULE:(j + 1) * _KEY_GRANULE]


def _inproj(x2, norm_gain, w_ku, w_qvt, gqt, gk, gsum, width):
    n, d = x2.shape
    tm = _TOKEN_TILE
    full = lambda a: pl.BlockSpec(a.shape, lambda i: (0,) * a.ndim)
    row_sd = jax.ShapeDtypeStruct((n, width), _BF16)
    row_spec = pl.BlockSpec((tm, width), lambda i: (i, 0))
    gran = tm // _KEY_GRANULE
    vrows = _V_DIM
    return pl.pallas_call(
        functools.partial(_inproj_body, width=width),
        grid=(n // tm,),
        in_specs=[pl.BlockSpec((tm, d), lambda i: (i, 0)), full(norm_gain), full(w_ku), full(w_qvt),
                  full(gqt), full(gk), full(gsum)],
        out_specs=[pl.BlockSpec((width, tm), lambda i: (0, i)), row_spec,
                   pl.BlockSpec((gran, _HEADS, vrows, _KEY_GRANULE), lambda i: (i, 0, 0, 0)), row_spec],
        out_shape=[jax.ShapeDtypeStruct((width, n), _BF16), row_sd,
                   jax.ShapeDtypeStruct((n // _KEY_GRANULE, _HEADS, vrows, _KEY_GRANULE), _BF16), row_sd],
        compiler_params=pltpu.CompilerParams(
            dimension_semantics=("parallel",), vmem_limit_bytes=_VMEM_LIMIT),
        name="inproj",
    )(x2, norm_gain, w_ku, w_qvt, gqt, gk, gsum)


def _t5_bucket_np(rel):
    nb = _REL_BUCKETS // 2
    max_exact = nb // 2
    side = np.where(rel > 0, nb, 0)
    n = np.abs(rel)
    nf = np.maximum(n, 1).astype(np.float32)
    large = max_exact + (np.log(nf / np.float32(max_exact)) / np.float32(math.log(_REL_MAX_DIST / max_exact))
                         * np.float32(nb - max_exact)).astype(np.int32)
    large = np.minimum(large, nb - 1)
    return side + np.where(n < max_exact, n, large)


def _bucket_blocks():
    assert _BIAS_BLOCK % _CHUNK == 0
    i = np.arange(_BIAS_BLOCK)[None, :]
    j = np.arange(_BIAS_BLOCK)[:, None]
    far_bucket = _REL_BUCKETS // 2 - 1
    assert (_t5_bucket_np(j - i - 2 * _BIAS_BLOCK) == far_bucket).all()
    diag = np.where((j // _CHUNK) <= (i // _CHUNK), _t5_bucket_np(j - i), -1)
    prev = _t5_bucket_np(j - i - _BIAS_BLOCK)
    return np.stack([diag, prev]).astype(np.int32)


def _bias_prep_body(tab_ref, tabv_ref, bucket_ref, lam4_ref, gains_ref,
                    bias_ref, lam_ref, flag_ref, *, lam_init, t):
    h = pl.program_id(0)
    far_bucket = _REL_BUCKETS // 2 - 1
    bkt = bucket_ref[...]
    far = tab_ref[far_bucket, h]
    val = jnp.full(bkt.shape, -jnp.inf, _F32)
    for b in range(_REL_BUCKETS):
        val = jnp.where(bkt == b, (tab_ref[b, h] - far) * _LOG2E, val)
    diag_block, prev_block = val[0], val[1]
    n_blk = t // _BIAS_BLOCK
    bias_ref[0, 0] = jnp.zeros((t, t), _F32)
    bias_ref[0, 1] = jnp.zeros((t, t), _F32)
    bias_ref[0, 1, t - _BIAS_BLOCK:, :_BIAS_BLOCK] = prev_block
    for bj in range(n_blk):
        rows = slice(bj * _BIAS_BLOCK, (bj + 1) * _BIAS_BLOCK)
        for bi in range(n_blk):
            cols = slice(bi * _BIAS_BLOCK, (bi + 1) * _BIAS_BLOCK)
            if bi == bj:
                block = diag_block
            elif bi == bj + 1:
                block = prev_block
            else:
                block = jnp.full((_BIAS_BLOCK, _BIAS_BLOCK), 0.0 if bi > bj else -jnp.inf, _F32)
            bias_ref[0, 2, rows, cols] = block
    l4 = lam4_ref[...]
    s1 = jnp.sum(l4[0:1] * l4[1:2], axis=-1, keepdims=True)
    s2 = jnp.sum(l4[2:3] * l4[3:4], axis=-1, keepdims=True)
    lam_ref[...] = jnp.broadcast_to(jnp.exp(s1) - jnp.exp(s2) + lam_init, lam_ref.shape)
    gmax = jnp.max(jnp.abs(gains_ref[...]), axis=-1, keepdims=True)
    qk_bound = (_SUB_DIM * 1.02) * gmax[0:1] * gmax[1:2]
    tv = tabv_ref[...]
    dev = jnp.abs(tv - tv[far_bucket:far_bucket + 1]) * _LOG2E
    bias_bound = jnp.max(jnp.max(dev, axis=-1, keepdims=True), axis=0, keepdims=True)
    flag_ref[...] = (qk_bound + bias_bound <= _SAFE_LOG2_RANGE).astype(jnp.int32)


def _bias_prep(rel_table, lam4, gains, lam_init, t):
    assert t % _BIAS_BLOCK == 0
    buckets = jnp.asarray(_bucket_blocks())
    whole = lambda a: pl.BlockSpec(a.shape, lambda h: (0,) * a.ndim)
    return pl.pallas_call(
        functools.partial(_bias_prep_body, lam_init=lam_init, t=t),
        grid=(_HEADS,),
        in_specs=[pl.BlockSpec(memory_space=pltpu.SMEM), whole(rel_table), whole(buckets),
                  whole(lam4), whole(gains)],
        out_specs=[pl.BlockSpec((1, 3, t, t), lambda h: (h, 0, 0, 0)),
                   pl.BlockSpec((1, _V_DIM), lambda h: (0, 0)),
                   pl.BlockSpec((1, 1), lambda h: (0, 0))],
        out_shape=[jax.ShapeDtypeStruct((_HEADS, 3, t, t), _F32),
                   jax.ShapeDtypeStruct((1, _V_DIM), _F32),
                   jax.ShapeDtypeStruct((1, 1), jnp.int32)],
        compiler_params=pltpu.CompilerParams(dimension_semantics=("arbitrary",)),
        name="attn_bias_prep",
    )(rel_table, rel_table, buckets, lam4, gains)


def _attn_body(flag_ref, qta_ref, qtb_ref, k_ref, vt_ref, bias_ref, lam_ref, gain_ref, o_ref,
               qw_ref, m_ref, acc_ref, den_ref, *, t, nq):
    p = pl.program_id(2)
    gran = t // _KEY_GRANULE
    for side, qt_ref in enumerate((qta_ref, qtb_ref)):
        qt = qt_ref[...]
        row = lax.broadcasted_iota(jnp.int32, qt.shape, 0)
        zero = jnp.zeros_like(qt)
        qw_ref[side, :, :t] = jnp.where(row < _SUB_DIM, qt, zero)
        qw_ref[side, :, t:] = jnp.where(row >= _SUB_DIM, qt, zero)
    acc_ref[...] = jnp.zeros(acc_ref.shape, _F32)
    den_ref[...] = jnp.zeros(den_ref.shape, _F32)

    def key_sum(e):
        return jnp.sum(e.reshape(e.shape[0] // 8, 8, e.shape[1]), axis=0)

    def schedule(step_index):
        mirrored = step_index > p
        side = mirrored.astype(jnp.int32)
        kj = jnp.where(mirrored, step_index - p - 1, p - step_index)
        q_tile = jnp.where(mirrored, nq - 1 - p, p)
        bias_index = jnp.where(kj == q_tile, 2, jnp.where(kj == q_tile - 1, 1, 0))
        return side, kj, bias_index

    def apply_to_values(kj, pt):
        pv = jnp.dot(vt_ref[kj * gran, 0], pt[:_KEY_GRANULE], preferred_element_type=_F32)
        for g in range(1, gran):
            pv = pv + jnp.dot(vt_ref[kj * gran + g, 0],
                              pt[g * _KEY_GRANULE:(g + 1) * _KEY_GRANULE],
                              preferred_element_type=_F32)
        return pv

    def key_tile(kj):
        return k_ref[0, pl.ds(pl.multiple_of(kj * t, t), t), :]

    def bounded_step(step_index):
        side, kj, bias_index = schedule(step_index)
        st = jnp.dot(key_tile(kj), qw_ref[side], preferred_element_type=_F32)
        bias = bias_ref[0, bias_index]
        e = [jnp.exp2(st[:, :t] + bias), jnp.exp2(st[:, t:] + bias)]
        den_ref[side] += jnp.concatenate([key_sum(e[0]), key_sum(e[1])], axis=1)
        pt = jnp.concatenate([e[0].astype(_BF16), e[1].astype(_BF16)], axis=1)
        acc_ref[side] += apply_to_values(kj, pt)

    def bounded_diagonal_step(side, q_tile):
        assert gran == 2
        half = _KEY_GRANULE
        late = [slice(half, t), slice(t + half, 2 * t)]
        k0 = pl.multiple_of(q_tile * t, t)
        bias = bias_ref[0, 2]
        st0 = jnp.dot(k_ref[0, pl.ds(k0, half), :], qw_ref[side], preferred_element_type=_F32)
        e0 = [jnp.exp2(st0[:, :t] + bias[:half]), jnp.exp2(st0[:, t:] + bias[:half])]
        den_ref[side] += jnp.concatenate([key_sum(e0[0]), key_sum(e0[1])], axis=1)
        pt0 = jnp.concatenate([e0[0].astype(_BF16), e0[1].astype(_BF16)], axis=1)
        qw_late = jnp.concatenate([qw_ref[side, :, lanes] for lanes in late], axis=1)
        st1 = jnp.dot(k_ref[0, pl.ds(k0 + half, half), :], qw_late, preferred_element_type=_F32)
        e1 = [jnp.exp2(st1[:, :half] + bias[half:, half:]), jnp.exp2(st1[:, half:] + bias[half:, half:])]
        pt1 = jnp.concatenate([e1[0].astype(_BF16), e1[1].astype(_BF16)], axis=1)
        acc_ref[side] += jnp.dot(vt_ref[q_tile * gran, 0], pt0, preferred_element_type=_F32)
        pv1 = jnp.dot(vt_ref[q_tile * gran + 1, 0], pt1, preferred_element_type=_F32)
        for n, lanes in enumerate(late):
            acc_ref[side, :, lanes] += pv1[:, n * half:(n + 1) * half]
            den_ref[side, :, lanes] += key_sum(e1[n])

    def online_step(step_index):
        side, kj, bias_index = schedule(step_index)
        kt = key_tile(kj)
        bias = bias_ref[0, bias_index]
        for s in range(2):
            lanes = slice(s * t, (s + 1) * t)
            st = jnp.dot(kt, qw_ref[side, :, lanes], preferred_element_type=_F32) + bias
            m_old = m_ref[side, s]
            m_new = jnp.maximum(m_old, jnp.max(st, axis=0, keepdims=True))
            alpha = jnp.exp2(m_old - m_new)
            e = jnp.exp2(st - m_new)
            den_ref[side, :, lanes] = alpha * den_ref[side, :, lanes] + key_sum(e)
            acc_ref[side, :, lanes] = alpha * acc_ref[side, :, lanes] + apply_to_values(
                kj, e.astype(_BF16))
            m_ref[side, s] = m_new

    bounded = flag_ref[0, 0] > 0

    @pl.when(bounded)
    def _():
        bounded_diagonal_step(0, p)
        for step_index in range(1, nq):
            bounded_step(step_index)
        bounded_diagonal_step(1, nq - 1 - p)

    @pl.when(jnp.logical_not(bounded))
    def _():
        m_ref[...] = jnp.full(m_ref.shape, -jnp.inf, _F32)
        for step_index in range(nq + 1):
            online_step(step_index)

    for side in range(2):
        o = acc_ref[side] / jnp.sum(den_ref[side], axis=0, keepdims=True)
        od = o[:, :t] - lam_ref[0:1, 0:1] * o[:, t:]
        scale = lax.rsqrt(jnp.mean(od * od, axis=0, keepdims=True) + _EPS)
        o_ref[side, 0, 0] = (od * scale * gain_ref[...]).T.astype(_BF16)


def _attention(flag, qt, k, vt, bias, lam, gain_t):
    b, l, w = k.shape
    t = _ATTN_TILE
    nq = l // t
    vrows = vt.shape[2]
    return pl.pallas_call(
        functools.partial(_attn_body, t=t, nq=nq),
        grid=(b, _HEADS, nq // 2),
        in_specs=[pl.BlockSpec(memory_space=pltpu.SMEM),
                  pl.BlockSpec((_V_DIM, t), lambda bi, h, p: (h, bi * nq + p)),
                  pl.BlockSpec((_V_DIM, t), lambda bi, h, p: (h, bi * nq + nq - 1 - p)),
                  pl.BlockSpec((1, l, _V_DIM), lambda bi, h, p: (bi, 0, h)),
                  pl.BlockSpec((l // _KEY_GRANULE, 1, vrows, _KEY_GRANULE),
                               lambda bi, h, p: (bi, h, 0, 0)),
                  pl.BlockSpec((1, 3, t, t), lambda bi, h, p: (h, 0, 0, 0)),
                  pl.BlockSpec(lam.shape, lambda bi, h, p: (0, 0)),
                  pl.BlockSpec(gain_t.shape, lambda bi, h, p: (0, 0))],
        out_specs=pl.BlockSpec((2, 1, 1, t, _V_DIM), lambda bi, h, p: (0, bi, p, 0, h)),
        out_shape=jax.ShapeDtypeStruct((2, b, nq // 2, t, w), _BF16),
        scratch_shapes=[pltpu.VMEM((2, _V_DIM, 2 * t), _BF16),
                        pltpu.VMEM((2, 2, 1, t), _F32),
                        pltpu.VMEM((2, vrows, 2 * t), _F32),
                        pltpu.VMEM((2, 8, 2 * t), _F32)],
        compiler_params=pltpu.CompilerParams(
            dimension_semantics=("parallel", "parallel", "arbitrary"),
            vmem_limit_bytes=_VMEM_LIMIT),
        name="diff_attention",
    )(flag, qt, qt, k, vt, bias, lam, gain_t)


def _attn_bounded_body(qt_ref, k_ref, vt_ref, bias_ref, lam_ref, gain_ref, o_ref,
                       qw_ref, acc_ref, den_ref, *, t, nq):
    gran = t // _KEY_GRANULE
    assert gran == 2
    half = _KEY_GRANULE
    row = lax.broadcasted_iota(jnp.int32, (_V_DIM, t), 0)
    for q in range(nq):
        qt = qt_ref[:, q * t:(q + 1) * t]
        zero = jnp.zeros_like(qt)
        qw_ref[q, :, :t] = jnp.where(row < _SUB_DIM, qt, zero)
        qw_ref[q, :, t:] = jnp.where(row >= _SUB_DIM, qt, zero)
    acc_ref[...] = jnp.zeros(acc_ref.shape, _F32)
    den_ref[...] = jnp.zeros(den_ref.shape, _F32)

    def key_sum(e):
        return jnp.sum(e.reshape(e.shape[0] // 8, 8, e.shape[1]), axis=0)

    def diagonal(kj):
        late = [slice(half, t), slice(t + half, 2 * t)]
        k0 = kj * t
        bias = bias_ref[0, 2]
        st0 = jnp.dot(k_ref[0, k0:k0 + half, :], qw_ref[kj], preferred_element_type=_F32)
        e0 = [jnp.exp2(st0[:, :t] + bias[:half]), jnp.exp2(st0[:, t:] + bias[:half])]
        den_ref[kj] += jnp.concatenate([key_sum(e0[0]), key_sum(e0[1])], axis=1)
        pt0 = jnp.concatenate([e0[0].astype(_BF16), e0[1].astype(_BF16)], axis=1)
        qw_late = jnp.concatenate([qw_ref[kj, :, lanes] for lanes in late], axis=1)
        st1 = jnp.dot(k_ref[0, k0 + half:k0 + t, :], qw_late, preferred_element_type=_F32)
        e1 = [jnp.exp2(st1[:, :half] + bias[half:, half:]), jnp.exp2(st1[:, half:] + bias[half:, half:])]
        pt1 = jnp.concatenate([e1[0].astype(_BF16), e1[1].astype(_BF16)], axis=1)
        acc_ref[kj] += jnp.dot(vt_ref[kj * gran, 0], pt0, preferred_element_type=_F32)
        pv1 = jnp.dot(vt_ref[kj * gran + 1, 0], pt1, preferred_element_type=_F32)
        for n, lanes in enumerate(late):
            acc_ref[kj, :, lanes] += pv1[:, n * half:(n + 1) * half]
            den_ref[kj, :, lanes] += key_sum(e1[n])

    def later_queries(kj, q_tiles):
        qw = jnp.concatenate([qw_ref[q] for q in q_tiles], axis=1)
        st = jnp.dot(k_ref[0, kj * t:(kj + 1) * t, :], qw, preferred_element_type=_F32)
        if q_tiles[0] == kj + 1:
            blk = _BIAS_BLOCK
            corner = bias_ref[0, 1, t - blk:, :blk]
            tail = st[t - blk:]
            tail = jnp.concatenate([tail[:, :blk] + corner, tail[:, blk:t],
                                    tail[:, t:t + blk] + corner, tail[:, t + blk:]], axis=1)
            st = jnp.concatenate([st[:t - blk], tail], axis=0)
        e = jnp.exp2(st)
        pv = (jnp.dot(vt_ref[kj * gran, 0], e[:half].astype(_BF16), preferred_element_type=_F32)
              + jnp.dot(vt_ref[kj * gran + 1, 0], e[half:].astype(_BF16), preferred_element_type=_F32))
        for n, q in enumerate(q_tiles):
            lanes = slice(n * 2 * t, (n + 1) * 2 * t)
            den_ref[q] += key_sum(e[:, lanes])
            acc_ref[q] += pv[:, lanes]

    for kj in range(nq):
        diagonal(kj)
        later = list(range(kj + 1, nq))
        for g0 in range(0, len(later), _QUERY_GROUP):
            later_queries(kj, later[g0:g0 + _QUERY_GROUP])

    for q in range(nq):
        o = acc_ref[q] / jnp.sum(den_ref[q], axis=0, keepdims=True)
        od = o[:, :t] - lam_ref[0:1, 0:1] * o[:, t:]
        scale = lax.rsqrt(jnp.mean(od * od, axis=0, keepdims=True) + _EPS)
        o_ref[0, q * t:(q + 1) * t, :] = (od * scale * gain_ref[...]).T.astype(_BF16)


def _attention_bounded(qt, k, vt, bias, lam, gain_t):
    b, l, w = k.shape
    t = _ATTN_TILE
    nq = l // t
    return pl.pallas_call(
        functools.partial(_attn_bounded_body, t=t, nq=nq),
        grid=(b, _HEADS),
        in_specs=[pl.BlockSpec((_V_DIM, l), lambda bi, h: (h, bi)),
                  pl.BlockSpec((1, l, _V_DIM), lambda bi, h: (bi, 0, h)),
                  pl.BlockSpec((l // _KEY_GRANULE, 1, _V_DIM, _KEY_GRANULE), lambda bi, h: (bi, h, 0, 0)),
                  pl.BlockSpec((1, 3, t, t), lambda bi, h: (h, 0, 0, 0)),
                  pl.BlockSpec(lam.shape, lambda bi, h: (0, 0)),
                  pl.BlockSpec(gain_t.shape, lambda bi, h: (0, 0))],
        out_specs=pl.BlockSpec((1, l, _V_DIM), lambda bi, h: (bi, 0, h)),
        out_shape=jax.ShapeDtypeStruct((b, l, w), _BF16),
        scratch_shapes=[pltpu.VMEM((nq, _V_DIM, 2 * t), _BF16),
                        pltpu.VMEM((nq, _V_DIM, 2 * t), _F32),
                        pltpu.VMEM((nq, 8, 2 * t), _F32)],
        compiler_params=pltpu.CompilerParams(
            dimension_semantics=("parallel", "parallel"), vmem_limit_bytes=_VMEM_LIMIT),
        name="diff_attention_bounded",
    )(qt, k, vt, bias, lam, gain_t)


def _s5_prep_body(are_ref, aim_ref, ldt_ref, bre_ref, bim_ref,
                  lbre_ref, lbim_ref, bbre_ref, bbim_ref):
    a_re = are_ref[...]
    a_im = aim_ref[...]
    dt = jnp.exp(ldt_ref[...])
    decay = jnp.exp(a_re * dt)
    lb_re = decay * jnp.cos(a_im * dt)
    lb_im = decay * jnp.sin(a_im * dt)
    nr = lb_re - 1.0
    ni = lb_im
    den = a_re * a_re + a_im * a_im
    q_re = (nr * a_re + ni * a_im) / den
    q_im = (ni * a_re - nr * a_im) / den
    b_re = bre_ref[...]
    b_im = bim_ref[...]
    bbre_ref[...] = q_re * b_re - q_im * b_im
    bbim_ref[...] = q_re * b_im + q_im * b_re
    lbre_ref[...] = lb_re
    lbim_ref[...] = lb_im


def _s5_prep(a_re, a_im, log_dt, b_re, b_im):
    sd = jax.ShapeDtypeStruct(a_re.shape, _F32)
    return pl.pallas_call(_s5_prep_body, out_shape=[sd] * 4, name="s5_discretise")(
        a_re, a_im, log_dt, b_re, b_im)


def _s5_body(u_ref, wbre_ref, wbim_ref, lbre_ref, lbim_ref, wcre_ref, wcim_ref, d_ref,
             gw_ref, gb_ref, o_ref, xre_ref, xim_ref, cre_ref, cim_ref, *, steps, nb):
    rows = nb * steps
    width = u_ref.shape[-1]
    n_tiles, _, tile_w = lbre_ref.shape
    halves = wcre_ref.shape[0]
    tiles_per_half = n_tiles // halves
    hw_in = width // halves

    @pl.when(pl.program_id(0) == 0)
    def _():
        cre_ref[...] = jnp.zeros(cre_ref.shape, _F32)
        cim_ref[...] = jnp.zeros(cim_ref.shape, _F32)

    u_tm = jnp.swapaxes(u_ref[...].astype(_F32), 0, 1).reshape(rows, width).astype(_BF16)

    for hf in range(halves):
        uh = u_tm[:, hf * hw_in:(hf + 1) * hw_in]
        bu_r = jnp.dot(uh, wbre_ref[hf], preferred_element_type=_F32)
        bu_i = jnp.dot(uh, wbim_ref[hf], preferred_element_type=_F32)
        group = range(hf * tiles_per_half, (hf + 1) * tiles_per_half)
        a_r = [jnp.broadcast_to(lbre_ref[j], (nb, tile_w)) for j in group]
        a_i = [jnp.broadcast_to(lbim_ref[j], (nb, tile_w)) for j in group]
        x_r = [cre_ref[j] for j in group]
        x_i = [cim_ref[j] for j in group]
        for ti in range(steps):
            r = slice(ti * nb, (ti + 1) * nb)
            for q, j in enumerate(group):
                lanes = slice(q * tile_w, (q + 1) * tile_w)
                n_r = a_r[q] * x_r[q] - a_i[q] * x_i[q] + bu_r[r, lanes]
                n_i = a_r[q] * x_i[q] + a_i[q] * x_r[q] + bu_i[r, lanes]
                xre_ref[j, r, :] = n_r
                xim_ref[j, r, :] = n_i
                x_r[q], x_i[q] = n_r, n_i
        for q, j in enumerate(group):
            cre_ref[j] = x_r[q]
            cim_ref[j] = x_i[q]

    ys = []
    for hf in range(halves):
        tiles = range(hf * tiles_per_half, (hf + 1) * tiles_per_half)
        x_r = jnp.concatenate([xre_ref[j] for j in tiles], axis=1).astype(_BF16)
        x_i = jnp.concatenate([xim_ref[j] for j in tiles], axis=1).astype(_BF16)
        ys.append(jnp.dot(x_r, wcre_ref[hf], preferred_element_type=_F32)
                  + jnp.dot(x_i, wcim_ref[hf], preferred_element_type=_F32))
    y = jnp.concatenate(ys, axis=-1) + d_ref[...] * u_tm.astype(_F32)
    g = jax.nn.gelu(y)
    o = g * jax.nn.sigmoid(jnp.dot(g.astype(_BF16), gw_ref[...], preferred_element_type=_F32)
                           + gb_ref[...])
    o_ref[...] = jnp.swapaxes(o.reshape(steps, nb, width), 0, 1).astype(_BF16)


def _s5(u, wb_re, wb_im, lb_re, lb_im, wc_re, wc_im, d_vec, glu_w, glu_b):
    nb, l, width = u.shape
    steps = _SCAN_STEPS
    n_tiles, _, tile_w = lb_re.shape
    full = lambda a: pl.BlockSpec(a.shape, lambda i: (0,) * a.ndim)
    io_spec = pl.BlockSpec((nb, steps, width), lambda i: (0, i, 0))
    consts = (wb_re, wb_im, lb_re, lb_im, wc_re, wc_im, d_vec, glu_w, glu_b)
    state_buf = pltpu.VMEM((n_tiles, nb * steps, tile_w), _F32)
    return pl.pallas_call(
        functools.partial(_s5_body, steps=steps, nb=nb),
        grid=(l // steps,),
        in_specs=[io_spec] + [full(a) for a in consts],
        out_specs=io_spec,
        out_shape=jax.ShapeDtypeStruct(u.shape, _BF16),
        scratch_shapes=[state_buf, state_buf,
                        pltpu.VMEM((n_tiles, nb, tile_w), _F32),
                        pltpu.VMEM((n_tiles, nb, tile_w), _F32)],
        compiler_params=pltpu.CompilerParams(
            dimension_semantics=("arbitrary",), vmem_limit_bytes=_VMEM_LIMIT),
        name="s5_branch",
    )(u, *consts)


def _outproj_body(x_ref, ng_ref, wg_ref, mb_ref, oa_ref, os_ref, pa_ref, ps_ref, wo_ref,
                  out_ref, *, width):
    x = x_ref[...]
    d = x.shape[-1]
    h = (x * _rms_scale(x) * ng_ref[...]).astype(_BF16)
    zg = jnp.dot(h, wg_ref[...], preferred_element_type=_F32)
    o_a = (oa_ref[...].astype(_F32) * jax.nn.silu(zg[:, :width])).astype(_BF16)
    o_s = (os_ref[...].astype(_F32) * jax.nn.silu(zg[:, width:2 * width])).astype(_BF16)
    p_a = jnp.dot(o_a, pa_ref[...], preferred_element_type=_F32)
    p_s = jnp.dot(o_s, ps_ref[...], preferred_element_type=_F32)
    g = jax.nn.sigmoid(zg[:, 2 * width:] + mb_ref[...])
    merged = g[:, :d] * p_a + g[:, d:] * p_s
    out_ref[...] = x + jnp.dot(merged.astype(_BF16), wo_ref[...], preferred_element_type=_F32)


def _outproj(x2, norm_gain, w_gates, merge_b, o_a, o_s, proj_a, proj_s, w_out, width):
    n, d = x2.shape
    tm = _TOKEN_TILE
    full = lambda a: pl.BlockSpec(a.shape, lambda i: (0,) * a.ndim)
    row = lambda a: pl.BlockSpec((tm, a.shape[-1]), lambda i: (i, 0))
    return pl.pallas_call(
        functools.partial(_outproj_body, width=width),
        grid=(n // tm,),
        in_specs=[row(x2), full(norm_gain), full(w_gates), full(merge_b), row(o_a), row(o_s),
                  full(proj_a), full(proj_s), full(w_out)],
        out_specs=row(x2),
        out_shape=jax.ShapeDtypeStruct(x2.shape, x2.dtype),
        compiler_params=pltpu.CompilerParams(
            dimension_semantics=("parallel",), vmem_limit_bytes=_VMEM_LIMIT),
        name="outproj",
    )(x2, norm_gain, w_gates, merge_b, o_a, o_s, proj_a, proj_s, w_out)


def _block_diag_halves(blocks, halves):
    g, r, c = blocks.shape
    gh = g // halves
    eye = jnp.eye(gh, dtype=blocks.dtype)
    b = blocks.reshape(halves, gh, r, 1, c) * eye.reshape(1, gh, 1, gh, 1)
    return b.reshape(halves, gh * r, gh * c)


def _layer(x, lam_init, norm_gain, w_in, merge_gate_b, q_norm_gain, k_norm_gain,
           lambda_q1, lambda_k1, lambda_q2, lambda_k2, diff_subln_gain, rel_bias_table,
           ssm_a_re, ssm_a_im, ssm_log_dt, ssm_b_re, ssm_b_im, ssm_c_re, ssm_c_im,
           ssm_d, ssm_glu_w, ssm_glu_b, proj_attn, proj_ssm, w_out):
    nb, l, d = x.shape
    n = nb * l
    aw = _HEADS * 2 * _SUB_DIM
    groups = ssm_a_re.shape[0]
    x2 = x.reshape(n, d)
    ng = norm_gain.reshape(1, d).astype(_F32)

    w_ku = jnp.concatenate([w_in[:, aw:2 * aw], w_in[:, 4 * aw:5 * aw]], axis=1).astype(_BF16)
    w_qvt = jnp.concatenate([w_in[:, :aw], w_in[:, 2 * aw:3 * aw]], axis=1).T.astype(_BF16)
    w_gates = jnp.concatenate([w_in[:, 3 * aw:4 * aw], w_in[:, 5 * aw:]], axis=1).astype(_BF16)
    gq = jnp.tile(q_norm_gain.astype(_F32), 2 * _HEADS) * (_SUB_DIM ** -0.5 * _LOG2E)
    gqt = jnp.broadcast_to(gq[:, None], (aw, _TOKEN_TILE))
    gk = jnp.tile(k_norm_gain.astype(_F32), 2 * _HEADS).reshape(1, aw)
    seg = np.arange(_MXU_TILE) // _SUB_DIM
    gsum = jnp.asarray((seg[:, None] == seg[None, :]).astype(np.float32), _BF16)
    qt, k, vt, u = _inproj(x2, ng, w_ku, w_qvt, gqt, gk, gsum, aw)

    lam4 = jnp.stack([lambda_q1, lambda_k1, lambda_q2, lambda_k2]).astype(_F32)
    qk_gains = jnp.stack([gq[:_SUB_DIM], gk[0, :_SUB_DIM]])
    bias, lam, flag = _bias_prep(rel_bias_table.astype(_F32), lam4, qk_gains, lam_init, _ATTN_TILE)
    subln = diff_subln_gain.astype(_F32) * (1.0 - lam_init)
    subln_t = jnp.broadcast_to(subln[:, None], (_V_DIM, _ATTN_TILE))
    shp = (nb, l, aw)
    k3 = k.reshape(shp)

    def online_attention():
        o = _attention(flag, qt, k3, vt, bias, lam, subln_t)
        return jnp.concatenate([o[0], o[1][:, ::-1]], axis=1).reshape(shp)

    o_a = lax.cond(flag[0, 0] > 0,
                   lambda: _attention_bounded(qt, k3, vt, bias, lam, subln_t),
                   online_attention)

    rep = lambda a: jnp.repeat(a.astype(_F32), _SSM_GROUP, axis=0)
    ldt = jnp.broadcast_to(ssm_log_dt.astype(_F32)[:, None], ssm_a_re.shape)
    bt = lambda a: a.astype(_F32).transpose(0, 2, 1).reshape(groups * _SSM_GROUP, _SSM_STATE)
    lb_re, lb_im, bb_re, bb_im = _s5_prep(rep(ssm_a_re), rep(ssm_a_im), rep(ldt),
                                          bt(ssm_b_re), bt(ssm_b_im))
    n_tiles = groups * _SSM_STATE // _STATE_TILE
    flat = lambda a: a[::_SSM_GROUP].reshape(n_tiles, 1, _STATE_TILE)
    gshape = (groups, _SSM_GROUP, _SSM_STATE)
    wb_re = _block_diag_halves(bb_re.reshape(gshape), 2).astype(_BF16)
    wb_im = _block_diag_halves(bb_im.reshape(gshape), 2).astype(_BF16)
    wc_re = _block_diag_halves(ssm_c_re.astype(_F32).transpose(0, 2, 1), 2).astype(_BF16)
    wc_im = _block_diag_halves(-ssm_c_im.astype(_F32).transpose(0, 2, 1), 2).astype(_BF16)
    o_s = _s5(u.reshape(shp), wb_re, wb_im, flat(lb_re), flat(lb_im), wc_re, wc_im,
              ssm_d.astype(_F32).reshape(1, aw), ssm_glu_w.astype(_BF16),
              ssm_glu_b.astype(_F32).reshape(1, aw))

    out = _outproj(x2, ng, w_gates, merge_gate_b.astype(_F32).reshape(1, 2 * d),
                   o_a.reshape(n, aw), o_s.reshape(n, aw),
                   proj_attn.astype(_BF16), proj_ssm.astype(_BF16), w_out.astype(_BF16), aw)
    return out.reshape(nb, l, d)


def kernel(x, norm_gain, w_in, merge_gate_b, q_norm_gain, k_norm_gain, lambda_q1, lambda_k1,
           lambda_q2, lambda_k2, diff_subln_gain, rel_bias_table, ssm_A_re, ssm_A_im, ssm_log_dt,
           ssm_B_re, ssm_B_im, ssm_C_re, ssm_C_im, ssm_D, ssm_glu_w, ssm_glu_b,
           proj_attn, proj_ssm, w_out):
    per_layer = (norm_gain, w_in, merge_gate_b, q_norm_gain, k_norm_gain, lambda_q1, lambda_k1,
                 lambda_q2, lambda_k2, diff_subln_gain)
    per_layer_tail = (ssm_A_re, ssm_A_im, ssm_log_dt, ssm_B_re, ssm_B_im, ssm_C_re, ssm_C_im,
                      ssm_D, ssm_glu_w, ssm_glu_b, proj_attn, proj_ssm, w_out)
    for layer in range(norm_gain.shape[0]):
        lam_init = 0.8 - 0.6 * math.exp(-0.3 * layer)
        x = _layer(x, lam_init, *(p[layer] for p in per_layer), rel_bias_table,
                   *(p[layer] for p in per_layer_tail))
    return x
```

```python
import functools
import math

import jax
import jax.numpy as jnp
import numpy as np
from jax import lax
from jax.experimental import pallas as pl
from jax.experimental.pallas import tpu as pltpu

_F32 = jnp.float32
_BF16 = jnp.bfloat16

_CHUNK = 64
_HEADS = 4
_SUB_DIM = 64
_V_DIM = 128
_SSM_GROUP = 16
_SSM_STATE = 64
_REL_BUCKETS = 32
_REL_MAX_DIST = 128
_EPS = 1e-6
_LOG2E = math.log2(math.e)
_SAFE_LOG2_RANGE = 100.0

_VMEM_LIMIT = 48 * 1024 * 1024
_MXU_TILE = 256
_TOKEN_TILE = 1024
_ATTN_TILE = 512
_KEY_GRANULE = 256
_BIAS_BLOCK = 128
_QUERY_GROUP = 2
_SCAN_STEPS = 128
_STATE_TILE = 256


def _rms_scale(x, eps=_EPS):
    return lax.rsqrt(jnp.mean(x * x, axis=-1, keepdims=True) + eps)


def _inproj_body(x_ref, ng_ref, wku_ref, wqvt_ref, gqt_ref, gk_ref, gsum_ref,
                 qt_ref, k_ref, vt_ref, u_ref, *, width):
    x = x_ref[...]
    tm = x.shape[0]
    h = (x * _rms_scale(x) * ng_ref[...]).astype(_BF16)
    z = jnp.dot(h, wku_ref[...], preferred_element_type=_F32)
    zt = lax.dot_general(wqvt_ref[...], h, (((1,), (1,)), ((), ())),
                         preferred_element_type=_F32)

    kk = z[:, :width]
    k2 = (kk * kk).astype(_BF16)
    mxu = gsum_ref.shape[0]
    ss = jnp.concatenate([jnp.dot(k2[:, c:c + mxu], gsum_ref[...], preferred_element_type=_F32)
                          for c in range(0, width, mxu)], axis=1)
    k_ref[...] = (kk * lax.rsqrt(ss * (1.0 / _SUB_DIM) + _EPS) * gk_ref[...]).astype(_BF16)
    u_ref[...] = z[:, width:].astype(_BF16)

    qt = zt[:width].reshape(width // _SUB_DIM, _SUB_DIM, tm)
    ms = jnp.mean(qt * qt, axis=1, keepdims=True)
    qt_ref[...] = ((qt * lax.rsqrt(ms + _EPS)).reshape(width, tm) * gqt_ref[...]).astype(_BF16)

    vt = zt[width:].astype(_BF16)
    for j in range(tm // _KEY_GRANULE):
        for hd in range(_HEADS):
            vt_ref[j, hd] = vt[hd * _V_DIM:(hd + 1) * _V_DIM,
                               j * _KEY_GRANULE:(j + 1) * _KEY_GRANULE]


def _inproj(x2, norm_gain, w_ku, w_qvt, gqt, gk, gsum, width):
    n, d = x2.shape
    tm = _TOKEN_TILE
    full = lambda a: pl.BlockSpec(a.shape, lambda i: (0,) * a.ndim)
    row_sd = jax.ShapeDtypeStruct((n, width), _BF16)
    row_spec = pl.BlockSpec((tm, width), lambda i: (i, 0))
    gran = tm // _KEY_GRANULE
    vrows = _V_DIM
    return pl.pallas_call(
        functools.partial(_inproj_body, width=width),
        grid=(n // tm,),
        in_specs=[pl.BlockSpec((tm, d), lambda i: (i, 0)), full(norm_gain), full(w_ku), full(w_qvt),
                  full(gqt), full(gk), full(gsum)],
        out_specs=[pl.BlockSpec((width, tm), lambda i: (0, i)), row_spec,
                   pl.BlockSpec((gran, _HEADS, vrows, _KEY_GRANULE), lambda i: (i, 0, 0, 0)), row_spec],
        out_shape=[jax.ShapeDtypeStruct((width, n), _BF16), row_sd,
                   jax.ShapeDtypeStruct((n // _KEY_GRANULE, _HEADS, vrows, _KEY_GRANULE), _BF16), row_sd],
        compiler_params=pltpu.CompilerParams(
            dimension_semantics=("parallel",), vmem_limit_bytes=_VMEM_LIMIT),
        name="inproj",
    )(x2, norm_gain, w_ku, w_qvt, gqt, gk, gsum)


def _t5_bucket_np(rel):
    nb = _REL_BUCKETS // 2
    max_exact = nb // 2
    side = np.where(rel > 0, nb, 0)
    n = np.abs(rel)
    nf = np.maximum(n, 1).astype(np.float32)
    large = max_exact + (np.log(nf / np.float32(max_exact)) / np.float32(math.log(_REL_MAX_DIST / max_exact))
                         * np.float32(nb - max_exact)).astype(np.int32)
    large = np.minimum(large, nb - 1)
    return side + np.where(n < max_exact, n, large)


def _bucket_blocks():
    assert _BIAS_BLOCK % _CHUNK == 0
    i = np.arange(_BIAS_BLOCK)[None, :]
    j = np.arange(_BIAS_BLOCK)[:, None]
    far_bucket = _REL_BUCKETS // 2 - 1
    assert (_t5_bucket_np(j - i - 2 * _BIAS_BLOCK) == far_bucket).all()
    diag = np.where((j // _CHUNK) <= (i // _CHUNK), _t5_bucket_np(j - i), -1)
    prev = _t5_bucket_np(j - i - _BIAS_BLOCK)
    return np.stack([diag, prev]).astype(np.int32)


def _bias_prep_body(tab_ref, tabv_ref, bucket_ref, lam4_ref, gains_ref,
                    bias_ref, lam_ref, flag_ref, *, lam_init, t):
    h = pl.program_id(0)
    far_bucket = _REL_BUCKETS // 2 - 1
    bkt = bucket_ref[...]
    far = tab_ref[far_bucket, h]
    val = jnp.full(bkt.shape, -jnp.inf, _F32)
    for b in range(_REL_BUCKETS):
        val = jnp.where(bkt == b, (tab_ref[b, h] - far) * _LOG2E, val)
    diag_block, prev_block = val[0], val[1]
    n_blk = t // _BIAS_BLOCK
    bias_ref[0, 0] = jnp.zeros((t, t), _F32)
    bias_ref[0, 1] = jnp.zeros((t, t), _F32)
    bias_ref[0, 1, t - _BIAS_BLOCK:, :_BIAS_BLOCK] = prev_block
    for bj in range(n_blk):
        rows = slice(bj * _BIAS_BLOCK, (bj + 1) * _BIAS_BLOCK)
        for bi in range(n_blk):
            cols = slice(bi * _BIAS_BLOCK, (bi + 1) * _BIAS_BLOCK)
            if bi == bj:
                block = diag_block
            elif bi == bj + 1:
                block = prev_block
            else:
                block = jnp.full((_BIAS_BLOCK, _BIAS_BLOCK), 0.0 if bi > bj else -jnp.inf, _F32)
            bias_ref[0, 2, rows, cols] = block
    l4 = lam4_ref[...]
    s1 = jnp.sum(l4[0:1] * l4[1:2], axis=-1, keepdims=True)
    s2 = jnp.sum(l4[2:3] * l4[3:4], axis=-1, keepdims=True)
    lam_ref[...] = jnp.broadcast_to(jnp.exp(s1) - jnp.exp(s2) + lam_init, lam_ref.shape)
    gmax = jnp.max(jnp.abs(gains_ref[...]), axis=-1, keepdims=True)
    qk_bound = (_SUB_DIM * 1.02) * gmax[0:1] * gmax[1:2]
    tv = tabv_ref[...]
    dev = jnp.abs(tv - tv[far_bucket:far_bucket + 1]) * _LOG2E
    bias_bound = jnp.max(jnp.max(dev, axis=-1, keepdims=True), axis=0, keepdims=True)
    flag_ref[...] = (qk_bound + bias_bound <= _SAFE_LOG2_RANGE).astype(jnp.int32)


def _bias_prep(rel_table, lam4, gains, lam_init, t):
    assert t % _BIAS_BLOCK == 0
    buckets = jnp.asarray(_bucket_blocks())
    whole = lambda a: pl.BlockSpec(a.shape, lambda h: (0,) * a.ndim)
    return pl.pallas_call(
        functools.partial(_bias_prep_body, lam_init=lam_init, t=t),
        grid=(_HEADS,),
        in_specs=[pl.BlockSpec(memory_space=pltpu.SMEM), whole(rel_table), whole(buckets),
                  whole(lam4), whole(gains)],
        out_specs=[pl.BlockSpec((1, 3, t, t), lambda h: (h, 0, 0, 0)),
                   pl.BlockSpec((1, _V_DIM), lambda h: (0, 0)),
                   pl.BlockSpec((1, 1), lambda h: (0, 0))],
        out_shape=[jax.ShapeDtypeStruct((_HEADS, 3, t, t), _F32),
                   jax.ShapeDtypeStruct((1, _V_DIM), _F32),
                   jax.ShapeDtypeStruct((1, 1), jnp.int32)],
        compiler_params=pltpu.CompilerParams(dimension_semantics=("arbitrary",)),
        name="attn_bias_prep",
    )(rel_table, rel_table, buckets, lam4, gains)


def _attn_body(flag_ref, qta_ref, qtb_ref, k_ref, vt_ref, bias_ref, lam_ref, gain_ref, o_ref,
               qw_ref, m_ref, acc_ref, den_ref, *, t, nq):
    p = pl.program_id(2)
    gran = t // _KEY_GRANULE
    for side, qt_ref in enumerate((qta_ref, qtb_ref)):
        qt = qt_ref[...]
        row = lax.broadcasted_iota(jnp.int32, qt.shape, 0)
        zero = jnp.zeros_like(qt)
        qw_ref[side, :, :t] = jnp.where(row < _SUB_DIM, qt, zero)
        qw_ref[side, :, t:] = jnp.where(row >= _SUB_DIM, qt, zero)
    acc_ref[...] = jnp.zeros(acc_ref.shape, _F32)
    den_ref[...] = jnp.zeros(den_ref.shape, _F32)

    def key_sum(e):
        return jnp.sum(e.reshape(e.shape[0] // 8, 8, e.shape[1]), axis=0)

    def schedule(step_index):
        mirrored = step_index > p
        side = mirrored.astype(jnp.int32)
        kj = jnp.where(mirrored, step_index - p - 1, p - step_index)
        q_tile = jnp.where(mirrored, nq - 1 - p, p)
        bias_index = jnp.where(kj == q_tile, 2, jnp.where(kj == q_tile - 1, 1, 0))
        return side, kj, bias_index

    def apply_to_values(kj, pt):
        pv = jnp.dot(vt_ref[kj * gran, 0], pt[:_KEY_GRANULE], preferred_element_type=_F32)
        for g in range(1, gran):
            pv = pv + jnp.dot(vt_ref[kj * gran + g, 0],
                              pt[g * _KEY_GRANULE:(g + 1) * _KEY_GRANULE],
                              preferred_element_type=_F32)
        return pv

    def key_tile(kj):
        return k_ref[0, pl.ds(pl.multiple_of(kj * t, t), t), :]

    def bounded_step(step_index):
        side, kj, bias_index = schedule(step_index)
        st = jnp.dot(key_tile(kj), qw_ref[side], preferred_element_type=_F32)
        bias = bias_ref[0, bias_index]
        e = [jnp.exp2(st[:, :t] + bias), jnp.exp2(st[:, t:] + bias)]
        den_ref[side] += jnp.concatenate([key_sum(e[0]), key_sum(e[1])], axis=1)
        pt = jnp.concatenate([e[0].astype(_BF16), e[1].astype(_BF16)], axis=1)
        acc_ref[side] += apply_to_values(kj, pt)

    def bounded_diagonal_step(side, q_tile):
        assert gran == 2
        half = _KEY_GRANULE
        late = [slice(half, t), slice(t + half, 2 * t)]
        k0 = pl.multiple_of(q_tile * t, t)
        bias = bias_ref[0, 2]
        st0 = jnp.dot(k_ref[0, pl.ds(k0, half), :], qw_ref[side], preferred_element_type=_F32)
        e0 = [jnp.exp2(st0[:, :t] + bias[:half]), jnp.exp2(st0[:, t:] + bias[:half])]
        den_ref[side] += jnp.concatenate([key_sum(e0[0]), key_sum(e0[1])], axis=1)
        pt0 = jnp.concatenate([e0[0].astype(_BF16), e0[1].astype(_BF16)], axis=1)
        qw_late = jnp.concatenate([qw_ref[side, :, lanes] for lanes in late], axis=1)
        st1 = jnp.dot(k_ref[0, pl.ds(k0 + half, half), :], qw_late, preferred_element_type=_F32)
        e1 = [jnp.exp2(st1[:, :half] + bias[half:, half:]), jnp.exp2(st1[:, half:] + bias[half:, half:])]
        pt1 = jnp.concatenate([e1[0].astype(_BF16), e1[1].astype(_BF16)], axis=1)
        acc_ref[side] += jnp.dot(vt_ref[q_tile * gran, 0], pt0, preferred_element_type=_F32)
        pv1 = jnp.dot(vt_ref[q_tile * gran + 1, 0], pt1, preferred_element_type=_F32)
        for n, lanes in enumerate(late):
            acc_ref[side, :, lanes] += pv1[:, n * half:(n + 1) * half]
            den_ref[side, :, lanes] += key_sum(e1[n])

    def online_step(step_index):
        side, kj, bias_index = schedule(step_index)
        kt = key_tile(kj)
        bias = bias_ref[0, bias_index]
        for s in range(2):
            lanes = slice(s * t, (s + 1) * t)
            st = jnp.dot(kt, qw_ref[side, :, lanes], preferred_element_type=_F32) + bias
            m_old = m_ref[side, s]
            m_new = jnp.maximum(m_old, jnp.max(st, axis=0, keepdims=True))
            alpha = jnp.exp2(m_old - m_new)
            e = jnp.exp2(st - m_new)
            den_ref[side, :, lanes] = alpha * den_ref[side, :, lanes] + key_sum(e)
            acc_ref[side, :, lanes] = alpha * acc_ref[side, :, lanes] + apply_to_values(
                kj, e.astype(_BF16))
            m_ref[side, s] = m_new

    bounded = flag_ref[0, 0] > 0

    @pl.when(bounded)
    def _():
        bounded_diagonal_step(0, p)
        for step_index in range(1, nq):
            bounded_step(step_index)
        bounded_diagonal_step(1, nq - 1 - p)

    @pl.when(jnp.logical_not(bounded))
    def _():
        m_ref[...] = jnp.full(m_ref.shape, -jnp.inf, _F32)
        for step_index in range(nq + 1):
            online_step(step_index)

    for side in range(2):
        o = acc_ref[side] / jnp.sum(den_ref[side], axis=0, keepdims=True)
        od = o[:, :t] - lam_ref[0:1, 0:1] * o[:, t:]
        scale = lax.rsqrt(jnp.mean(od * od, axis=0, keepdims=True) + _EPS)
        o_ref[side, 0, 0] = (od * scale * gain_ref[...]).T.astype(_BF16)


def _attention(flag, qt, k, vt, bias, lam, gain_t):
    b, l, w = k.shape
    t = _ATTN_TILE
    nq = l // t
    vrows = vt.shape[2]
    return pl.pallas_call(
        functools.partial(_attn_body, t=t, nq=nq),
        grid=(b, _HEADS, nq // 2),
        in_specs=[pl.BlockSpec(memory_space=pltpu.SMEM),
                  pl.BlockSpec((_V_DIM, t), lambda bi, h, p: (h, bi * nq + p)),
                  pl.BlockSpec((_V_DIM, t), lambda bi, h, p: (h, bi * nq + nq - 1 - p)),
                  pl.BlockSpec((1, l, _V_DIM), lambda bi, h, p: (bi, 0, h)),
                  pl.BlockSpec((l // _KEY_GRANULE, 1, vrows, _KEY_GRANULE),
                               lambda bi, h, p: (bi, h, 0, 0)),
                  pl.BlockSpec((1, 3, t, t), lambda bi, h, p: (h, 0, 0, 0)),
                  pl.BlockSpec(lam.shape, lambda bi, h, p: (0, 0)),
                  pl.BlockSpec(gain_t.shape, lambda bi, h, p: (0, 0))],
        out_specs=pl.BlockSpec((2, 1, 1, t, _V_DIM), lambda bi, h, p: (0, bi, p, 0, h)),
        out_shape=jax.ShapeDtypeStruct((2, b, nq // 2, t, w), _BF16),
        scratch_shapes=[pltpu.VMEM((2, _V_DIM, 2 * t), _BF16),
                        pltpu.VMEM((2, 2, 1, t), _F32),
                        pltpu.VMEM((2, vrows, 2 * t), _F32),
                        pltpu.VMEM((2, 8, 2 * t), _F32)],
        compiler_params=pltpu.CompilerParams(
            dimension_semantics=("parallel", "parallel", "arbitrary"),
            vmem_limit_bytes=_VMEM_LIMIT),
        name="diff_attention",
    )(flag, qt, qt, k, vt, bias, lam, gain_t)


def _attn_bounded_body(qt_ref, k_ref, vt_ref, bias_ref, lam_ref, gain_ref, o_ref,
                       qw_ref, acc_ref, den_ref, *, t, nq):
    gran = t // _KEY_GRANULE
    assert gran == 2
    half = _KEY_GRANULE
    row = lax.broadcasted_iota(jnp.int32, (_V_DIM, t), 0)
    for q in range(nq):
        qt = qt_ref[:, q * t:(q + 1) * t]
        zero = jnp.zeros_like(qt)
        qw_ref[q, :, :t] = jnp.where(row < _SUB_DIM, qt, zero)
        qw_ref[q, :, t:] = jnp.where(row >= _SUB_DIM, qt, zero)
    acc_ref[...] = jnp.zeros(acc_ref.shape, _F32)
    den_ref[...] = jnp.zeros(den_ref.shape, _F32)

    def key_sum(e):
        return jnp.sum(e.reshape(e.shape[0] // 8, 8, e.shape[1]), axis=0)

    def diagonal(kj):
        late = [slice(half, t), slice(t + half, 2 * t)]
        k0 = kj * t
        bias = bias_ref[0, 2]
        st0 = jnp.dot(k_ref[0, k0:k0 + half, :], qw_ref[kj], preferred_element_type=_F32)
        e0 = [jnp.exp2(st0[:, :t] + bias[:half]), jnp.exp2(st0[:, t:] + bias[:half])]
        den_ref[kj] += jnp.concatenate([key_sum(e0[0]), key_sum(e0[1])], axis=1)
        pt0 = jnp.concatenate([e0[0].astype(_BF16), e0[1].astype(_BF16)], axis=1)
        qw_late = jnp.concatenate([qw_ref[kj, :, lanes] for lanes in late], axis=1)
        st1 = jnp.dot(k_ref[0, k0 + half:k0 + t, :], qw_late, preferred_element_type=_F32)
        e1 = [jnp.exp2(st1[:, :half] + bias[half:, half:]), jnp.exp2(st1[:, half:] + bias[half:, half:])]
        pt1 = jnp.concatenate([e1[0].astype(_BF16), e1[1].astype(_BF16)], axis=1)
        acc_ref[kj] += jnp.dot(vt_ref[kj * gran, 0], pt0, preferred_element_type=_F32)
        pv1 = jnp.dot(vt_ref[kj * gran + 1, 0], pt1, preferred_element_type=_F32)
        for n, lanes in enumerate(late):
            acc_ref[kj, :, lanes] += pv1[:, n * half:(n + 1) * half]
            den_ref[kj, :, lanes] += key_sum(e1[n])

    def later_queries(kj, q_tiles):
        qw = jnp.concatenate([qw_ref[q] for q in q_tiles], axis=1)
        st = jnp.dot(k_ref[0, kj * t:(kj + 1) * t, :], qw, preferred_element_type=_F32)
        if q_tiles[0] == kj + 1:
            blk = _BIAS_BLOCK
            corner = bias_ref[0, 1, t - blk:, :blk]
            tail = st[t - blk:]
            tail = jnp.concatenate([tail[:, :blk] + corner, tail[:, blk:t],
                                    tail[:, t:t + blk] + corner, tail[:, t + blk:]], axis=1)
            st = jnp.concatenate([st[:t - blk], tail], axis=0)
        e = jnp.exp2(st)
        pv = (jnp.dot(vt_ref[kj * gran, 0], e[:half].astype(_BF16), preferred_element_type=_F32)
              + jnp.dot(vt_ref[kj * gran + 1, 0], e[half:].astype(_BF16), preferred_element_type=_F32))
        for n, q in enumerate(q_tiles):
            lanes = slice(n * 2 * t, (n + 1) * 2 * t)
            den_ref[q] += key_sum(e[:, lanes])
            acc_ref[q] += pv[:, lanes]

    for kj in range(nq):
        diagonal(kj)
        later = list(range(kj + 1, nq))
        for g0 in range(0, len(later), _QUERY_GROUP):
            later_queries(kj, later[g0:g0 + _QUERY_GROUP])

    for q in range(nq):
        o = acc_ref[q] / jnp.sum(den_ref[q], axis=0, keepdims=True)
        od = o[:, :t] - lam_ref[0:1, 0:1] * o[:, t:]
        scale = lax.rsqrt(jnp.mean(od * od, axis=0, keepdims=True) + _EPS)
        o_ref[0, q * t:(q + 1) * t, :] = (od * scale * gain_ref[...]).T.astype(_BF16)


def _attention_bounded(qt, k, vt, bias, lam, gain_t):
    b, l, w = k.shape
    t = _ATTN_TILE
    nq = l // t
    return pl.pallas_call(
        functools.partial(_attn_bounded_body, t=t, nq=nq),
        grid=(b, _HEADS),
        in_specs=[pl.BlockSpec((_V_DIM, l), lambda bi, h: (h, bi)),
                  pl.BlockSpec((1, l, _V_DIM), lambda bi, h: (bi, 0, h)),
                  pl.BlockSpec((l // _KEY_GRANULE, 1, _V_DIM, _KEY_GRANULE), lambda bi, h: (bi, h, 0, 0)),
                  pl.BlockSpec((1, 3, t, t), lambda bi, h: (h, 0, 0, 0)),
                  pl.BlockSpec(lam.shape, lambda bi, h: (0, 0)),
                  pl.BlockSpec(gain_t.shape, lambda bi, h: (0, 0))],
        out_specs=pl.BlockSpec((1, l, _V_DIM), lambda bi, h: (bi, 0, h)),
        out_shape=jax.ShapeDtypeStruct((b, l, w), _BF16),
        scratch_shapes=[pltpu.VMEM((nq, _V_DIM, 2 * t), _BF16),
                        pltpu.VMEM((nq, _V_DIM, 2 * t), _F32),
                        pltpu.VMEM((nq, 8, 2 * t), _F32)],
        compiler_params=pltpu.CompilerParams(
            dimension_semantics=("parallel", "parallel"), vmem_limit_bytes=_VMEM_LIMIT),
        name="diff_attention_bounded",
    )(qt, k, vt, bias, lam, gain_t)


def _s5_prep_body(are_ref, aim_ref, ldt_ref, bre_ref, bim_ref,
                  lbre_ref, lbim_ref, bbre_ref, bbim_ref):
    a_re = are_ref[...]
    a_im = aim_ref[...]
    dt = jnp.exp(ldt_ref[...])
    decay = jnp.exp(a_re * dt)
    lb_re = decay * jnp.cos(a_im * dt)
    lb_im = decay * jnp.sin(a_im * dt)
    nr = lb_re - 1.0
    ni = lb_im
    den = a_re * a_re + a_im * a_im
    q_re = (nr * a_re + ni * a_im) / den
    q_im = (ni * a_re - nr * a_im) / den
    b_re = bre_ref[...]
    b_im = bim_ref[...]
    bbre_ref[...] = q_re * b_re - q_im * b_im
    bbim_ref[...] = q_re * b_im + q_im * b_re
    lbre_ref[...] = lb_re
    lbim_ref[...] = lb_im


def _s5_prep(a_re, a_im, log_dt, b_re, b_im):
    sd = jax.ShapeDtypeStruct(a_re.shape, _F32)
    return pl.pallas_call(_s5_prep_body, out_shape=[sd] * 4, name="s5_discretise")(
        a_re, a_im, log_dt, b_re, b_im)


def _s5_body(u_ref, wbre_ref, wbim_ref, lbre_ref, lbim_ref, wcre_ref, wcim_ref, d_ref,
             gw_ref, gb_ref, o_ref, xre_ref, xim_ref, cre_ref, cim_ref, *, steps, nb):
    rows = nb * steps
    width = u_ref.shape[-1]
    n_tiles, _, tile_w = lbre_ref.shape
    halves = wcre_ref.shape[0]
    tiles_per_half = n_tiles // halves
    hw_in = width // halves

    @pl.when(pl.program_id(0) == 0)
    def _():
        cre_ref[...] = jnp.zeros(cre_ref.shape, _F32)
        cim_ref[...] = jnp.zeros(cim_ref.shape, _F32)

    u_tm = jnp.swapaxes(u_ref[...].astype(_F32), 0, 1).reshape(rows, width).astype(_BF16)

    for hf in range(halves):
        uh = u_tm[:, hf * hw_in:(hf + 1) * hw_in]
        bu_r = jnp.dot(uh, wbre_ref[hf], preferred_element_type=_F32)
        bu_i = jnp.dot(uh, wbim_ref[hf], preferred_element_type=_F32)
        group = range(hf * tiles_per_half, (hf + 1) * tiles_per_half)
        a_r = [jnp.broadcast_to(lbre_ref[j], (nb, tile_w)) for j in group]
        a_i = [jnp.broadcast_to(lbim_ref[j], (nb, tile_w)) for j in group]
        x_r = [cre_ref[j] for j in group]
        x_i = [cim_ref[j] for j in group]
        for ti in range(steps):
            r = slice(ti * nb, (ti + 1) * nb)
            for q, j in enumerate(group):
                lanes = slice(q * tile_w, (q + 1) * tile_w)
                n_r = a_r[q] * x_r[q] - a_i[q] * x_i[q] + bu_r[r, lanes]
                n_i = a_r[q] * x_i[q] + a_i[q] * x_r[q] + bu_i[r, lanes]
                xre_ref[j, r, :] = n_r
                xim_ref[j, r, :] = n_i
                x_r[q], x_i[q] = n_r, n_i
        for q, j in enumerate(group):
            cre_ref[j] = x_r[q]
            cim_ref[j] = x_i[q]

    ys = []
    for hf in range(halves):
        tiles = range(hf * tiles_per_half, (hf + 1) * tiles_per_half)
        x_r = jnp.concatenate([xre_ref[j] for j in tiles], axis=1).astype(_BF16)
        x_i = jnp.concatenate([xim_ref[j] for j in tiles], axis=1).astype(_BF16)
        ys.append(jnp.dot(x_r, wcre_ref[hf], preferred_element_type=_F32)
                  + jnp.dot(x_i, wcim_ref[hf], preferred_element_type=_F32))
    y = jnp.concatenate(ys, axis=-1) + d_ref[...] * u_tm.astype(_F32)
    g = jax.nn.gelu(y)
    o = g * jax.nn.sigmoid(jnp.dot(g.astype(_BF16), gw_ref[...], preferred_element_type=_F32)
                           + gb_ref[...])
    o_ref[...] = jnp.swapaxes(o.reshape(steps, nb, width), 0, 1).astype(_BF16)


def _s5(u, wb_re, wb_im, lb_re, lb_im, wc_re, wc_im, d_vec, glu_w, glu_b):
    nb, l, width = u.shape
    steps = _SCAN_STEPS
    n_tiles, _, tile_w = lb_re.shape
    full = lambda a: pl.BlockSpec(a.shape, lambda i: (0,) * a.ndim)
    io_spec = pl.BlockSpec((nb, steps, width), lambda i: (0, i, 0))
    consts = (wb_re, wb_im, lb_re, lb_im, wc_re, wc_im, d_vec, glu_w, glu_b)
    state_buf = pltpu.VMEM((n_tiles, nb * steps, tile_w), _F32)
    return pl.pallas_call(
        functools.partial(_s5_body, steps=steps, nb=nb),
        grid=(l // steps,),
        in_specs=[io_spec] + [full(a) for a in consts],
        out_specs=io_spec,
        out_shape=jax.ShapeDtypeStruct(u.shape, _BF16),
        scratch_shapes=[state_buf, state_buf,
                        pltpu.VMEM((n_tiles, nb, tile_w), _F32),
                        pltpu.VMEM((n_tiles, nb, tile_w), _F32)],
        compiler_params=pltpu.CompilerParams(
            dimension_semantics=("arbitrary",), vmem_limit_bytes=_VMEM_LIMIT),
        name="s5_branch",
    )(u, *consts)


def _outproj_body(x_ref, ng_ref, wg_ref, mb_ref, oa_ref, os_ref, pa_ref, ps_ref, wo_ref,
                  out_ref, *, width):
    x = x_ref[...]
    d = x.shape[-1]
    h = (x * _rms_scale(x) * ng_ref[...]).astype(_BF16)
    zg = jnp.dot(h, wg_ref[...], preferred_element_type=_F32)
    o_a = (oa_ref[...].astype(_F32) * jax.nn.silu(zg[:, :width])).astype(_BF16)
    o_s = (os_ref[...].astype(_F32) * jax.nn.silu(zg[:, width:2 * width])).astype(_BF16)
    p_a = jnp.dot(o_a, pa_ref[...], preferred_element_type=_F32)
    p_s = jnp.dot(o_s, ps_ref[...], preferred_element_type=_F32)
    g = jax.nn.sigmoid(zg[:, 2 * width:] + mb_ref[...])
    merged = g[:, :d] * p_a + g[:, d:] * p_s
    out_ref[...] = x + jnp.dot(merged.astype(_BF16), wo_ref[...], preferred_element_type=_F32)


def _outproj(x2, norm_gain, w_gates, merge_b, o_a, o_s, proj_a, proj_s, w_out, width):
    n, d = x2.shape
    tm = _TOKEN_TILE
    full = lambda a: pl.BlockSpec(a.shape, lambda i: (0,) * a.ndim)
    row = lambda a: pl.BlockSpec((tm, a.shape[-1]), lambda i: (i, 0))
    return pl.pallas_call(
        functools.partial(_outproj_body, width=width),
        grid=(n // tm,),
        in_specs=[row(x2), full(norm_gain), full(w_gates), full(merge_b), row(o_a), row(o_s),
                  full(proj_a), full(proj_s), full(w_out)],
        out_specs=row(x2),
        out_shape=jax.ShapeDtypeStruct(x2.shape, x2.dtype),
        compiler_params=pltpu.CompilerParams(
            dimension_semantics=("parallel",), vmem_limit_bytes=_VMEM_LIMIT),
        name="outproj",
    )(x2, norm_gain, w_gates, merge_b, o_a, o_s, proj_a, proj_s, w_out)


def _block_diag_halves(blocks, halves):
    g, r, c = blocks.shape
    gh = g // halves
    on_diagonal = (np.arange(gh * r)[:, None] // r) == (np.arange(gh * c)[None, :] // c)
    tiled = jnp.tile(blocks.reshape(halves, gh * r, c), (1, 1, gh))
    return jnp.where(on_diagonal, tiled, 0)


def _layer(x, lam_init, norm_gain, w_in, merge_gate_b, q_norm_gain, k_norm_gain,
           lambda_q1, lambda_k1, lambda_q2, lambda_k2, diff_subln_gain, rel_bias_table,
           ssm_a_re, ssm_a_im, ssm_log_dt, ssm_b_re, ssm_b_im, ssm_c_re, ssm_c_im,
           ssm_d, ssm_glu_w, ssm_glu_b, proj_attn, proj_ssm, w_out):
    nb, l, d = x.shape
    n = nb * l
    aw = _HEADS * 2 * _SUB_DIM
    groups = ssm_a_re.shape[0]
    x2 = x.reshape(n, d)
    ng = norm_gain.reshape(1, d).astype(_F32)

    w_ku = jnp.concatenate([w_in[:, aw:2 * aw], w_in[:, 4 * aw:5 * aw]], axis=1).astype(_BF16)
    w_qvt = jnp.concatenate([w_in[:, :aw], w_in[:, 2 * aw:3 * aw]], axis=1).T.astype(_BF16)
    w_gates = jnp.concatenate([w_in[:, 3 * aw:4 * aw], w_in[:, 5 * aw:]], axis=1).astype(_BF16)
    gq = jnp.tile(q_norm_gain.astype(_F32), 2 * _HEADS) * (_SUB_DIM ** -0.5 * _LOG2E)
    gqt = jnp.broadcast_to(gq[:, None], (aw, _TOKEN_TILE))
    gk = jnp.tile(k_norm_gain.astype(_F32), 2 * _HEADS).reshape(1, aw)
    seg = np.arange(_MXU_TILE) // _SUB_DIM
    gsum = jnp.asarray((seg[:, None] == seg[None, :]).astype(np.float32), _BF16)
    qt, k, vt, u = _inproj(x2, ng, w_ku, w_qvt, gqt, gk, gsum, aw)

    lam4 = jnp.stack([lambda_q1, lambda_k1, lambda_q2, lambda_k2]).astype(_F32)
    qk_gains = jnp.stack([gq[:_SUB_DIM], gk[0, :_SUB_DIM]])
    bias, lam, flag = _bias_prep(rel_bias_table.astype(_F32), lam4, qk_gains, lam_init, _ATTN_TILE)
    subln = diff_subln_gain.astype(_F32) * (1.0 - lam_init)
    subln_t = jnp.broadcast_to(subln[:, None], (_V_DIM, _ATTN_TILE))
    shp = (nb, l, aw)
    k3 = k.reshape(shp)

    def online_attention():
        o = _attention(flag, qt, k3, vt, bias, lam, subln_t)
        return jnp.concatenate([o[0], o[1][:, ::-1]], axis=1).reshape(shp)

    o_a = lax.cond(flag[0, 0] > 0,
                   lambda: _attention_bounded(qt, k3, vt, bias, lam, subln_t),
                   online_attention)

    rep = lambda a: jnp.repeat(a.astype(_F32), _SSM_GROUP, axis=0)
    ldt = jnp.broadcast_to(ssm_log_dt.astype(_F32)[:, None], ssm_a_re.shape)
    bt = lambda a: a.astype(_F32).transpose(0, 2, 1).reshape(groups * _SSM_GROUP, _SSM_STATE)
    lb_re, lb_im, bb_re, bb_im = _s5_prep(rep(ssm_a_re), rep(ssm_a_im), rep(ldt),
                                          bt(ssm_b_re), bt(ssm_b_im))
    n_tiles = groups * _SSM_STATE // _STATE_TILE
    flat = lambda a: a[::_SSM_GROUP].reshape(n_tiles, 1, _STATE_TILE)
    gshape = (groups, _SSM_GROUP, _SSM_STATE)
    wb_re = _block_diag_halves(bb_re.reshape(gshape), 2).astype(_BF16)
    wb_im = _block_diag_halves(bb_im.reshape(gshape), 2).astype(_BF16)
    wc_re = _block_diag_halves(ssm_c_re.astype(_F32).transpose(0, 2, 1), 2).astype(_BF16)
    wc_im = _block_diag_halves(-ssm_c_im.astype(_F32).transpose(0, 2, 1), 2).astype(_BF16)
    o_s = _s5(u.reshape(shp), wb_re, wb_im, flat(lb_re), flat(lb_im), wc_re, wc_im,
              ssm_d.astype(_F32).reshape(1, aw), ssm_glu_w.astype(_BF16),
              ssm_glu_b.astype(_F32).reshape(1, aw))

    out = _outproj(x2, ng, w_gates, merge_gate_b.astype(_F32).reshape(1, 2 * d),
                   o_a.reshape(n, aw), o_s.reshape(n, aw),
                   proj_attn.astype(_BF16), proj_ssm.astype(_BF16), w_out.astype(_BF16), aw)
    return out.reshape(nb, l, d)


def kernel(x, norm_gain, w_in, merge_gate_b, q_norm_gain, k_norm_gain, lambda_q1, lambda_k1,
           lambda_q2, lambda_k2, diff_subln_gain, rel_bias_table, ssm_A_re, ssm_A_im, ssm_log_dt,
           ssm_B_re, ssm_B_im, ssm_C_re, ssm_C_im, ssm_D, ssm_glu_w, ssm_glu_b,
           proj_attn, proj_ssm, w_out):
    per_layer = (norm_gain, w_in, merge_gate_b, q_norm_gain, k_norm_gain, lambda_q1, lambda_k1,
                 lambda_q2, lambda_k2, diff_subln_gain)
    per_layer_tail = (ssm_A_re, ssm_A_im, ssm_log_dt, ssm_B_re, ssm_B_im, ssm_C_re, ssm_C_im,
                      ssm_D, ssm_glu_w, ssm_glu_b, proj_attn, proj_ssm, w_out)
    for layer in range(norm_gain.shape[0]):
        lam_init = 0.8 - 0.6 * math.exp(-0.3 * layer)
        x = _layer(x, lam_init, *(p[layer] for p in per_layer), rel_bias_table,
                   *(p[layer] for p in per_layer_tail))
    return x
```

```python
import functools
import math

import jax
import jax.numpy as jnp
import numpy as np
from jax import lax
from jax.experimental import pallas as pl
from jax.experimental.pallas import tpu as pltpu

_F32 = jnp.float32
_BF16 = jnp.bfloat16

_CHUNK = 64
_HEADS = 4
_SUB_DIM = 64
_V_DIM = 128
_SSM_GROUP = 16
_SSM_STATE = 64
_REL_BUCKETS = 32
_REL_MAX_DIST = 128
_EPS = 1e-6
_LOG2E = math.log2(math.e)
_SAFE_LOG2_RANGE = 100.0

_VMEM_LIMIT = 48 * 1024 * 1024
_MXU_TILE = 256
_TOKEN_TILE = 1024
_ATTN_TILE = 512
_KEY_GRANULE = 256
_BIAS_BLOCK = 128
_QUERY_GROUP = 2
_SCAN_STEPS = 128
_STATE_TILE = 256


def _rms_scale(x, eps=_EPS):
    return lax.rsqrt(jnp.mean(x * x, axis=-1, keepdims=True) + eps)


def _inproj_body(x_ref, ng_ref, wku_ref, wqvt_ref, gqt_ref, gk_ref, gsum_ref,
                 qt_ref, k_ref, vt_ref, u_ref, *, width):
    x = x_ref[...]
    tm = x.shape[0]
    h = (x * _rms_scale(x) * ng_ref[...]).astype(_BF16)
    z = jnp.dot(h, wku_ref[...], preferred_element_type=_F32)
    zt = lax.dot_general(wqvt_ref[...], h, (((1,), (1,)), ((), ())),
                         preferred_element_type=_F32)

    kk = z[:, :width]
    k2 = (kk * kk).astype(_BF16)
    mxu = gsum_ref.shape[0]
    ss = jnp.concatenate([jnp.dot(k2[:, c:c + mxu], gsum_ref[...], preferred_element_type=_F32)
                          for c in range(0, width, mxu)], axis=1)
    k_ref[...] = (kk * lax.rsqrt(ss * (1.0 / _SUB_DIM) + _EPS) * gk_ref[...]).astype(_BF16)
    u_ref[...] = z[:, width:].astype(_BF16)

    qt = zt[:width].reshape(width // _SUB_DIM, _SUB_DIM, tm)
    ms = jnp.mean(qt * qt, axis=1, keepdims=True)
    qt_ref[...] = ((qt * lax.rsqrt(ms + _EPS)).reshape(width, tm) * gqt_ref[...]).astype(_BF16)

    vt = zt[width:].astype(_BF16)
    for j in range(tm // _KEY_GRANULE):
        for hd in range(_HEADS):
            vt_ref[j, hd] = vt[hd * _V_DIM:(hd + 1) * _V_DIM,
                               j * _KEY_GRANULE:(j + 1) * _KEY_GRANULE]


def _inproj(x2, norm_gain, w_ku, w_qvt, gqt, gk, gsum, width):
    n, d = x2.shape
    tm = _TOKEN_TILE
    full = lambda a: pl.BlockSpec(a.shape, lambda i: (0,) * a.ndim)
    row_sd = jax.ShapeDtypeStruct((n, width), _BF16)
    row_spec = pl.BlockSpec((tm, width), lambda i: (i, 0))
    gran = tm // _KEY_GRANULE
    vrows = _V_DIM
    return pl.pallas_call(
        functools.partial(_inproj_body, width=width),
        grid=(n // tm,),
        in_specs=[pl.BlockSpec((tm, d), lambda i: (i, 0)), full(norm_gain), full(w_ku), full(w_qvt),
                  full(gqt), full(gk), full(gsum)],
        out_specs=[pl.BlockSpec((width, tm), lambda i: (0, i)), row_spec,
                   pl.BlockSpec((gran, _HEADS, vrows, _KEY_GRANULE), lambda i: (i, 0, 0, 0)), row_spec],
        out_shape=[jax.ShapeDtypeStruct((width, n), _BF16), row_sd,
                   jax.ShapeDtypeStruct((n // _KEY_GRANULE, _HEADS, vrows, _KEY_GRANULE), _BF16), row_sd],
        compiler_params=pltpu.CompilerParams(
            dimension_semantics=("parallel",), vmem_limit_bytes=_VMEM_LIMIT),
        name="inproj",
    )(x2, norm_gain, w_ku, w_qvt, gqt, gk, gsum)


def _t5_bucket_np(rel):
    nb = _REL_BUCKETS // 2
    max_exact = nb // 2
    side = np.where(rel > 0, nb, 0)
    n = np.abs(rel)
    nf = np.maximum(n, 1).astype(np.float32)
    large = max_exact + (np.log(nf / np.float32(max_exact)) / np.float32(math.log(_REL_MAX_DIST / max_exact))
                         * np.float32(nb - max_exact)).astype(np.int32)
    large = np.minimum(large, nb - 1)
    return side + np.where(n < max_exact, n, large)


def _bucket_blocks():
    assert _BIAS_BLOCK % _CHUNK == 0
    i = np.arange(_BIAS_BLOCK)[None, :]
    j = np.arange(_BIAS_BLOCK)[:, None]
    far_bucket = _REL_BUCKETS // 2 - 1
    assert (_t5_bucket_np(j - i - 2 * _BIAS_BLOCK) == far_bucket).all()
    diag = np.where((j // _CHUNK) <= (i // _CHUNK), _t5_bucket_np(j - i), -1)
    prev = _t5_bucket_np(j - i - _BIAS_BLOCK)
    return np.stack([diag, prev]).astype(np.int32)


def _bias_prep_body(tab_ref, tabv_ref, bucket_ref, lam4_ref, gains_ref,
                    bias_ref, lam_ref, flag_ref, *, lam_init, t):
    h = pl.program_id(0)
    far_bucket = _REL_BUCKETS // 2 - 1
    bkt = bucket_ref[...]
    far = tab_ref[far_bucket, h]
    val = jnp.full(bkt.shape, -jnp.inf, _F32)
    for b in range(_REL_BUCKETS):
        val = jnp.where(bkt == b, (tab_ref[b, h] - far) * _LOG2E, val)
    diag_block, prev_block = val[0], val[1]
    n_blk = t // _BIAS_BLOCK
    bias_ref[0, 0] = jnp.zeros((t, t), _F32)
    bias_ref[0, 1] = jnp.zeros((t, t), _F32)
    bias_ref[0, 1, t - _BIAS_BLOCK:, :_BIAS_BLOCK] = prev_block
    for bj in range(n_blk):
        rows = slice(bj * _BIAS_BLOCK, (bj + 1) * _BIAS_BLOCK)
        for bi in range(n_blk):
            cols = slice(bi * _BIAS_BLOCK, (bi + 1) * _BIAS_BLOCK)
            if bi == bj:
                block = diag_block
            elif bi == bj + 1:
                block = prev_block
            else:
                block = jnp.full((_BIAS_BLOCK, _BIAS_BLOCK), 0.0 if bi > bj else -jnp.inf, _F32)
            bias_ref[0, 2, rows, cols] = block
    l4 = lam4_ref[...]
    s1 = jnp.sum(l4[0:1] * l4[1:2], axis=-1, keepdims=True)
    s2 = jnp.sum(l4[2:3] * l4[3:4], axis=-1, keepdims=True)
    lam_ref[...] = jnp.broadcast_to(jnp.exp(s1) - jnp.exp(s2) + lam_init, lam_ref.shape)
    gmax = jnp.max(jnp.abs(gains_ref[...]), axis=-1, keepdims=True)
    qk_bound = (_SUB_DIM * 1.02) * gmax[0:1] * gmax[1:2]
    tv = tabv_ref[...]
    dev = jnp.abs(tv - tv[far_bucket:far_bucket + 1]) * _LOG2E
    bias_bound = jnp.max(jnp.max(dev, axis=-1, keepdims=True), axis=0, keepdims=True)
    flag_ref[...] = (qk_bound + bias_bound <= _SAFE_LOG2_RANGE).astype(jnp.int32)


def _bias_prep(rel_table, lam4, gains, lam_init, t):
    assert t % _BIAS_BLOCK == 0
    buckets = jnp.asarray(_bucket_blocks())
    whole = lambda a: pl.BlockSpec(a.shape, lambda h: (0,) * a.ndim)
    return pl.pallas_call(
        functools.partial(_bias_prep_body, lam_init=lam_init, t=t),
        grid=(_HEADS,),
        in_specs=[pl.BlockSpec(memory_space=pltpu.SMEM), whole(rel_table), whole(buckets),
                  whole(lam4), whole(gains)],
        out_specs=[pl.BlockSpec((1, 3, t, t), lambda h: (h, 0, 0, 0)),
                   pl.BlockSpec((1, _V_DIM), lambda h: (0, 0)),
                   pl.BlockSpec((1, 1), lambda h: (0, 0))],
        out_shape=[jax.ShapeDtypeStruct((_HEADS, 3, t, t), _F32),
                   jax.ShapeDtypeStruct((1, _V_DIM), _F32),
                   jax.ShapeDtypeStruct((1, 1), jnp.int32)],
        compiler_params=pltpu.CompilerParams(dimension_semantics=("arbitrary",)),
        name="attn_bias_prep",
    )(rel_table, rel_table, buckets, lam4, gains)


def _attn_body(flag_ref, qt_ref, k_ref, vt_ref, bias_ref, lam_ref, gain_ref, o_ref,
               qw_ref, acc_ref, den_ref, *, t, nq):
    gran = t // _KEY_GRANULE
    assert gran == 2
    half = _KEY_GRANULE
    row = lax.broadcasted_iota(jnp.int32, (_V_DIM, t), 0)
    for q in range(nq):
        qt = qt_ref[:, q * t:(q + 1) * t]
        zero = jnp.zeros_like(qt)
        qw_ref[q, :, :t] = jnp.where(row < _SUB_DIM, qt, zero)
        qw_ref[q, :, t:] = jnp.where(row >= _SUB_DIM, qt, zero)
    acc_ref[...] = jnp.zeros(acc_ref.shape, _F32)
    den_ref[...] = jnp.zeros(den_ref.shape, _F32)

    def key_sum(e):
        return jnp.sum(e.reshape(e.shape[0] // 8, 8, e.shape[1]), axis=0)

    def diagonal(kj):
        late = [slice(half, t), slice(t + half, 2 * t)]
        k0 = kj * t
        bias = bias_ref[0, 2]
        st0 = jnp.dot(k_ref[0, k0:k0 + half, :], qw_ref[kj], preferred_element_type=_F32)
        e0 = [jnp.exp2(st0[:, :t] + bias[:half]), jnp.exp2(st0[:, t:] + bias[:half])]
        den_ref[kj] += jnp.concatenate([key_sum(e0[0]), key_sum(e0[1])], axis=1)
        pt0 = jnp.concatenate([e0[0].astype(_BF16), e0[1].astype(_BF16)], axis=1)
        qw_late = jnp.concatenate([qw_ref[kj, :, lanes] for lanes in late], axis=1)
        st1 = jnp.dot(k_ref[0, k0 + half:k0 + t, :], qw_late, preferred_element_type=_F32)
        e1 = [jnp.exp2(st1[:, :half] + bias[half:, half:]), jnp.exp2(st1[:, half:] + bias[half:, half:])]
        pt1 = jnp.concatenate([e1[0].astype(_BF16), e1[1].astype(_BF16)], axis=1)
        acc_ref[kj] += jnp.dot(vt_ref[kj * gran, 0], pt0, preferred_element_type=_F32)
        pv1 = jnp.dot(vt_ref[kj * gran + 1, 0], pt1, preferred_element_type=_F32)
        for n, lanes in enumerate(late):
            acc_ref[kj, :, lanes] += pv1[:, n * half:(n + 1) * half]
            den_ref[kj, :, lanes] += key_sum(e1[n])

    def later_queries(kj, q_tiles):
        qw = jnp.concatenate([qw_ref[q] for q in q_tiles], axis=1)
        st = jnp.dot(k_ref[0, kj * t:(kj + 1) * t, :], qw, preferred_element_type=_F32)
        if q_tiles[0] == kj + 1:
            blk = _BIAS_BLOCK
            corner = bias_ref[0, 1, t - blk:, :blk]
            tail = st[t - blk:]
            tail = jnp.concatenate([tail[:, :blk] + corner, tail[:, blk:t],
                                    tail[:, t:t + blk] + corner, tail[:, t + blk:]], axis=1)
            st = jnp.concatenate([st[:t - blk], tail], axis=0)
        e = jnp.exp2(st)
        pv = (jnp.dot(vt_ref[kj * gran, 0], e[:half].astype(_BF16), preferred_element_type=_F32)
              + jnp.dot(vt_ref[kj * gran + 1, 0], e[half:].astype(_BF16), preferred_element_type=_F32))
        for n, q in enumerate(q_tiles):
            lanes = slice(n * 2 * t, (n + 1) * 2 * t)
            den_ref[q] += key_sum(e[:, lanes])
            acc_ref[q] += pv[:, lanes]

    def write_outputs():
        for q in range(nq):
            o = acc_ref[q] / jnp.sum(den_ref[q], axis=0, keepdims=True)
            od = o[:, :t] - lam_ref[0:1, 0:1] * o[:, t:]
            scale = lax.rsqrt(jnp.mean(od * od, axis=0, keepdims=True) + _EPS)
            o_ref[0, q * t:(q + 1) * t, :] = (od * scale * gain_ref[...]).T.astype(_BF16)

    for kj in range(nq):
        diagonal(kj)
        later = list(range(kj + 1, nq))
        for g0 in range(0, len(later), _QUERY_GROUP):
            later_queries(kj, later[g0:g0 + _QUERY_GROUP])
    write_outputs()

    def recompute_online(m_ref):
        acc_ref[...] = jnp.zeros(acc_ref.shape, _F32)
        den_ref[...] = jnp.zeros(den_ref.shape, _F32)
        m_ref[...] = jnp.full(m_ref.shape, -jnp.inf, _F32)

        def query_tile(q, carry):
            def key_tile(kj, inner):
                kt = k_ref[0, pl.ds(pl.multiple_of(kj * t, t), t), :]
                bias = bias_ref[0, jnp.where(kj == q, 2, jnp.where(kj == q - 1, 1, 0))]
                for s in range(2):
                    lanes = slice(s * t, (s + 1) * t)
                    st = jnp.dot(kt, qw_ref[q, :, lanes], preferred_element_type=_F32) + bias
                    m_old = m_ref[q, :, lanes]
                    m_new = jnp.maximum(m_old, jnp.max(st, axis=0, keepdims=True))
                    alpha = jnp.exp2(m_old - m_new)
                    e = jnp.exp2(st - m_new)
                    pv = (jnp.dot(vt_ref[kj * gran, 0], e[:half].astype(_BF16),
                                  preferred_element_type=_F32)
                          + jnp.dot(vt_ref[kj * gran + 1, 0], e[half:].astype(_BF16),
                                    preferred_element_type=_F32))
                    den_ref[q, :, lanes] = alpha * den_ref[q, :, lanes] + key_sum(e)
                    acc_ref[q, :, lanes] = alpha * acc_ref[q, :, lanes] + pv
                    m_ref[q, :, lanes] = m_new
                return inner

            lax.fori_loop(0, q + 1, key_tile, 0)
            return carry

        lax.fori_loop(0, nq, query_tile, 0)
        write_outputs()

    @pl.when(flag_ref[0, 0] == 0)
    def _():
        pl.run_scoped(recompute_online, pltpu.VMEM((nq, 1, 2 * t), _F32))


def _attention(flag, qt, k, vt, bias, lam, gain_t):
    b, l, w = k.shape
    t = _ATTN_TILE
    nq = l // t
    return pl.pallas_call(
        functools.partial(_attn_body, t=t, nq=nq),
        grid=(b, _HEADS),
        in_specs=[pl.BlockSpec(memory_space=pltpu.SMEM),
                  pl.BlockSpec((_V_DIM, l), lambda bi, h: (h, bi)),
                  pl.BlockSpec((1, l, _V_DIM), lambda bi, h: (bi, 0, h)),
                  pl.BlockSpec((l // _KEY_GRANULE, 1, _V_DIM, _KEY_GRANULE), lambda bi, h: (bi, h, 0, 0)),
                  pl.BlockSpec((1, 3, t, t), lambda bi, h: (h, 0, 0, 0)),
                  pl.BlockSpec(lam.shape, lambda bi, h: (0, 0)),
                  pl.BlockSpec(gain_t.shape, lambda bi, h: (0, 0))],
        out_specs=pl.BlockSpec((1, l, _V_DIM), lambda bi, h: (bi, 0, h)),
        out_shape=jax.ShapeDtypeStruct((b, l, w), _BF16),
        scratch_shapes=[pltpu.VMEM((nq, _V_DIM, 2 * t), _BF16),
                        pltpu.VMEM((nq, _V_DIM, 2 * t), _F32),
                        pltpu.VMEM((nq, 8, 2 * t), _F32)],
        compiler_params=pltpu.CompilerParams(
            dimension_semantics=("parallel", "parallel"), vmem_limit_bytes=_VMEM_LIMIT),
        name="diff_attention",
    )(flag, qt, k, vt, bias, lam, gain_t)


def _s5_prep_body(are_ref, aim_ref, ldt_ref, bre_ref, bim_ref,
                  lbre_ref, lbim_ref, bbre_ref, bbim_ref):
    a_re = are_ref[...]
    a_im = aim_ref[...]
    dt = jnp.exp(ldt_ref[...])
    decay = jnp.exp(a_re * dt)
    lb_re = decay * jnp.cos(a_im * dt)
    lb_im = decay * jnp.sin(a_im * dt)
    nr = lb_re - 1.0
    ni = lb_im
    den = a_re * a_re + a_im * a_im
    q_re = (nr * a_re + ni * a_im) / den
    q_im = (ni * a_re - nr * a_im) / den
    b_re = bre_ref[...]
    b_im = bim_ref[...]
    bbre_ref[...] = q_re * b_re - q_im * b_im
    bbim_ref[...] = q_re * b_im + q_im * b_re
    lbre_ref[...] = lb_re
    lbim_ref[...] = lb_im


def _s5_prep(a_re, a_im, log_dt, b_re, b_im):
    sd = jax.ShapeDtypeStruct(a_re.shape, _F32)
    return pl.pallas_call(_s5_prep_body, out_shape=[sd] * 4, name="s5_discretise")(
        a_re, a_im, log_dt, b_re, b_im)


def _s5_body(u_ref, wbre_ref, wbim_ref, lbre_ref, lbim_ref, wcre_ref, wcim_ref, d_ref,
             gw_ref, gb_ref, o_ref, xre_ref, xim_ref, cre_ref, cim_ref, *, steps, nb):
    rows = nb * steps
    width = u_ref.shape[-1]
    n_tiles, _, tile_w = lbre_ref.shape
    halves = wcre_ref.shape[0]
    tiles_per_half = n_tiles // halves
    hw_in = width // halves

    @pl.when(pl.program_id(0) == 0)
    def _():
        cre_ref[...] = jnp.zeros(cre_ref.shape, _F32)
        cim_ref[...] = jnp.zeros(cim_ref.shape, _F32)

    u_tm = jnp.swapaxes(u_ref[...].astype(_F32), 0, 1).reshape(rows, width).astype(_BF16)

    for hf in range(halves):
        uh = u_tm[:, hf * hw_in:(hf + 1) * hw_in]
        bu_r = jnp.dot(uh, wbre_ref[hf], preferred_element_type=_F32)
        bu_i = jnp.dot(uh, wbim_ref[hf], preferred_element_type=_F32)
        group = range(hf * tiles_per_half, (hf + 1) * tiles_per_half)
        a_r = [jnp.broadcast_to(lbre_ref[j], (nb, tile_w)) for j in group]
        a_i = [jnp.broadcast_to(lbim_ref[j], (nb, tile_w)) for j in group]
        x_r = [cre_ref[j] for j in group]
        x_i = [cim_ref[j] for j in group]
        for ti in range(steps):
            r = slice(ti * nb, (ti + 1) * nb)
            for q, j in enumerate(group):
                lanes = slice(q * tile_w, (q + 1) * tile_w)
                n_r = a_r[q] * x_r[q] - a_i[q] * x_i[q] + bu_r[r, lanes]
                n_i = a_r[q] * x_i[q] + a_i[q] * x_r[q] + bu_i[r, lanes]
                xre_ref[j, r, :] = n_r
                xim_ref[j, r, :] = n_i
                x_r[q], x_i[q] = n_r, n_i
        for q, j in enumerate(group):
            cre_ref[j] = x_r[q]
            cim_ref[j] = x_i[q]

    ys = []
    for hf in range(halves):
        tiles = range(hf * tiles_per_half, (hf + 1) * tiles_per_half)
        x_r = jnp.concatenate([xre_ref[j] for j in tiles], axis=1).astype(_BF16)
        x_i = jnp.concatenate([xim_ref[j] for j in tiles], axis=1).astype(_BF16)
        ys.append(jnp.dot(x_r, wcre_ref[hf], preferred_element_type=_F32)
                  + jnp.dot(x_i, wcim_ref[hf], preferred_element_type=_F32))
    y = jnp.concatenate(ys, axis=-1) + d_ref[...] * u_tm.astype(_F32)
    g = jax.nn.gelu(y)
    o = g * jax.nn.sigmoid(jnp.dot(g.astype(_BF16), gw_ref[...], preferred_element_type=_F32)
                           + gb_ref[...])
    o_ref[...] = jnp.swapaxes(o.reshape(steps, nb, width), 0, 1).astype(_BF16)


def _s5(u, wb_re, wb_im, lb_re, lb_im, wc_re, wc_im, d_vec, glu_w, glu_b):
    nb, l, width = u.shape
    steps = _SCAN_STEPS
    n_tiles, _, tile_w = lb_re.shape
    full = lambda a: pl.BlockSpec(a.shape, lambda i: (0,) * a.ndim)
    io_spec = pl.BlockSpec((nb, steps, width), lambda i: (0, i, 0))
    consts = (wb_re, wb_im, lb_re, lb_im, wc_re, wc_im, d_vec, glu_w, glu_b)
    state_buf = pltpu.VMEM((n_tiles, nb * steps, tile_w), _F32)
    return pl.pallas_call(
        functools.partial(_s5_body, steps=steps, nb=nb),
        grid=(l // steps,),
        in_specs=[io_spec] + [full(a) for a in consts],
        out_specs=io_spec,
        out_shape=jax.ShapeDtypeStruct(u.shape, _BF16),
        scratch_shapes=[state_buf, state_buf,
                        pltpu.VMEM((n_tiles, nb, tile_w), _F32),
                        pltpu.VMEM((n_tiles, nb, tile_w), _F32)],
        compiler_params=pltpu.CompilerParams(
            dimension_semantics=("arbitrary",), vmem_limit_bytes=_VMEM_LIMIT),
        name="s5_branch",
    )(u, *consts)


def _outproj_body(x_ref, ng_ref, wg_ref, mb_ref, oa_ref, os_ref, pa_ref, ps_ref, wo_ref,
                  out_ref, *, width):
    x = x_ref[...]
    d = x.shape[-1]
    h = (x * _rms_scale(x) * ng_ref[...]).astype(_BF16)
    zg = jnp.dot(h, wg_ref[...], preferred_element_type=_F32)
    o_a = (oa_ref[...].astype(_F32) * jax.nn.silu(zg[:, :width])).astype(_BF16)
    o_s = (os_ref[...].astype(_F32) * jax.nn.silu(zg[:, width:2 * width])).astype(_BF16)
    p_a = jnp.dot(o_a, pa_ref[...], preferred_element_type=_F32)
    p_s = jnp.dot(o_s, ps_ref[...], preferred_element_type=_F32)
    g = jax.nn.sigmoid(zg[:, 2 * width:] + mb_ref[...])
    merged = g[:, :d] * p_a + g[:, d:] * p_s
    out_ref[...] = x + jnp.dot(merged.astype(_BF16), wo_ref[...], preferred_element_type=_F32)


def _outproj(x2, norm_gain, w_gates, merge_b, o_a, o_s, proj_a, proj_s, w_out, width):
    n, d = x2.shape
    tm = _TOKEN_TILE
    full = lambda a: pl.BlockSpec(a.shape, lambda i: (0,) * a.ndim)
    row = lambda a: pl.BlockSpec((tm, a.shape[-1]), lambda i: (i, 0))
    return pl.pallas_call(
        functools.partial(_outproj_body, width=width),
        grid=(n // tm,),
        in_specs=[row(x2), full(norm_gain), full(w_gates), full(merge_b), row(o_a), row(o_s),
                  full(proj_a), full(proj_s), full(w_out)],
        out_specs=row(x2),
        out_shape=jax.ShapeDtypeStruct(x2.shape, x2.dtype),
        compiler_params=pltpu.CompilerParams(
            dimension_semantics=("parallel",), vmem_limit_bytes=_VMEM_LIMIT),
        name="outproj",
    )(x2, norm_gain, w_gates, merge_b, o_a, o_s, proj_a, proj_s, w_out)


def _block_diag_halves(blocks, halves):
    g, r, c = blocks.shape
    gh = g // halves
    on_diagonal = (np.arange(gh * r)[:, None] // r) == (np.arange(gh * c)[None, :] // c)
    tiled = jnp.tile(blocks.reshape(halves, gh * r, c), (1, 1, gh))
    return jnp.where(on_diagonal, tiled, 0)


def _layer(x, lam_init, norm_gain, w_in, merge_gate_b, q_norm_gain, k_norm_gain,
           lambda_q1, lambda_k1, lambda_q2, lambda_k2, diff_subln_gain, rel_bias_table,
           ssm_a_re, ssm_a_im, ssm_log_dt, ssm_b_re, ssm_b_im, ssm_c_re, ssm_c_im,
           ssm_d, ssm_glu_w, ssm_glu_b, proj_attn, proj_ssm, w_out):
    nb, l, d = x.shape
    n = nb * l
    aw = _HEADS * 2 * _SUB_DIM
    groups = ssm_a_re.shape[0]
    x2 = x.reshape(n, d)
    ng = norm_gain.reshape(1, d).astype(_F32)

    w_ku = jnp.concatenate([w_in[:, aw:2 * aw], w_in[:, 4 * aw:5 * aw]], axis=1).astype(_BF16)
    w_qvt = jnp.concatenate([w_in[:, :aw], w_in[:, 2 * aw:3 * aw]], axis=1).T.astype(_BF16)
    w_gates = jnp.concatenate([w_in[:, 3 * aw:4 * aw], w_in[:, 5 * aw:]], axis=1).astype(_BF16)
    gq = jnp.tile(q_norm_gain.astype(_F32), 2 * _HEADS) * (_SUB_DIM ** -0.5 * _LOG2E)
    gqt = jnp.broadcast_to(gq[:, None], (aw, _TOKEN_TILE))
    gk = jnp.tile(k_norm_gain.astype(_F32), 2 * _HEADS).reshape(1, aw)
    seg = np.arange(_MXU_TILE) // _SUB_DIM
    gsum = jnp.asarray((seg[:, None] == seg[None, :]).astype(np.float32), _BF16)
    qt, k, vt, u = _inproj(x2, ng, w_ku, w_qvt, gqt, gk, gsum, aw)

    lam4 = jnp.stack([lambda_q1, lambda_k1, lambda_q2, lambda_k2]).astype(_F32)
    qk_gains = jnp.stack([gq[:_SUB_DIM], gk[0, :_SUB_DIM]])
    bias, lam, flag = _bias_prep(rel_bias_table.astype(_F32), lam4, qk_gains, lam_init, _ATTN_TILE)
    subln = diff_subln_gain.astype(_F32) * (1.0 - lam_init)
    subln_t = jnp.broadcast_to(subln[:, None], (_V_DIM, _ATTN_TILE))
    shp = (nb, l, aw)
    o_a = _attention(flag, qt, k.reshape(shp), vt, bias, lam, subln_t)

    rep = lambda a: jnp.repeat(a.astype(_F32), _SSM_GROUP, axis=0)
    ldt = jnp.broadcast_to(ssm_log_dt.astype(_F32)[:, None], ssm_a_re.shape)
    bt = lambda a: a.astype(_F32).transpose(0, 2, 1).reshape(groups * _SSM_GROUP, _SSM_STATE)
    lb_re, lb_im, bb_re, bb_im = _s5_prep(rep(ssm_a_re), rep(ssm_a_im), rep(ldt),
                                          bt(ssm_b_re), bt(ssm_b_im))
    n_tiles = groups * _SSM_STATE // _STATE_TILE
    flat = lambda a: a[::_SSM_GROUP].reshape(n_tiles, 1, _STATE_TILE)
    gshape = (groups, _SSM_GROUP, _SSM_STATE)
    wb_re = _block_diag_halves(bb_re.reshape(gshape), 2).astype(_BF16)
    wb_im = _block_diag_halves(bb_im.reshape(gshape), 2).astype(_BF16)
    wc_re = _block_diag_halves(ssm_c_re.astype(_F32).transpose(0, 2, 1), 2).astype(_BF16)
    wc_im = _block_diag_halves(-ssm_c_im.astype(_F32).transpose(0, 2, 1), 2).astype(_BF16)
    o_s = _s5(u.reshape(shp), wb_re, wb_im, flat(lb_re), flat(lb_im), wc_re, wc_im,
              ssm_d.astype(_F32).reshape(1, aw), ssm_glu_w.astype(_BF16),
              ssm_glu_b.astype(_F32).reshape(1, aw))

    out = _outproj(x2, ng, w_gates, merge_gate_b.astype(_F32).reshape(1, 2 * d),
                   o_a.reshape(n, aw), o_s.reshape(n, aw),
                   proj_attn.astype(_BF16), proj_ssm.astype(_BF16), w_out.astype(_BF16), aw)
    return out.reshape(nb, l, d)


def kernel(x, norm_gain, w_in, merge_gate_b, q_norm_gain, k_norm_gain, lambda_q1, lambda_k1,
           lambda_q2, lambda_k2, diff_subln_gain, rel_bias_table, ssm_A_re, ssm_A_im, ssm_log_dt,
           ssm_B_re, ssm_B_im, ssm_C_re, ssm_C_im, ssm_D, ssm_glu_w, ssm_glu_b,
           proj_attn, proj_ssm, w_out):
    per_layer = (norm_gain, w_in, merge_gate_b, q_norm_gain, k_norm_gain, lambda_q1, lambda_k1,
                 lambda_q2, lambda_k2, diff_subln_gain)
    per_layer_tail = (ssm_A_re, ssm_A_im, ssm_log_dt, ssm_B_re, ssm_B_im, ssm_C_re, ssm_C_im,
                      ssm_D, ssm_glu_w, ssm_glu_b, proj_attn, proj_ssm, w_out)
    for layer in range(norm_gain.shape[0]):
        lam_init = 0.8 - 0.6 * math.exp(-0.3 * layer)
        x = _layer(x, lam_init, *(p[layer] for p in per_layer), rel_bias_table,
                   *(p[layer] for p in per_layer_tail))
    return x
```

```python
import functools
import math

import jax
import jax.numpy as jnp
import numpy as np
from jax import lax
from jax.experimental import pallas as pl
from jax.experimental.pallas import tpu as pltpu

_F32 = jnp.float32
_BF16 = jnp.bfloat16

_CHUNK = 64
_HEADS = 4
_SUB_DIM = 64
_V_DIM = 128
_SSM_GROUP = 16
_SSM_STATE = 64
_REL_BUCKETS = 32
_REL_MAX_DIST = 128
_EPS = 1e-6
_LOG2E = math.log2(math.e)
_SAFE_LOG2_RANGE = 100.0
_BF16_SLACK = 1.02

_VMEM_LIMIT = 48 * 1024 * 1024
_MXU_TILE = 256
_SUBLANES = 8
_TOKEN_TILE = 1024
_ATTN_TILE = 512
_KEY_GRANULE = 256
_BIAS_BLOCK = 128
_QUERY_GROUP = 2
_SCAN_STEPS = 128
_STATE_TILE = 256


def _rms_scale(x, eps=_EPS):
    return lax.rsqrt(jnp.mean(x * x, axis=-1, keepdims=True) + eps)


def _inproj_body(x_ref, ng_ref, wku_ref, wqvt_ref, gqt_ref, gk_ref, gsum_ref,
                 qt_ref, k_ref, vt_ref, u_ref, *, width):
    x = x_ref[...]
    tm = x.shape[0]
    h = (x * _rms_scale(x) * ng_ref[...]).astype(_BF16)
    z = jnp.dot(h, wku_ref[...], preferred_element_type=_F32)
    zt = lax.dot_general(wqvt_ref[...], h, (((1,), (1,)), ((), ())),
                         preferred_element_type=_F32)

    kk = z[:, :width]
    k2 = (kk * kk).astype(_BF16)
    mxu = gsum_ref.shape[0]
    ss = jnp.concatenate([jnp.dot(k2[:, c:c + mxu], gsum_ref[...], preferred_element_type=_F32)
                          for c in range(0, width, mxu)], axis=1)
    k_ref[...] = (kk * lax.rsqrt(ss * (1.0 / _SUB_DIM) + _EPS) * gk_ref[...]).astype(_BF16)
    u_ref[...] = z[:, width:].astype(_BF16)

    qt = zt[:width].reshape(width // _SUB_DIM, _SUB_DIM, tm)
    ms = jnp.mean(qt * qt, axis=1, keepdims=True)
    qt_ref[...] = ((qt * lax.rsqrt(ms + _EPS)).reshape(width, tm) * gqt_ref[...]).astype(_BF16)

    vt = zt[width:].astype(_BF16)
    for j in range(tm // _KEY_GRANULE):
        for hd in range(_HEADS):
            vt_ref[j, hd] = vt[hd * _V_DIM:(hd + 1) * _V_DIM,
                               j * _KEY_GRANULE:(j + 1) * _KEY_GRANULE]


def _inproj(x2, norm_gain, w_ku, w_qvt, gqt, gk, gsum, width):
    n, d = x2.shape
    tm = _TOKEN_TILE
    full = lambda a: pl.BlockSpec(a.shape, lambda i: (0,) * a.ndim)
    row_sd = jax.ShapeDtypeStruct((n, width), _BF16)
    row_spec = pl.BlockSpec((tm, width), lambda i: (i, 0))
    gran = tm // _KEY_GRANULE
    vrows = _V_DIM
    return pl.pallas_call(
        functools.partial(_inproj_body, width=width),
        grid=(n // tm,),
        in_specs=[pl.BlockSpec((tm, d), lambda i: (i, 0)), full(norm_gain), full(w_ku), full(w_qvt),
                  full(gqt), full(gk), full(gsum)],
        out_specs=[pl.BlockSpec((width, tm), lambda i: (0, i)), row_spec,
                   pl.BlockSpec((gran, _HEADS, vrows, _KEY_GRANULE), lambda i: (i, 0, 0, 0)), row_spec],
        out_shape=[jax.ShapeDtypeStruct((width, n), _BF16), row_sd,
                   jax.ShapeDtypeStruct((n // _KEY_GRANULE, _HEADS, vrows, _KEY_GRANULE), _BF16), row_sd],
        compiler_params=pltpu.CompilerParams(
            dimension_semantics=("parallel",), vmem_limit_bytes=_VMEM_LIMIT),
        name="inproj",
    )(x2, norm_gain, w_ku, w_qvt, gqt, gk, gsum)


def _t5_bucket_np(rel):
    nb = _REL_BUCKETS // 2
    max_exact = nb // 2
    side = np.where(rel > 0, nb, 0)
    n = np.abs(rel)
    nf = np.maximum(n, 1).astype(np.float32)
    large = max_exact + (np.log(nf / np.float32(max_exact)) / np.float32(math.log(_REL_MAX_DIST / max_exact))
                         * np.float32(nb - max_exact)).astype(np.int32)
    large = np.minimum(large, nb - 1)
    return side + np.where(n < max_exact, n, large)


def _bucket_blocks():
    assert _BIAS_BLOCK % _CHUNK == 0
    i = np.arange(_BIAS_BLOCK)[None, :]
    j = np.arange(_BIAS_BLOCK)[:, None]
    far_bucket = _REL_BUCKETS // 2 - 1
    assert (_t5_bucket_np(j - i - 2 * _BIAS_BLOCK) == far_bucket).all()
    diag = np.where((j // _CHUNK) <= (i // _CHUNK), _t5_bucket_np(j - i), -1)
    prev = _t5_bucket_np(j - i - _BIAS_BLOCK)
    return np.stack([diag, prev]).astype(np.int32)


def _bias_prep_body(tab_ref, tabv_ref, bucket_ref, lam4_ref, gains_ref,
                    bias_ref, lam_ref, *, lam_init, t):
    h = pl.program_id(0)
    far_bucket = _REL_BUCKETS // 2 - 1
    bkt = bucket_ref[...]
    far = tab_ref[far_bucket, h]
    val = jnp.full(bkt.shape, -jnp.inf, _F32)
    for b in range(_REL_BUCKETS):
        val = jnp.where(bkt == b, (tab_ref[b, h] - far) * _LOG2E, val)
    diag_block, prev_block = val[0], val[1]
    n_blk = t // _BIAS_BLOCK
    bias_ref[0, 0] = jnp.zeros((t, t), _F32)
    bias_ref[0, 1] = jnp.zeros((t, t), _F32)
    bias_ref[0, 1, t - _BIAS_BLOCK:, :_BIAS_BLOCK] = prev_block
    for bj in range(n_blk):
        rows = slice(bj * _BIAS_BLOCK, (bj + 1) * _BIAS_BLOCK)
        for bi in range(n_blk):
            cols = slice(bi * _BIAS_BLOCK, (bi + 1) * _BIAS_BLOCK)
            if bi == bj:
                block = diag_block
            elif bi == bj + 1:
                block = prev_block
            else:
                block = jnp.full((_BIAS_BLOCK, _BIAS_BLOCK), 0.0 if bi > bj else -jnp.inf, _F32)
            bias_ref[0, 2, rows, cols] = block
    l4 = lam4_ref[...]
    s1 = jnp.sum(l4[0:1] * l4[1:2], axis=-1, keepdims=True)
    s2 = jnp.sum(l4[2:3] * l4[3:4], axis=-1, keepdims=True)
    lam_ref[0:1] = jnp.broadcast_to(jnp.exp(s1) - jnp.exp(s2) + lam_init, (1, lam_ref.shape[1]))
    gmax = jnp.max(jnp.abs(gains_ref[...]), axis=-1, keepdims=True)
    qk_bound = (_SUB_DIM * _BF16_SLACK) * gmax[0:1] * gmax[1:2]
    tv = tabv_ref[...]
    dev = jnp.abs(tv - tv[far_bucket:far_bucket + 1]) * _LOG2E
    bias_bound = jnp.max(jnp.max(dev, axis=-1, keepdims=True), axis=0, keepdims=True)
    bounded = (qk_bound + bias_bound <= _SAFE_LOG2_RANGE).astype(_F32)
    lam_ref[1:2] = jnp.broadcast_to(bounded, (1, lam_ref.shape[1]))


def _bias_prep(rel_table, lam4, gains, lam_init, t):
    assert t % _BIAS_BLOCK == 0
    buckets = jnp.asarray(_bucket_blocks())
    whole = lambda a: pl.BlockSpec(a.shape, lambda h: (0,) * a.ndim)
    return pl.pallas_call(
        functools.partial(_bias_prep_body, lam_init=lam_init, t=t),
        grid=(_HEADS,),
        in_specs=[pl.BlockSpec(memory_space=pltpu.SMEM), whole(rel_table), whole(buckets),
                  whole(lam4), whole(gains)],
        out_specs=[pl.BlockSpec((1, 3, t, t), lambda h: (h, 0, 0, 0)),
                   pl.BlockSpec((2, _V_DIM), lambda h: (0, 0))],
        out_shape=[jax.ShapeDtypeStruct((_HEADS, 3, t, t), _F32),
                   jax.ShapeDtypeStruct((2, _V_DIM), _F32)],
        compiler_params=pltpu.CompilerParams(dimension_semantics=("arbitrary",)),
        name="attn_bias_prep",
    )(rel_table, rel_table, buckets, lam4, gains)


def _attn_body(qt_ref, k_ref, vt_ref, bias_ref, lam_ref, gain_ref, o_ref,
               qw_ref, acc_ref, den_ref, *, t, nq, online):
    gran = t // _KEY_GRANULE
    assert gran == 2
    half = _KEY_GRANULE
    row = lax.broadcasted_iota(jnp.int32, (_V_DIM, t), 0)
    for q in range(nq):
        qt = qt_ref[:, q * t:(q + 1) * t]
        zero = jnp.zeros_like(qt)
        qw_ref[q, :, :t] = jnp.where(row < _SUB_DIM, qt, zero)
        qw_ref[q, :, t:] = jnp.where(row >= _SUB_DIM, qt, zero)
    acc_ref[...] = jnp.zeros(acc_ref.shape, _F32)
    den_ref[...] = jnp.zeros(den_ref.shape, _F32)

    def key_sum(e):
        return jnp.sum(e.reshape(e.shape[0] // _SUBLANES, _SUBLANES, e.shape[1]), axis=0)

    def diagonal(kj):
        late = [slice(half, t), slice(t + half, 2 * t)]
        k0 = kj * t
        bias = bias_ref[0, 2]
        st0 = jnp.dot(k_ref[0, k0:k0 + half, :], qw_ref[kj], preferred_element_type=_F32)
        e0 = [jnp.exp2(st0[:, :t] + bias[:half]), jnp.exp2(st0[:, t:] + bias[:half])]
        den_ref[kj] += jnp.concatenate([key_sum(e0[0]), key_sum(e0[1])], axis=1)
        pt0 = jnp.concatenate([e0[0].astype(_BF16), e0[1].astype(_BF16)], axis=1)
        qw_late = jnp.concatenate([qw_ref[kj, :, lanes] for lanes in late], axis=1)
        st1 = jnp.dot(k_ref[0, k0 + half:k0 + t, :], qw_late, preferred_element_type=_F32)
        e1 = [jnp.exp2(st1[:, :half] + bias[half:, half:]), jnp.exp2(st1[:, half:] + bias[half:, half:])]
        pt1 = jnp.concatenate([e1[0].astype(_BF16), e1[1].astype(_BF16)], axis=1)
        acc_ref[kj] += jnp.dot(vt_ref[kj * gran, 0], pt0, preferred_element_type=_F32)
        pv1 = jnp.dot(vt_ref[kj * gran + 1, 0], pt1, preferred_element_type=_F32)
        for n, lanes in enumerate(late):
            acc_ref[kj, :, lanes] += pv1[:, n * half:(n + 1) * half]
            den_ref[kj, :, lanes] += key_sum(e1[n])

    def later_queries(kj, q_tiles):
        qw = jnp.concatenate([qw_ref[q] for q in q_tiles], axis=1)
        st = jnp.dot(k_ref[0, kj * t:(kj + 1) * t, :], qw, preferred_element_type=_F32)
        if q_tiles[0] == kj + 1:
            blk = _BIAS_BLOCK
            corner = bias_ref[0, 1, t - blk:, :blk]
            tail = st[t - blk:]
            tail = jnp.concatenate([tail[:, :blk] + corner, tail[:, blk:t],
                                    tail[:, t:t + blk] + corner, tail[:, t + blk:]], axis=1)
            st = jnp.concatenate([st[:t - blk], tail], axis=0)
        e = jnp.exp2(st)
        pv = (jnp.dot(vt_ref[kj * gran, 0], e[:half].astype(_BF16), preferred_element_type=_F32)
              + jnp.dot(vt_ref[kj * gran + 1, 0], e[half:].astype(_BF16), preferred_element_type=_F32))
        for n, q in enumerate(q_tiles):
            lanes = slice(n * 2 * t, (n + 1) * 2 * t)
            den_ref[q] += key_sum(e[:, lanes])
            acc_ref[q] += pv[:, lanes]

    def write_outputs():
        for q in range(nq):
            o = acc_ref[q] / jnp.sum(den_ref[q], axis=0, keepdims=True)
            od = o[:, :t] - lam_ref[0:1, 0:1] * o[:, t:]
            scale = lax.rsqrt(jnp.mean(od * od, axis=0, keepdims=True) + _EPS)
            o_ref[0, q * t:(q + 1) * t, :] = (od * scale * gain_ref[...]).T.astype(_BF16)

    def online_sweep(m_ref):
        m_ref[...] = jnp.full(m_ref.shape, -jnp.inf, _F32)

        def query_tile(q, carry):
            def key_tile(kj, inner):
                kt = k_ref[0, pl.ds(pl.multiple_of(kj * t, t), t), :]
                bias = bias_ref[0, jnp.where(kj == q, 2, jnp.where(kj == q - 1, 1, 0))]
                for s in range(2):
                    lanes = slice(s * t, (s + 1) * t)
                    st = jnp.dot(kt, qw_ref[q, :, lanes], preferred_element_type=_F32) + bias
                    m_old = m_ref[q, :, lanes]
                    m_new = jnp.maximum(m_old, jnp.max(st, axis=0, keepdims=True))
                    alpha = jnp.exp2(m_old - m_new)
                    e = jnp.exp2(st - m_new)
                    pv = (jnp.dot(vt_ref[kj * gran, 0], e[:half].astype(_BF16),
                                  preferred_element_type=_F32)
                          + jnp.dot(vt_ref[kj * gran + 1, 0], e[half:].astype(_BF16),
                                    preferred_element_type=_F32))
                    den_ref[q, :, lanes] = alpha * den_ref[q, :, lanes] + key_sum(e)
                    acc_ref[q, :, lanes] = alpha * acc_ref[q, :, lanes] + pv
                    m_ref[q, :, lanes] = m_new
                return inner

            lax.fori_loop(0, q + 1, key_tile, 0)
            return carry

        lax.fori_loop(0, nq, query_tile, 0)

    if online:
        pl.run_scoped(online_sweep, pltpu.VMEM((nq, 1, 2 * t), _F32))
    else:
        for kj in range(nq):
            diagonal(kj)
            later = list(range(kj + 1, nq))
            for g0 in range(0, len(later), _QUERY_GROUP):
                later_queries(kj, later[g0:g0 + _QUERY_GROUP])
    write_outputs()


def _attention(qt, k, vt, bias, lam, gain_t):
    b, l, w = k.shape
    t = _ATTN_TILE
    nq = l // t

    def sweep(online):
        return pl.pallas_call(
            functools.partial(_attn_body, t=t, nq=nq, online=online),
            grid=(b, _HEADS),
            in_specs=[pl.BlockSpec((_V_DIM, l), lambda bi, h: (h, bi)),
                      pl.BlockSpec((1, l, _V_DIM), lambda bi, h: (bi, 0, h)),
                      pl.BlockSpec((l // _KEY_GRANULE, 1, _V_DIM, _KEY_GRANULE),
                                   lambda bi, h: (bi, h, 0, 0)),
                      pl.BlockSpec((1, 3, t, t), lambda bi, h: (h, 0, 0, 0)),
                      pl.BlockSpec(lam.shape, lambda bi, h: (0, 0)),
                      pl.BlockSpec(gain_t.shape, lambda bi, h: (0, 0))],
            out_specs=pl.BlockSpec((1, l, _V_DIM), lambda bi, h: (bi, 0, h)),
            out_shape=jax.ShapeDtypeStruct((b, l, w), _BF16),
            scratch_shapes=[pltpu.VMEM((nq, _V_DIM, 2 * t), _BF16),
                            pltpu.VMEM((nq, _V_DIM, 2 * t), _F32),
                            pltpu.VMEM((nq, _SUBLANES, 2 * t), _F32)],
            compiler_params=pltpu.CompilerParams(
                dimension_semantics=("parallel", "parallel"), vmem_limit_bytes=_VMEM_LIMIT),
            name="diff_attention_online" if online else "diff_attention",
        )(qt, k, vt, bias, lam, gain_t)

    return lax.cond(lam[1, 0] > 0, lambda: sweep(False), lambda: sweep(True))


def _s5_prep_body(are_ref, aim_ref, ldt_ref, bre_ref, bim_ref,
                  lbre_ref, lbim_ref, bbre_ref, bbim_ref):
    a_re = are_ref[...]
    a_im = aim_ref[...]
    dt = jnp.exp(ldt_ref[...])
    decay = jnp.exp(a_re * dt)
    lb_re = decay * jnp.cos(a_im * dt)
    lb_im = decay * jnp.sin(a_im * dt)
    nr = lb_re - 1.0
    ni = lb_im
    den = a_re * a_re + a_im * a_im
    q_re = (nr * a_re + ni * a_im) / den
    q_im = (ni * a_re - nr * a_im) / den
    b_re = bre_ref[...]
    b_im = bim_ref[...]
    bbre_ref[...] = q_re * b_re - q_im * b_im
    bbim_ref[...] = q_re * b_im + q_im * b_re
    lbre_ref[...] = lb_re
    lbim_ref[...] = lb_im


def _s5_prep(a_re, a_im, log_dt, b_re, b_im):
    sd = jax.ShapeDtypeStruct(a_re.shape, _F32)
    return pl.pallas_call(_s5_prep_body, out_shape=[sd] * 4, name="s5_discretise")(
        a_re, a_im, log_dt, b_re, b_im)


def _s5_body(u_ref, wbre_ref, wbim_ref, lbre_ref, lbim_ref, wcre_ref, wcim_ref, d_ref,
             gw_ref, gb_ref, o_ref, xre_ref, xim_ref, cre_ref, cim_ref, *, steps, nb):
    rows = nb * steps
    width = u_ref.shape[-1]
    n_tiles, _, tile_w = lbre_ref.shape
    halves = wcre_ref.shape[0]
    tiles_per_half = n_tiles // halves
    hw_in = width // halves

    @pl.when(pl.program_id(0) == 0)
    def _():
        cre_ref[...] = jnp.zeros(cre_ref.shape, _F32)
        cim_ref[...] = jnp.zeros(cim_ref.shape, _F32)

    u_tm = jnp.swapaxes(u_ref[...].astype(_F32), 0, 1).reshape(rows, width).astype(_BF16)

    for hf in range(halves):
        uh = u_tm[:, hf * hw_in:(hf + 1) * hw_in]
        bu_r = jnp.dot(uh, wbre_ref[hf], preferred_element_type=_F32)
        bu_i = jnp.dot(uh, wbim_ref[hf], preferred_element_type=_F32)
        group = range(hf * tiles_per_half, (hf + 1) * tiles_per_half)
        a_r = [jnp.broadcast_to(lbre_ref[j], (nb, tile_w)) for j in group]
        a_i = [jnp.broadcast_to(lbim_ref[j], (nb, tile_w)) for j in group]
        x_r = [cre_ref[j] for j in group]
        x_i = [cim_ref[j] for j in group]
        for ti in range(steps):
            r = slice(ti * nb, (ti + 1) * nb)
            for q, j in enumerate(group):
                lanes = slice(q * tile_w, (q + 1) * tile_w)
                n_r = a_r[q] * x_r[q] - a_i[q] * x_i[q] + bu_r[r, lanes]
                n_i = a_r[q] * x_i[q] + a_i[q] * x_r[q] + bu_i[r, lanes]
                xre_ref[j, r, :] = n_r
                xim_ref[j, r, :] = n_i
                x_r[q], x_i[q] = n_r, n_i
        for q, j in enumerate(group):
            cre_ref[j] = x_r[q]
            cim_ref[j] = x_i[q]

    ys = []
    for hf in range(halves):
        tiles = range(hf * tiles_per_half, (hf + 1) * tiles_per_half)
        x_r = jnp.concatenate([xre_ref[j] for j in tiles], axis=1).astype(_BF16)
        x_i = jnp.concatenate([xim_ref[j] for j in tiles], axis=1).astype(_BF16)
        ys.append(jnp.dot(x_r, wcre_ref[hf], preferred_element_type=_F32)
                  + jnp.dot(x_i, wcim_ref[hf], preferred_element_type=_F32))
    y = jnp.concatenate(ys, axis=-1) + d_ref[...] * u_tm.astype(_F32)
    g = jax.nn.gelu(y)
    o = g * jax.nn.sigmoid(jnp.dot(g.astype(_BF16), gw_ref[...], preferred_element_type=_F32)
                           + gb_ref[...])
    o_ref[...] = jnp.swapaxes(o.reshape(steps, nb, width), 0, 1).astype(_BF16)


def _s5(u, wb_re, wb_im, lb_re, lb_im, wc_re, wc_im, d_vec, glu_w, glu_b):
    nb, l, width = u.shape
    steps = _SCAN_STEPS
    n_tiles, _, tile_w = lb_re.shape
    full = lambda a: pl.BlockSpec(a.shape, lambda i: (0,) * a.ndim)
    io_spec = pl.BlockSpec((nb, steps, width), lambda i: (0, i, 0))
    consts = (wb_re, wb_im, lb_re, lb_im, wc_re, wc_im, d_vec, glu_w, glu_b)
    state_buf = pltpu.VMEM((n_tiles, nb * steps, tile_w), _F32)
    return pl.pallas_call(
        functools.partial(_s5_body, steps=steps, nb=nb),
        grid=(l // steps,),
        in_specs=[io_spec] + [full(a) for a in consts],
        out_specs=io_spec,
        out_shape=jax.ShapeDtypeStruct(u.shape, _BF16),
        scratch_shapes=[state_buf, state_buf,
                        pltpu.VMEM((n_tiles, nb, tile_w), _F32),
                        pltpu.VMEM((n_tiles, nb, tile_w), _F32)],
        compiler_params=pltpu.CompilerParams(
            dimension_semantics=("arbitrary",), vmem_limit_bytes=_VMEM_LIMIT),
        name="s5_branch",
    )(u, *consts)


def _outproj_body(x_ref, ng_ref, wg_ref, mb_ref, oa_ref, os_ref, pa_ref, ps_ref, wo_ref,
                  out_ref, *, width):
    x = x_ref[...]
    d = x.shape[-1]
    h = (x * _rms_scale(x) * ng_ref[...]).astype(_BF16)
    zg = jnp.dot(h, wg_ref[...], preferred_element_type=_F32)
    o_a = (oa_ref[...].astype(_F32) * jax.nn.silu(zg[:, :width])).astype(_BF16)
    o_s = (os_ref[...].astype(_F32) * jax.nn.silu(zg[:, width:2 * width])).astype(_BF16)
    p_a = jnp.dot(o_a, pa_ref[...], preferred_element_type=_F32)
    p_s = jnp.dot(o_s, ps_ref[...], preferred_element_type=_F32)
    g = jax.nn.sigmoid(zg[:, 2 * width:] + mb_ref[...])
    merged = g[:, :d] * p_a + g[:, d:] * p_s
    out_ref[...] = x + jnp.dot(merged.astype(_BF16), wo_ref[...], preferred_element_type=_F32)


def _outproj(x2, norm_gain, w_gates, merge_b, o_a, o_s, proj_a, proj_s, w_out, width):
    n, d = x2.shape
    tm = _TOKEN_TILE
    full = lambda a: pl.BlockSpec(a.shape, lambda i: (0,) * a.ndim)
    row = lambda a: pl.BlockSpec((tm, a.shape[-1]), lambda i: (i, 0))
    return pl.pallas_call(
        functools.partial(_outproj_body, width=width),
        grid=(n // tm,),
        in_specs=[row(x2), full(norm_gain), full(w_gates), full(merge_b), row(o_a), row(o_s),
                  full(proj_a), full(proj_s), full(w_out)],
        out_specs=row(x2),
        out_shape=jax.ShapeDtypeStruct(x2.shape, x2.dtype),
        compiler_params=pltpu.CompilerParams(
            dimension_semantics=("parallel",), vmem_limit_bytes=_VMEM_LIMIT),
        name="outproj",
    )(x2, norm_gain, w_gates, merge_b, o_a, o_s, proj_a, proj_s, w_out)


def _block_diag_halves(blocks, halves):
    g, r, c = blocks.shape
    gh = g // halves
    on_diagonal = (np.arange(gh * r)[:, None] // r) == (np.arange(gh * c)[None, :] // c)
    tiled = jnp.tile(blocks.reshape(halves, gh * r, c), (1, 1, gh))
    return jnp.where(on_diagonal, tiled, 0)


def _layer(x, lam_init, norm_gain, w_in, merge_gate_b, q_norm_gain, k_norm_gain,
           lambda_q1, lambda_k1, lambda_q2, lambda_k2, diff_subln_gain, rel_bias_table,
           ssm_a_re, ssm_a_im, ssm_log_dt, ssm_b_re, ssm_b_im, ssm_c_re, ssm_c_im,
           ssm_d, ssm_glu_w, ssm_glu_b, proj_attn, proj_ssm, w_out):
    nb, l, d = x.shape
    n = nb * l
    aw = _HEADS * 2 * _SUB_DIM
    groups = ssm_a_re.shape[0]
    x2 = x.reshape(n, d)
    ng = norm_gain.reshape(1, d).astype(_F32)

    w_ku = jnp.concatenate([w_in[:, aw:2 * aw], w_in[:, 4 * aw:5 * aw]], axis=1).astype(_BF16)
    w_qvt = jnp.concatenate([w_in[:, :aw], w_in[:, 2 * aw:3 * aw]], axis=1).T.astype(_BF16)
    w_gates = jnp.concatenate([w_in[:, 3 * aw:4 * aw], w_in[:, 5 * aw:]], axis=1).astype(_BF16)
    gq = jnp.tile(q_norm_gain.astype(_F32), 2 * _HEADS) * (_SUB_DIM ** -0.5 * _LOG2E)
    gqt = jnp.broadcast_to(gq[:, None], (aw, _TOKEN_TILE))
    gk = jnp.tile(k_norm_gain.astype(_F32), 2 * _HEADS).reshape(1, aw)
    seg = np.arange(_MXU_TILE) // _SUB_DIM
    gsum = jnp.asarray((seg[:, None] == seg[None, :]).astype(np.float32), _BF16)
    qt, k, vt, u = _inproj(x2, ng, w_ku, w_qvt, gqt, gk, gsum, aw)

    lam4 = jnp.stack([lambda_q1, lambda_k1, lambda_q2, lambda_k2]).astype(_F32)
    qk_gains = jnp.stack([gq[:_SUB_DIM], gk[0, :_SUB_DIM]])
    bias, lam = _bias_prep(rel_bias_table.astype(_F32), lam4, qk_gains, lam_init, _ATTN_TILE)
    subln = diff_subln_gain.astype(_F32) * (1.0 - lam_init)
    subln_t = jnp.broadcast_to(subln[:, None], (_V_DIM, _ATTN_TILE))
    shp = (nb, l, aw)
    o_a = _attention(qt, k.reshape(shp), vt, bias, lam, subln_t)

    rep = lambda a: jnp.repeat(a.astype(_F32), _SSM_GROUP, axis=0)
    ldt = jnp.broadcast_to(ssm_log_dt.astype(_F32)[:, None], ssm_a_re.shape)
    bt = lambda a: a.astype(_F32).transpose(0, 2, 1).reshape(groups * _SSM_GROUP, _SSM_STATE)
    lb_re, lb_im, bb_re, bb_im = _s5_prep(rep(ssm_a_re), rep(ssm_a_im), rep(ldt),
                                          bt(ssm_b_re), bt(ssm_b_im))
    n_tiles = groups * _SSM_STATE // _STATE_TILE
    flat = lambda a: a[::_SSM_GROUP].reshape(n_tiles, 1, _STATE_TILE)
    gshape = (groups, _SSM_GROUP, _SSM_STATE)
    wb_re = _block_diag_halves(bb_re.reshape(gshape), 2).astype(_BF16)
    wb_im = _block_diag_halves(bb_im.reshape(gshape), 2).astype(_BF16)
    wc_re = _block_diag_halves(ssm_c_re.astype(_F32).transpose(0, 2, 1), 2).astype(_BF16)
    wc_im = _block_diag_halves(-ssm_c_im.astype(_F32).transpose(0, 2, 1), 2).astype(_BF16)
    o_s = _s5(u.reshape(shp), wb_re, wb_im, flat(lb_re), flat(lb_im), wc_re, wc_im,
              ssm_d.astype(_F32).reshape(1, aw), ssm_glu_w.astype(_BF16),
              ssm_glu_b.astype(_F32).reshape(1, aw))

    out = _outproj(x2, ng, w_gates, merge_gate_b.astype(_F32).reshape(1, 2 * d),
                   o_a.reshape(n, aw), o_s.reshape(n, aw),
                   proj_attn.astype(_BF16), proj_ssm.astype(_BF16), w_out.astype(_BF16), aw)
    return out.reshape(nb, l, d)


def kernel(x, norm_gain, w_in, merge_gate_b, q_norm_gain, k_norm_gain, lambda_q1, lambda_k1,
           lambda_q2, lambda_k2, diff_subln_gain, rel_bias_table, ssm_A_re, ssm_A_im, ssm_log_dt,
           ssm_B_re, ssm_B_im, ssm_C_re, ssm_C_im, ssm_D, ssm_glu_w, ssm_glu_b,
           proj_attn, proj_ssm, w_out):
    per_layer = (norm_gain, w_in, merge_gate_b, q_norm_gain, k_norm_gain, lambda_q1, lambda_k1,
                 lambda_q2, lambda_k2, diff_subln_gain)
    per_layer_tail = (ssm_A_re, ssm_A_im, ssm_log_dt, ssm_B_re, ssm_B_im, ssm_C_re, ssm_C_im,
                      ssm_D, ssm_glu_w, ssm_glu_b, proj_attn, proj_ssm, w_out)
    for layer in range(norm_gain.shape[0]):
        lam_init = 0.8 - 0.6 * math.exp(-0.3 * layer)
        x = _layer(x, lam_init, *(p[layer] for p in per_layer), rel_bias_table,
                   *(p[layer] for p in per_layer_tail))
    return x
```

```python
import functools
import math

import jax
import jax.numpy as jnp
import numpy as np
from jax import lax
from jax.experimental import pallas as pl
from jax.experimental.pallas import tpu as pltpu

_F32 = jnp.float32
_BF16 = jnp.bfloat16

_CHUNK = 64
_HEADS = 4
_SUB_DIM = 64
_V_DIM = 128
_SSM_GROUP = 16
_SSM_STATE = 64
_REL_BUCKETS = 32
_REL_MAX_DIST = 128
_EPS = 1e-6
_LOG2E = math.log2(math.e)
_SAFE_LOG2_RANGE = 100.0
_BF16_SLACK = 1.02

_VMEM_LIMIT = 48 * 1024 * 1024
_MXU_TILE = 256
_SUBLANES = 8
_TOKEN_TILE = 1024
_ATTN_TILE = 512
_KEY_GRANULE = 256
_BIAS_BLOCK = 128
_QUERY_GROUP = 2
_SCAN_STEPS = 128
_STATE_TILE = 256


def _rms_scale(x, eps=_EPS):
    return lax.rsqrt(jnp.mean(x * x, axis=-1, keepdims=True) + eps)


def _inproj_body(x_ref, ng_ref, wku_ref, wqvt_ref, gqt_ref, gk_ref, gsum_ref,
                 qt_ref, k_ref, vt_ref, u_ref, *, width):
    x = x_ref[...]
    tm = x.shape[0]
    h = (x * _rms_scale(x) * ng_ref[...]).astype(_BF16)
    z = jnp.dot(h, wku_ref[...], preferred_element_type=_F32)
    zt = lax.dot_general(wqvt_ref[...], h, (((1,), (1,)), ((), ())),
                         preferred_element_type=_F32)

    kk = z[:, :width]
    k2 = (kk * kk).astype(_BF16)
    mxu = gsum_ref.shape[0]
    ss = jnp.concatenate([jnp.dot(k2[:, c:c + mxu], gsum_ref[...], preferred_element_type=_F32)
                          for c in range(0, width, mxu)], axis=1)
    k_ref[...] = (kk * lax.rsqrt(ss * (1.0 / _SUB_DIM) + _EPS) * gk_ref[...]).astype(_BF16)
    u_ref[...] = z[:, width:].astype(_BF16)

    qt = zt[:width].reshape(width // _SUB_DIM, _SUB_DIM, tm)
    ms = jnp.mean(qt * qt, axis=1, keepdims=True)
    qt_ref[...] = ((qt * lax.rsqrt(ms + _EPS)).reshape(width, tm) * gqt_ref[...]).astype(_BF16)

    vt = zt[width:].astype(_BF16)
    for j in range(tm // _KEY_GRANULE):
        for hd in range(_HEADS):
            vt_ref[j, hd] = vt[hd * _V_DIM:(hd + 1) * _V_DIM,
                               j * _KEY_GRANULE:(j + 1) * _KEY_GRANULE]


def _inproj(x2, norm_gain, w_ku, w_qvt, gqt, gk, gsum, width):
    n, d = x2.shape
    tm = _TOKEN_TILE
    full = lambda a: pl.BlockSpec(a.shape, lambda i: (0,) * a.ndim)
    row_sd = jax.ShapeDtypeStruct((n, width), _BF16)
    row_spec = pl.BlockSpec((tm, width), lambda i: (i, 0))
    gran = tm // _KEY_GRANULE
    vrows = _V_DIM
    return pl.pallas_call(
        functools.partial(_inproj_body, width=width),
        grid=(n // tm,),
        in_specs=[pl.BlockSpec((tm, d), lambda i: (i, 0)), full(norm_gain), full(w_ku), full(w_qvt),
                  full(gqt), full(gk), full(gsum)],
        out_specs=[pl.BlockSpec((width, tm), lambda i: (0, i)), row_spec,
                   pl.BlockSpec((gran, _HEADS, vrows, _KEY_GRANULE), lambda i: (i, 0, 0, 0)), row_spec],
        out_shape=[jax.ShapeDtypeStruct((width, n), _BF16), row_sd,
                   jax.ShapeDtypeStruct((n // _KEY_GRANULE, _HEADS, vrows, _KEY_GRANULE), _BF16), row_sd],
        compiler_params=pltpu.CompilerParams(
            dimension_semantics=("parallel",), vmem_limit_bytes=_VMEM_LIMIT),
        name="inproj",
    )(x2, norm_gain, w_ku, w_qvt, gqt, gk, gsum)


def _t5_bucket_np(rel):
    nb = _REL_BUCKETS // 2
    max_exact = nb // 2
    side = np.where(rel > 0, nb, 0)
    n = np.abs(rel)
    nf = np.maximum(n, 1).astype(np.float32)
    large = max_exact + (np.log(nf / np.float32(max_exact)) / np.float32(math.log(_REL_MAX_DIST / max_exact))
                         * np.float32(nb - max_exact)).astype(np.int32)
    large = np.minimum(large, nb - 1)
    return side + np.where(n < max_exact, n, large)


def _bucket_blocks():
    assert _BIAS_BLOCK % _CHUNK == 0
    i = np.arange(_BIAS_BLOCK)[None, :]
    j = np.arange(_BIAS_BLOCK)[:, None]
    far_bucket = _REL_BUCKETS // 2 - 1
    assert (_t5_bucket_np(j - i - 2 * _BIAS_BLOCK) == far_bucket).all()
    diag = np.where((j // _CHUNK) <= (i // _CHUNK), _t5_bucket_np(j - i), -1)
    prev = _t5_bucket_np(j - i - _BIAS_BLOCK)
    return np.stack([diag, prev]).astype(np.int32)


def _bias_prep_body(tab_ref, tabv_ref, bucket_ref, lam4_ref, gains_ref,
                    bias_ref, lam_ref, *, lam_init, t):
    h = pl.program_id(0)
    far_bucket = _REL_BUCKETS // 2 - 1
    bkt = bucket_ref[...]
    far = tab_ref[far_bucket, h]
    val = jnp.full(bkt.shape, -jnp.inf, _F32)
    for b in range(_REL_BUCKETS):
        val = jnp.where(bkt == b, (tab_ref[b, h] - far) * _LOG2E, val)
    diag_block, prev_block = val[0], val[1]
    n_blk = t // _BIAS_BLOCK
    bias_ref[0, 0] = jnp.zeros((t, t), _F32)
    bias_ref[0, 1] = jnp.zeros((t, t), _F32)
    bias_ref[0, 1, t - _BIAS_BLOCK:, :_BIAS_BLOCK] = prev_block
    for bj in range(n_blk):
        rows = slice(bj * _BIAS_BLOCK, (bj + 1) * _BIAS_BLOCK)
        for bi in range(n_blk):
            cols = slice(bi * _BIAS_BLOCK, (bi + 1) * _BIAS_BLOCK)
            if bi == bj:
                block = diag_block
            elif bi == bj + 1:
                block = prev_block
            else:
                block = jnp.full((_BIAS_BLOCK, _BIAS_BLOCK), 0.0 if bi > bj else -jnp.inf, _F32)
            bias_ref[0, 2, rows, cols] = block
    l4 = lam4_ref[...]
    s1 = jnp.sum(l4[0:1] * l4[1:2], axis=-1, keepdims=True)
    s2 = jnp.sum(l4[2:3] * l4[3:4], axis=-1, keepdims=True)
    lam_ref[0:1] = jnp.broadcast_to(jnp.exp(s1) - jnp.exp(s2) + lam_init, (1, lam_ref.shape[1]))
    gmax = jnp.max(jnp.abs(gains_ref[...]), axis=-1, keepdims=True)
    qk_bound = (_SUB_DIM * _BF16_SLACK) * gmax[0:1] * gmax[1:2]
    tv = tabv_ref[...]
    dev = jnp.abs(tv - tv[far_bucket:far_bucket + 1]) * _LOG2E
    bias_bound = jnp.max(jnp.max(dev, axis=-1, keepdims=True), axis=0, keepdims=True)
    bounded = (qk_bound + bias_bound <= _SAFE_LOG2_RANGE).astype(_F32)
    lam_ref[1:2] = jnp.broadcast_to(bounded, (1, lam_ref.shape[1]))


def _bias_prep(rel_table, lam4, gains, lam_init, t):
    assert t % _BIAS_BLOCK == 0
    buckets = jnp.asarray(_bucket_blocks())
    whole = lambda a: pl.BlockSpec(a.shape, lambda h: (0,) * a.ndim)
    return pl.pallas_call(
        functools.partial(_bias_prep_body, lam_init=lam_init, t=t),
        grid=(_HEADS,),
        in_specs=[pl.BlockSpec(memory_space=pltpu.SMEM), whole(rel_table), whole(buckets),
                  whole(lam4), whole(gains)],
        out_specs=[pl.BlockSpec((1, 3, t, t), lambda h: (h, 0, 0, 0)),
                   pl.BlockSpec((2, _V_DIM), lambda h: (0, 0))],
        out_shape=[jax.ShapeDtypeStruct((_HEADS, 3, t, t), _F32),
                   jax.ShapeDtypeStruct((2, _V_DIM), _F32)],
        compiler_params=pltpu.CompilerParams(dimension_semantics=("arbitrary",)),
        name="attn_bias_prep",
    )(rel_table, rel_table, buckets, lam4, gains)


def _attn_body(qt_ref, k_ref, vt_ref, bias_ref, lam_ref, gain_ref, o_ref,
               qw_ref, acc_ref, den_ref, *, t, nq, online):
    gran = t // _KEY_GRANULE
    assert gran == 2
    half = _KEY_GRANULE
    row = lax.broadcasted_iota(jnp.int32, (_V_DIM, t), 0)
    for q in range(nq):
        qt = qt_ref[:, q * t:(q + 1) * t]
        zero = jnp.zeros_like(qt)
        qw_ref[q, :, :t] = jnp.where(row < _SUB_DIM, qt, zero)
        qw_ref[q, :, t:] = jnp.where(row >= _SUB_DIM, qt, zero)
    acc_ref[...] = jnp.zeros(acc_ref.shape, _F32)
    den_ref[...] = jnp.zeros(den_ref.shape, _F32)

    def key_sum(e):
        return jnp.sum(e.reshape(e.shape[0] // _SUBLANES, _SUBLANES, e.shape[1]), axis=0)

    def diagonal(kj):
        late = [slice(half, t), slice(t + half, 2 * t)]
        k0 = kj * t
        bias = bias_ref[0, 2]
        st0 = jnp.dot(k_ref[0, k0:k0 + half, :], qw_ref[kj], preferred_element_type=_F32)
        e0 = [jnp.exp2(st0[:, :t] + bias[:half]), jnp.exp2(st0[:, t:] + bias[:half])]
        den_ref[kj] += jnp.concatenate([key_sum(e0[0]), key_sum(e0[1])], axis=1)
        pt0 = jnp.concatenate([e0[0].astype(_BF16), e0[1].astype(_BF16)], axis=1)
        qw_late = jnp.concatenate([qw_ref[kj, :, lanes] for lanes in late], axis=1)
        st1 = jnp.dot(k_ref[0, k0 + half:k0 + t, :], qw_late, preferred_element_type=_F32)
        e1 = [jnp.exp2(st1[:, :half] + bias[half:, half:]), jnp.exp2(st1[:, half:] + bias[half:, half:])]
        pt1 = jnp.concatenate([e1[0].astype(_BF16), e1[1].astype(_BF16)], axis=1)
        acc_ref[kj] += jnp.dot(vt_ref[kj * gran, 0], pt0, preferred_element_type=_F32)
        pv1 = jnp.dot(vt_ref[kj * gran + 1, 0], pt1, preferred_element_type=_F32)
        for n, lanes in enumerate(late):
            acc_ref[kj, :, lanes] += pv1[:, n * half:(n + 1) * half]
            den_ref[kj, :, lanes] += key_sum(e1[n])

    def later_queries(kj, q_tiles):
        qw = jnp.concatenate([qw_ref[q] for q in q_tiles], axis=1)
        st = jnp.dot(k_ref[0, kj * t:(kj + 1) * t, :], qw, preferred_element_type=_F32)
        if q_tiles[0] == kj + 1:
            blk = _BIAS_BLOCK
            corner = bias_ref[0, 1, t - blk:, :blk]
            tail = st[t - blk:]
            tail = jnp.concatenate([tail[:, :blk] + corner, tail[:, blk:t],
                                    tail[:, t:t + blk] + corner, tail[:, t + blk:]], axis=1)
            st = jnp.concatenate([st[:t - blk], tail], axis=0)
        e = jnp.exp2(st)
        pv = (jnp.dot(vt_ref[kj * gran, 0], e[:half].astype(_BF16), preferred_element_type=_F32)
              + jnp.dot(vt_ref[kj * gran + 1, 0], e[half:].astype(_BF16), preferred_element_type=_F32))
        for n, q in enumerate(q_tiles):
            lanes = slice(n * 2 * t, (n + 1) * 2 * t)
            den_ref[q] += key_sum(e[:, lanes])
            acc_ref[q] += pv[:, lanes]

    def write_outputs():
        for q in range(nq):
            o = acc_ref[q] / jnp.sum(den_ref[q], axis=0, keepdims=True)
            od = o[:, :t] - lam_ref[0:1, 0:1] * o[:, t:]
            scale = lax.rsqrt(jnp.mean(od * od, axis=0, keepdims=True) + _EPS)
            o_ref[0, q * t:(q + 1) * t, :] = (od * scale * gain_ref[...]).T.astype(_BF16)

    def online_sweep(m_ref):
        m_ref[...] = jnp.full(m_ref.shape, -jnp.inf, _F32)

        def query_tile(q, carry):
            def key_tile(kj, inner):
                kt = k_ref[0, pl.ds(pl.multiple_of(kj * t, t), t), :]
                bias = bias_ref[0, jnp.where(kj == q, 2, jnp.where(kj == q - 1, 1, 0))]
                for s in range(2):
                    lanes = slice(s * t, (s + 1) * t)
                    st = jnp.dot(kt, qw_ref[q, :, lanes], preferred_element_type=_F32) + bias
                    m_old = m_ref[q, :, lanes]
                    m_new = jnp.maximum(m_old, jnp.max(st, axis=0, keepdims=True))
                    alpha = jnp.exp2(m_old - m_new)
                    e = jnp.exp2(st - m_new)
                    pv = (jnp.dot(vt_ref[kj * gran, 0], e[:half].astype(_BF16),
                                  preferred_element_type=_F32)
                          + jnp.dot(vt_ref[kj * gran + 1, 0], e[half:].astype(_BF16),
                                    preferred_element_type=_F32))
                    den_ref[q, :, lanes] = alpha * den_ref[q, :, lanes] + key_sum(e)
                    acc_ref[q, :, lanes] = alpha * acc_ref[q, :, lanes] + pv
                    m_ref[q, :, lanes] = m_new
                return inner

            lax.fori_loop(0, q + 1, key_tile, 0)
            return carry

        lax.fori_loop(0, nq, query_tile, 0)

    if online:
        pl.run_scoped(online_sweep, pltpu.VMEM((nq, 1, 2 * t), _F32))
    else:
        for kj in range(nq):
            diagonal(kj)
            later = list(range(kj + 1, nq))
            for g0 in range(0, len(later), _QUERY_GROUP):
                later_queries(kj, later[g0:g0 + _QUERY_GROUP])
    write_outputs()


def _attention(qt, k, vt, bias, lam, gain_t):
    b, l, w = k.shape
    t = _ATTN_TILE
    nq = l // t

    def sweep(online):
        return pl.pallas_call(
            functools.partial(_attn_body, t=t, nq=nq, online=online),
            grid=(_HEADS, b),
            in_specs=[pl.BlockSpec((_V_DIM, l), lambda h, bi: (h, bi)),
                      pl.BlockSpec((1, l, _V_DIM), lambda h, bi: (bi, 0, h)),
                      pl.BlockSpec((l // _KEY_GRANULE, 1, _V_DIM, _KEY_GRANULE),
                                   lambda h, bi: (bi, h, 0, 0)),
                      pl.BlockSpec((1, 3, t, t), lambda h, bi: (h, 0, 0, 0)),
                      pl.BlockSpec(lam.shape, lambda h, bi: (0, 0)),
                      pl.BlockSpec(gain_t.shape, lambda h, bi: (0, 0))],
            out_specs=pl.BlockSpec((1, l, _V_DIM), lambda h, bi: (bi, 0, h)),
            out_shape=jax.ShapeDtypeStruct((b, l, w), _BF16),
            scratch_shapes=[pltpu.VMEM((nq, _V_DIM, 2 * t), _BF16),
                            pltpu.VMEM((nq, _V_DIM, 2 * t), _F32),
                            pltpu.VMEM((nq, _SUBLANES, 2 * t), _F32)],
            compiler_params=pltpu.CompilerParams(
                dimension_semantics=("parallel", "parallel"), vmem_limit_bytes=_VMEM_LIMIT),
            name="diff_attention_online" if online else "diff_attention",
        )(qt, k, vt, bias, lam, gain_t)

    return lax.cond(lam[1, 0] > 0, lambda: sweep(False), lambda: sweep(True))


def _s5_prep_body(are_ref, aim_ref, ldt_ref, bre_ref, bim_ref,
                  lbre_ref, lbim_ref, bbre_ref, bbim_ref):
    a_re = are_ref[...]
    a_im = aim_ref[...]
    dt = jnp.exp(ldt_ref[...])
    decay = jnp.exp(a_re * dt)
    lb_re = decay * jnp.cos(a_im * dt)
    lb_im = decay * jnp.sin(a_im * dt)
    nr = lb_re - 1.0
    ni = lb_im
    den = a_re * a_re + a_im * a_im
    q_re = (nr * a_re + ni * a_im) / den
    q_im = (ni * a_re - nr * a_im) / den
    b_re = bre_ref[...]
    b_im = bim_ref[...]
    bbre_ref[...] = q_re * b_re - q_im * b_im
    bbim_ref[...] = q_re * b_im + q_im * b_re
    lbre_ref[...] = lb_re
    lbim_ref[...] = lb_im


def _s5_prep(a_re, a_im, log_dt, b_re, b_im):
    sd = jax.ShapeDtypeStruct(a_re.shape, _F32)
    return pl.pallas_call(_s5_prep_body, out_shape=[sd] * 4, name="s5_discretise")(
        a_re, a_im, log_dt, b_re, b_im)


def _s5_body(u_ref, wbre_ref, wbim_ref, lbre_ref, lbim_ref, wcre_ref, wcim_ref, d_ref,
             gw_ref, gb_ref, o_ref, xre_ref, xim_ref, cre_ref, cim_ref, *, steps, nb):
    rows = nb * steps
    width = u_ref.shape[-1]
    n_tiles, _, tile_w = lbre_ref.shape
    halves = wcre_ref.shape[0]
    tiles_per_half = n_tiles // halves
    hw_in = width // halves

    @pl.when(pl.program_id(0) == 0)
    def _():
        cre_ref[...] = jnp.zeros(cre_ref.shape, _F32)
        cim_ref[...] = jnp.zeros(cim_ref.shape, _F32)

    u_tm = jnp.swapaxes(u_ref[...].astype(_F32), 0, 1).reshape(rows, width).astype(_BF16)

    for hf in range(halves):
        uh = u_tm[:, hf * hw_in:(hf + 1) * hw_in]
        bu_r = jnp.dot(uh, wbre_ref[hf], preferred_element_type=_F32)
        bu_i = jnp.dot(uh, wbim_ref[hf], preferred_element_type=_F32)
        group = range(hf * tiles_per_half, (hf + 1) * tiles_per_half)
        a_r = [jnp.broadcast_to(lbre_ref[j], (nb, tile_w)) for j in group]
        a_i = [jnp.broadcast_to(lbim_ref[j], (nb, tile_w)) for j in group]
        x_r = [cre_ref[j] for j in group]
        x_i = [cim_ref[j] for j in group]
        for ti in range(steps):
            r = slice(ti * nb, (ti + 1) * nb)
            for q, j in enumerate(group):
                lanes = slice(q * tile_w, (q + 1) * tile_w)
                n_r = a_r[q] * x_r[q] - a_i[q] * x_i[q] + bu_r[r, lanes]
                n_i = a_r[q] * x_i[q] + a_i[q] * x_r[q] + bu_i[r, lanes]
                xre_ref[j, r, :] = n_r
                xim_ref[j, r, :] = n_i
                x_r[q], x_i[q] = n_r, n_i
        for q, j in enumerate(group):
            cre_ref[j] = x_r[q]
            cim_ref[j] = x_i[q]

    ys = []
    for hf in range(halves):
        tiles = range(hf * tiles_per_half, (hf + 1) * tiles_per_half)
        x_r = jnp.concatenate([xre_ref[j] for j in tiles], axis=1).astype(_BF16)
        x_i = jnp.concatenate([xim_ref[j] for j in tiles], axis=1).astype(_BF16)
        ys.append(jnp.dot(x_r, wcre_ref[hf], preferred_element_type=_F32)
                  + jnp.dot(x_i, wcim_ref[hf], preferred_element_type=_F32))
    y = jnp.concatenate(ys, axis=-1) + d_ref[...] * u_tm.astype(_F32)
    g = jax.nn.gelu(y)
    o = g * jax.nn.sigmoid(jnp.dot(g.astype(_BF16), gw_ref[...], preferred_element_type=_F32)
                           + gb_ref[...])
    o_ref[...] = jnp.swapaxes(o.reshape(steps, nb, width), 0, 1).astype(_BF16)


def _s5(u, wb_re, wb_im, lb_re, lb_im, wc_re, wc_im, d_vec, glu_w, glu_b):
    nb, l, width = u.shape
    steps = _SCAN_STEPS
    n_tiles, _, tile_w = lb_re.shape
    full = lambda a: pl.BlockSpec(a.shape, lambda i: (0,) * a.ndim)
    io_spec = pl.BlockSpec((nb, steps, width), lambda i: (0, i, 0))
    consts = (wb_re, wb_im, lb_re, lb_im, wc_re, wc_im, d_vec, glu_w, glu_b)
    state_buf = pltpu.VMEM((n_tiles, nb * steps, tile_w), _F32)
    return pl.pallas_call(
        functools.partial(_s5_body, steps=steps, nb=nb),
        grid=(l // steps,),
        in_specs=[io_spec] + [full(a) for a in consts],
        out_specs=io_spec,
        out_shape=jax.ShapeDtypeStruct(u.shape, _BF16),
        scratch_shapes=[state_buf, state_buf,
                        pltpu.VMEM((n_tiles, nb, tile_w), _F32),
                        pltpu.VMEM((n_tiles, nb, tile_w), _F32)],
        compiler_params=pltpu.CompilerParams(
            dimension_semantics=("arbitrary",), vmem_limit_bytes=_VMEM_LIMIT),
        name="s5_branch",
    )(u, *consts)


def _outproj_body(x_ref, ng_ref, wg_ref, mb_ref, oa_ref, os_ref, pa_ref, ps_ref, wo_ref,
                  out_ref, *, width):
    x = x_ref[...]
    d = x.shape[-1]
    h = (x * _rms_scale(x) * ng_ref[...]).astype(_BF16)
    zg = jnp.dot(h, wg_ref[...], preferred_element_type=_F32)
    o_a = (oa_ref[...].astype(_F32) * jax.nn.silu(zg[:, :width])).astype(_BF16)
    o_s = (os_ref[...].astype(_F32) * jax.nn.silu(zg[:, width:2 * width])).astype(_BF16)
    p_a = jnp.dot(o_a, pa_ref[...], preferred_element_type=_F32)
    p_s = jnp.dot(o_s, ps_ref[...], preferred_element_type=_F32)
    g = jax.nn.sigmoid(zg[:, 2 * width:] + mb_ref[...])
    merged = g[:, :d] * p_a + g[:, d:] * p_s
    out_ref[...] = x + jnp.dot(merged.astype(_BF16), wo_ref[...], preferred_element_type=_F32)


def _outproj(x2, norm_gain, w_gates, merge_b, o_a, o_s, proj_a, proj_s, w_out, width):
    n, d = x2.shape
    tm = _TOKEN_TILE
    full = lambda a: pl.BlockSpec(a.shape, lambda i: (0,) * a.ndim)
    row = lambda a: pl.BlockSpec((tm, a.shape[-1]), lambda i: (i, 0))
    return pl.pallas_call(
        functools.partial(_outproj_body, width=width),
        grid=(n // tm,),
        in_specs=[row(x2), full(norm_gain), full(w_gates), full(merge_b), row(o_a), row(o_s),
                  full(proj_a), full(proj_s), full(w_out)],
        out_specs=row(x2),
        out_shape=jax.ShapeDtypeStruct(x2.shape, x2.dtype),
        compiler_params=pltpu.CompilerParams(
            dimension_semantics=("parallel",), vmem_limit_bytes=_VMEM_LIMIT),
        name="outproj",
    )(x2, norm_gain, w_gates, merge_b, o_a, o_s, proj_a, proj_s, w_out)


def _block_diag_halves(blocks, halves):
    g, r, c = blocks.shape
    gh = g // halves
    on_diagonal = (np.arange(gh * r)[:, None] // r) == (np.arange(gh * c)[None, :] // c)
    tiled = jnp.tile(blocks.reshape(halves, gh * r, c), (1, 1, gh))
    return jnp.where(on_diagonal, tiled, 0)


def _layer(x, lam_init, norm_gain, w_in, merge_gate_b, q_norm_gain, k_norm_gain,
           lambda_q1, lambda_k1, lambda_q2, lambda_k2, diff_subln_gain, rel_bias_table,
           ssm_a_re, ssm_a_im, ssm_log_dt, ssm_b_re, ssm_b_im, ssm_c_re, ssm_c_im,
           ssm_d, ssm_glu_w, ssm_glu_b, proj_attn, proj_ssm, w_out):
    nb, l, d = x.shape
    n = nb * l
    aw = _HEADS * 2 * _SUB_DIM
    groups = ssm_a_re.shape[0]
    x2 = x.reshape(n, d)
    ng = norm_gain.reshape(1, d).astype(_F32)

    w_ku = jnp.concatenate([w_in[:, aw:2 * aw], w_in[:, 4 * aw:5 * aw]], axis=1).astype(_BF16)
    w_qvt = jnp.concatenate([w_in[:, :aw], w_in[:, 2 * aw:3 * aw]], axis=1).T.astype(_BF16)
    w_gates = jnp.concatenate([w_in[:, 3 * aw:4 * aw], w_in[:, 5 * aw:]], axis=1).astype(_BF16)
    gq = jnp.tile(q_norm_gain.astype(_F32), 2 * _HEADS) * (_SUB_DIM ** -0.5 * _LOG2E)
    gqt = jnp.broadcast_to(gq[:, None], (aw, _TOKEN_TILE))
    gk = jnp.tile(k_norm_gain.astype(_F32), 2 * _HEADS).reshape(1, aw)
    seg = np.arange(_MXU_TILE) // _SUB_DIM
    gsum = jnp.asarray((seg[:, None] == seg[None, :]).astype(np.float32), _BF16)
    qt, k, vt, u = _inproj(x2, ng, w_ku, w_qvt, gqt, gk, gsum, aw)

    lam4 = jnp.stack([lambda_q1, lambda_k1, lambda_q2, lambda_k2]).astype(_F32)
    qk_gains = jnp.stack([gq[:_SUB_DIM], gk[0, :_SUB_DIM]])
    bias, lam = _bias_prep(rel_bias_table.astype(_F32), lam4, qk_gains, lam_init, _ATTN_TILE)
    subln = diff_subln_gain.astype(_F32) * (1.0 - lam_init)
    subln_t = jnp.broadcast_to(subln[:, None], (_V_DIM, _ATTN_TILE))
    shp = (nb, l, aw)
    o_a = _attention(qt, k.reshape(shp), vt, bias, lam, subln_t)

    rep = lambda a: jnp.repeat(a.astype(_F32), _SSM_GROUP, axis=0)
    ldt = jnp.broadcast_to(ssm_log_dt.astype(_F32)[:, None], ssm_a_re.shape)
    bt = lambda a: a.astype(_F32).transpose(0, 2, 1).reshape(groups * _SSM_GROUP, _SSM_STATE)
    lb_re, lb_im, bb_re, bb_im = _s5_prep(rep(ssm_a_re), rep(ssm_a_im), rep(ldt),
                                          bt(ssm_b_re), bt(ssm_b_im))
    n_tiles = groups * _SSM_STATE // _STATE_TILE
    flat = lambda a: a[::_SSM_GROUP].reshape(n_tiles, 1, _STATE_TILE)
    gshape = (groups, _SSM_GROUP, _SSM_STATE)
    wb_re = _block_diag_halves(bb_re.reshape(gshape), 2).astype(_BF16)
    wb_im = _block_diag_halves(bb_im.reshape(gshape), 2).astype(_BF16)
    wc_re = _block_diag_halves(ssm_c_re.astype(_F32).transpose(0, 2, 1), 2).astype(_BF16)
    wc_im = _block_diag_halves(-ssm_c_im.astype(_F32).transpose(0, 2, 1), 2).astype(_BF16)
    o_s = _s5(u.reshape(shp), wb_re, wb_im, flat(lb_re), flat(lb_im), wc_re, wc_im,
              ssm_d.astype(_F32).reshape(1, aw), ssm_glu_w.astype(_BF16),
              ssm_glu_b.astype(_F32).reshape(1, aw))

    out = _outproj(x2, ng, w_gates, merge_gate_b.astype(_F32).reshape(1, 2 * d),
                   o_a.reshape(n, aw), o_s.reshape(n, aw),
                   proj_attn.astype(_BF16), proj_ssm.astype(_BF16), w_out.astype(_BF16), aw)
    return out.reshape(nb, l, d)


def kernel(x, norm_gain, w_in, merge_gate_b, q_norm_gain, k_norm_gain, lambda_q1, lambda_k1,
           lambda_q2, lambda_k2, diff_subln_gain, rel_bias_table, ssm_A_re, ssm_A_im, ssm_log_dt,
           ssm_B_re, ssm_B_im, ssm_C_re, ssm_C_im, ssm_D, ssm_glu_w, ssm_glu_b,
           proj_attn, proj_ssm, w_out):
    per_layer = (norm_gain, w_in, merge_gate_b, q_norm_gain, k_norm_gain, lambda_q1, lambda_k1,
                 lambda_q2, lambda_k2, diff_subln_gain)
    per_layer_tail = (ssm_A_re, ssm_A_im, ssm_log_dt, ssm_B_re, ssm_B_im, ssm_C_re, ssm_C_im,
                      ssm_D, ssm_glu_w, ssm_glu_b, proj_attn, proj_ssm, w_out)
    for layer in range(norm_gain.shape[0]):
        lam_init = 0.8 - 0.6 * math.exp(-0.3 * layer)
        x = _layer(x, lam_init, *(p[layer] for p in per_layer), rel_bias_table,
                   *(p[layer] for p in per_layer_tail))
    return x
```

```python
import functools
import math

import jax
import jax.numpy as jnp
import numpy as np
from jax import lax
from jax.experimental import pallas as pl
from jax.experimental.pallas import tpu as pltpu

_F32 = jnp.float32
_BF16 = jnp.bfloat16

_CHUNK = 64
_HEADS = 4
_SUB_DIM = 64
_V_DIM = 128
_SSM_GROUP = 16
_SSM_STATE = 64
_REL_BUCKETS = 32
_REL_MAX_DIST = 128
_EPS = 1e-6
_LOG2E = math.log2(math.e)
_SAFE_LOG2_RANGE = 100.0
_BF16_SLACK = 1.02

_VMEM_LIMIT = 48 * 1024 * 1024
_MXU_TILE = 256
_SUBLANES = 8
_TOKEN_TILE = 1024
_ATTN_TILE = 512
_KEY_GRANULE = 256
_BIAS_BLOCK = 128
_QUERY_GROUP = 2
_SCAN_STEPS = 128
_STATE_TILE = 256
_SCAN_STRIDE = 4
_CHANNEL_BLOCK = 128


def _rms_scale(x, eps=_EPS):
    return lax.rsqrt(jnp.mean(x * x, axis=-1, keepdims=True) + eps)


def _inproj_body(x_ref, ng_ref, wku_ref, wqvt_ref, gqt_ref, gk_ref, gsum_ref,
                 qt_ref, k_ref, vt_ref, u_ref, *, width):
    x = x_ref[...]
    tm = x.shape[0]
    h = (x * _rms_scale(x) * ng_ref[...]).astype(_BF16)
    z = jnp.dot(h, wku_ref[...], preferred_element_type=_F32)
    zt = lax.dot_general(wqvt_ref[...], h, (((1,), (1,)), ((), ())),
                         preferred_element_type=_F32)

    kk = z[:, :width]
    k2 = (kk * kk).astype(_BF16)
    mxu = gsum_ref.shape[0]
    ss = jnp.concatenate([jnp.dot(k2[:, c:c + mxu], gsum_ref[...], preferred_element_type=_F32)
                          for c in range(0, width, mxu)], axis=1)
    k_ref[...] = (kk * lax.rsqrt(ss * (1.0 / _SUB_DIM) + _EPS) * gk_ref[...]).astype(_BF16)
    u_ref[...] = z[:, width:].astype(_BF16)

    qt = zt[:width].reshape(width // _SUB_DIM, _SUB_DIM, tm)
    ms = jnp.mean(qt * qt, axis=1, keepdims=True)
    qt_ref[...] = ((qt * lax.rsqrt(ms + _EPS)).reshape(width, tm) * gqt_ref[...]).astype(_BF16)

    vt = zt[width:].astype(_BF16)
    for j in range(tm // _KEY_GRANULE):
        for hd in range(_HEADS):
            vt_ref[j, hd] = vt[hd * _V_DIM:(hd + 1) * _V_DIM,
                               j * _KEY_GRANULE:(j + 1) * _KEY_GRANULE]


def _inproj(x2, norm_gain, w_ku, w_qvt, gqt, gk, gsum, width):
    n, d = x2.shape
    tm = _TOKEN_TILE
    full = lambda a: pl.BlockSpec(a.shape, lambda i: (0,) * a.ndim)
    row_sd = jax.ShapeDtypeStruct((n, width), _BF16)
    row_spec = pl.BlockSpec((tm, width), lambda i: (i, 0))
    gran = tm // _KEY_GRANULE
    vrows = _V_DIM
    return pl.pallas_call(
        functools.partial(_inproj_body, width=width),
        grid=(n // tm,),
        in_specs=[pl.BlockSpec((tm, d), lambda i: (i, 0)), full(norm_gain), full(w_ku), full(w_qvt),
                  full(gqt), full(gk), full(gsum)],
        out_specs=[pl.BlockSpec((width, tm), lambda i: (0, i)), row_spec,
                   pl.BlockSpec((gran, _HEADS, vrows, _KEY_GRANULE), lambda i: (i, 0, 0, 0)), row_spec],
        out_shape=[jax.ShapeDtypeStruct((width, n), _BF16), row_sd,
                   jax.ShapeDtypeStruct((n // _KEY_GRANULE, _HEADS, vrows, _KEY_GRANULE), _BF16), row_sd],
        compiler_params=pltpu.CompilerParams(
            dimension_semantics=("parallel",), vmem_limit_bytes=_VMEM_LIMIT),
        name="inproj",
    )(x2, norm_gain, w_ku, w_qvt, gqt, gk, gsum)


def _t5_bucket_np(rel):
    nb = _REL_BUCKETS // 2
    max_exact = nb // 2
    side = np.where(rel > 0, nb, 0)
    n = np.abs(rel)
    nf = np.maximum(n, 1).astype(np.float32)
    large = max_exact + (np.log(nf / np.float32(max_exact)) / np.float32(math.log(_REL_MAX_DIST / max_exact))
                         * np.float32(nb - max_exact)).astype(np.int32)
    large = np.minimum(large, nb - 1)
    return side + np.where(n < max_exact, n, large)


def _bucket_blocks():
    assert _BIAS_BLOCK % _CHUNK == 0
    i = np.arange(_BIAS_BLOCK)[None, :]
    j = np.arange(_BIAS_BLOCK)[:, None]
    far_bucket = _REL_BUCKETS // 2 - 1
    assert (_t5_bucket_np(j - i - 2 * _BIAS_BLOCK) == far_bucket).all()
    diag = np.where((j // _CHUNK) <= (i // _CHUNK), _t5_bucket_np(j - i), -1)
    prev = _t5_bucket_np(j - i - _BIAS_BLOCK)
    return np.stack([diag, prev]).astype(np.int32)


def _bias_prep_body(tab_ref, tabv_ref, bucket_ref, lam4_ref, gains_ref,
                    bias_ref, lam_ref, *, lam_init, t):
    h = pl.program_id(0)
    far_bucket = _REL_BUCKETS // 2 - 1
    bkt = bucket_ref[...]
    far = tab_ref[far_bucket, h]
    val = jnp.full(bkt.shape, -jnp.inf, _F32)
    for b in range(_REL_BUCKETS):
        val = jnp.where(bkt == b, (tab_ref[b, h] - far) * _LOG2E, val)
    diag_block, prev_block = val[0], val[1]
    n_blk = t // _BIAS_BLOCK
    bias_ref[0, 0] = jnp.zeros((t, t), _F32)
    bias_ref[0, 1] = jnp.zeros((t, t), _F32)
    bias_ref[0, 1, t - _BIAS_BLOCK:, :_BIAS_BLOCK] = prev_block
    for bj in range(n_blk):
        rows = slice(bj * _BIAS_BLOCK, (bj + 1) * _BIAS_BLOCK)
        for bi in range(n_blk):
            cols = slice(bi * _BIAS_BLOCK, (bi + 1) * _BIAS_BLOCK)
            if bi == bj:
                block = diag_block
            elif bi == bj + 1:
                block = prev_block
            else:
                block = jnp.full((_BIAS_BLOCK, _BIAS_BLOCK), 0.0 if bi > bj else -jnp.inf, _F32)
            bias_ref[0, 2, rows, cols] = block
    l4 = lam4_ref[...]
    s1 = jnp.sum(l4[0:1] * l4[1:2], axis=-1, keepdims=True)
    s2 = jnp.sum(l4[2:3] * l4[3:4], axis=-1, keepdims=True)
    lam_ref[0:1] = jnp.broadcast_to(jnp.exp(s1) - jnp.exp(s2) + lam_init, (1, lam_ref.shape[1]))
    gmax = jnp.max(jnp.abs(gains_ref[...]), axis=-1, keepdims=True)
    qk_bound = (_SUB_DIM * _BF16_SLACK) * gmax[0:1] * gmax[1:2]
    tv = tabv_ref[...]
    dev = jnp.abs(tv - tv[far_bucket:far_bucket + 1]) * _LOG2E
    bias_bound = jnp.max(jnp.max(dev, axis=-1, keepdims=True), axis=0, keepdims=True)
    bounded = (qk_bound + bias_bound <= _SAFE_LOG2_RANGE).astype(_F32)
    lam_ref[1:2] = jnp.broadcast_to(bounded, (1, lam_ref.shape[1]))


def _bias_prep(rel_table, lam4, gains, lam_init, t):
    assert t % _BIAS_BLOCK == 0
    buckets = jnp.asarray(_bucket_blocks())
    whole = lambda a: pl.BlockSpec(a.shape, lambda h: (0,) * a.ndim)
    return pl.pallas_call(
        functools.partial(_bias_prep_body, lam_init=lam_init, t=t),
        grid=(_HEADS,),
        in_specs=[pl.BlockSpec(memory_space=pltpu.SMEM), whole(rel_table), whole(buckets),
                  whole(lam4), whole(gains)],
        out_specs=[pl.BlockSpec((1, 3, t, t), lambda h: (h, 0, 0, 0)),
                   pl.BlockSpec((2, _V_DIM), lambda h: (0, 0))],
        out_shape=[jax.ShapeDtypeStruct((_HEADS, 3, t, t), _F32),
                   jax.ShapeDtypeStruct((2, _V_DIM), _F32)],
        compiler_params=pltpu.CompilerParams(dimension_semantics=("arbitrary",)),
        name="attn_bias_prep",
    )(rel_table, rel_table, buckets, lam4, gains)


def _attn_body(qt_ref, k_ref, vt_ref, bias_ref, lam_ref, gain_ref, o_ref,
               qw_ref, acc_ref, den_ref, *, t, nq, online):
    gran = t // _KEY_GRANULE
    assert gran == 2
    half = _KEY_GRANULE
    row = lax.broadcasted_iota(jnp.int32, (_V_DIM, t), 0)
    for q in range(nq):
        qt = qt_ref[:, q * t:(q + 1) * t]
        zero = jnp.zeros_like(qt)
        qw_ref[q, :, :t] = jnp.where(row < _SUB_DIM, qt, zero)
        qw_ref[q, :, t:] = jnp.where(row >= _SUB_DIM, qt, zero)
    acc_ref[...] = jnp.zeros(acc_ref.shape, _F32)
    den_ref[...] = jnp.zeros(den_ref.shape, _F32)

    def key_sum(e):
        return jnp.sum(e.reshape(e.shape[0] // _SUBLANES, _SUBLANES, e.shape[1]), axis=0)

    def diagonal(kj):
        late = [slice(half, t), slice(t + half, 2 * t)]
        k0 = kj * t
        bias = bias_ref[0, 2]
        st0 = jnp.dot(k_ref[0, k0:k0 + half, :], qw_ref[kj], preferred_element_type=_F32)
        e0 = [jnp.exp2(st0[:, :t] + bias[:half]), jnp.exp2(st0[:, t:] + bias[:half])]
        den_ref[kj] += jnp.concatenate([key_sum(e0[0]), key_sum(e0[1])], axis=1)
        pt0 = jnp.concatenate([e0[0].astype(_BF16), e0[1].astype(_BF16)], axis=1)
        qw_late = jnp.concatenate([qw_ref[kj, :, lanes] for lanes in late], axis=1)
        st1 = jnp.dot(k_ref[0, k0 + half:k0 + t, :], qw_late, preferred_element_type=_F32)
        e1 = [jnp.exp2(st1[:, :half] + bias[half:, half:]), jnp.exp2(st1[:, half:] + bias[half:, half:])]
        pt1 = jnp.concatenate([e1[0].astype(_BF16), e1[1].astype(_BF16)], axis=1)
        acc_ref[kj] += jnp.dot(vt_ref[kj * gran, 0], pt0, preferred_element_type=_F32)
        pv1 = jnp.dot(vt_ref[kj * gran + 1, 0], pt1, preferred_element_type=_F32)
        for n, lanes in enumerate(late):
            acc_ref[kj, :, lanes] += pv1[:, n * half:(n + 1) * half]
            den_ref[kj, :, lanes] += key_sum(e1[n])

    def later_queries(kj, q_tiles):
        qw = jnp.concatenate([qw_ref[q] for q in q_tiles], axis=1)
        st = jnp.dot(k_ref[0, kj * t:(kj + 1) * t, :], qw, preferred_element_type=_F32)
        if q_tiles[0] == kj + 1:
            blk = _BIAS_BLOCK
            corner = bias_ref[0, 1, t - blk:, :blk]
            tail = st[t - blk:]
            tail = jnp.concatenate([tail[:, :blk] + corner, tail[:, blk:t],
                                    tail[:, t:t + blk] + corner, tail[:, t + blk:]], axis=1)
            st = jnp.concatenate([st[:t - blk], tail], axis=0)
        e = jnp.exp2(st)
        pv = (jnp.dot(vt_ref[kj * gran, 0], e[:half].astype(_BF16), preferred_element_type=_F32)
              + jnp.dot(vt_ref[kj * gran + 1, 0], e[half:].astype(_BF16), preferred_element_type=_F32))
        for n, q in enumerate(q_tiles):
            lanes = slice(n * 2 * t, (n + 1) * 2 * t)
            den_ref[q] += key_sum(e[:, lanes])
            acc_ref[q] += pv[:, lanes]

    def write_outputs():
        for q in range(nq):
            o = acc_ref[q] / jnp.sum(den_ref[q], axis=0, keepdims=True)
            od = o[:, :t] - lam_ref[0:1, 0:1] * o[:, t:]
            scale = lax.rsqrt(jnp.mean(od * od, axis=0, keepdims=True) + _EPS)
            o_ref[0, q * t:(q + 1) * t, :] = (od * scale * gain_ref[...]).T.astype(_BF16)

    def online_sweep(m_ref):
        m_ref[...] = jnp.full(m_ref.shape, -jnp.inf, _F32)

        def query_tile(q, carry):
            def key_tile(kj, inner):
                kt = k_ref[0, pl.ds(pl.multiple_of(kj * t, t), t), :]
                bias = bias_ref[0, jnp.where(kj == q, 2, jnp.where(kj == q - 1, 1, 0))]
                for s in range(2):
                    lanes = slice(s * t, (s + 1) * t)
                    st = jnp.dot(kt, qw_ref[q, :, lanes], preferred_element_type=_F32) + bias
                    m_old = m_ref[q, :, lanes]
                    m_new = jnp.maximum(m_old, jnp.max(st, axis=0, keepdims=True))
                    alpha = jnp.exp2(m_old - m_new)
                    e = jnp.exp2(st - m_new)
                    pv = (jnp.dot(vt_ref[kj * gran, 0], e[:half].astype(_BF16),
                                  preferred_element_type=_F32)
                          + jnp.dot(vt_ref[kj * gran + 1, 0], e[half:].astype(_BF16),
                                    preferred_element_type=_F32))
                    den_ref[q, :, lanes] = alpha * den_ref[q, :, lanes] + key_sum(e)
                    acc_ref[q, :, lanes] = alpha * acc_ref[q, :, lanes] + pv
                    m_ref[q, :, lanes] = m_new
                return inner

            lax.fori_loop(0, q + 1, key_tile, 0)
            return carry

        lax.fori_loop(0, nq, query_tile, 0)

    if online:
        pl.run_scoped(online_sweep, pltpu.VMEM((nq, 1, 2 * t), _F32))
    else:
        for kj in range(nq):
            diagonal(kj)
            later = list(range(kj + 1, nq))
            for g0 in range(0, len(later), _QUERY_GROUP):
                later_queries(kj, later[g0:g0 + _QUERY_GROUP])
    write_outputs()


def _attention(qt, k, vt, bias, lam, gain_t):
    b, l, w = k.shape
    t = _ATTN_TILE
    nq = l // t

    def sweep(online):
        return pl.pallas_call(
            functools.partial(_attn_body, t=t, nq=nq, online=online),
            grid=(b, _HEADS),
            in_specs=[pl.BlockSpec((_V_DIM, l), lambda bi, h: (h, bi)),
                      pl.BlockSpec((1, l, _V_DIM), lambda bi, h: (bi, 0, h)),
                      pl.BlockSpec((l // _KEY_GRANULE, 1, _V_DIM, _KEY_GRANULE),
                                   lambda bi, h: (bi, h, 0, 0)),
                      pl.BlockSpec((1, 3, t, t), lambda bi, h: (h, 0, 0, 0)),
                      pl.BlockSpec(lam.shape, lambda bi, h: (0, 0)),
                      pl.BlockSpec(gain_t.shape, lambda bi, h: (0, 0))],
            out_specs=pl.BlockSpec((1, l, _V_DIM), lambda bi, h: (bi, 0, h)),
            out_shape=jax.ShapeDtypeStruct((b, l, w), _BF16),
            scratch_shapes=[pltpu.VMEM((nq, _V_DIM, 2 * t), _BF16),
                            pltpu.VMEM((nq, _V_DIM, 2 * t), _F32),
                            pltpu.VMEM((nq, _SUBLANES, 2 * t), _F32)],
            compiler_params=pltpu.CompilerParams(
                dimension_semantics=("parallel", "parallel"), vmem_limit_bytes=_VMEM_LIMIT),
            name="diff_attention_online" if online else "diff_attention",
        )(qt, k, vt, bias, lam, gain_t)

    return lax.cond(lam[1, 0] > 0, lambda: sweep(False), lambda: sweep(True))


def _s5_prep_body(are_ref, aim_ref, ldt_ref, bre_ref, bim_ref,
                  lbre_ref, lbim_ref, bbre_ref, bbim_ref):
    a_re = are_ref[...]
    a_im = aim_ref[...]
    dt = jnp.exp(ldt_ref[...])
    decay = jnp.exp(a_re * dt)
    lb_re = decay * jnp.cos(a_im * dt)
    lb_im = decay * jnp.sin(a_im * dt)
    nr = lb_re - 1.0
    ni = lb_im
    den = a_re * a_re + a_im * a_im
    q_re = (nr * a_re + ni * a_im) / den
    q_im = (ni * a_re - nr * a_im) / den
    b_re = bre_ref[...]
    b_im = bim_ref[...]
    bbre_ref[...] = q_re * b_re - q_im * b_im
    bbim_ref[...] = q_re * b_im + q_im * b_re
    lbre_ref[...] = lb_re
    lbim_ref[...] = lb_im


def _s5_prep(a_re, a_im, log_dt, b_re, b_im):
    sd = jax.ShapeDtypeStruct(a_re.shape, _F32)
    return pl.pallas_call(_s5_prep_body, out_shape=[sd] * 4, name="s5_discretise")(
        a_re, a_im, log_dt, b_re, b_im)


def _s5_body(u_ref, wbre_ref, wbim_ref, lbre_ref, lbim_ref, wcre_ref, wcim_ref, wd_ref, d_ref,
             gw_ref, gb_ref, o_ref, xre_ref, xim_ref, cre_ref, cim_ref, *, points, nb, stride):
    rows = nb * points
    lanes = u_ref.shape[-1]
    width = lanes // stride
    n_tiles, _, tile_w = lbre_ref.shape
    n_blocks = wd_ref.shape[0]
    cb = width // n_blocks
    tiles_per_block = n_tiles // n_blocks
    halves = 2
    blocks_per_half = n_blocks // halves
    tiles_per_half = n_tiles // halves

    @pl.when(pl.program_id(0) == 0)
    def _():
        cre_ref[...] = jnp.zeros(cre_ref.shape, _F32)
        cim_ref[...] = jnp.zeros(cim_ref.shape, _F32)

    u_tm = jnp.swapaxes(u_ref[...].astype(_F32), 0, 1).reshape(rows, lanes).astype(_BF16)
    taps = [jnp.concatenate([u_tm[:, i * width + o * cb:i * width + (o + 1) * cb]
                             for i in range(stride)], axis=1) for o in range(n_blocks)]

    for hf in range(halves):
        blocks = range(hf * blocks_per_half, (hf + 1) * blocks_per_half)
        bu_r = jnp.concatenate(
            [jnp.dot(taps[o], wbre_ref[o], preferred_element_type=_F32) for o in blocks], axis=1)
        bu_i = jnp.concatenate(
            [jnp.dot(taps[o], wbim_ref[o], preferred_element_type=_F32) for o in blocks], axis=1)
        group = range(hf * tiles_per_half, (hf + 1) * tiles_per_half)
        a_r = [jnp.broadcast_to(lbre_ref[j], (nb, tile_w)) for j in group]
        a_i = [jnp.broadcast_to(lbim_ref[j], (nb, tile_w)) for j in group]
        x_r = [cre_ref[j] for j in group]
        x_i = [cim_ref[j] for j in group]
        for ti in range(points):
            r = slice(ti * nb, (ti + 1) * nb)
            for q, j in enumerate(group):
                cols = slice(q * tile_w, (q + 1) * tile_w)
                xre_ref[j, r, :] = x_r[q]
                xim_ref[j, r, :] = x_i[q]
                n_r = a_r[q] * x_r[q] - a_i[q] * x_i[q] + bu_r[r, cols]
                n_i = a_r[q] * x_i[q] + a_i[q] * x_r[q] + bu_i[r, cols]
                x_r[q], x_i[q] = n_r, n_i
        for q, j in enumerate(group):
            cre_ref[j] = x_r[q]
            cim_ref[j] = x_i[q]

    ys = []
    for o in range(n_blocks):
        tiles = range(o * tiles_per_block, (o + 1) * tiles_per_block)
        x_r = jnp.concatenate([xre_ref[j] for j in tiles], axis=1).astype(_BF16)
        x_i = jnp.concatenate([xim_ref[j] for j in tiles], axis=1).astype(_BF16)
        ys.append(jnp.dot(x_r, wcre_ref[o], preferred_element_type=_F32)
                  + jnp.dot(x_i, wcim_ref[o], preferred_element_type=_F32)
                  + jnp.dot(taps[o], wd_ref[o], preferred_element_type=_F32))
    y = jnp.concatenate([ys[o][:, j * cb:(j + 1) * cb]
                         for j in range(stride) for o in range(n_blocks)], axis=1)
    y = y + d_ref[...] * u_tm.astype(_F32)
    g = jax.nn.gelu(y)
    gate = jnp.concatenate(
        [jnp.dot(g[:, j * width:(j + 1) * width].astype(_BF16), gw_ref[...],
                 preferred_element_type=_F32) for j in range(stride)], axis=1)
    o = g * jax.nn.sigmoid(gate + gb_ref[...])
    o_ref[...] = jnp.swapaxes(o.reshape(points, nb, lanes), 0, 1).astype(_BF16)


def _s5(u, wb_re, wb_im, lb_re, lb_im, wc_re, wc_im, wd, d_vec, glu_w, glu_b):
    nb, l, lanes = u.shape
    stride = _SCAN_STRIDE
    steps = _SCAN_STEPS // stride
    n_tiles, _, tile_w = lb_re.shape
    full = lambda a: pl.BlockSpec(a.shape, lambda i: (0,) * a.ndim)
    io_spec = pl.BlockSpec((nb, steps, lanes), lambda i: (0, i, 0))
    consts = (wb_re, wb_im, lb_re, lb_im, wc_re, wc_im, wd, d_vec, glu_w, glu_b)
    state_buf = pltpu.VMEM((n_tiles, nb * steps, tile_w), _F32)
    return pl.pallas_call(
        functools.partial(_s5_body, points=steps, nb=nb, stride=stride),
        grid=(l // steps,),
        in_specs=[io_spec] + [full(a) for a in consts],
        out_specs=io_spec,
        out_shape=jax.ShapeDtypeStruct(u.shape, _BF16),
        scratch_shapes=[state_buf, state_buf,
                        pltpu.VMEM((n_tiles, nb, tile_w), _F32),
                        pltpu.VMEM((n_tiles, nb, tile_w), _F32)],
        compiler_params=pltpu.CompilerParams(
            dimension_semantics=("arbitrary",), vmem_limit_bytes=_VMEM_LIMIT),
        name="s5_branch",
    )(u, *consts)


def _outproj_body(x_ref, ng_ref, wg_ref, mb_ref, oa_ref, os_ref, pa_ref, ps_ref, wo_ref,
                  out_ref, *, width):
    x = x_ref[...]
    d = x.shape[-1]
    h = (x * _rms_scale(x) * ng_ref[...]).astype(_BF16)
    zg = jnp.dot(h, wg_ref[...], preferred_element_type=_F32)
    o_a = (oa_ref[...].astype(_F32) * jax.nn.silu(zg[:, :width])).astype(_BF16)
    o_s = (os_ref[...].astype(_F32) * jax.nn.silu(zg[:, width:2 * width])).astype(_BF16)
    p_a = jnp.dot(o_a, pa_ref[...], preferred_element_type=_F32)
    p_s = jnp.dot(o_s, ps_ref[...], preferred_element_type=_F32)
    g = jax.nn.sigmoid(zg[:, 2 * width:] + mb_ref[...])
    merged = g[:, :d] * p_a + g[:, d:] * p_s
    out_ref[...] = x + jnp.dot(merged.astype(_BF16), wo_ref[...], preferred_element_type=_F32)


def _outproj(x2, norm_gain, w_gates, merge_b, o_a, o_s, proj_a, proj_s, w_out, width):
    n, d = x2.shape
    tm = _TOKEN_TILE
    full = lambda a: pl.BlockSpec(a.shape, lambda i: (0,) * a.ndim)
    row = lambda a: pl.BlockSpec((tm, a.shape[-1]), lambda i: (i, 0))
    return pl.pallas_call(
        functools.partial(_outproj_body, width=width),
        grid=(n // tm,),
        in_specs=[row(x2), full(norm_gain), full(w_gates), full(merge_b), row(o_a), row(o_s),
                  full(proj_a), full(proj_s), full(w_out)],
        out_specs=row(x2),
        out_shape=jax.ShapeDtypeStruct(x2.shape, x2.dtype),
        compiler_params=pltpu.CompilerParams(
            dimension_semantics=("parallel",), vmem_limit_bytes=_VMEM_LIMIT),
        name="outproj",
    )(x2, norm_gain, w_gates, merge_b, o_a, o_s, proj_a, proj_s, w_out)


def _block_diag_halves(blocks, halves):
    g, r, c = blocks.shape
    gh = g // halves
    on_diagonal = (np.arange(gh * r)[:, None] // r) == (np.arange(gh * c)[None, :] // c)
    tiled = jnp.tile(blocks.reshape(halves, gh * r, c), (1, 1, gh))
    return jnp.where(on_diagonal, tiled, 0)


def _layer(x, lam_init, norm_gain, w_in, merge_gate_b, q_norm_gain, k_norm_gain,
           lambda_q1, lambda_k1, lambda_q2, lambda_k2, diff_subln_gain, rel_bias_table,
           ssm_a_re, ssm_a_im, ssm_log_dt, ssm_b_re, ssm_b_im, ssm_c_re, ssm_c_im,
           ssm_d, ssm_glu_w, ssm_glu_b, proj_attn, proj_ssm, w_out):
    nb, l, d = x.shape
    n = nb * l
    aw = _HEADS * 2 * _SUB_DIM
    groups = ssm_a_re.shape[0]
    x2 = x.reshape(n, d)
    ng = norm_gain.reshape(1, d).astype(_F32)

    w_ku = jnp.concatenate([w_in[:, aw:2 * aw], w_in[:, 4 * aw:5 * aw]], axis=1).astype(_BF16)
    w_qvt = jnp.concatenate([w_in[:, :aw], w_in[:, 2 * aw:3 * aw]], axis=1).T.astype(_BF16)
    w_gates = jnp.concatenate([w_in[:, 3 * aw:4 * aw], w_in[:, 5 * aw:]], axis=1).astype(_BF16)
    gq = jnp.tile(q_norm_gain.astype(_F32), 2 * _HEADS) * (_SUB_DIM ** -0.5 * _LOG2E)
    gqt = jnp.broadcast_to(gq[:, None], (aw, _TOKEN_TILE))
    gk = jnp.tile(k_norm_gain.astype(_F32), 2 * _HEADS).reshape(1, aw)
    seg = np.arange(_MXU_TILE) // _SUB_DIM
    gsum = jnp.asarray((seg[:, None] == seg[None, :]).astype(np.float32), _BF16)
    qt, k, vt, u = _inproj(x2, ng, w_ku, w_qvt, gqt, gk, gsum, aw)

    lam4 = jnp.stack([lambda_q1, lambda_k1, lambda_q2, lambda_k2]).astype(_F32)
    qk_gains = jnp.stack([gq[:_SUB_DIM], gk[0, :_SUB_DIM]])
    bias, lam = _bias_prep(rel_bias_table.astype(_F32), lam4, qk_gains, lam_init, _ATTN_TILE)
    subln = diff_subln_gain.astype(_F32) * (1.0 - lam_init)
    subln_t = jnp.broadcast_to(subln[:, None], (_V_DIM, _ATTN_TILE))
    shp = (nb, l, aw)
    o_a = _attention(qt, k.reshape(shp), vt, bias, lam, subln_t)

    rep = lambda a: jnp.repeat(a.astype(_F32), _SSM_GROUP, axis=0)
    ldt = jnp.broadcast_to(ssm_log_dt.astype(_F32)[:, None], ssm_a_re.shape)
    bt = lambda a: a.astype(_F32).transpose(0, 2, 1).reshape(groups * _SSM_GROUP, _SSM_STATE)
    lb_re, lb_im, bb_re, bb_im = _s5_prep(rep(ssm_a_re), rep(ssm_a_im), rep(ldt),
                                          bt(ssm_b_re), bt(ssm_b_im))
    n_tiles = groups * _SSM_STATE // _STATE_TILE
    flat = lambda a: a.reshape(n_tiles, 1, _STATE_TILE)
    gshape = (groups, _SSM_GROUP, _SSM_STATE)
    s = _SCAN_STRIDE
    powers = [(jnp.ones_like(lb_re[::_SSM_GROUP]), jnp.zeros_like(lb_re[::_SSM_GROUP]))]
    for _ in range(s):
        p_re, p_im = powers[-1]
        powers.append((p_re * lb_re[::_SSM_GROUP] - p_im * lb_im[::_SSM_GROUP],
                       p_re * lb_im[::_SSM_GROUP] + p_im * lb_re[::_SSM_GROUP]))

    def times_power(z_re, z_im, k):
        p_re, p_im = powers[k][0][:, None, :], powers[k][1][:, None, :]
        return z_re * p_re - z_im * p_im, z_re * p_im + z_im * p_re

    blocks = aw // _CHANNEL_BLOCK
    place = lambda b: _block_diag_halves(b, blocks)
    b_re, b_im = bb_re.reshape(gshape), bb_im.reshape(gshape)
    c_re, c_im = ssm_c_re.astype(_F32), ssm_c_im.astype(_F32)
    b_taps = [times_power(b_re, b_im, s - 1 - i) for i in range(s)]
    wb_re = jnp.concatenate([place(t[0]) for t in b_taps], axis=1).astype(_BF16)
    wb_im = jnp.concatenate([place(t[1]) for t in b_taps], axis=1).astype(_BF16)
    c_steps = [times_power(c_re, c_im, j + 1) for j in range(s)]
    wc_re = jnp.concatenate([place(t[0].transpose(0, 2, 1)) for t in c_steps], axis=2).astype(_BF16)
    wc_im = jnp.concatenate([place(-t[1].transpose(0, 2, 1)) for t in c_steps], axis=2).astype(_BF16)

    def direct(k):
        t_re, t_im = times_power(c_re, c_im, k)
        mix = lambda a, b: jnp.einsum("gcs,gds->gdc", a, b, precision=lax.Precision.HIGHEST)
        return place(mix(t_re, b_re) - mix(t_im, b_im))

    lag = [direct(k) for k in range(s)]
    none = jnp.zeros_like(lag[0])
    wd = jnp.concatenate(
        [jnp.concatenate([lag[j - i] if j >= i else none for j in range(s)], axis=2)
         for i in range(s)], axis=1).astype(_BF16)
    o_s = _s5(u.reshape(nb, l // s, s * aw), wb_re, wb_im, flat(powers[s][0]), flat(powers[s][1]),
              wc_re, wc_im, wd, jnp.tile(ssm_d.astype(_F32).reshape(aw), s).reshape(1, s * aw),
              ssm_glu_w.astype(_BF16), jnp.tile(ssm_glu_b.astype(_F32), s).reshape(1, s * aw))

    out = _outproj(x2, ng, w_gates, merge_gate_b.astype(_F32).reshape(1, 2 * d),
                   o_a.reshape(n, aw), o_s.reshape(n, aw),
                   proj_attn.astype(_BF16), proj_ssm.astype(_BF16), w_out.astype(_BF16), aw)
    return out.reshape(nb, l, d)


def kernel(x, norm_gain, w_in, merge_gate_b, q_norm_gain, k_norm_gain, lambda_q1, lambda_k1,
           lambda_q2, lambda_k2, diff_subln_gain, rel_bias_table, ssm_A_re, ssm_A_im, ssm_log_dt,
           ssm_B_re, ssm_B_im, ssm_C_re, ssm_C_im, ssm_D, ssm_glu_w, ssm_glu_b,
           proj_attn, proj_ssm, w_out):
    per_layer = (norm_gain, w_in, merge_gate_b, q_norm_gain, k_norm_gain, lambda_q1, lambda_k1,
                 lambda_q2, lambda_k2, diff_subln_gain)
    per_layer_tail = (ssm_A_re, ssm_A_im, ssm_log_dt, ssm_B_re, ssm_B_im, ssm_C_re, ssm_C_im,
                      ssm_D, ssm_glu_w, ssm_glu_b, proj_attn, proj_ssm, w_out)
    for layer in range(norm_gain.shape[0]):
        lam_init = 0.8 - 0.6 * math.exp(-0.3 * layer)
        x = _layer(x, lam_init, *(p[layer] for p in per_layer), rel_bias_table,
                   *(p[layer] for p in per_layer_tail))
    return x
```

```python
import functools
import math

import jax
import jax.numpy as jnp
import numpy as np
from jax import lax
from jax.experimental import pallas as pl
from jax.experimental.pallas import tpu as pltpu

_F32 = jnp.float32
_BF16 = jnp.bfloat16

_CHUNK = 64
_HEADS = 4
_SUB_DIM = 64
_V_DIM = 128
_SSM_GROUP = 16
_SSM_STATE = 64
_REL_BUCKETS = 32
_REL_MAX_DIST = 128
_EPS = 1e-6
_LOG2E = math.log2(math.e)
_SAFE_LOG2_RANGE = 100.0
_BF16_SLACK = 1.02

_VMEM_LIMIT = 48 * 1024 * 1024
_MXU_TILE = 256
_SUBLANES = 8
_TOKEN_TILE = 1024
_ATTN_TILE = 512
_KEY_GRANULE = 256
_BIAS_BLOCK = 128
_QUERY_GROUP = 2
_SCAN_STEPS = 128
_STATE_TILE = 256
_SCAN_STRIDE = 4
_CHANNEL_BLOCK = 128


def _rms_scale(x, eps=_EPS):
    return lax.rsqrt(jnp.mean(x * x, axis=-1, keepdims=True) + eps)


def _inproj_body(x_ref, ng_ref, wku_ref, wqvt_ref, gqt_ref, gk_ref, gsum_ref,
                 qt_ref, k_ref, vt_ref, u_ref, *, width):
    x = x_ref[...]
    tm = x.shape[0]
    h = (x * _rms_scale(x) * ng_ref[...]).astype(_BF16)
    z = jnp.dot(h, wku_ref[...], preferred_element_type=_F32)
    zt = lax.dot_general(wqvt_ref[...], h, (((1,), (1,)), ((), ())),
                         preferred_element_type=_F32)

    kk = z[:, :width]
    k2 = (kk * kk).astype(_BF16)
    mxu = gsum_ref.shape[0]
    ss = jnp.concatenate([jnp.dot(k2[:, c:c + mxu], gsum_ref[...], preferred_element_type=_F32)
                          for c in range(0, width, mxu)], axis=1)
    k_ref[...] = (kk * lax.rsqrt(ss * (1.0 / _SUB_DIM) + _EPS) * gk_ref[...]).astype(_BF16)
    u_ref[...] = z[:, width:].astype(_BF16)

    qt = zt[:width].reshape(width // _SUB_DIM, _SUB_DIM, tm)
    ms = jnp.mean(qt * qt, axis=1, keepdims=True)
    qt_ref[...] = ((qt * lax.rsqrt(ms + _EPS)).reshape(width, tm) * gqt_ref[...]).astype(_BF16)

    vt = zt[width:].astype(_BF16)
    for j in range(tm // _KEY_GRANULE):
        for hd in range(_HEADS):
            vt_ref[j, hd] = vt[hd * _V_DIM:(hd + 1) * _V_DIM,
                               j * _KEY_GRANULE:(j + 1) * _KEY_GRANULE]


def _inproj(x2, norm_gain, w_ku, w_qvt, gqt, gk, gsum, width):
    n, d = x2.shape
    tm = _TOKEN_TILE
    full = lambda a: pl.BlockSpec(a.shape, lambda i: (0,) * a.ndim)
    row_sd = jax.ShapeDtypeStruct((n, width), _BF16)
    row_spec = pl.BlockSpec((tm, width), lambda i: (i, 0))
    gran = tm // _KEY_GRANULE
    vrows = _V_DIM
    return pl.pallas_call(
        functools.partial(_inproj_body, width=width),
        grid=(n // tm,),
        in_specs=[pl.BlockSpec((tm, d), lambda i: (i, 0)), full(norm_gain), full(w_ku), full(w_qvt),
                  full(gqt), full(gk), full(gsum)],
        out_specs=[pl.BlockSpec((width, tm), lambda i: (0, i)), row_spec,
                   pl.BlockSpec((gran, _HEADS, vrows, _KEY_GRANULE), lambda i: (i, 0, 0, 0)), row_spec],
        out_shape=[jax.ShapeDtypeStruct((width, n), _BF16), row_sd,
                   jax.ShapeDtypeStruct((n // _KEY_GRANULE, _HEADS, vrows, _KEY_GRANULE), _BF16), row_sd],
        compiler_params=pltpu.CompilerParams(
            dimension_semantics=("parallel",), vmem_limit_bytes=_VMEM_LIMIT),
        name="inproj",
    )(x2, norm_gain, w_ku, w_qvt, gqt, gk, gsum)


def _t5_bucket_np(rel):
    nb = _REL_BUCKETS // 2
    max_exact = nb // 2
    side = np.where(rel > 0, nb, 0)
    n = np.abs(rel)
    nf = np.maximum(n, 1).astype(np.float32)
    large = max_exact + (np.log(nf / np.float32(max_exact)) / np.float32(math.log(_REL_MAX_DIST / max_exact))
                         * np.float32(nb - max_exact)).astype(np.int32)
    large = np.minimum(large, nb - 1)
    return side + np.where(n < max_exact, n, large)


def _bucket_blocks():
    assert _BIAS_BLOCK % _CHUNK == 0
    i = np.arange(_BIAS_BLOCK)[None, :]
    j = np.arange(_BIAS_BLOCK)[:, None]
    far_bucket = _REL_BUCKETS // 2 - 1
    assert (_t5_bucket_np(j - i - 2 * _BIAS_BLOCK) == far_bucket).all()
    diag = np.where((j // _CHUNK) <= (i // _CHUNK), _t5_bucket_np(j - i), -1)
    prev = _t5_bucket_np(j - i - _BIAS_BLOCK)
    return np.stack([diag, prev]).astype(np.int32)


def _bias_prep_body(tab_ref, tabv_ref, bucket_ref, lam4_ref, gains_ref,
                    bias_ref, lam_ref, *, lam_init, t):
    h = pl.program_id(0)
    far_bucket = _REL_BUCKETS // 2 - 1
    bkt = bucket_ref[...]
    far = tab_ref[far_bucket, h]
    val = jnp.full(bkt.shape, -jnp.inf, _F32)
    for b in range(_REL_BUCKETS):
        val = jnp.where(bkt == b, (tab_ref[b, h] - far) * _LOG2E, val)
    diag_block, prev_block = val[0], val[1]
    n_blk = t // _BIAS_BLOCK
    bias_ref[0, 0] = jnp.zeros((t, t), _F32)
    bias_ref[0, 1] = jnp.zeros((t, t), _F32)
    bias_ref[0, 1, t - _BIAS_BLOCK:, :_BIAS_BLOCK] = prev_block
    for bj in range(n_blk):
        rows = slice(bj * _BIAS_BLOCK, (bj + 1) * _BIAS_BLOCK)
        for bi in range(n_blk):
            cols = slice(bi * _BIAS_BLOCK, (bi + 1) * _BIAS_BLOCK)
            if bi == bj:
                block = diag_block
            elif bi == bj + 1:
                block = prev_block
            else:
                block = jnp.full((_BIAS_BLOCK, _BIAS_BLOCK), 0.0 if bi > bj else -jnp.inf, _F32)
            bias_ref[0, 2, rows, cols] = block
    l4 = lam4_ref[...]
    s1 = jnp.sum(l4[0:1] * l4[1:2], axis=-1, keepdims=True)
    s2 = jnp.sum(l4[2:3] * l4[3:4], axis=-1, keepdims=True)
    lam_ref[0:1] = jnp.broadcast_to(jnp.exp(s1) - jnp.exp(s2) + lam_init, (1, lam_ref.shape[1]))
    gmax = jnp.max(jnp.abs(gains_ref[...]), axis=-1, keepdims=True)
    qk_bound = (_SUB_DIM * _BF16_SLACK) * gmax[0:1] * gmax[1:2]
    tv = tabv_ref[...]
    dev = jnp.abs(tv - tv[far_bucket:far_bucket + 1]) * _LOG2E
    bias_bound = jnp.max(jnp.max(dev, axis=-1, keepdims=True), axis=0, keepdims=True)
    bounded = (qk_bound + bias_bound <= _SAFE_LOG2_RANGE).astype(_F32)
    lam_ref[1:2] = jnp.broadcast_to(bounded, (1, lam_ref.shape[1]))


def _bias_prep(rel_table, lam4, gains, lam_init, t):
    assert t % _BIAS_BLOCK == 0
    buckets = jnp.asarray(_bucket_blocks())
    whole = lambda a: pl.BlockSpec(a.shape, lambda h: (0,) * a.ndim)
    return pl.pallas_call(
        functools.partial(_bias_prep_body, lam_init=lam_init, t=t),
        grid=(_HEADS,),
        in_specs=[pl.BlockSpec(memory_space=pltpu.SMEM), whole(rel_table), whole(buckets),
                  whole(lam4), whole(gains)],
        out_specs=[pl.BlockSpec((1, 3, t, t), lambda h: (h, 0, 0, 0)),
                   pl.BlockSpec((2, _V_DIM), lambda h: (0, 0))],
        out_shape=[jax.ShapeDtypeStruct((_HEADS, 3, t, t), _F32),
                   jax.ShapeDtypeStruct((2, _V_DIM), _F32)],
        compiler_params=pltpu.CompilerParams(dimension_semantics=("arbitrary",)),
        name="attn_bias_prep",
    )(rel_table, rel_table, buckets, lam4, gains)


def _attn_body(qt_ref, k_ref, vt_ref, bias_ref, lam_ref, gain_ref, o_ref,
               qw_ref, acc_ref, den_ref, *, t, nq, online):
    gran = t // _KEY_GRANULE
    assert gran == 2
    half = _KEY_GRANULE
    row = lax.broadcasted_iota(jnp.int32, (_V_DIM, t), 0)
    for q in range(nq):
        qt = qt_ref[:, q * t:(q + 1) * t]
        zero = jnp.zeros_like(qt)
        qw_ref[q, :, :t] = jnp.where(row < _SUB_DIM, qt, zero)
        qw_ref[q, :, t:] = jnp.where(row >= _SUB_DIM, qt, zero)
    acc_ref[...] = jnp.zeros(acc_ref.shape, _F32)
    den_ref[...] = jnp.zeros(den_ref.shape, _F32)

    def key_sum(e):
        return jnp.sum(e.reshape(e.shape[0] // _SUBLANES, _SUBLANES, e.shape[1]), axis=0)

    def diagonal(kj):
        late = [slice(half, t), slice(t + half, 2 * t)]
        k0 = kj * t
        bias = bias_ref[0, 2]
        st0 = jnp.dot(k_ref[0, k0:k0 + half, :], qw_ref[kj], preferred_element_type=_F32)
        e0 = [jnp.exp2(st0[:, :t] + bias[:half]), jnp.exp2(st0[:, t:] + bias[:half])]
        den_ref[kj] += jnp.concatenate([key_sum(e0[0]), key_sum(e0[1])], axis=1)
        pt0 = jnp.concatenate([e0[0].astype(_BF16), e0[1].astype(_BF16)], axis=1)
        qw_late = jnp.concatenate([qw_ref[kj, :, lanes] for lanes in late], axis=1)
        st1 = jnp.dot(k_ref[0, k0 + half:k0 + t, :], qw_late, preferred_element_type=_F32)
        e1 = [jnp.exp2(st1[:, :half] + bias[half:, half:]), jnp.exp2(st1[:, half:] + bias[half:, half:])]
        pt1 = jnp.concatenate([e1[0].astype(_BF16), e1[1].astype(_BF16)], axis=1)
        acc_ref[kj] += jnp.dot(vt_ref[kj * gran, 0], pt0, preferred_element_type=_F32)
        pv1 = jnp.dot(vt_ref[kj * gran + 1, 0], pt1, preferred_element_type=_F32)
        for n, lanes in enumerate(late):
            acc_ref[kj, :, lanes] += pv1[:, n * half:(n + 1) * half]
            den_ref[kj, :, lanes] += key_sum(e1[n])

    def later_queries(kj, q_tiles):
        qw = jnp.concatenate([qw_ref[q] for q in q_tiles], axis=1)
        st = jnp.dot(k_ref[0, kj * t:(kj + 1) * t, :], qw, preferred_element_type=_F32)
        if q_tiles[0] == kj + 1:
            blk = _BIAS_BLOCK
            corner = bias_ref[0, 1, t - blk:, :blk]
            tail = st[t - blk:]
            tail = jnp.concatenate([tail[:, :blk] + corner, tail[:, blk:t],
                                    tail[:, t:t + blk] + corner, tail[:, t + blk:]], axis=1)
            st = jnp.concatenate([st[:t - blk], tail], axis=0)
        e = jnp.exp2(st)
        pv = (jnp.dot(vt_ref[kj * gran, 0], e[:half].astype(_BF16), preferred_element_type=_F32)
              + jnp.dot(vt_ref[kj * gran + 1, 0], e[half:].astype(_BF16), preferred_element_type=_F32))
        for n, q in enumerate(q_tiles):
            lanes = slice(n * 2 * t, (n + 1) * 2 * t)
            den_ref[q] += key_sum(e[:, lanes])
            acc_ref[q] += pv[:, lanes]

    def write_outputs():
        for q in range(nq):
            o = acc_ref[q] / jnp.sum(den_ref[q], axis=0, keepdims=True)
            od = o[:, :t] - lam_ref[0:1, 0:1] * o[:, t:]
            scale = lax.rsqrt(jnp.mean(od * od, axis=0, keepdims=True) + _EPS)
            o_ref[0, q * t:(q + 1) * t, :] = (od * scale * gain_ref[...]).T.astype(_BF16)

    def online_sweep(m_ref):
        m_ref[...] = jnp.full(m_ref.shape, -jnp.inf, _F32)

        def query_tile(q, carry):
            def key_tile(kj, inner):
                kt = k_ref[0, pl.ds(pl.multiple_of(kj * t, t), t), :]
                bias = bias_ref[0, jnp.where(kj == q, 2, jnp.where(kj == q - 1, 1, 0))]
                for s in range(2):
                    lanes = slice(s * t, (s + 1) * t)
                    st = jnp.dot(kt, qw_ref[q, :, lanes], preferred_element_type=_F32) + bias
                    m_old = m_ref[q, :, lanes]
                    m_new = jnp.maximum(m_old, jnp.max(st, axis=0, keepdims=True))
                    alpha = jnp.exp2(m_old - m_new)
                    e = jnp.exp2(st - m_new)
                    pv = (jnp.dot(vt_ref[kj * gran, 0], e[:half].astype(_BF16),
                                  preferred_element_type=_F32)
                          + jnp.dot(vt_ref[kj * gran + 1, 0], e[half:].astype(_BF16),
                                    preferred_element_type=_F32))
                    den_ref[q, :, lanes] = alpha * den_ref[q, :, lanes] + key_sum(e)
                    acc_ref[q, :, lanes] = alpha * acc_ref[q, :, lanes] + pv
                    m_ref[q, :, lanes] = m_new
                return inner

            lax.fori_loop(0, q + 1, key_tile, 0)
            return carry

        lax.fori_loop(0, nq, query_tile, 0)

    if online:
        pl.run_scoped(online_sweep, pltpu.VMEM((nq, 1, 2 * t), _F32))
    else:
        for kj in range(nq):
            diagonal(kj)
            later = list(range(kj + 1, nq))
            for g0 in range(0, len(later), _QUERY_GROUP):
                later_queries(kj, later[g0:g0 + _QUERY_GROUP])
    write_outputs()


def _attention(qt, k, vt, bias, lam, gain_t):
    b, l, w = k.shape
    t = _ATTN_TILE
    nq = l // t

    def sweep(online):
        return pl.pallas_call(
            functools.partial(_attn_body, t=t, nq=nq, online=online),
            grid=(b, _HEADS),
            in_specs=[pl.BlockSpec((_V_DIM, l), lambda bi, h: (h, bi)),
                      pl.BlockSpec((1, l, _V_DIM), lambda bi, h: (bi, 0, h)),
                      pl.BlockSpec((l // _KEY_GRANULE, 1, _V_DIM, _KEY_GRANULE),
                                   lambda bi, h: (bi, h, 0, 0)),
                      pl.BlockSpec((1, 3, t, t), lambda bi, h: (h, 0, 0, 0)),
                      pl.BlockSpec(lam.shape, lambda bi, h: (0, 0)),
                      pl.BlockSpec(gain_t.shape, lambda bi, h: (0, 0))],
            out_specs=pl.BlockSpec((1, l, _V_DIM), lambda bi, h: (bi, 0, h)),
            out_shape=jax.ShapeDtypeStruct((b, l, w), _BF16),
            scratch_shapes=[pltpu.VMEM((nq, _V_DIM, 2 * t), _BF16),
                            pltpu.VMEM((nq, _V_DIM, 2 * t), _F32),
                            pltpu.VMEM((nq, _SUBLANES, 2 * t), _F32)],
            compiler_params=pltpu.CompilerParams(
                dimension_semantics=("parallel", "parallel"), vmem_limit_bytes=_VMEM_LIMIT),
            name="diff_attention_online" if online else "diff_attention",
        )(qt, k, vt, bias, lam, gain_t)

    return lax.cond(lam[1, 0] > 0, lambda: sweep(False), lambda: sweep(True))


def _s5_prep_body(are_ref, aim_ref, ldt_ref, bre_ref, bim_ref,
                  lbre_ref, lbim_ref, bbre_ref, bbim_ref):
    a_re = are_ref[...]
    a_im = aim_ref[...]
    dt = jnp.exp(ldt_ref[...])
    decay = jnp.exp(a_re * dt)
    lb_re = decay * jnp.cos(a_im * dt)
    lb_im = decay * jnp.sin(a_im * dt)
    nr = lb_re - 1.0
    ni = lb_im
    den = a_re * a_re + a_im * a_im
    q_re = (nr * a_re + ni * a_im) / den
    q_im = (ni * a_re - nr * a_im) / den
    b_re = bre_ref[...]
    b_im = bim_ref[...]
    bbre_ref[...] = q_re * b_re - q_im * b_im
    bbim_ref[...] = q_re * b_im + q_im * b_re
    lbre_ref[...] = lb_re
    lbim_ref[...] = lb_im


def _s5_prep(a_re, a_im, log_dt, b_re, b_im):
    sd = jax.ShapeDtypeStruct(a_re.shape, _F32)
    return pl.pallas_call(_s5_prep_body, out_shape=[sd] * 4, name="s5_discretise")(
        a_re, a_im, log_dt, b_re, b_im)


def _s5_body(u_ref, wbre_ref, wbim_ref, lbre_ref, lbim_ref, wcre_ref, wcim_ref, wd_ref, d_ref,
             gw_ref, gb_ref, o_ref, xre_ref, xim_ref, cre_ref, cim_ref, stage_ref,
             *, points, nb, stride):
    rows = nb * points
    width = u_ref.shape[-1]
    n_tiles, _, tile_w = lbre_ref.shape
    n_blocks = wd_ref.shape[0]
    cb = width // n_blocks
    tiles_per_block = n_tiles // n_blocks
    halves = 2
    blocks_per_half = n_blocks // halves
    tiles_per_half = n_tiles // halves

    @pl.when(pl.program_id(0) == 0)
    def _():
        cre_ref[...] = jnp.zeros(cre_ref.shape, _F32)
        cim_ref[...] = jnp.zeros(cim_ref.shape, _F32)

    u_all = u_ref[...].astype(_F32)
    for o in range(n_blocks):
        stage_ref[o * nb:(o + 1) * nb] = u_all[:, :, o * cb:(o + 1) * cb]
    u_step = []
    for i in range(stride):
        picked = stage_ref[:, pl.ds(i, points, stride=stride), :]
        u_step.append([jnp.swapaxes(picked[o * nb:(o + 1) * nb], 0, 1).reshape(rows, cb)
                       for o in range(n_blocks)])
    taps = [jnp.concatenate([u_step[i][o] for i in range(stride)], axis=1).astype(_BF16)
            for o in range(n_blocks)]

    for hf in range(halves):
        blocks = range(hf * blocks_per_half, (hf + 1) * blocks_per_half)
        bu_r = jnp.concatenate(
            [jnp.dot(taps[o], wbre_ref[o], preferred_element_type=_F32) for o in blocks], axis=1)
        bu_i = jnp.concatenate(
            [jnp.dot(taps[o], wbim_ref[o], preferred_element_type=_F32) for o in blocks], axis=1)
        group = range(hf * tiles_per_half, (hf + 1) * tiles_per_half)
        a_r = [jnp.broadcast_to(lbre_ref[j], (nb, tile_w)) for j in group]
        a_i = [jnp.broadcast_to(lbim_ref[j], (nb, tile_w)) for j in group]
        x_r = [cre_ref[j] for j in group]
        x_i = [cim_ref[j] for j in group]
        for ti in range(points):
            r = slice(ti * nb, (ti + 1) * nb)
            for q, j in enumerate(group):
                cols = slice(q * tile_w, (q + 1) * tile_w)
                xre_ref[j, r, :] = x_r[q]
                xim_ref[j, r, :] = x_i[q]
                n_r = a_r[q] * x_r[q] - a_i[q] * x_i[q] + bu_r[r, cols]
                n_i = a_r[q] * x_i[q] + a_i[q] * x_r[q] + bu_i[r, cols]
                x_r[q], x_i[q] = n_r, n_i
        for q, j in enumerate(group):
            cre_ref[j] = x_r[q]
            cim_ref[j] = x_i[q]

    ys = []
    for o in range(n_blocks):
        tiles = range(o * tiles_per_block, (o + 1) * tiles_per_block)
        x_r = jnp.concatenate([xre_ref[j] for j in tiles], axis=1).astype(_BF16)
        x_i = jnp.concatenate([xim_ref[j] for j in tiles], axis=1).astype(_BF16)
        ys.append(jnp.dot(x_r, wcre_ref[o], preferred_element_type=_F32)
                  + jnp.dot(x_i, wcim_ref[o], preferred_element_type=_F32)
                  + jnp.dot(taps[o], wd_ref[o], preferred_element_type=_F32))
    for j in range(stride):
        y = jnp.concatenate([ys[o][:, j * cb:(j + 1) * cb] for o in range(n_blocks)], axis=1)
        g = jax.nn.gelu(y + d_ref[...] * jnp.concatenate(u_step[j], axis=1))
        o_j = g * jax.nn.sigmoid(jnp.dot(g.astype(_BF16), gw_ref[...], preferred_element_type=_F32)
                                 + gb_ref[...])
        stage_ref[:, pl.ds(j, points, stride=stride), :] = jnp.concatenate(
            [jnp.swapaxes(o_j[:, o * cb:(o + 1) * cb].reshape(points, nb, cb), 0, 1)
             for o in range(n_blocks)], axis=0)
    o_ref[...] = jnp.concatenate([stage_ref[o * nb:(o + 1) * nb] for o in range(n_blocks)],
                                 axis=2).astype(_BF16)


def _s5(u, wb_re, wb_im, lb_re, lb_im, wc_re, wc_im, wd, d_vec, glu_w, glu_b):
    nb, l, width = u.shape
    stride = _SCAN_STRIDE
    steps = _SCAN_STEPS
    points = steps // stride
    n_tiles, _, tile_w = lb_re.shape
    full = lambda a: pl.BlockSpec(a.shape, lambda i: (0,) * a.ndim)
    io_spec = pl.BlockSpec((nb, steps, width), lambda i: (0, i, 0))
    consts = (wb_re, wb_im, lb_re, lb_im, wc_re, wc_im, wd, d_vec, glu_w, glu_b)
    state_buf = pltpu.VMEM((n_tiles, nb * points, tile_w), _F32)
    return pl.pallas_call(
        functools.partial(_s5_body, points=points, nb=nb, stride=stride),
        grid=(l // steps,),
        in_specs=[io_spec] + [full(a) for a in consts],
        out_specs=io_spec,
        out_shape=jax.ShapeDtypeStruct(u.shape, _BF16),
        scratch_shapes=[state_buf, state_buf,
                        pltpu.VMEM((n_tiles, nb, tile_w), _F32),
                        pltpu.VMEM((n_tiles, nb, tile_w), _F32),
                        pltpu.VMEM((width // _CHANNEL_BLOCK * nb, steps, _CHANNEL_BLOCK), _F32)],
        compiler_params=pltpu.CompilerParams(
            dimension_semantics=("arbitrary",), vmem_limit_bytes=_VMEM_LIMIT),
        name="s5_branch",
    )(u, *consts)


def _outproj_body(x_ref, ng_ref, wg_ref, mb_ref, oa_ref, os_ref, pa_ref, ps_ref, wo_ref,
                  out_ref, *, width):
    x = x_ref[...]
    d = x.shape[-1]
    h = (x * _rms_scale(x) * ng_ref[...]).astype(_BF16)
    zg = jnp.dot(h, wg_ref[...], preferred_element_type=_F32)
    o_a = (oa_ref[...].astype(_F32) * jax.nn.silu(zg[:, :width])).astype(_BF16)
    o_s = (os_ref[...].astype(_F32) * jax.nn.silu(zg[:, width:2 * width])).astype(_BF16)
    p_a = jnp.dot(o_a, pa_ref[...], preferred_element_type=_F32)
    p_s = jnp.dot(o_s, ps_ref[...], preferred_element_type=_F32)
    g = jax.nn.sigmoid(zg[:, 2 * width:] + mb_ref[...])
    merged = g[:, :d] * p_a + g[:, d:] * p_s
    out_ref[...] = x + jnp.dot(merged.astype(_BF16), wo_ref[...], preferred_element_type=_F32)


def _outproj(x2, norm_gain, w_gates, merge_b, o_a, o_s, proj_a, proj_s, w_out, width):
    n, d = x2.shape
    tm = _TOKEN_TILE
    full = lambda a: pl.BlockSpec(a.shape, lambda i: (0,) * a.ndim)
    row = lambda a: pl.BlockSpec((tm, a.shape[-1]), lambda i: (i, 0))
    return pl.pallas_call(
        functools.partial(_outproj_body, width=width),
        grid=(n // tm,),
        in_specs=[row(x2), full(norm_gain), full(w_gates), full(merge_b), row(o_a), row(o_s),
                  full(proj_a), full(proj_s), full(w_out)],
        out_specs=row(x2),
        out_shape=jax.ShapeDtypeStruct(x2.shape, x2.dtype),
        compiler_params=pltpu.CompilerParams(
            dimension_semantics=("parallel",), vmem_limit_bytes=_VMEM_LIMIT),
        name="outproj",
    )(x2, norm_gain, w_gates, merge_b, o_a, o_s, proj_a, proj_s, w_out)


def _block_diag_halves(blocks, halves):
    g, r, c = blocks.shape
    gh = g // halves
    on_diagonal = (np.arange(gh * r)[:, None] // r) == (np.arange(gh * c)[None, :] // c)
    tiled = jnp.tile(blocks.reshape(halves, gh * r, c), (1, 1, gh))
    return jnp.where(on_diagonal, tiled, 0)


def _layer(x, lam_init, norm_gain, w_in, merge_gate_b, q_norm_gain, k_norm_gain,
           lambda_q1, lambda_k1, lambda_q2, lambda_k2, diff_subln_gain, rel_bias_table,
           ssm_a_re, ssm_a_im, ssm_log_dt, ssm_b_re, ssm_b_im, ssm_c_re, ssm_c_im,
           ssm_d, ssm_glu_w, ssm_glu_b, proj_attn, proj_ssm, w_out):
    nb, l, d = x.shape
    n = nb * l
    aw = _HEADS * 2 * _SUB_DIM
    groups = ssm_a_re.shape[0]
    x2 = x.reshape(n, d)
    ng = norm_gain.reshape(1, d).astype(_F32)

    w_ku = jnp.concatenate([w_in[:, aw:2 * aw], w_in[:, 4 * aw:5 * aw]], axis=1).astype(_BF16)
    w_qvt = jnp.concatenate([w_in[:, :aw], w_in[:, 2 * aw:3 * aw]], axis=1).T.astype(_BF16)
    w_gates = jnp.concatenate([w_in[:, 3 * aw:4 * aw], w_in[:, 5 * aw:]], axis=1).astype(_BF16)
    gq = jnp.tile(q_norm_gain.astype(_F32), 2 * _HEADS) * (_SUB_DIM ** -0.5 * _LOG2E)
    gqt = jnp.broadcast_to(gq[:, None], (aw, _TOKEN_TILE))
    gk = jnp.tile(k_norm_gain.astype(_F32), 2 * _HEADS).reshape(1, aw)
    seg = np.arange(_MXU_TILE) // _SUB_DIM
    gsum = jnp.asarray((seg[:, None] == seg[None, :]).astype(np.float32), _BF16)
    qt, k, vt, u = _inproj(x2, ng, w_ku, w_qvt, gqt, gk, gsum, aw)

    lam4 = jnp.stack([lambda_q1, lambda_k1, lambda_q2, lambda_k2]).astype(_F32)
    qk_gains = jnp.stack([gq[:_SUB_DIM], gk[0, :_SUB_DIM]])
    bias, lam = _bias_prep(rel_bias_table.astype(_F32), lam4, qk_gains, lam_init, _ATTN_TILE)
    subln = diff_subln_gain.astype(_F32) * (1.0 - lam_init)
    subln_t = jnp.broadcast_to(subln[:, None], (_V_DIM, _ATTN_TILE))
    shp = (nb, l, aw)
    o_a = _attention(qt, k.reshape(shp), vt, bias, lam, subln_t)

    rep = lambda a: jnp.repeat(a.astype(_F32), _SSM_GROUP, axis=0)
    ldt = jnp.broadcast_to(ssm_log_dt.astype(_F32)[:, None], ssm_a_re.shape)
    bt = lambda a: a.astype(_F32).transpose(0, 2, 1).reshape(groups * _SSM_GROUP, _SSM_STATE)
    lb_re, lb_im, bb_re, bb_im = _s5_prep(rep(ssm_a_re), rep(ssm_a_im), rep(ldt),
                                          bt(ssm_b_re), bt(ssm_b_im))
    n_tiles = groups * _SSM_STATE // _STATE_TILE
    flat = lambda a: a.reshape(n_tiles, 1, _STATE_TILE)
    gshape = (groups, _SSM_GROUP, _SSM_STATE)
    s, hh = _SCAN_STRIDE, _SSM_GROUP
    a_re, a_im = lb_re[::hh], lb_im[::hh]
    powers = [(jnp.ones_like(a_re), jnp.zeros_like(a_re))]
    for _ in range(s):
        p_re, p_im = powers[-1]
        powers.append((p_re * a_re - p_im * a_im, p_re * a_im + p_im * a_re))

    def times_power(z_re, z_im, k):
        p_re, p_im = powers[k][0][:, None, :], powers[k][1][:, None, :]
        return z_re * p_re - z_im * p_im, z_re * p_im + z_im * p_re

    blocks = aw // _CHANNEL_BLOCK
    gb = groups // blocks
    b_re, b_im = bb_re.reshape(gshape), bb_im.reshape(gshape)
    c_re, c_im = ssm_c_re.astype(_F32), ssm_c_im.astype(_F32)

    def taps_to_states(parts):
        m = _block_diag_halves(jnp.concatenate(parts, axis=1), blocks)
        m = m.reshape(blocks, gb, s, hh, gb * _SSM_STATE).transpose(0, 2, 1, 3, 4)
        return m.reshape(blocks, s * gb * hh, gb * _SSM_STATE).astype(_BF16)

    def states_to_steps(parts):
        m = _block_diag_halves(jnp.concatenate(parts, axis=2), blocks)
        m = m.reshape(blocks, gb * _SSM_STATE, gb, s, hh).transpose(0, 1, 3, 2, 4)
        return m.reshape(blocks, gb * _SSM_STATE, s * gb * hh).astype(_BF16)

    b_taps = [times_power(b_re, b_im, s - 1 - i) for i in range(s)]
    wb_re = taps_to_states([t[0] for t in b_taps])
    wb_im = taps_to_states([t[1] for t in b_taps])
    c_steps = [times_power(c_re, c_im, j + 1) for j in range(s)]
    wc_re = states_to_steps([t[0].transpose(0, 2, 1) for t in c_steps])
    wc_im = states_to_steps([-t[1].transpose(0, 2, 1) for t in c_steps])
    c_lag = [times_power(c_re, c_im, k) for k in range(s)]
    mix = lambda a, b: jnp.einsum("kgcs,gds->kgdc", a, b, precision=lax.Precision.HIGHEST)
    lag = (mix(jnp.stack([t[0] for t in c_lag]), b_re)
           - mix(jnp.stack([t[1] for t in c_lag]), b_im))
    none = jnp.zeros_like(lag[0])
    per_group = jnp.concatenate(
        [jnp.concatenate([lag[j - i] if j >= i else none for j in range(s)], axis=2)
         for i in range(s)], axis=1)
    wd = _block_diag_halves(per_group, blocks)
    wd = wd.reshape(blocks, gb, s, hh, gb, s, hh).transpose(0, 2, 1, 3, 5, 4, 6)
    wd = wd.reshape(blocks, s * gb * hh, s * gb * hh).astype(_BF16)
    o_s = _s5(u.reshape(shp), wb_re, wb_im, flat(powers[s][0]), flat(powers[s][1]),
              wc_re, wc_im, wd, ssm_d.astype(_F32).reshape(1, aw), ssm_glu_w.astype(_BF16),
              ssm_glu_b.astype(_F32).reshape(1, aw))

    out = _outproj(x2, ng, w_gates, merge_gate_b.astype(_F32).reshape(1, 2 * d),
                   o_a.reshape(n, aw), o_s.reshape(n, aw),
                   proj_attn.astype(_BF16), proj_ssm.astype(_BF16), w_out.astype(_BF16), aw)
    return out.reshape(nb, l, d)


def kernel(x, norm_gain, w_in, merge_gate_b, q_norm_gain, k_norm_gain, lambda_q1, lambda_k1,
           lambda_q2, lambda_k2, diff_subln_gain, rel_bias_table, ssm_A_re, ssm_A_im, ssm_log_dt,
           ssm_B_re, ssm_B_im, ssm_C_re, ssm_C_im, ssm_D, ssm_glu_w, ssm_glu_b,
           proj_attn, proj_ssm, w_out):
    per_layer = (norm_gain, w_in, merge_gate_b, q_norm_gain, k_norm_gain, lambda_q1, lambda_k1,
                 lambda_q2, lambda_k2, diff_subln_gain)
    per_layer_tail = (ssm_A_re, ssm_A_im, ssm_log_dt, ssm_B_re, ssm_B_im, ssm_C_re, ssm_C_im,
                      ssm_D, ssm_glu_w, ssm_glu_b, proj_attn, proj_ssm, w_out)
    for layer in range(norm_gain.shape[0]):
        lam_init = 0.8 - 0.6 * math.exp(-0.3 * layer)
        x = _layer(x, lam_init, *(p[layer] for p in per_layer), rel_bias_table,
                   *(p[layer] for p in per_layer_tail))
    return x
```
